```python
import math
import functools
import jax
import jax.numpy as jnp
from jax import lax
import numpy as np

D_MODEL = 2048
BATCH = 4
SEQ = 4096
DEPTH = 2

GRID_W = 64
CTX_LEN = 256
ROPE_THETA = 10000.0
EPS = 1e-6
Q_BLOCK = 128

SWA_HEADS = 8
SWA_KV_HEADS = 2
SWA_HEAD_DIM = 128
SWA_WINDOW = 128

MLA_HEADS = 8
MLA_Q_RANK = 512
MLA_KV_RANK = 256
MLA_NOPE = 128
MLA_ROPE = 64
MLA_V = 128

SSM_WIDTH = 1024
SSM_GROUP = 16
SSM_GROUPS = SSM_WIDTH // SSM_GROUP
SSM_STATE = 64

N_BRANCH = 3
BRANCH_WIDTH = 1024
COL_SIZES = (SWA_HEADS * SWA_HEAD_DIM, SWA_KV_HEADS * SWA_HEAD_DIM, SWA_KV_HEADS * SWA_HEAD_DIM, MLA_Q_RANK, MLA_KV_RANK, MLA_ROPE, SSM_WIDTH, N_BRANCH * D_MODEL)
IN_COLS = sum(COL_SIZES)

FFN_DIM = 5632
N_EXPERTS = 8
TOP_K = 2
EXPERT_DIM = 7168

kernel_name = 'hybrid_diffusion_trunk'


def rms_norm(x, g):
    xf = x.astype(jnp.float32)
    y = xf * lax.rsqrt(jnp.mean(xf * xf, axis=-1, keepdims=True) + EPS)
    return (y * g.astype(jnp.float32)).astype(x.dtype)


def modulate(x, g, shift, scale):
    return rms_norm(x, g) * (1.0 + scale) + shift


def ada_mod(cond, w, b):
    return jnp.split(jax.nn.silu(cond) @ w + b, 6, axis=-1)


def axial_rope(rows, rot_dim):
    t_row = jnp.repeat(jnp.arange(rows, dtype=jnp.float32), GRID_W)
    t_col = jnp.tile(jnp.arange(GRID_W, dtype=jnp.float32), rows)
    quarter = rot_dim // 4
    inv_freq = ROPE_THETA ** (-jnp.arange(quarter, dtype=jnp.float32) / quarter)
    ang = jnp.concatenate([t_row[:, None] * inv_freq, t_col[:, None] * inv_freq], axis=-1)
    return jnp.cos(ang), jnp.sin(ang)


def apply_rope(x, cos, sin):
    half = x.shape[-1] // 2
    xf = x.astype(jnp.float32)
    x1, x2 = xf[..., :half], xf[..., half:]
    cs, sn = cos[None, :, None, :], sin[None, :, None, :]
    return jnp.concatenate([x1 * cs - x2 * sn, x2 * cs + x1 * sn], axis=-1).astype(x.dtype)


def windowed_gqa_sink(q_c, k_c, v_c, q_l, k_l, v_l, sink, with_ctx):
    B, S, H, Dh = q_l.shape
    KVH = k_l.shape[2]
    G = H // KVH
    L = k_c.shape[1]
    blk = SWA_WINDOW
    nb = S // blk
    scale = Dh ** -0.5
    qb = q_l.reshape(B, nb, blk, KVH, G, Dh)
    pad = ((0, 0), (blk, blk), (0, 0), (0, 0))
    kp = jnp.pad(k_l, pad).reshape(B, nb + 2, blk, KVH, Dh)
    vp = jnp.pad(v_l, pad).reshape(B, nb + 2, blk, KVH, Dh)
    kb = jnp.concatenate([kp[:, :-2], kp[:, 1:-1], kp[:, 2:]], axis=2)
    vb = jnp.concatenate([vp[:, :-2], vp[:, 1:-1], vp[:, 2:]], axis=2)
    s_win = jnp.einsum('bnqhgd,bnjhd->bnhgqj', qb, kb).astype(jnp.float32) * scale
    rel = jnp.arange(3 * blk)[None, :] - blk - jnp.arange(blk)[:, None]
    key_pos = jnp.arange(nb)[:, None] * blk - blk + jnp.arange(3 * blk)[None, :]
    mask = (jnp.abs(rel) <= SWA_WINDOW)[None] & ((key_pos >= 0) & (key_pos < S))[:, None, :]
    s_win = jnp.where(mask[None, :, None, None], s_win, -jnp.inf)
    s_ctx = jnp.einsum('bnqhgd,bjhd->bnhgqj', qb, k_c).astype(jnp.float32) * scale
    sink_f = sink.astype(jnp.float32).reshape(1, 1, KVH, G, 1, 1)
    sink_l = jnp.broadcast_to(sink_f, s_win.shape[:-1] + (1,))
    p = jax.nn.softmax(jnp.concatenate([s_win, s_ctx, sink_l], axis=-1), axis=-1).astype(v_l.dtype)
    W = 3 * blk
    o = jnp.einsum('bnhgqj,bnjhd->bnqhgd', p[..., :W], vb) + jnp.einsum('bnhgqj,bjhd->bnqhgd', p[..., W:W + L], v_c)
    o_l = o.reshape(B, S, H * Dh)
    o_c = None
    if with_ctx:
        qc = q_c.reshape(B, L, KVH, G, Dh)
        s_c = jnp.einsum('bqhgd,bjhd->bhgqj', qc, k_c).astype(jnp.float32) * scale
        sink_c = jnp.broadcast_to(sink.astype(jnp.float32).reshape(1, KVH, G, 1, 1), s_c.shape[:-1] + (1,))
        pc = jax.nn.softmax(jnp.concatenate([s_c, sink_c], axis=-1), axis=-1).astype(v_c.dtype)
        o_c = jnp.einsum('bhgqj,bjhd->bqhgd', pc[..., :L], v_c).reshape(B, L, H * Dh)
    return o_c, o_l


def mla_attention(c_q, c_kv, k_pe, q_a_norm_g, w_uq, kv_a_norm_g, w_ukv, q_norm_g, k_norm_g, rope_b, L, with_ctx):
    B, T, _ = c_q.shape
    S = T - L
    q = (rms_norm(c_q, q_a_norm_g) @ w_uq).reshape(B, T, MLA_HEADS, MLA_NOPE + MLA_ROPE)
    kv = (rms_norm(c_kv, kv_a_norm_g) @ w_ukv).reshape(B, T, MLA_HEADS, MLA_NOPE + MLA_V)
    k_nope, v = kv[..., :MLA_NOPE], kv[..., MLA_NOPE:]
    k = jnp.concatenate([k_nope, jnp.broadcast_to(k_pe[:, :, None, :], (B, T, MLA_HEADS, MLA_ROPE))], axis=-1)
    q = rms_norm(q, q_norm_g)
    k = rms_norm(k, k_norm_g)

    def rope_latent(t):
        tl = t[:, L:]
        return jnp.concatenate([tl[..., :MLA_NOPE], apply_rope(tl[..., MLA_NOPE:], *rope_b)], axis=-1)

    q_l, k_l = rope_latent(q), rope_latent(k)
    q_c, k_c, v_c, v_l = q[:, :L], k[:, :L], v[:, :L], v[:, L:]
    scale = (MLA_NOPE + MLA_ROPE) ** -0.5
    k_all = jnp.concatenate([k_c, k_l], axis=1)
    v_all = jnp.concatenate([v_c, v_l], axis=1)
    nb = S // Q_BLOCK
    q_blocks = jnp.moveaxis(q_l.reshape(B, nb, Q_BLOCK, MLA_HEADS, MLA_NOPE + MLA_ROPE), 1, 0)

    def attend_block(q_blk):
        s = jnp.einsum('bqhd,bjhd->bhqj', q_blk, k_all).astype(jnp.float32) * scale
        p = jax.nn.softmax(s, axis=-1).astype(v_all.dtype)
        return jnp.einsum('bhqj,bjhd->bqhd', p, v_all)

    o_l = jnp.moveaxis(lax.map(attend_block, q_blocks), 0, 1).reshape(B, S, MLA_HEADS * MLA_V)
    o_c = None
    if with_ctx:
        s_c = jnp.einsum('bqhd,bjhd->bhqj', q_c, k_c).astype(jnp.float32) * scale
        p_c = jax.nn.softmax(s_c, axis=-1).astype(v_c.dtype)
        o_c = jnp.einsum('bhqj,bjhd->bqhd', p_c, v_c).reshape(B, L, MLA_HEADS * MLA_V)
    return o_c, o_l


def ssm_combine(e1, e2):
    a1, b1 = e1
    a2, b2 = e2
    return a1 * a2, a2 * b1 + b2


def s5_branch(u_c, u_l, lam_re, lam_im, log_step, b_re, b_im, c_re, c_im, d_skip, w_glu, b_glu, with_ctx):
    B, L, W = u_c.shape
    S = u_l.shape[1]
    f32 = jnp.float32
    uc = u_c.astype(f32).reshape(B, L, SSM_GROUPS, SSM_GROUP)
    ul = u_l.astype(f32).reshape(B, S, SSM_GROUPS, SSM_GROUP)
    d4 = d_skip.astype(f32).reshape(SSM_GROUPS, SSM_GROUP)
    y_c = uc * d4
    y_l = ul * d4
    for direction in range(2):
        reverse = direction == 1
        lam = lax.complex(lam_re[direction].astype(f32), lam_im[direction].astype(f32))
        delta = jnp.exp(log_step[direction].astype(f32))[:, None]
        lam_bar = jnp.exp(lam * delta)
        b_bar = ((lam_bar - 1.0) / lam)[:, :, None] * lax.complex(b_re[direction].astype(f32), b_im[direction].astype(f32))
        c_mat = lax.complex(c_re[direction].astype(f32), c_im[direction].astype(f32))
        bu_c = jnp.einsum('blgh,gph->blgp', uc.astype(jnp.complex64), b_bar)
        a_c = jnp.broadcast_to(lam_bar, (1, L) + lam_bar.shape)
        _, x_c = lax.associative_scan(ssm_combine, (a_c, bu_c), reverse=reverse, axis=1)
        s0 = x_c[:, 0] if reverse else x_c[:, -1]
        bu_l = jnp.einsum('bsgh,gph->bsgp', ul.astype(jnp.complex64), b_bar)
        a_l = jnp.broadcast_to(lam_bar, (1, S) + lam_bar.shape)
        a_cum, x_l = lax.associative_scan(ssm_combine, (a_l, bu_l), reverse=reverse, axis=1)
        x_l = x_l + a_cum * s0[:, None]
        y_l = y_l + jnp.einsum('bsgp,ghp->bsgh', x_l, c_mat).real
        if with_ctx:
            y_c = y_c + jnp.einsum('blgp,ghp->blgh', x_c, c_mat).real

    def glu(y, n):
        g = jax.nn.gelu(y.reshape(B, n, W).astype(u_l.dtype)) @ w_glu + b_glu
        return g[..., :W] * jax.nn.sigmoid(g[..., W:])

    return (glu(y_c, L) if with_ctx else None), glu(y_l, S)


def merge_branches(ys, gate_logits, w_branch, w_out):
    gates = jax.nn.sigmoid(gate_logits.reshape(gate_logits.shape[:-1] + (N_BRANCH, D_MODEL)))
    proj = jnp.einsum('btnw,nwd->btnd', ys, w_branch)
    return jnp.sum(gates * proj, axis=2) @ w_out


def mixer(h_c, h_l, w_in, swa_q_norm_g, swa_k_norm_g, swa_sink,
          mla_q_a_norm_g, mla_w_uq, mla_kv_a_norm_g, mla_w_ukv, mla_q_norm_g, mla_k_norm_g,
          ssm_lam_re, ssm_lam_im, ssm_log_step, ssm_b_re, ssm_b_im, ssm_c_re, ssm_c_im,
          ssm_d, ssm_w_glu, ssm_b_glu, w_branch, w_out, rope_a, rope_b, with_ctx):
    B, L, _ = h_c.shape
    T = L + h_l.shape[1]
    z = jnp.concatenate([h_c, h_l], axis=1) @ w_in
    split_points = np.cumsum(COL_SIZES)[:-1].tolist()
    q_a, k_a, v_a, c_q, c_kv, k_pe, u, gate_logits = jnp.split(z, split_points, axis=-1)
    q_a = rms_norm(q_a.reshape(B, T, SWA_HEADS, SWA_HEAD_DIM), swa_q_norm_g)
    k_a = rms_norm(k_a.reshape(B, T, SWA_KV_HEADS, SWA_HEAD_DIM), swa_k_norm_g)
    v_a = v_a.reshape(B, T, SWA_KV_HEADS, SWA_HEAD_DIM)
    ya_c, ya_l = windowed_gqa_sink(q_a[:, :L], k_a[:, :L], v_a[:, :L],
                                   apply_rope(q_a[:, L:], *rope_a), apply_rope(k_a[:, L:], *rope_a), v_a[:, L:],
                                   swa_sink, with_ctx)
    yb_c, yb_l = mla_attention(c_q, c_kv, k_pe, mla_q_a_norm_g, mla_w_uq, mla_kv_a_norm_g, mla_w_ukv,
                               mla_q_norm_g, mla_k_norm_g, rope_b, L, with_ctx)
    yc_c, yc_l = s5_branch(u[:, :L], u[:, L:], ssm_lam_re, ssm_lam_im, ssm_log_step, ssm_b_re, ssm_b_im,
                           ssm_c_re, ssm_c_im, ssm_d, ssm_w_glu, ssm_b_glu, with_ctx)
    m_l = merge_branches(jnp.stack([ya_l, yb_l, yc_l], axis=2), gate_logits[:, L:], w_branch, w_out)
    m_c = None
    if with_ctx:
        m_c = merge_branches(jnp.stack([ya_c, yb_c, yc_c], axis=2), gate_logits[:, :L], w_branch, w_out)
    return m_c, m_l


def swiglu(h, w13, w2):
    gu = h @ w13
    g, u = jnp.split(gu, 2, axis=-1)
    return (jax.nn.silu(g) * u) @ w2


def moe_swiglu(h, w_router, b_router, w13, w2):
    logits = (h @ w_router + b_router).astype(jnp.float32)
    top_val, top_idx = lax.top_k(logits, TOP_K)
    top_w = jax.nn.softmax(top_val, axis=-1)
    combine = jnp.einsum('bsk,bske->bse', top_w, jax.nn.one_hot(top_idx, N_EXPERTS, dtype=jnp.float32)).astype(h.dtype)
    out = jnp.zeros_like(h)
    for e in range(N_EXPERTS):
        out = out + combine[..., e:e + 1] * swiglu(h, w13[e], w2[e])
    return out


def setup_inputs(seed: int = 0) -> dict:
    key = jax.random.key(seed)
    ks = iter(jax.random.split(key, 48))
    f32 = jnp.float32
    n_dense = (DEPTH + 1) // 2
    n_moe = DEPTH // 2
    G, P = SSM_GROUPS, SSM_STATE

    def normal(shape, scale):
        return jax.random.normal(next(ks), shape, f32) * scale

    def gain(shape):
        return 1.0 + normal(shape, 0.02)

    return {
        'x': normal((BATCH, SEQ, D_MODEL), 1.0),
        'c': normal((BATCH, D_MODEL), 1.0),
        'ctx': normal((BATCH, CTX_LEN, D_MODEL), 1.0),
        'c_ctx': normal((D_MODEL,), 1.0),
        'mod_w': normal((DEPTH, D_MODEL, 6 * D_MODEL), 0.5 * D_MODEL ** -0.5),
        'mod_b': normal((DEPTH, 6 * D_MODEL), 0.02),
        'norm_mix_g': gain((DEPTH, D_MODEL)),
        'norm_ffn_g': gain((DEPTH, D_MODEL)),
        'w_in': normal((DEPTH, D_MODEL, IN_COLS), D_MODEL ** -0.5),
        'swa_q_norm_g': gain((DEPTH, SWA_HEAD_DIM)),
        'swa_k_norm_g': gain((DEPTH, SWA_HEAD_DIM)),
        'swa_sink': normal((DEPTH, SWA_HEADS), 0.5),
        'mla_q_a_norm_g': gain((DEPTH, MLA_Q_RANK)),
        'mla_w_uq': normal((DEPTH, MLA_Q_RANK, MLA_HEADS * (MLA_NOPE + MLA_ROPE)), MLA_Q_RANK ** -0.5),
        'mla_kv_a_norm_g': gain((DEPTH, MLA_KV_RANK)),
        'mla_w_ukv': normal((DEPTH, MLA_KV_RANK, MLA_HEADS * (MLA_NOPE + MLA_V)), MLA_KV_RANK ** -0.5),
        'mla_q_norm_g': gain((DEPTH, MLA_NOPE + MLA_ROPE)),
        'mla_k_norm_g': gain((DEPTH, MLA_NOPE + MLA_ROPE)),
        'ssm_lam_re': -0.5 + normal((DEPTH, 2, G, P), 0.01),
        'ssm_lam_im': jnp.pi * jnp.arange(P, dtype=f32) + normal((DEPTH, 2, G, P), 0.01),
        'ssm_log_step': jax.random.uniform(next(ks), (DEPTH, 2, G), f32, math.log(1e-3), math.log(1e-1)),
        'ssm_b_re': normal((DEPTH, 2, G, P, SSM_GROUP), (2 * SSM_GROUP) ** -0.5),
        'ssm_b_im': normal((DEPTH, 2, G, P, SSM_GROUP), (2 * SSM_GROUP) ** -0.5),
        'ssm_c_re': normal((DEPTH, 2, G, SSM_GROUP, P), (2 * P) ** -0.5),
        'ssm_c_im': normal((DEPTH, 2, G, SSM_GROUP, P), (2 * P) ** -0.5),
        'ssm_d': normal((DEPTH, SSM_WIDTH), 1.0),
        'ssm_w_glu': normal((DEPTH, SSM_WIDTH, 2 * SSM_WIDTH), SSM_WIDTH ** -0.5),
        'ssm_b_glu': normal((DEPTH, 2 * SSM_WIDTH), 0.01),
        'w_branch': normal((DEPTH, N_BRANCH, BRANCH_WIDTH, D_MODEL), BRANCH_WIDTH ** -0.5),
        'w_out': normal((DEPTH, D_MODEL, D_MODEL), D_MODEL ** -0.5),
        'ffn_w13': normal((n_dense, D_MODEL, 2 * FFN_DIM), D_MODEL ** -0.5),
        'ffn_w2': normal((n_dense, FFN_DIM, D_MODEL), FFN_DIM ** -0.5),
        'moe_w_router': normal((n_moe, D_MODEL, N_EXPERTS), D_MODEL ** -0.5),
        'moe_b_router': normal((n_moe, N_EXPERTS), 0.01),
        'moe_w13': normal((n_moe, N_EXPERTS, D_MODEL, 2 * EXPERT_DIM), D_MODEL ** -0.5),
        'moe_w2': normal((n_moe, N_EXPERTS, EXPERT_DIM, D_MODEL), EXPERT_DIM ** -0.5),
    }


def reference(x, c, ctx, c_ctx, mod_w, mod_b, norm_mix_g, norm_ffn_g, w_in,
              swa_q_norm_g, swa_k_norm_g, swa_sink,
              mla_q_a_norm_g, mla_w_uq, mla_kv_a_norm_g, mla_w_ukv, mla_q_norm_g, mla_k_norm_g,
              ssm_lam_re, ssm_lam_im, ssm_log_step, ssm_b_re, ssm_b_im, ssm_c_re, ssm_c_im,
              ssm_d, ssm_w_glu, ssm_b_glu, w_branch, w_out,
              ffn_w13, ffn_w2, moe_w_router, moe_b_router, moe_w13, moe_w2):
    ROWS = x.shape[1] // GRID_W
    rope_a = axial_rope(ROWS, SWA_HEAD_DIM)
    rope_b = axial_rope(ROWS, MLA_ROPE)
    x_l, x_c = x, ctx
    for layer in range(DEPTH):
        with_ctx = layer < DEPTH - 1
        sh_m, sc_m, g_m, sh_f, sc_f, g_f = ada_mod(c, mod_w[layer], mod_b[layer])
        csh_m, csc_m, cg_m, csh_f, csc_f, cg_f = ada_mod(c_ctx, mod_w[layer], mod_b[layer])
        h_l = modulate(x_l, norm_mix_g[layer], sh_m[:, None], sc_m[:, None])
        h_c = modulate(x_c, norm_mix_g[layer], csh_m, csc_m)
        m_c, m_l = mixer(h_c, h_l, w_in[layer], swa_q_norm_g[layer], swa_k_norm_g[layer], swa_sink[layer],
                         mla_q_a_norm_g[layer], mla_w_uq[layer], mla_kv_a_norm_g[layer], mla_w_ukv[layer],
                         mla_q_norm_g[layer], mla_k_norm_g[layer],
                         ssm_lam_re[layer], ssm_lam_im[layer], ssm_log_step[layer], ssm_b_re[layer], ssm_b_im[layer],
                         ssm_c_re[layer], ssm_c_im[layer], ssm_d[layer], ssm_w_glu[layer], ssm_b_glu[layer],
                         w_branch[layer], w_out[layer], rope_a, rope_b, with_ctx)
        x_l = x_l + g_m[:, None] * m_l
        if with_ctx:
            x_c = x_c + cg_m * m_c
        if layer % 2 == 0:
            ffn = functools.partial(swiglu, w13=ffn_w13[layer // 2], w2=ffn_w2[layer // 2])
        else:
            ffn = functools.partial(moe_swiglu, w_router=moe_w_router[layer // 2], b_router=moe_b_router[layer // 2],
                                    w13=moe_w13[layer // 2], w2=moe_w2[layer // 2])
        x_l = x_l + g_f[:, None] * ffn(modulate(x_l, norm_ffn_g[layer], sh_f[:, None], sc_f[:, None]))
        if with_ctx:
            x_c = x_c + cg_f * ffn(modulate(x_c, norm_ffn_g[layer], csh_f, csc_f))
    return x_l
```

```python
import functools
import math

import jax
import jax.numpy as jnp
from jax import lax
from jax.experimental import pallas as pl
from jax.experimental.pallas import tpu as pltpu

F32 = jnp.float32
BF16 = jnp.bfloat16

GRID_W = 64
ROPE_THETA = 10000.0
EPS = 1e-6
SWA_HEADS = 8
SWA_KV_HEADS = 2
SWA_HEAD_DIM = 128
SWA_WINDOW = 128
MLA_HEADS = 8
MLA_NOPE = 128
MLA_ROPE = 64
MLA_V = 128
MLA_QK = MLA_NOPE + MLA_ROPE
MLA_QK_PAD = 256
SSM_GROUP = 16
SSM_STATE = 64
SSM_CHUNK = 16
N_BRANCH = 3
N_EXPERTS = 8
TOP_K = 2
LANE = 128
VMEM_LIMIT_BYTES = 56 * 1024 * 1024
MOE_TM = 512
NEG_BIG = -1e30


def _params(n_grid):
    return pltpu.CompilerParams(dimension_semantics=("arbitrary",) * n_grid, vmem_limit_bytes=VMEM_LIMIT_BYTES)


def _pick_tile(candidates, *sizes):
    for t in candidates:
        if all(s % t == 0 for s in sizes):
            return t
    raise ValueError(f"no tile in {candidates} divides {sizes}")


class Dims:
    def __init__(self, batch, seq, ctx_len):
        self.B, self.S, self.L = batch, seq, ctx_len
        self.RC = batch * ctx_len
        self.RL = batch * seq
        self.R = self.RC + self.RL
        self.tm = _pick_tile((1024, 512, 256, 128), ctx_len * batch, seq)

    def mod_row(self, tile, tm):
        nct = self.RC // tm
        return jnp.where(tile < nct, 0, 1 + (tile - nct) // (self.S // tm))


def _silu(x):
    return x * (1.0 / (1.0 + jnp.exp(-x)))


def _sigmoid(x):
    return 1.0 / (1.0 + jnp.exp(-x))


def _gelu_tanh(x):
    c = math.sqrt(2.0 / math.pi)
    return 0.5 * x * (1.0 + jnp.tanh(c * (x + 0.044715 * (x * x * x))))


def _dot(a, b):
    return jnp.dot(a, b, preferred_element_type=F32)


def _dot_nt(a, b):
    return lax.dot_general(a, b, (((1,), (1,)), ((), ())), preferred_element_type=F32)


def _modulate_kernel(x_ref, g_ref, sh_ref, sc_ref, o_ref):
    x = x_ref[...]
    ms = jnp.mean(x * x, axis=-1, keepdims=True)
    y = x * lax.rsqrt(ms + EPS) * g_ref[...]
    o_ref[...] = (y * (1.0 + sc_ref[0]) + sh_ref[0]).astype(o_ref.dtype)


def modulate(dm, x, g, shift, scale, *, mod_row0):
    n_rows, D = x.shape
    tm = min(dm.tm, 512)
    t0 = mod_row0 // tm
    return pl.pallas_call(
        _modulate_kernel,
        grid=(n_rows // tm,),
        in_specs=[
            pl.BlockSpec((tm, D), lambda i: (i, 0)),
            pl.BlockSpec((1, D), lambda i: (0, 0)),
            pl.BlockSpec((1, 1, D), lambda i: (dm.mod_row(i + t0, tm), 0, 0)),
            pl.BlockSpec((1, 1, D), lambda i: (dm.mod_row(i + t0, tm), 0, 0)),
        ],
        out_specs=pl.BlockSpec((tm, D), lambda i: (i, 0)),
        out_shape=jax.ShapeDtypeStruct((n_rows, D), BF16),
        compiler_params=_params(1),
        name="modulate",
    )(x, g.reshape(1, D), shift, scale)


def _mm1_kernel(*refs, n_w, n_extra, epilogue, prologue):
    a_ref = refs[0]
    w_refs = refs[1:1 + n_w]
    extra = refs[1 + n_w:1 + n_w + n_extra]
    o_ref = refs[1 + n_w + n_extra]
    wb = refs[2 + n_w + n_extra:]

    @pl.when(pl.program_id(1) == 0)
    def _():
        for w_ref, b in zip(w_refs, wb):
            b[...] = w_ref[...].astype(BF16)

    a = a_ref[...]
    if prologue is not None:
        a = prologue(a)
    accs = [_dot(a, b[...]) for b in wb]
    o_ref[...] = epilogue(accs, *extra).astype(o_ref.dtype)


def mm1(a, weights, epilogue, *, n_rows, n_cols, tm, tn, out_dtype, a_row0=0, extras=(), prologue=None, name):
    K = a.shape[1]
    t0 = a_row0 // tm
    in_specs = [pl.BlockSpec((tm, K), lambda j, i: (i + t0, 0))]
    operands = [a]
    for w, lead, col0 in weights:
        c0 = col0 // tn
        in_specs.append(pl.BlockSpec((None,) * len(lead) + (K, tn),
                                     functools.partial(lambda j, i, lead, c0: lead + (0, j + c0), lead=lead, c0=c0)))
        operands.append(w)
    for arr, bshape, imap in extras:
        in_specs.append(pl.BlockSpec(bshape, imap))
        operands.append(arr)
    kern = functools.partial(_mm1_kernel, n_w=len(weights), n_extra=len(extras), epilogue=epilogue,
                             prologue=prologue)
    return pl.pallas_call(
        kern,
        grid=(n_cols // tn, n_rows // tm),
        in_specs=in_specs,
        out_specs=pl.BlockSpec((tm, tn), lambda j, i: (i, j)),
        out_shape=jax.ShapeDtypeStruct((n_rows, n_cols), out_dtype),
        scratch_shapes=[pltpu.VMEM((K, tn), BF16) for _ in weights],
        compiler_params=_params(2),
        name=name,
    )(*operands)


def _epi_id(accs):
    return accs[0]


def _epi_sigmoid(accs):
    return _sigmoid(accs[0])


def _epi_swiglu(accs):
    return _silu(accs[0]) * accs[1]


def _epi_bias(accs, b_ref):
    return accs[0] + b_ref[...]


def _epi_glu_bias(accs, ba_ref, bb_ref):
    return (accs[0] + ba_ref[...]) * _sigmoid(accs[1] + bb_ref[...])


def _epi_residual(accs, x_ref, gate_ref):
    return x_ref[...] + gate_ref[0] * accs[0]


def _mm2_kernel(a_ref, w_ref, x_ref, gate_ref, o_ref, acc_ref):
    k = pl.program_id(1)

    @pl.when(k == 0)
    def _():
        acc_ref[...] = jnp.zeros_like(acc_ref)

    acc_ref[...] += _dot(a_ref[...], w_ref[...])

    @pl.when(k == pl.num_programs(1) - 1)
    def _():
        o_ref[...] = x_ref[...] + gate_ref[0] * acc_ref[...]


def mm2_residual(dm, a, w, x, gate, *, mod_row0, tm, tk):
    M, K = a.shape
    N = w.shape[1]
    t0 = mod_row0 // tm
    return pl.pallas_call(
        _mm2_kernel,
        grid=(M // tm, K // tk),
        in_specs=[
            pl.BlockSpec((tm, tk), lambda i, k: (i, k)),
            pl.BlockSpec((tk, N), lambda i, k: (k, 0)),
            pl.BlockSpec((tm, N), lambda i, k: (i, 0)),
            pl.BlockSpec((1, 1, N), lambda i, k: (dm.mod_row(i + t0, tm), 0, 0)),
        ],
        out_specs=pl.BlockSpec((tm, N), lambda i, k: (i, 0)),
        out_shape=jax.ShapeDtypeStruct((M, N), F32),
        scratch_shapes=[pltpu.VMEM((tm, N), F32)],
        compiler_params=_params(2),
        name="mm2_residual",
    )(a, w, x, gate)


def _swa_prep_kernel(q_ref, k_ref, v_ref, gq_ref, gk_ref, cos_ref, sin_ref, qo_ref, ko_ref, vo_ref):
    c = cos_ref[...]
    s = sin_ref[...]

    def norm_rope(x, g, scale):
        ms = jnp.mean(x * x, axis=-1, keepdims=True)
        y = x * lax.rsqrt(ms + EPS) * g
        return (y * c + pltpu.roll(y, SWA_HEAD_DIM // 2, 1) * s) * scale

    gq = gq_ref[...]
    gk = gk_ref[...]
    for h in range(SWA_HEADS):
        sl = slice(h * SWA_HEAD_DIM, (h + 1) * SWA_HEAD_DIM)
        qo_ref[:, sl] = norm_rope(q_ref[:, sl], gq, SWA_HEAD_DIM ** -0.5).astype(BF16)
    for h in range(SWA_KV_HEADS):
        sl = slice(h * SWA_HEAD_DIM, (h + 1) * SWA_HEAD_DIM)
        ko_ref[:, sl] = norm_rope(k_ref[:, sl], gk, 1.0).astype(BF16)
    vo_ref[...] = v_ref[...].astype(BF16)


def _rope_tile_index(dm, tm):
    nct = dm.RC // tm
    return lambda i: (jnp.where(i < nct, 0, 1 + (i - nct) % (dm.S // tm)), 0)


def swa_prep(dm, z, col, gq, gk, cos_t, sin_t):
    tm = min(dm.tm, 256)
    QW = SWA_HEADS * SWA_HEAD_DIM
    KW = SWA_KV_HEADS * SWA_HEAD_DIM
    ridx = _rope_tile_index(dm, tm)
    return pl.pallas_call(
        _swa_prep_kernel,
        grid=(dm.R // tm,),
        in_specs=[
            pl.BlockSpec((tm, QW), lambda i: (i, col["q"] // QW)),
            pl.BlockSpec((tm, KW), lambda i: (i, col["k"] // KW)),
            pl.BlockSpec((tm, KW), lambda i: (i, col["v"] // KW)),
            pl.BlockSpec((1, SWA_HEAD_DIM), lambda i: (0, 0)),
            pl.BlockSpec((1, SWA_HEAD_DIM), lambda i: (0, 0)),
            pl.BlockSpec((tm, SWA_HEAD_DIM), ridx),
            pl.BlockSpec((tm, SWA_HEAD_DIM), ridx),
        ],
        out_specs=[
            pl.BlockSpec((tm, QW), lambda i: (i, 0)),
            pl.BlockSpec((tm, KW), lambda i: (i, 0)),
            pl.BlockSpec((tm, KW), lambda i: (i, 0)),
        ],
        out_shape=[
            jax.ShapeDtypeStruct((dm.R, QW), BF16),
            jax.ShapeDtypeStruct((dm.R, KW), BF16),
            jax.ShapeDtypeStruct((dm.R, KW), BF16),
        ],
        compiler_params=_params(1),
        name="swa_prep",
    )(z, z, z, gq.reshape(1, -1), gk.reshape(1, -1), cos_t, sin_t)


def _swa_attn_kernel(*refs, windowed, nb):
    if windowed:
        q_ref, kc_ref, kp_ref, kk_ref, kn_ref, vc_ref, vp_ref, vk_ref, vn_ref, sink_ref, o_ref = refs
    else:
        q_ref, kc_ref, vc_ref, sink_ref, o_ref = refs
    G = SWA_HEADS // SWA_KV_HEADS
    blk = q_ref.shape[0]
    Dh = SWA_HEAD_DIM
    q = jnp.concatenate([q_ref[:, g * Dh:(g + 1) * Dh] for g in range(G)], axis=0)
    sink = sink_ref[0][:, 0:1]
    scores = [_dot_nt(q, kc_ref[...])]
    values = [vc_ref[...]]
    if windowed:
        n = pl.program_id(2)
        qi = lax.broadcasted_iota(jnp.int32, (G * blk, blk), 0) % blk
        kj = lax.broadcasted_iota(jnp.int32, (G * blk, blk), 1)
        s_p = _dot_nt(q, kp_ref[...])
        s_p = jnp.where(kj >= qi, s_p, NEG_BIG)
        s_p = jnp.where(n >= 1, s_p, NEG_BIG)
        s_n = _dot_nt(q, kn_ref[...])
        s_n = jnp.where(kj <= qi, s_n, NEG_BIG)
        s_n = jnp.where(n <= nb - 2, s_n, NEG_BIG)
        scores += [s_p, _dot_nt(q, kk_ref[...]), s_n]
        values += [vp_ref[...], vk_ref[...], vn_ref[...]]
    m = sink
    for s in scores:
        m = jnp.maximum(m, jnp.max(s, axis=-1, keepdims=True))
    l = jnp.exp(sink - m)
    o = None
    for s, v in zip(scores, values):
        p = jnp.exp(s - m)
        l = l + jnp.sum(p, axis=-1, keepdims=True)
        pv = _dot(p.astype(BF16), v)
        o = pv if o is None else o + pv
    o = o / l
    for g in range(G):
        o_ref[:, g * Dh:(g + 1) * Dh] = o[g * blk:(g + 1) * blk].astype(o_ref.dtype)


def swa_attention(dm, qa, ka, va, sink, *, latent):
    G = SWA_HEADS // SWA_KV_HEADS
    Dh = SWA_HEAD_DIM
    blk = SWA_WINDOW
    L = dm.L
    sink_col = jnp.broadcast_to(sink.astype(F32).reshape(SWA_KV_HEADS, G, 1, 1),
                                (SWA_KV_HEADS, G, blk, LANE)).reshape(SWA_KV_HEADS, G * blk, LANE)
    sink_spec = pl.BlockSpec((1, G * blk, LANE), lambda b, h, n: (h, 0, 0))
    ctx_spec = pl.BlockSpec((L, Dh), lambda b, h, n: (b, h))
    if latent:
        nb = dm.S // blk
        base = dm.RC // blk

        def q_map(b, h, n):
            return (base + b * nb + n, h)

        def kv_map(off):
            return lambda b, h, n: (base + b * nb + jnp.clip(n + off, 0, nb - 1), h)

        win_specs = [pl.BlockSpec((blk, Dh), kv_map(off)) for off in (-1, 0, 1)]
        in_specs = ([pl.BlockSpec((blk, G * Dh), q_map), ctx_spec] + win_specs + [ctx_spec] + win_specs
                    + [sink_spec])
        operands = (qa, ka, ka, ka, ka, va, va, va, va, sink_col)
        n_out = dm.RL
    else:
        nb = L // blk
        in_specs = [pl.BlockSpec((blk, G * Dh), lambda b, h, n: (b * nb + n, h)), ctx_spec, ctx_spec, sink_spec]
        operands = (qa, ka, va, sink_col)
        n_out = dm.RC
    return pl.pallas_call(
        functools.partial(_swa_attn_kernel, windowed=latent, nb=nb),
        grid=(dm.B, SWA_KV_HEADS, nb),
        in_specs=in_specs,
        out_specs=pl.BlockSpec((blk, G * Dh), lambda b, h, n: (b * nb + n, h)),
        out_shape=jax.ShapeDtypeStruct((n_out, SWA_HEADS * Dh), BF16),
        compiler_params=_params(3),
        name="swa_attn_lat" if latent else "swa_attn_ctx",
    )(*operands)


def _rms_rows_kernel(a_ref, b_ref, ga_ref, gb_ref, ao_ref, bo_ref):
    for x_ref, g_ref, o_ref in ((a_ref, ga_ref, ao_ref), (b_ref, gb_ref, bo_ref)):
        x = x_ref[...]
        ms = jnp.mean(x * x, axis=-1, keepdims=True)
        o_ref[...] = (x * lax.rsqrt(ms + EPS) * g_ref[...]).astype(o_ref.dtype)


def mla_lowrank_norm(dm, z, col, q_rank, kv_rank, gq, gkv):
    tm = min(dm.tm, 512)
    return pl.pallas_call(
        _rms_rows_kernel,
        grid=(dm.R // tm,),
        in_specs=[
            pl.BlockSpec((tm, q_rank), lambda i: (i, col["c_q"] // q_rank)),
            pl.BlockSpec((tm, kv_rank), lambda i: (i, col["c_kv"] // kv_rank)),
            pl.BlockSpec((1, q_rank), lambda i: (0, 0)),
            pl.BlockSpec((1, kv_rank), lambda i: (0, 0)),
        ],
        out_specs=[pl.BlockSpec((tm, q_rank), lambda i: (i, 0)), pl.BlockSpec((tm, kv_rank), lambda i: (i, 0))],
        out_shape=[jax.ShapeDtypeStruct((dm.R, q_rank), BF16), jax.ShapeDtypeStruct((dm.R, kv_rank), BF16)],
        compiler_params=_params(1),
        name="mla_lowrank_norm",
    )(z, z, gq.reshape(1, -1), gkv.reshape(1, -1))


def _mla_prep_kernel(q_ref, kn_ref, v_ref, pe_ref, gq_ref, gk_ref, c_ref, s1_ref, s2_ref, qo_ref, ko_ref, vo_ref):
    c = c_ref[...]
    s1 = s1_ref[...]
    s2 = s2_ref[...]
    gq = gq_ref[...]
    gk = gk_ref[...]
    scale = MLA_QK ** -0.5

    def rope(x):
        return x * c + pltpu.roll(x, LANE - MLA_ROPE // 2, 1) * s1 + pltpu.roll(x, MLA_ROPE // 2, 1) * s2

    pe = pe_ref[...]
    pe_ss = jnp.sum(pe * pe, axis=-1, keepdims=True)
    for h in range(MLA_HEADS):
        lo = h * MLA_QK_PAD
        qh = q_ref[:, lo:lo + MLA_QK_PAD]
        inv = lax.rsqrt(jnp.sum(qh * qh, axis=-1, keepdims=True) * (1.0 / MLA_QK) + EPS)
        qn = qh * inv * gq
        qo_ref[:, lo:lo + MLA_NOPE] = (qn[:, :MLA_NOPE] * scale).astype(BF16)
        qo_ref[:, lo + MLA_NOPE:lo + MLA_QK_PAD] = (rope(qn[:, MLA_NOPE:]) * scale).astype(BF16)
        kh = kn_ref[:, h * MLA_NOPE:(h + 1) * MLA_NOPE]
        inv = lax.rsqrt((jnp.sum(kh * kh, axis=-1, keepdims=True) + pe_ss) * (1.0 / MLA_QK) + EPS)
        ko_ref[:, lo:lo + MLA_NOPE] = (kh * inv * gk[:, :MLA_NOPE]).astype(BF16)
        ko_ref[:, lo + MLA_NOPE:lo + MLA_QK_PAD] = rope(pe * inv * gk[:, MLA_NOPE:]).astype(BF16)
    vo_ref[...] = v_ref[...].astype(BF16)


def mla_prep(dm, qf, kvf, z, col, gq_pad, gk_pad, tabs, *, row0, n_rows):
    tm = min(dm.tm, 256)
    t0 = row0 // tm
    QW = MLA_HEADS * MLA_QK_PAD
    NW = MLA_HEADS * MLA_NOPE
    ridx = _rope_tile_index(dm, tm)
    rspec = pl.BlockSpec((tm, LANE), lambda i: ridx(i + t0))
    return pl.pallas_call(
        _mla_prep_kernel,
        grid=(n_rows // tm,),
        in_specs=[
            pl.BlockSpec((tm, QW), lambda i: (i + t0, 0)),
            pl.BlockSpec((tm, NW), lambda i: (i + t0, 0)),
            pl.BlockSpec((tm, NW), lambda i: (i + t0, 1)),
            pl.BlockSpec((tm, LANE), lambda i: (i + t0, col["kpe"] // LANE)),
            pl.BlockSpec((1, MLA_QK_PAD), lambda i: (0, 0)),
            pl.BlockSpec((1, MLA_QK_PAD), lambda i: (0, 0)),
            rspec, rspec, rspec,
        ],
        out_specs=[
            pl.BlockSpec((tm, QW), lambda i: (i, 0)),
            pl.BlockSpec((tm, QW), lambda i: (i, 0)),
            pl.BlockSpec((tm, NW), lambda i: (i, 0)),
        ],
        out_shape=[
            jax.ShapeDtypeStruct((n_rows, QW), BF16),
            jax.ShapeDtypeStruct((n_rows, QW), BF16),
            jax.ShapeDtypeStruct((n_rows, NW), BF16),
        ],
        compiler_params=_params(1),
        name="mla_prep",
    )(qf, kvf, kvf, z, gq_pad, gk_pad, *tabs)


def _mla_attn_kernel(*refs, with_latent):
    if with_latent:
        q_ref, kc_ref, kl_ref, vc_ref, vl_ref, o_ref = refs
    else:
        q_ref, kc_ref, vc_ref, o_ref = refs
    q = q_ref[...]
    s_c = _dot_nt(q, kc_ref[...])
    m = jnp.max(s_c, axis=-1, keepdims=True)
    if with_latent:
        s_l = _dot_nt(q, kl_ref[...])
        m = jnp.maximum(m, jnp.max(s_l, axis=-1, keepdims=True))
    p_c = jnp.exp(s_c - m)
    l = jnp.sum(p_c, axis=-1, keepdims=True)
    o = _dot(p_c.astype(BF16), vc_ref[...])
    if with_latent:
        p_l = jnp.exp(s_l - m)
        l = l + jnp.sum(p_l, axis=-1, keepdims=True)
        o = o + _dot(p_l.astype(BF16), vl_ref[...])
    o_ref[...] = (o / l).astype(o_ref.dtype)


def mla_attention(dm, q, kc, vc, kl=None, vl=None):
    with_latent = kl is not None
    n_q = dm.S if with_latent else dm.L
    tq = min(256, n_q)
    nq = n_q // tq
    QP, V = MLA_QK_PAD, MLA_V
    in_specs = [pl.BlockSpec((tq, QP), lambda b, h, n: (b * nq + n, h)),
                pl.BlockSpec((dm.L, QP), lambda b, h, n: (b, h))]
    operands = [q, kc]
    if with_latent:
        in_specs.append(pl.BlockSpec((dm.S, QP), lambda b, h, n: (b, h)))
        operands.append(kl)
    in_specs.append(pl.BlockSpec((dm.L, V), lambda b, h, n: (b, h)))
    operands.append(vc)
    if with_latent:
        in_specs.append(pl.BlockSpec((dm.S, V), lambda b, h, n: (b, h)))
        operands.append(vl)
    return pl.pallas_call(
        functools.partial(_mla_attn_kernel, with_latent=with_latent),
        grid=(dm.B, MLA_HEADS, nq),
        in_specs=in_specs,
        out_specs=pl.BlockSpec((tq, V), lambda b, h, n: (b * nq + n, h)),
        out_shape=jax.ShapeDtypeStruct((dm.B * n_q, MLA_HEADS * V), BF16),
        compiler_params=_params(3),
        name="mla_attn_lat" if with_latent else "mla_attn_ctx",
    )(*operands)


def _ssm_in_kernel(u_ref, w_ref, y_ref, sf_ref, sb_ref):
    z = _dot(u_ref[...], w_ref[...])
    W = SSM_CHUNK * SSM_GROUP
    P2 = 2 * SSM_STATE
    y_ref[...] = z[:, :W]
    sf_ref[...] = z[:, W:W + P2]
    sb_ref[...] = z[:, W + P2:]


def ssm_chunk_in(u_chunks, w_cat):
    G, NR, W = u_chunks.shape
    P2 = 2 * SSM_STATE
    return pl.pallas_call(
        _ssm_in_kernel,
        grid=(G,),
        in_specs=[pl.BlockSpec((None, NR, W), lambda g: (g, 0, 0)),
                  pl.BlockSpec((None, W, W + 2 * P2), lambda g: (g, 0, 0))],
        out_specs=[pl.BlockSpec((None, NR, W), lambda g: (g, 0, 0)),
                   pl.BlockSpec((None, NR, P2), lambda g: (g, 0, 0)),
                   pl.BlockSpec((None, NR, P2), lambda g: (g, 0, 0))],
        out_shape=[jax.ShapeDtypeStruct((G, NR, W), F32),
                   jax.ShapeDtypeStruct((G, NR, P2), F32),
                   jax.ShapeDtypeStruct((G, NR, P2), F32)],
        compiler_params=_params(1),
        name="ssm_chunk_in",
    )(u_chunks, w_cat)


def _ssm_scan_kernel(sc_ref, sl_ref, a1_ref, a2_ref, xc_ref, xl_ref):
    d = pl.program_id(0)
    a1 = a1_ref[...]
    a2 = a2_ref[...]

    def run(s_ref, x_ref, x0):
        n_steps = s_ref.shape[0]

        def body(n, x):
            idx = jnp.where(d == 0, n, n_steps - 1 - n)
            x_ref[idx] = x.astype(x_ref.dtype)
            return a1 * x + a2 * pltpu.roll(x, SSM_STATE, 1) + s_ref[idx]

        return lax.fori_loop(0, n_steps, body, x0)

    x = run(sc_ref, xc_ref, jnp.zeros(a1.shape, F32))
    run(sl_ref, xl_ref, x)


def ssm_scan(s_ctx, s_lat, a1, a2):
    _, NC, GB, P2 = s_ctx.shape
    NL = s_lat.shape[1]
    rb = _pick_tile((64, 32, 16), GB)
    return pl.pallas_call(
        _ssm_scan_kernel,
        grid=(2, GB // rb),
        in_specs=[pl.BlockSpec((None, NC, rb, P2), lambda d, r: (d, 0, r, 0)),
                  pl.BlockSpec((None, NL, rb, P2), lambda d, r: (d, 0, r, 0)),
                  pl.BlockSpec((None, rb, P2), lambda d, r: (d, r, 0)),
                  pl.BlockSpec((None, rb, P2), lambda d, r: (d, r, 0))],
        out_specs=[pl.BlockSpec((None, NC, rb, P2), lambda d, r: (d, 0, r, 0)),
                   pl.BlockSpec((None, NL, rb, P2), lambda d, r: (d, 0, r, 0))],
        out_shape=[jax.ShapeDtypeStruct(s_ctx.shape, BF16), jax.ShapeDtypeStruct(s_lat.shape, BF16)],
        compiler_params=_params(2),
        name="ssm_scan",
    )(s_ctx, s_lat, a1, a2)


def _ssm_out_kernel(y_ref, xf_ref, xb_ref, mf_ref, mb_ref, u_ref, d_ref, o_ref):
    y = y_ref[...] + _dot(xf_ref[...], mf_ref[...]) + _dot(xb_ref[...], mb_ref[...]) + d_ref[...] * u_ref[...]
    o_ref[...] = _gelu_tanh(y).astype(o_ref.dtype)


def ssm_chunk_out(y_intra, xf, xb, m_out, u_chunks_f32, d_tiled):
    G, NR, W = y_intra.shape
    P2 = 2 * SSM_STATE
    return pl.pallas_call(
        _ssm_out_kernel,
        grid=(G,),
        in_specs=[pl.BlockSpec((None, NR, W), lambda g: (g, 0, 0)),
                  pl.BlockSpec((None, NR, P2), lambda g: (g, 0, 0)),
                  pl.BlockSpec((None, NR, P2), lambda g: (g, 0, 0)),
                  pl.BlockSpec((None, None, P2, W), lambda g: (0, g, 0, 0)),
                  pl.BlockSpec((None, None, P2, W), lambda g: (1, g, 0, 0)),
                  pl.BlockSpec((None, NR, W), lambda g: (g, 0, 0)),
                  pl.BlockSpec((None, 1, W), lambda g: (g, 0, 0))],
        out_specs=pl.BlockSpec((None, NR, W), lambda g: (g, 0, 0)),
        out_shape=jax.ShapeDtypeStruct((G, NR, W), BF16),
        compiler_params=_params(1),
        name="ssm_chunk_out",
    )(y_intra, xf, xb, m_out, m_out, u_chunks_f32, d_tiled)


def _ssm_tables(lam_re, lam_im, log_step, b_re, b_im, c_re, c_im):
    C, H, P = SSM_CHUNK, SSM_GROUP, SSM_STATE
    G = lam_re.shape[1]
    delta = jnp.exp(log_step.astype(F32))[..., None]
    zr = lam_re.astype(F32) * delta
    zi = lam_im.astype(F32) * delta
    k = jnp.arange(C + 1, dtype=F32)[:, None, None, None]
    mag = jnp.exp(k * zr[None])
    pw_re = mag * jnp.cos(k * zi[None])
    pw_im = mag * jnp.sin(k * zi[None])
    lb_re, lb_im = pw_re[1], pw_im[1]
    lr, li = lam_re.astype(F32), lam_im.astype(F32)
    den = lr * lr + li * li
    f_re = ((lb_re - 1.0) * lr + lb_im * li) / den
    f_im = (lb_im * lr - (lb_re - 1.0) * li) / den
    br, bi = b_re.astype(F32), b_im.astype(F32)
    bb_re = f_re[..., None] * br - f_im[..., None] * bi
    bb_im = f_re[..., None] * bi + f_im[..., None] * br
    cr, ci = c_re.astype(F32), c_im.astype(F32)
    cl_re = cr[None] * pw_re[:, :, :, None, :] - ci[None] * pw_im[:, :, :, None, :]
    cl_im = cr[None] * pw_im[:, :, :, None, :] + ci[None] * pw_re[:, :, :, None, :]
    hp = lax.Precision.HIGHEST
    kern = (jnp.einsum("kdghp,dgpj->dgkhj", cl_re[:C], bb_re, precision=hp)
            - jnp.einsum("kdghp,dgpj->dgkhj", cl_im[:C], bb_im, precision=hp))
    s_idx = jnp.arange(C)[:, None]
    t_idx = jnp.arange(C)[None, :]

    def toeplitz(kd, lag, valid):
        m = kd[:, jnp.clip(lag, 0, C - 1)]
        m = jnp.where(valid[None, :, :, None, None], m, 0.0)
        return m.transpose(0, 1, 4, 2, 3).reshape(G, C * H, C * H)

    m_intra = toeplitz(kern[0], t_idx - s_idx, t_idx >= s_idx) + toeplitz(kern[1], s_idx - t_idx, s_idx >= t_idx)

    def state_in(d, power_of_s):
        pr = pw_re[power_of_s, d]
        pi = pw_im[power_of_s, d]
        re = pr[..., None] * bb_re[d][None] - pi[..., None] * bb_im[d][None]
        im = pr[..., None] * bb_im[d][None] + pi[..., None] * bb_re[d][None]
        m = jnp.concatenate([re, im], axis=2)
        return m.transpose(1, 0, 3, 2).reshape(G, C * H, 2 * P)

    m_sum_f = state_in(0, C - 1 - jnp.arange(C))
    m_sum_b = state_in(1, jnp.arange(C))
    w_cat = jnp.concatenate([m_intra, m_sum_f, m_sum_b], axis=-1).astype(BF16)

    def state_out(d, power_of_t):
        re = cl_re[power_of_t, d]
        im = cl_im[power_of_t, d]
        m = jnp.concatenate([re, -im], axis=-1)
        return m.transpose(1, 3, 0, 2).reshape(G, 2 * P, C * H)

    m_out = jnp.stack([state_out(0, 1 + jnp.arange(C)), state_out(1, C - jnp.arange(C))]).astype(BF16)
    a1 = jnp.concatenate([pw_re[C], pw_re[C]], axis=-1)
    a2 = jnp.concatenate([-pw_im[C], pw_im[C]], axis=-1)
    return w_cat, m_out, a1, a2


def s5_branch(dm, z, col, width, tables, d_skip):
    w_cat, m_out, a1, a2 = tables
    C, H = SSM_CHUNK, SSM_GROUP
    G = width // H
    B = dm.B
    NR = dm.R // C
    ncc, ncl = dm.L // C, dm.S // C
    u = lax.slice_in_dim(z, col["u"], col["u"] + width, axis=1)
    u_chunks = u.reshape(NR, C, G, H).transpose(2, 0, 1, 3).reshape(G, NR, C * H)
    y_intra, s_f, s_b = ssm_chunk_in(u_chunks.astype(BF16), w_cat)
    P2 = 2 * SSM_STATE

    def to_scan(s):
        sc = s[:, :B * ncc].reshape(G, B, ncc, P2).transpose(2, 0, 1, 3).reshape(ncc, G * B, P2)
        sl = s[:, B * ncc:].reshape(G, B, ncl, P2).transpose(2, 0, 1, 3).reshape(ncl, G * B, P2)
        return sc, sl

    scf, slf = to_scan(s_f)
    scb, slb = to_scan(s_b)
    a1r = jnp.repeat(a1, B, axis=1)
    a2r = jnp.repeat(a2, B, axis=1)
    xc, xl = ssm_scan(jnp.stack([scf, scb]), jnp.stack([slf, slb]), a1r, a2r)

    def from_scan(xc_d, xl_d):
        c = xc_d.reshape(ncc, G, B, P2).transpose(1, 2, 0, 3).reshape(G, B * ncc, P2)
        l = xl_d.reshape(ncl, G, B, P2).transpose(1, 2, 0, 3).reshape(G, B * ncl, P2)
        return jnp.concatenate([c, l], axis=1)

    xf = from_scan(xc[0], xl[0])
    xb = from_scan(xc[1], xl[1])
    d_tiled = jnp.tile(d_skip.astype(F32).reshape(G, 1, H), (1, C, 1)).reshape(G, 1, C * H)
    y = ssm_chunk_out(y_intra, xf, xb, m_out, u_chunks, d_tiled)
    return y.reshape(G, NR, C, H).transpose(1, 2, 0, 3).reshape(dm.R, width)


def _merge_kernel(ya_ref, yb_ref, yc_ref, ga_ref, gb_ref, gc_ref, wa_ref, wb_ref, wc_ref, o_ref, sa, sb, sc):
    @pl.when(pl.program_id(1) == 0)
    def _():
        for w_ref, s in ((wa_ref, sa), (wb_ref, sb), (wc_ref, sc)):
            s[...] = w_ref[...].astype(BF16)

    acc = ga_ref[...].astype(F32) * _dot(ya_ref[...], sa[...])
    acc = acc + gb_ref[...].astype(F32) * _dot(yb_ref[...], sb[...])
    acc = acc + gc_ref[...].astype(F32) * _dot(yc_ref[...], sc[...])
    o_ref[...] = acc.astype(o_ref.dtype)


def merge_branches(dm, ys, y_row0s, gates, w_branch, layer, *, row0, n_rows):
    BW = ys[0].shape[1]
    D = w_branch.shape[-1]
    tm = dm.tm
    tn = 512
    t0 = row0 // tm
    nj = D // tn
    y_specs = [pl.BlockSpec((tm, BW), functools.partial(lambda j, i, o: (i + o, 0), o=(row0 - y0) // tm))
               for y0 in y_row0s]
    g_specs = [pl.BlockSpec((tm, tn), functools.partial(lambda j, i, n: (i + t0, n * nj + j), n=n))
               for n in range(N_BRANCH)]
    w_specs = [pl.BlockSpec((None, None, BW, tn), functools.partial(lambda j, i, n: (layer, n, 0, j), n=n))
               for n in range(N_BRANCH)]
    return pl.pallas_call(
        _merge_kernel,
        grid=(nj, n_rows // tm),
        in_specs=y_specs + g_specs + w_specs,
        out_specs=pl.BlockSpec((tm, tn), lambda j, i: (i, j)),
        out_shape=jax.ShapeDtypeStruct((n_rows, D), BF16),
        scratch_shapes=[pltpu.VMEM((BW, tn), BF16)] * 3,
        compiler_params=_params(2),
        name="merge_branches",
    )(*ys, gates, gates, gates, w_branch, w_branch, w_branch)


def _router_kernel(h_ref, whi_ref, wlo_ref, b_ref, idx_ref, w_ref):
    h = h_ref[...]
    logits = _dot(h, whi_ref[...]) + _dot(h, wlo_ref[...]) + b_ref[...]
    lane = lax.broadcasted_iota(jnp.int32, logits.shape, 1).astype(F32)
    logits = jnp.where(lane < N_EXPERTS, logits, NEG_BIG)
    m1 = jnp.max(logits, axis=-1, keepdims=True)
    i1 = jnp.min(jnp.where(logits == m1, lane, float(LANE)), axis=-1, keepdims=True)
    rest = jnp.where(lane == i1, NEG_BIG, logits)
    m2 = jnp.max(rest, axis=-1, keepdims=True)
    i2 = jnp.min(jnp.where(rest == m2, lane, float(LANE)), axis=-1, keepdims=True)
    e = jnp.exp(m2 - m1)
    w1 = 1.0 / (1.0 + e)
    w2 = e / (1.0 + e)
    idx_ref[...] = jnp.where(lane == 0.0, i1, jnp.where(lane == 1.0, i2, 0.0)).astype(jnp.int32)
    w_ref[...] = jnp.where(lane == 0.0, w1, jnp.where(lane == 1.0, w2, 0.0))


def moe_router(h, w_router, b_router):
    M, D = h.shape
    tm = _pick_tile((1024, 512, 256, 128), M)
    w_pad = jnp.zeros((D, LANE), F32).at[:, :N_EXPERTS].set(w_router.astype(F32))
    w_hi = w_pad.astype(BF16)
    w_lo = (w_pad - w_hi.astype(F32)).astype(BF16)
    b_pad = jnp.zeros((1, LANE), F32).at[0, :N_EXPERTS].set(b_router.astype(F32))
    idx, wts = pl.pallas_call(
        _router_kernel,
        grid=(M // tm,),
        in_specs=[pl.BlockSpec((tm, D), lambda i: (i, 0)),
                  pl.BlockSpec((D, LANE), lambda i: (0, 0)),
                  pl.BlockSpec((D, LANE), lambda i: (0, 0)),
                  pl.BlockSpec((1, LANE), lambda i: (0, 0))],
        out_specs=[pl.BlockSpec((tm, LANE), lambda i: (i, 0)), pl.BlockSpec((tm, LANE), lambda i: (i, 0))],
        out_shape=[jax.ShapeDtypeStruct((M, LANE), jnp.int32), jax.ShapeDtypeStruct((M, LANE), F32)],
        compiler_params=_params(1),
        name="moe_router",
    )(h, w_hi, w_lo, b_pad)
    return idx[:, :TOP_K], wts[:, :TOP_K]


def _gather_rows_kernel(idx_ref, src_ref, o_ref, sem):
    tg = o_ref.shape[0]
    base = pl.program_id(0) * tg

    def row_copy(r):
        return pltpu.make_async_copy(src_ref.at[idx_ref[base + r]], o_ref.at[r], sem)

    def start(r, carry):
        row_copy(r).start()
        return carry

    lax.fori_loop(0, tg, start, 0)

    def wait(r, carry):
        row_copy(r).wait()
        return carry

    lax.fori_loop(0, tg, wait, 0)


def gather_rows(src, idx, *, tg=256):
    M = idx.shape[0]
    _, S_, Ln = src.shape
    return pl.pallas_call(
        _gather_rows_kernel,
        grid_spec=pltpu.PrefetchScalarGridSpec(
            num_scalar_prefetch=1,
            grid=(M // tg,),
            in_specs=[pl.BlockSpec(memory_space=pl.ANY)],
            out_specs=pl.BlockSpec((tg, S_, Ln), lambda i, idx_ref: (i, 0, 0)),
            scratch_shapes=[pltpu.SemaphoreType.DMA(())],
        ),
        out_shape=jax.ShapeDtypeStruct((M, S_, Ln), src.dtype),
        compiler_params=_params(1),
        name="gather_rows",
    )(idx, src)


def _moe_w13_kernel(te_ref, tv_ref, a_ref, wg_ref, wu_ref, o_ref, sg, su):
    i = pl.program_id(1)
    prev = te_ref[jnp.maximum(i - 1, 0)]

    @pl.when(jnp.logical_or(i == 0, te_ref[i] != prev))
    def _():
        sg[...] = wg_ref[...].astype(BF16)
        su[...] = wu_ref[...].astype(BF16)

    @pl.when(tv_ref[i] == 1)
    def _():
        a = a_ref[...]
        o_ref[...] = (_silu(_dot(a, sg[...])) * _dot(a, su[...])).astype(o_ref.dtype)

    @pl.when(tv_ref[i] == 0)
    def _():
        o_ref[...] = jnp.zeros_like(o_ref)


def moe_w13(xs, w13, moe_idx, tile_expert, tile_valid, *, tn):
    P, D = xs.shape
    F = w13.shape[-1] // 2
    tm = MOE_TM
    nj = F // tn
    return pl.pallas_call(
        _moe_w13_kernel,
        grid_spec=pltpu.PrefetchScalarGridSpec(
            num_scalar_prefetch=2,
            grid=(nj, P // tm),
            in_specs=[pl.BlockSpec((tm, D), lambda j, i, te, tv: (i, 0)),
                      pl.BlockSpec((None, None, D, tn), lambda j, i, te, tv: (moe_idx, te[i], 0, j)),
                      pl.BlockSpec((None, None, D, tn), lambda j, i, te, tv: (moe_idx, te[i], 0, j + nj))],
            out_specs=pl.BlockSpec((tm, tn), lambda j, i, te, tv: (i, j)),
            scratch_shapes=[pltpu.VMEM((D, tn), BF16), pltpu.VMEM((D, tn), BF16)],
        ),
        out_shape=jax.ShapeDtypeStruct((P, F), BF16),
        compiler_params=_params(2),
        name="moe_w13",
    )(tile_expert, tile_valid, xs, w13, w13)


def _moe_w2_kernel(te_ref, tv_ref, a_ref, w_ref, rw_ref, o_ref, acc_ref):
    i = pl.program_id(0)
    k = pl.program_id(1)

    @pl.when(tv_ref[i] == 1)
    def _():
        @pl.when(k == 0)
        def _():
            acc_ref[...] = jnp.zeros_like(acc_ref)

        acc_ref[...] += _dot(a_ref[...], w_ref[...])

        @pl.when(k == pl.num_programs(1) - 1)
        def _():
            o_ref[...] = acc_ref[...] * rw_ref[...]

    @pl.when(jnp.logical_and(tv_ref[i] == 0, k == 0))
    def _():
        o_ref[...] = jnp.zeros_like(o_ref)


def moe_w2(act, w2, row_w, tile_expert, tile_valid, *, tk):
    P, F = act.shape
    D = w2.shape[-1]
    tm = MOE_TM
    return pl.pallas_call(
        _moe_w2_kernel,
        grid_spec=pltpu.PrefetchScalarGridSpec(
            num_scalar_prefetch=2,
            grid=(P // tm, F // tk),
            in_specs=[pl.BlockSpec((tm, tk), lambda i, k, te, tv: (i, k * tv[i])),
                      pl.BlockSpec((None, tk, D), lambda i, k, te, tv: (te[i], k * tv[i], 0)),
                      pl.BlockSpec((tm, 1), lambda i, k, te, tv: (i, 0))],
            out_specs=pl.BlockSpec((tm, D), lambda i, k, te, tv: (i, 0)),
            scratch_shapes=[pltpu.VMEM((tm, D), F32)],
        ),
        out_shape=jax.ShapeDtypeStruct((P, D), F32),
        compiler_params=_params(2),
        name="moe_w2",
    )(tile_expert, tile_valid, act, w2, row_w)


def _moe_combine_kernel(p0_ref, p1_ref, y_ref, x_ref, gate_ref, o_ref, b0, b1, sem):
    tc = o_ref.shape[0]
    base = pl.program_id(0) * tc

    def copies(r):
        return (pltpu.make_async_copy(y_ref.at[pl.ds(p0_ref[base + r], 1)], b0.at[pl.ds(r, 1)], sem.at[0]),
                pltpu.make_async_copy(y_ref.at[pl.ds(p1_ref[base + r], 1)], b1.at[pl.ds(r, 1)], sem.at[1]))

    def start(r, carry):
        for cp in copies(r):
            cp.start()
        return carry

    lax.fori_loop(0, tc, start, 0)

    def wait(r, carry):
        for cp in copies(r):
            cp.wait()
        return carry

    lax.fori_loop(0, tc, wait, 0)
    o_ref[...] = x_ref[...] + gate_ref[0] * (b0[...] + b1[...])


def moe_combine(dm, y_sorted, pos0, pos1, x, gate, *, mod_row0):
    M, D = x.shape
    tc = 128
    t0 = mod_row0 // tc
    return pl.pallas_call(
        _moe_combine_kernel,
        grid_spec=pltpu.PrefetchScalarGridSpec(
            num_scalar_prefetch=2,
            grid=(M // tc,),
            in_specs=[pl.BlockSpec(memory_space=pl.ANY),
                      pl.BlockSpec((tc, D), lambda i, p0, p1: (i, 0)),
                      pl.BlockSpec((1, 1, D), lambda i, p0, p1: (dm.mod_row(i + t0, tc), 0, 0))],
            out_specs=pl.BlockSpec((tc, D), lambda i, p0, p1: (i, 0)),
            scratch_shapes=[pltpu.VMEM((tc, D), F32), pltpu.VMEM((tc, D), F32), pltpu.SemaphoreType.DMA((2,))],
        ),
        out_shape=jax.ShapeDtypeStruct((M, D), F32),
        compiler_params=_params(1),
        name="moe_combine",
    )(pos0, pos1, y_sorted, x, gate)


def moe_ffn(dm, h, x, gate, w_router, b_router, w13, w2, moe_idx, *, mod_row0):
    M, D = h.shape
    E = N_EXPERTS
    tm = MOE_TM
    top_idx, top_w = moe_router(h, w_router, b_router)
    e_flat = top_idx.T.reshape(-1)
    onehot = (e_flat[:, None] == jnp.arange(E, dtype=jnp.int32)[None, :]).astype(jnp.int32)
    csum = jnp.cumsum(onehot, axis=0)
    counts = csum[-1]
    rank = jnp.sum((csum - onehot) * onehot, axis=1)
    padded = ((counts + tm - 1) // tm) * tm
    ends = jnp.cumsum(padded)
    starts = ends - padded
    pos = starts[e_flat] + rank
    P = TOP_K * M + E * tm
    n_tiles = P // tm
    tok = jnp.tile(jnp.arange(M, dtype=jnp.int32), TOP_K)
    gidx = jnp.zeros((P,), jnp.int32).at[pos].set(tok)
    row_w = jnp.zeros((P,), F32).at[pos].set(top_w.T.reshape(-1)).reshape(P, 1)
    tile_start = jnp.arange(n_tiles, dtype=jnp.int32) * tm
    tile_valid = (tile_start < ends[-1]).astype(jnp.int32)
    te = jnp.sum((tile_start[:, None] >= ends[None, :]).astype(jnp.int32), axis=1)
    last_e = jnp.sum((ends[-1] - 1 >= ends).astype(jnp.int32))
    tile_expert = jnp.minimum(te, last_e).astype(jnp.int32)

    S_ = D // LANE
    xs = gather_rows(h.reshape(M, S_, LANE), gidx).reshape(P, D)
    F = w13.shape[-1] // 2
    act = moe_w13(xs, w13, moe_idx, tile_expert, tile_valid, tn=_pick_tile((1024, 512, 256, 128), F))
    y_sorted = moe_w2(act, w2[moe_idx].astype(BF16), row_w, tile_expert, tile_valid,
                      tk=_pick_tile((512, 256, 128), F))
    return moe_combine(dm, y_sorted, pos[:M], pos[M:], x, gate, mod_row0=mod_row0)


def _axial_angles(seq, rot_dim):
    rows = seq // GRID_W
    t_row = jnp.repeat(jnp.arange(rows, dtype=F32), GRID_W)
    t_col = jnp.tile(jnp.arange(GRID_W, dtype=F32), rows)
    quarter = rot_dim // 4
    inv_freq = ROPE_THETA ** (-jnp.arange(quarter, dtype=F32) / quarter)
    return jnp.concatenate([t_row[:, None] * inv_freq, t_col[:, None] * inv_freq], axis=-1)


def _rope_tables(dm):
    ident = min(dm.tm, 256)
    ang = _axial_angles(dm.S, SWA_HEAD_DIM)
    cos_a = jnp.concatenate([jnp.cos(ang), jnp.cos(ang)], axis=-1)
    sin_a = jnp.concatenate([-jnp.sin(ang), jnp.sin(ang)], axis=-1)
    cos_a = jnp.concatenate([jnp.ones((ident, LANE), F32), cos_a], axis=0)
    sin_a = jnp.concatenate([jnp.zeros((ident, LANE), F32), sin_a], axis=0)
    ang = _axial_angles(dm.S, MLA_ROPE)
    half = MLA_ROPE // 2
    zeros = jnp.zeros((dm.S, half), F32)
    pad = jnp.zeros((dm.S, LANE - MLA_ROPE), F32)
    c_b = jnp.concatenate([jnp.cos(ang), jnp.cos(ang), pad], axis=-1)
    s1_b = jnp.concatenate([-jnp.sin(ang), zeros, pad], axis=-1)
    s2_b = jnp.concatenate([zeros, jnp.sin(ang), pad], axis=-1)
    c_b = jnp.concatenate([jnp.ones((ident, LANE), F32), c_b], axis=0)
    s1_b = jnp.concatenate([jnp.zeros((ident, LANE), F32), s1_b], axis=0)
    s2_b = jnp.concatenate([jnp.zeros((ident, LANE), F32), s2_b], axis=0)
    return (cos_a, sin_a), (c_b, s1_b, s2_b)


def _pad_head_vec(g):
    return jnp.zeros((1, MLA_QK_PAD), F32).at[0, :MLA_QK].set(g.astype(F32))


def _trunk(x, c, ctx, c_ctx, mod_w, mod_b, norm_mix_g, norm_ffn_g, w_in,
           swa_q_norm_g, swa_k_norm_g, swa_sink,
           mla_q_a_norm_g, mla_w_uq, mla_kv_a_norm_g, mla_w_ukv, mla_q_norm_g, mla_k_norm_g,
           ssm_lam_re, ssm_lam_im, ssm_log_step, ssm_b_re, ssm_b_im, ssm_c_re, ssm_c_im,
           ssm_d, ssm_w_glu, ssm_b_glu, w_branch, w_out,
           ffn_w13, ffn_w2, moe_w_router, moe_b_router, moe_w13, moe_w2):
    B, S, D = x.shape
    L = ctx.shape[1]
    depth = mod_w.shape[0]
    dm = Dims(B, S, L)
    tm = dm.tm
    RC, RL, R = dm.RC, dm.RL, dm.R
    q_w = SWA_HEADS * SWA_HEAD_DIM
    kv_w = SWA_KV_HEADS * SWA_HEAD_DIM
    q_rank = mla_w_uq.shape[1]
    kv_rank = mla_w_ukv.shape[1]
    ssm_w = ssm_d.shape[1]
    n_gate = N_BRANCH * D
    src = {}
    off = 0
    for name, width in (("q", q_w), ("k", kv_w), ("v", kv_w), ("c_q", q_rank), ("c_kv", kv_rank),
                        ("kpe", MLA_ROPE), ("u", ssm_w), ("gates", n_gate)):
        src[name] = (off, width)
        off += width
    order = ("q", "u", "c_q", "k", "v", "c_kv", "kpe")
    col = {}
    off = 0
    for name in order:
        col[name] = off
        off += src[name][1]
    z_tn = 512
    z_cols = -(-off // z_tn) * z_tn

    (cos_a, sin_a), tabs_b = _rope_tables(dm)
    xall = jnp.concatenate([ctx.reshape(RC, D), x.reshape(RL, D)], axis=0).astype(F32)
    cond = jnp.zeros((8, D), F32).at[0].set(c_ctx.astype(F32)).at[1:1 + B].set(c.astype(F32))

    for layer in range(depth):
        with_ctx = layer < depth - 1
        row0 = 0 if with_ctx else RC
        n_rows = R - row0
        mods = mm1(cond, [(mod_w, (layer,), 0)], _epi_bias, n_rows=8, n_cols=6 * D, tm=8, tn=512, out_dtype=F32,
                   extras=[(mod_b.reshape(depth, 1, 6 * D), (None, 1, 512), lambda j, i: (layer, 0, j))],
                   prologue=lambda a: _silu(a).astype(BF16), name="ada_mod")
        sh_m, sc_m, g_m, sh_f, sc_f, g_f = [mods[:, i * D:(i + 1) * D].reshape(8, 1, D) for i in range(6)]

        h = modulate(dm, xall, norm_mix_g[layer], sh_m, sc_m, mod_row0=0)
        w_l = w_in[layer]
        w_rest = jnp.concatenate([w_l[:, src[n][0]:src[n][0] + src[n][1]] for n in order]
                                 + [jnp.zeros((D, z_cols - off), w_l.dtype)], axis=1)
        w_gates = w_l[:, src["gates"][0]:]
        z = mm1(h, [(w_rest, (), 0)], _epi_id, n_rows=R, n_cols=z_cols, tm=tm, tn=z_tn, out_dtype=F32, name="w_in")
        gates = mm1(h, [(w_gates, (), 0)], _epi_sigmoid, n_rows=R, n_cols=n_gate, tm=tm, tn=512,
                    out_dtype=BF16, name="w_in_gates")

        qa, ka, va = swa_prep(dm, z, col, swa_q_norm_g[layer], swa_k_norm_g[layer], cos_a, sin_a)
        ya_l = swa_attention(dm, qa, ka, va, swa_sink[layer], latent=True)
        cqn, ckvn = mla_lowrank_norm(dm, z, col, q_rank, kv_rank, mla_q_a_norm_g[layer], mla_kv_a_norm_g[layer])
        w_uq = mla_w_uq[layer].reshape(q_rank, MLA_HEADS, MLA_QK)
        w_uq = jnp.pad(w_uq, ((0, 0), (0, 0), (0, MLA_QK_PAD - MLA_QK))).reshape(q_rank, MLA_HEADS * MLA_QK_PAD)
        w_ukv = mla_w_ukv[layer].reshape(kv_rank, MLA_HEADS, MLA_NOPE + MLA_V)
        w_ukv = jnp.concatenate([w_ukv[:, :, :MLA_NOPE].reshape(kv_rank, -1),
                                 w_ukv[:, :, MLA_NOPE:].reshape(kv_rank, -1)], axis=1)
        qf = mm1(cqn, [(w_uq, (), 0)], _epi_id, n_rows=R, n_cols=w_uq.shape[1], tm=tm, tn=512, out_dtype=F32,
                 name="mla_uq")
        kvf = mm1(ckvn, [(w_ukv, (), 0)], _epi_id, n_rows=R, n_cols=w_ukv.shape[1], tm=tm, tn=512, out_dtype=F32,
                  name="mla_ukv")
        gq_pad = _pad_head_vec(mla_q_norm_g[layer])
        gk_pad = _pad_head_vec(mla_k_norm_g[layer])
        qm_c, km_c, vm_c = mla_prep(dm, qf, kvf, z, col, gq_pad, gk_pad, tabs_b, row0=0, n_rows=RC)
        qm_l, km_l, vm_l = mla_prep(dm, qf, kvf, z, col, gq_pad, gk_pad, tabs_b, row0=RC, n_rows=RL)
        yb_l = mla_attention(dm, qm_l, km_c, vm_c, km_l, vm_l)
        if with_ctx:
            ya = jnp.concatenate([swa_attention(dm, qa, ka, va, swa_sink[layer], latent=False), ya_l], axis=0)
            yb = jnp.concatenate([mla_attention(dm, qm_c, km_c, vm_c), yb_l], axis=0)
            y_row0s = (0, 0, 0)
        else:
            ya, yb = ya_l, yb_l
            y_row0s = (RC, RC, 0)
        tables = _ssm_tables(ssm_lam_re[layer], ssm_lam_im[layer], ssm_log_step[layer], ssm_b_re[layer],
                             ssm_b_im[layer], ssm_c_re[layer], ssm_c_im[layer])
        yg = s5_branch(dm, z, col, ssm_w, tables, ssm_d[layer])
        b_glu = ssm_b_glu.reshape(depth, 1, 2 * ssm_w)
        gl_tn = 512
        yc = mm1(yg, [(ssm_w_glu, (layer,), 0), (ssm_w_glu, (layer,), ssm_w)], _epi_glu_bias,
                 n_rows=R, n_cols=ssm_w, tm=tm, tn=gl_tn, out_dtype=BF16,
                 extras=[(b_glu, (None, 1, gl_tn), lambda j, i: (layer, 0, j)),
                         (b_glu, (None, 1, gl_tn), lambda j, i: (layer, 0, j + ssm_w // gl_tn))],
                 name="ssm_glu")
        mixed = merge_branches(dm, (ya, yb, yc), y_row0s, gates, w_branch, layer, row0=row0, n_rows=n_rows)
        t0 = row0 // tm
        x1 = mm1(mixed, [(w_out, (layer,), 0)], _epi_residual, n_rows=n_rows, n_cols=D, tm=tm, tn=512, out_dtype=F32,
                 extras=[(xall, (tm, 512), lambda j, i: (i + t0, j)),
                         (g_m, (1, 1, 512), lambda j, i: (dm.mod_row(i + t0, tm), 0, j))],
                 name="w_out")
        h2 = modulate(dm, x1, norm_ffn_g[layer], sh_f, sc_f, mod_row0=row0)
        if layer % 2 == 0:
            F = ffn_w13.shape[-1] // 2
            f_tn = _pick_tile((512, 256, 128), F)
            act = mm1(h2, [(ffn_w13, (layer // 2,), 0), (ffn_w13, (layer // 2,), F)], _epi_swiglu, n_rows=n_rows,
                      n_cols=F, tm=tm, tn=f_tn, out_dtype=BF16, name="ffn_w13")
            x2 = mm2_residual(dm, act, ffn_w2[layer // 2].astype(BF16), x1, g_f, mod_row0=row0, tm=tm, tk=f_tn)
        else:
            if with_ctx:
                raise NotImplementedError("a mixture-of-experts layer that still feeds context rows")
            x2 = moe_ffn(dm, h2, x1, g_f, moe_w_router[layer // 2], moe_b_router[layer // 2], moe_w13, moe_w2,
                         layer // 2, mod_row0=row0)
        xall = x2
    return xall.reshape(B, S, D)


def kernel(x, c, ctx, c_ctx, mod_w, mod_b, norm_mix_g, norm_ffn_g, w_in, swa_q_norm_g, swa_k_norm_g, swa_sink, mla_q_a_norm_g, mla_w_uq, mla_kv_a_norm_g, mla_w_ukv, mla_q_norm_g, mla_k_norm_g, ssm_lam_re, ssm_lam_im, ssm_log_step, ssm_b_re, ssm_b_im, ssm_c_re, ssm_c_im, ssm_d, ssm_w_glu, ssm_b_glu, w_branch, w_out, ffn_w13, ffn_w2, moe_w_router, moe_b_router, moe_w13, moe_w2):
    return _trunk(x, c, ctx, c_ctx, mod_w, mod_b, norm_mix_g, norm_ffn_g, w_in, swa_q_norm_g, swa_k_norm_g, swa_sink,
                  mla_q_a_norm_g, mla_w_uq, mla_kv_a_norm_g, mla_w_ukv, mla_q_norm_g, mla_k_norm_g,
                  ssm_lam_re, ssm_lam_im, ssm_log_step, ssm_b_re, ssm_b_im, ssm_c_re, ssm_c_im,
                  ssm_d, ssm_w_glu, ssm_b_glu, w_branch, w_out, ffn_w13, ffn_w2, moe_w_router, moe_b_router,
                  moe_w13, moe_w2)
```

```python
import functools
import math

import jax
import jax.numpy as jnp
from jax import lax
from jax.experimental import pallas as pl
from jax.experimental.pallas import tpu as pltpu

F32 = jnp.float32
BF16 = jnp.bfloat16

GRID_W = 64
ROPE_THETA = 10000.0
EPS = 1e-6
SWA_HEADS = 8
SWA_KV_HEADS = 2
SWA_HEAD_DIM = 128
SWA_WINDOW = 128
MLA_HEADS = 8
MLA_NOPE = 128
MLA_ROPE = 64
MLA_V = 128
MLA_QK = MLA_NOPE + MLA_ROPE
MLA_QK_PAD = 256
SSM_GROUP = 16
SSM_STATE = 64
SSM_CHUNK = 16
N_BRANCH = 3
N_EXPERTS = 8
TOP_K = 2
LANE = 128
VMEM_LIMIT_BYTES = 56 * 1024 * 1024
MOE_TM = 512
NEG_BIG = -1e30


def _params(n_grid):
    return pltpu.CompilerParams(dimension_semantics=("arbitrary",) * n_grid, vmem_limit_bytes=VMEM_LIMIT_BYTES)


def _pick_tile(candidates, *sizes):
    for t in candidates:
        if all(s % t == 0 for s in sizes):
            return t
    raise ValueError(f"no tile in {candidates} divides {sizes}")


class Dims:
    def __init__(self, batch, seq, ctx_len):
        self.B, self.S, self.L = batch, seq, ctx_len
        self.RC = batch * ctx_len
        self.RL = batch * seq
        self.R = self.RC + self.RL
        self.tm = _pick_tile((1024, 512, 256, 128), ctx_len * batch, seq)

    def mod_row(self, tile, tm):
        nct = self.RC // tm
        return jnp.where(tile < nct, 0, 1 + (tile - nct) // (self.S // tm))


def _silu(x):
    return x * (1.0 / (1.0 + jnp.exp(-x)))


def _sigmoid(x):
    return 1.0 / (1.0 + jnp.exp(-x))


def _gelu_tanh(x):
    c = math.sqrt(2.0 / math.pi)
    return 0.5 * x * (1.0 + jnp.tanh(c * (x + 0.044715 * (x * x * x))))


def _dot(a, b):
    return jnp.dot(a, b, preferred_element_type=F32)


def _dot_nt(a, b):
    return lax.dot_general(a, b, (((1,), (1,)), ((), ())), preferred_element_type=F32)


def _pack_bf16_pairs(y):
    half = y.shape[1] // 2
    bits = lax.bitcast_convert_type(y.astype(BF16).astype(F32), jnp.uint32)
    return (bits[:, :half] >> 16) | (bits[:, half:] & jnp.uint32(0xFFFF0000))


def _unpack_bf16_pairs(p):
    lo = lax.bitcast_convert_type(p << 16, F32).astype(BF16)
    hi = lax.bitcast_convert_type(p & jnp.uint32(0xFFFF0000), F32).astype(BF16)
    return lo, hi


def _modulate_kernel(x_ref, g_ref, sh_ref, sc_ref, o_ref, *, pack):
    x = x_ref[...]
    ms = jnp.mean(x * x, axis=-1, keepdims=True)
    y = x * lax.rsqrt(ms + EPS) * g_ref[...]
    y = y * (1.0 + sc_ref[0]) + sh_ref[0]
    o_ref[...] = _pack_bf16_pairs(y) if pack else y.astype(o_ref.dtype)


def modulate(dm, x, g, shift, scale, *, mod_row0, pack=False):
    n_rows, D = x.shape
    tm = min(dm.tm, 512)
    t0 = mod_row0 // tm
    out_w, out_dtype = (D // 2, jnp.uint32) if pack else (D, BF16)
    return pl.pallas_call(
        functools.partial(_modulate_kernel, pack=pack),
        grid=(n_rows // tm,),
        in_specs=[
            pl.BlockSpec((tm, D), lambda i: (i, 0)),
            pl.BlockSpec((1, D), lambda i: (0, 0)),
            pl.BlockSpec((1, 1, D), lambda i: (dm.mod_row(i + t0, tm), 0, 0)),
            pl.BlockSpec((1, 1, D), lambda i: (dm.mod_row(i + t0, tm), 0, 0)),
        ],
        out_specs=pl.BlockSpec((tm, out_w), lambda i: (i, 0)),
        out_shape=jax.ShapeDtypeStruct((n_rows, out_w), out_dtype),
        compiler_params=_params(1),
        name="modulate_packed" if pack else "modulate",
    )(x, g.reshape(1, D), shift, scale)


def _mm1_kernel(*refs, n_w, n_extra, epilogue, prologue):
    a_ref = refs[0]
    w_refs = refs[1:1 + n_w]
    extra = refs[1 + n_w:1 + n_w + n_extra]
    o_ref = refs[1 + n_w + n_extra]
    wb = refs[2 + n_w + n_extra:]

    @pl.when(pl.program_id(1) == 0)
    def _():
        for w_ref, b in zip(w_refs, wb):
            b[...] = w_ref[...].astype(BF16)

    a = a_ref[...]
    if prologue is not None:
        a = prologue(a)
    accs = [_dot(a, b[...]) for b in wb]
    o_ref[...] = epilogue(accs, *extra).astype(o_ref.dtype)


def mm1(a, weights, epilogue, *, n_rows, n_cols, tm, tn, out_dtype, a_row0=0, extras=(), prologue=None, name):
    K = a.shape[1]
    t0 = a_row0 // tm
    in_specs = [pl.BlockSpec((tm, K), lambda j, i: (i + t0, 0))]
    operands = [a]
    for w, lead, col0 in weights:
        c0 = col0 // tn
        in_specs.append(pl.BlockSpec((None,) * len(lead) + (K, tn),
                                     functools.partial(lambda j, i, lead, c0: lead + (0, j + c0), lead=lead, c0=c0)))
        operands.append(w)
    for arr, bshape, imap in extras:
        in_specs.append(pl.BlockSpec(bshape, imap))
        operands.append(arr)
    kern = functools.partial(_mm1_kernel, n_w=len(weights), n_extra=len(extras), epilogue=epilogue,
                             prologue=prologue)
    return pl.pallas_call(
        kern,
        grid=(n_cols // tn, n_rows // tm),
        in_specs=in_specs,
        out_specs=pl.BlockSpec((tm, tn), lambda j, i: (i, j)),
        out_shape=jax.ShapeDtypeStruct((n_rows, n_cols), out_dtype),
        scratch_shapes=[pltpu.VMEM((K, tn), BF16) for _ in weights],
        compiler_params=_params(2),
        name=name,
    )(*operands)


def _epi_id(accs):
    return accs[0]


def _epi_sigmoid(accs):
    return _sigmoid(accs[0])


def _epi_swiglu(accs):
    return _silu(accs[0]) * accs[1]


def _epi_bias(accs, b_ref):
    return accs[0] + b_ref[...]


def _epi_glu_bias(accs, ba_ref, bb_ref):
    return (accs[0] + ba_ref[...]) * _sigmoid(accs[1] + bb_ref[...])


def _epi_residual(accs, x_ref, gate_ref):
    return x_ref[...] + gate_ref[0] * accs[0]


def _mm2_kernel(a_ref, w_ref, x_ref, gate_ref, o_ref, acc_ref):
    k = pl.program_id(1)

    @pl.when(k == 0)
    def _():
        acc_ref[...] = jnp.zeros_like(acc_ref)

    acc_ref[...] += _dot(a_ref[...], w_ref[...])

    @pl.when(k == pl.num_programs(1) - 1)
    def _():
        o_ref[...] = x_ref[...] + gate_ref[0] * acc_ref[...]


def mm2_residual(dm, a, w, x, gate, *, mod_row0, tm, tk):
    M, K = a.shape
    N = w.shape[1]
    t0 = mod_row0 // tm
    return pl.pallas_call(
        _mm2_kernel,
        grid=(M // tm, K // tk),
        in_specs=[
            pl.BlockSpec((tm, tk), lambda i, k: (i, k)),
            pl.BlockSpec((tk, N), lambda i, k: (k, 0)),
            pl.BlockSpec((tm, N), lambda i, k: (i, 0)),
            pl.BlockSpec((1, 1, N), lambda i, k: (dm.mod_row(i + t0, tm), 0, 0)),
        ],
        out_specs=pl.BlockSpec((tm, N), lambda i, k: (i, 0)),
        out_shape=jax.ShapeDtypeStruct((M, N), F32),
        scratch_shapes=[pltpu.VMEM((tm, N), F32)],
        compiler_params=_params(2),
        name="mm2_residual",
    )(a, w, x, gate)


def _swa_prep_kernel(q_ref, k_ref, v_ref, gq_ref, gk_ref, cos_ref, sin_ref, qo_ref, ko_ref, vo_ref):
    c = cos_ref[...]
    s = sin_ref[...]

    def norm_rope(x, g, scale):
        ms = jnp.mean(x * x, axis=-1, keepdims=True)
        y = x * lax.rsqrt(ms + EPS) * g
        return (y * c + pltpu.roll(y, SWA_HEAD_DIM // 2, 1) * s) * scale

    gq = gq_ref[...]
    gk = gk_ref[...]
    for h in range(SWA_HEADS):
        sl = slice(h * SWA_HEAD_DIM, (h + 1) * SWA_HEAD_DIM)
        qo_ref[:, sl] = norm_rope(q_ref[:, sl], gq, SWA_HEAD_DIM ** -0.5).astype(BF16)
    for h in range(SWA_KV_HEADS):
        sl = slice(h * SWA_HEAD_DIM, (h + 1) * SWA_HEAD_DIM)
        ko_ref[:, sl] = norm_rope(k_ref[:, sl], gk, 1.0).astype(BF16)
    vo_ref[...] = v_ref[...].astype(BF16)


def _rope_tile_index(dm, tm):
    nct = dm.RC // tm
    return lambda i: (jnp.where(i < nct, 0, 1 + (i - nct) % (dm.S // tm)), 0)


def swa_prep(dm, z, col, gq, gk, cos_t, sin_t):
    tm = min(dm.tm, 256)
    QW = SWA_HEADS * SWA_HEAD_DIM
    KW = SWA_KV_HEADS * SWA_HEAD_DIM
    ridx = _rope_tile_index(dm, tm)
    return pl.pallas_call(
        _swa_prep_kernel,
        grid=(dm.R // tm,),
        in_specs=[
            pl.BlockSpec((tm, QW), lambda i: (i, col["q"] // QW)),
            pl.BlockSpec((tm, KW), lambda i: (i, col["k"] // KW)),
            pl.BlockSpec((tm, KW), lambda i: (i, col["v"] // KW)),
            pl.BlockSpec((1, SWA_HEAD_DIM), lambda i: (0, 0)),
            pl.BlockSpec((1, SWA_HEAD_DIM), lambda i: (0, 0)),
            pl.BlockSpec((tm, SWA_HEAD_DIM), ridx),
            pl.BlockSpec((tm, SWA_HEAD_DIM), ridx),
        ],
        out_specs=[
            pl.BlockSpec((tm, QW), lambda i: (i, 0)),
            pl.BlockSpec((tm, KW), lambda i: (i, 0)),
            pl.BlockSpec((tm, KW), lambda i: (i, 0)),
        ],
        out_shape=[
            jax.ShapeDtypeStruct((dm.R, QW), BF16),
            jax.ShapeDtypeStruct((dm.R, KW), BF16),
            jax.ShapeDtypeStruct((dm.R, KW), BF16),
        ],
        compiler_params=_params(1),
        name="swa_prep",
    )(z, z, z, gq.reshape(1, -1), gk.reshape(1, -1), cos_t, sin_t)


def _swa_attn_kernel(*refs, windowed, nb):
    if windowed:
        q_ref, kc_ref, kp_ref, kk_ref, kn_ref, vc_ref, vp_ref, vk_ref, vn_ref, sink_ref, o_ref = refs
    else:
        q_ref, kc_ref, vc_ref, sink_ref, o_ref = refs
    G = SWA_HEADS // SWA_KV_HEADS
    blk = q_ref.shape[0]
    Dh = SWA_HEAD_DIM
    q = jnp.concatenate([q_ref[:, g * Dh:(g + 1) * Dh] for g in range(G)], axis=0)
    sink = sink_ref[0][:, 0:1]
    scores = [_dot_nt(q, kc_ref[...])]
    values = [vc_ref[...]]
    if windowed:
        n = pl.program_id(2)
        qi = lax.broadcasted_iota(jnp.int32, (G * blk, blk), 0) % blk
        kj = lax.broadcasted_iota(jnp.int32, (G * blk, blk), 1)
        s_p = _dot_nt(q, kp_ref[...])
        s_p = jnp.where(kj >= qi, s_p, NEG_BIG)
        s_p = jnp.where(n >= 1, s_p, NEG_BIG)
        s_n = _dot_nt(q, kn_ref[...])
        s_n = jnp.where(kj <= qi, s_n, NEG_BIG)
        s_n = jnp.where(n <= nb - 2, s_n, NEG_BIG)
        scores += [s_p, _dot_nt(q, kk_ref[...]), s_n]
        values += [vp_ref[...], vk_ref[...], vn_ref[...]]
    m = sink
    for s in scores:
        m = jnp.maximum(m, jnp.max(s, axis=-1, keepdims=True))
    l = jnp.exp(sink - m)
    o = None
    for s, v in zip(scores, values):
        p = jnp.exp(s - m)
        l = l + jnp.sum(p, axis=-1, keepdims=True)
        pv = _dot(p.astype(BF16), v)
        o = pv if o is None else o + pv
    o = o / l
    for g in range(G):
        o_ref[:, g * Dh:(g + 1) * Dh] = o[g * blk:(g + 1) * blk].astype(o_ref.dtype)


def swa_attention(dm, qa, ka, va, sink, *, latent):
    G = SWA_HEADS // SWA_KV_HEADS
    Dh = SWA_HEAD_DIM
    blk = SWA_WINDOW
    L = dm.L
    sink_col = jnp.broadcast_to(sink.astype(F32).reshape(SWA_KV_HEADS, G, 1, 1),
                                (SWA_KV_HEADS, G, blk, LANE)).reshape(SWA_KV_HEADS, G * blk, LANE)
    sink_spec = pl.BlockSpec((1, G * blk, LANE), lambda b, h, n: (h, 0, 0))
    ctx_spec = pl.BlockSpec((L, Dh), lambda b, h, n: (b, h))
    if latent:
        nb = dm.S // blk
        base = dm.RC // blk

        def q_map(b, h, n):
            return (base + b * nb + n, h)

        def kv_map(off):
            return lambda b, h, n: (base + b * nb + jnp.clip(n + off, 0, nb - 1), h)

        win_specs = [pl.BlockSpec((blk, Dh), kv_map(off)) for off in (-1, 0, 1)]
        in_specs = ([pl.BlockSpec((blk, G * Dh), q_map), ctx_spec] + win_specs + [ctx_spec] + win_specs
                    + [sink_spec])
        operands = (qa, ka, ka, ka, ka, va, va, va, va, sink_col)
        n_out = dm.RL
    else:
        nb = L // blk
        in_specs = [pl.BlockSpec((blk, G * Dh), lambda b, h, n: (b * nb + n, h)), ctx_spec, ctx_spec, sink_spec]
        operands = (qa, ka, va, sink_col)
        n_out = dm.RC
    return pl.pallas_call(
        functools.partial(_swa_attn_kernel, windowed=latent, nb=nb),
        grid=(dm.B, SWA_KV_HEADS, nb),
        in_specs=in_specs,
        out_specs=pl.BlockSpec((blk, G * Dh), lambda b, h, n: (b * nb + n, h)),
        out_shape=jax.ShapeDtypeStruct((n_out, SWA_HEADS * Dh), BF16),
        compiler_params=_params(3),
        name="swa_attn_lat" if latent else "swa_attn_ctx",
    )(*operands)


def _rms_rows_kernel(a_ref, b_ref, ga_ref, gb_ref, ao_ref, bo_ref):
    for x_ref, g_ref, o_ref in ((a_ref, ga_ref, ao_ref), (b_ref, gb_ref, bo_ref)):
        x = x_ref[...]
        ms = jnp.mean(x * x, axis=-1, keepdims=True)
        o_ref[...] = (x * lax.rsqrt(ms + EPS) * g_ref[...]).astype(o_ref.dtype)


def mla_lowrank_norm(dm, z, col, q_rank, kv_rank, gq, gkv):
    tm = min(dm.tm, 512)
    return pl.pallas_call(
        _rms_rows_kernel,
        grid=(dm.R // tm,),
        in_specs=[
            pl.BlockSpec((tm, q_rank), lambda i: (i, col["c_q"] // q_rank)),
            pl.BlockSpec((tm, kv_rank), lambda i: (i, col["c_kv"] // kv_rank)),
            pl.BlockSpec((1, q_rank), lambda i: (0, 0)),
            pl.BlockSpec((1, kv_rank), lambda i: (0, 0)),
        ],
        out_specs=[pl.BlockSpec((tm, q_rank), lambda i: (i, 0)), pl.BlockSpec((tm, kv_rank), lambda i: (i, 0))],
        out_shape=[jax.ShapeDtypeStruct((dm.R, q_rank), BF16), jax.ShapeDtypeStruct((dm.R, kv_rank), BF16)],
        compiler_params=_params(1),
        name="mla_lowrank_norm",
    )(z, z, gq.reshape(1, -1), gkv.reshape(1, -1))


def _mla_prep_kernel(q_ref, kn_ref, v_ref, pe_ref, gq_ref, gk_ref, c_ref, s1_ref, s2_ref, qo_ref, ko_ref, vo_ref):
    c = c_ref[...]
    s1 = s1_ref[...]
    s2 = s2_ref[...]
    gq = gq_ref[...]
    gk = gk_ref[...]
    scale = MLA_QK ** -0.5

    def rope(x):
        return x * c + pltpu.roll(x, LANE - MLA_ROPE // 2, 1) * s1 + pltpu.roll(x, MLA_ROPE // 2, 1) * s2

    pe = pe_ref[...]
    pe_ss = jnp.sum(pe * pe, axis=-1, keepdims=True)
    for h in range(MLA_HEADS):
        lo = h * MLA_QK_PAD
        qh = q_ref[:, lo:lo + MLA_QK_PAD]
        inv = lax.rsqrt(jnp.sum(qh * qh, axis=-1, keepdims=True) * (1.0 / MLA_QK) + EPS)
        qn = qh * inv * gq
        qo_ref[:, lo:lo + MLA_NOPE] = (qn[:, :MLA_NOPE] * scale).astype(BF16)
        qo_ref[:, lo + MLA_NOPE:lo + MLA_QK_PAD] = (rope(qn[:, MLA_NOPE:]) * scale).astype(BF16)
        kh = kn_ref[:, h * MLA_NOPE:(h + 1) * MLA_NOPE]
        inv = lax.rsqrt((jnp.sum(kh * kh, axis=-1, keepdims=True) + pe_ss) * (1.0 / MLA_QK) + EPS)
        ko_ref[:, lo:lo + MLA_NOPE] = (kh * inv * gk[:, :MLA_NOPE]).astype(BF16)
        ko_ref[:, lo + MLA_NOPE:lo + MLA_QK_PAD] = rope(pe * inv * gk[:, MLA_NOPE:]).astype(BF16)
    vo_ref[...] = v_ref[...].astype(BF16)


def mla_prep(dm, qf, kvf, z, col, gq_pad, gk_pad, tabs, *, row0, n_rows):
    tm = min(dm.tm, 256)
    t0 = row0 // tm
    QW = MLA_HEADS * MLA_QK_PAD
    NW = MLA_HEADS * MLA_NOPE
    ridx = _rope_tile_index(dm, tm)
    rspec = pl.BlockSpec((tm, LANE), lambda i: ridx(i + t0))
    return pl.pallas_call(
        _mla_prep_kernel,
        grid=(n_rows // tm,),
        in_specs=[
            pl.BlockSpec((tm, QW), lambda i: (i + t0, 0)),
            pl.BlockSpec((tm, NW), lambda i: (i + t0, 0)),
            pl.BlockSpec((tm, NW), lambda i: (i + t0, 1)),
            pl.BlockSpec((tm, LANE), lambda i: (i + t0, col["kpe"] // LANE)),
            pl.BlockSpec((1, MLA_QK_PAD), lambda i: (0, 0)),
            pl.BlockSpec((1, MLA_QK_PAD), lambda i: (0, 0)),
            rspec, rspec, rspec,
        ],
        out_specs=[
            pl.BlockSpec((tm, QW), lambda i: (i, 0)),
            pl.BlockSpec((tm, QW), lambda i: (i, 0)),
            pl.BlockSpec((tm, NW), lambda i: (i, 0)),
        ],
        out_shape=[
            jax.ShapeDtypeStruct((n_rows, QW), BF16),
            jax.ShapeDtypeStruct((n_rows, QW), BF16),
            jax.ShapeDtypeStruct((n_rows, NW), BF16),
        ],
        compiler_params=_params(1),
        name="mla_prep",
    )(qf, kvf, kvf, z, gq_pad, gk_pad, *tabs)


def _mla_attn_kernel(*refs, with_latent):
    if with_latent:
        q_ref, kc_ref, kl_ref, vc_ref, vl_ref, o_ref = refs
    else:
        q_ref, kc_ref, vc_ref, o_ref = refs
    q = q_ref[...]
    s_c = _dot_nt(q, kc_ref[...])
    m = jnp.max(s_c, axis=-1, keepdims=True)
    if with_latent:
        s_l = _dot_nt(q, kl_ref[...])
        m = jnp.maximum(m, jnp.max(s_l, axis=-1, keepdims=True))
    p_c = jnp.exp(s_c - m)
    l = jnp.sum(p_c, axis=-1, keepdims=True)
    o = _dot(p_c.astype(BF16), vc_ref[...])
    if with_latent:
        p_l = jnp.exp(s_l - m)
        l = l + jnp.sum(p_l, axis=-1, keepdims=True)
        o = o + _dot(p_l.astype(BF16), vl_ref[...])
    o_ref[...] = (o / l).astype(o_ref.dtype)


def mla_attention(dm, q, kc, vc, kl=None, vl=None):
    with_latent = kl is not None
    n_q = dm.S if with_latent else dm.L
    tq = min(256, n_q)
    nq = n_q // tq
    QP, V = MLA_QK_PAD, MLA_V
    in_specs = [pl.BlockSpec((tq, QP), lambda b, h, n: (b * nq + n, h)),
                pl.BlockSpec((dm.L, QP), lambda b, h, n: (b, h))]
    operands = [q, kc]
    if with_latent:
        in_specs.append(pl.BlockSpec((dm.S, QP), lambda b, h, n: (b, h)))
        operands.append(kl)
    in_specs.append(pl.BlockSpec((dm.L, V), lambda b, h, n: (b, h)))
    operands.append(vc)
    if with_latent:
        in_specs.append(pl.BlockSpec((dm.S, V), lambda b, h, n: (b, h)))
        operands.append(vl)
    return pl.pallas_call(
        functools.partial(_mla_attn_kernel, with_latent=with_latent),
        grid=(dm.B, MLA_HEADS, nq),
        in_specs=in_specs,
        out_specs=pl.BlockSpec((tq, V), lambda b, h, n: (b * nq + n, h)),
        out_shape=jax.ShapeDtypeStruct((dm.B * n_q, MLA_HEADS * V), BF16),
        compiler_params=_params(3),
        name="mla_attn_lat" if with_latent else "mla_attn_ctx",
    )(*operands)


SSM_SUPER = 16
SSM_BLOCK_GROUPS = LANE // SSM_GROUP


def _chunk_rows(ref, nr):
    return jnp.concatenate([ref[pl.ds(s, nr, stride=SSM_CHUNK), :] for s in range(SSM_CHUNK)], axis=1)


def _ssm_in_kernel(u_ref, wi_ref, ws_ref, y_ref, s_ref):
    nr = y_ref.shape[0]
    a = _chunk_rows(u_ref, nr).astype(BF16)
    y_ref[...] = _dot(a, wi_ref[...])
    zs = _dot(a, ws_ref[...])
    for c in range(2 * SSM_BLOCK_GROUPS):
        s_ref[pl.ds(c, nr, stride=2 * SSM_BLOCK_GROUPS), :] = zs[:, c * LANE:(c + 1) * LANE]


def ssm_chunk_in(z, u_col, w_intra, w_state, *, tr):
    R = z.shape[0]
    nblk, CW, _ = w_intra.shape
    nr = tr // SSM_CHUNK
    c0 = u_col // LANE
    w_spec = pl.BlockSpec((None, CW, CW), lambda j, i: (j, 0, 0), pipeline_mode=pl.Buffered(1))
    return pl.pallas_call(
        _ssm_in_kernel,
        grid=(nblk, R // tr),
        in_specs=[pl.BlockSpec((tr, LANE), lambda j, i: (i, c0 + j)), w_spec, w_spec],
        out_specs=[pl.BlockSpec((nr, CW), lambda j, i: (i, j)),
                   pl.BlockSpec((tr, LANE), lambda j, i: (i, j))],
        out_shape=[jax.ShapeDtypeStruct((R // SSM_CHUNK, nblk * CW), F32),
                   jax.ShapeDtypeStruct((R, nblk * LANE), F32)],
        compiler_params=_params(2),
        name="ssm_chunk_in",
    )(z, w_intra, w_state)


def _ssm_scan_kernel(s_ref, p1_ref, p2_ref, x_ref, t_ref, e_ref, *, batch, n_ctx_sc, n_lat_sc):
    SC = SSM_SUPER
    GB = SSM_BLOCK_GROUPS
    FWD, BWD = slice(0, GB), slice(GB, 2 * GB)

    def cmul(i, rows, x):
        swapped = pltpu.roll(x.reshape(-1, LANE), SSM_STATE, 1).reshape(x.shape)
        return p1_ref[i, rows, :] * x + p2_ref[i, rows, :] * swapped

    n_sc = s_ref.shape[0]
    lf = jnp.zeros((n_sc, GB, LANE), F32)
    lb = jnp.zeros((n_sc, GB, LANE), F32)
    for i in range(SC):
        x_ref[:, i, FWD, :] = lf
        x_ref[:, SC - 1 - i, BWD, :] = lb
        lf = cmul(1, FWD, lf) + s_ref[:, i, FWD, :]
        lb = cmul(1, BWD, lb) + s_ref[:, SC - 1 - i, BWD, :]
    t_ref[:, FWD, :] = lf
    t_ref[:, BWD, :] = lb
    n_ctx = batch * n_ctx_sc
    for rows, order in ((FWD, 1), (BWD, -1)):
        e = jnp.zeros((batch, GB, LANE), F32)
        for region_start, per_batch in ((0, n_ctx_sc), (n_ctx, n_lat_sc)):
            steps = range(per_batch) if order == 1 else range(per_batch - 1, -1, -1)
            for m in steps:
                idx = pl.ds(region_start + m, batch, stride=per_batch)
                e_ref[idx, rows, :] = e
                e = cmul(SC, rows, e) + t_ref[idx, rows, :]
    ef = e_ref[:, FWD, :]
    eb = e_ref[:, BWD, :]
    for i in range(SC):
        x_ref[:, i, FWD, :] += cmul(i, FWD, ef)
        x_ref[:, SC - 1 - i, BWD, :] += cmul(i, BWD, eb)


def ssm_scan(dm, s, p1, p2):
    R, W = s.shape
    nblk = W // LANE
    rows_per_sc = SSM_SUPER * 2 * SSM_BLOCK_GROUPS
    n_sc = R // rows_per_sc
    n_ctx_sc = dm.L // (SSM_CHUNK * SSM_SUPER)
    n_lat_sc = dm.S // (SSM_CHUNK * SSM_SUPER)
    s4 = s.reshape(n_sc, SSM_SUPER, 2 * SSM_BLOCK_GROUPS, W)
    blk = (n_sc, SSM_SUPER, 2 * SSM_BLOCK_GROUPS, LANE)
    pspec = pl.BlockSpec((None, SSM_SUPER + 1, 2 * SSM_BLOCK_GROUPS, LANE), lambda j: (j, 0, 0, 0))
    x4 = pl.pallas_call(
        functools.partial(_ssm_scan_kernel, batch=dm.B, n_ctx_sc=n_ctx_sc, n_lat_sc=n_lat_sc),
        grid=(nblk,),
        in_specs=[pl.BlockSpec(blk, lambda j: (0, 0, 0, j)), pspec, pspec],
        out_specs=pl.BlockSpec(blk, lambda j: (0, 0, 0, j)),
        out_shape=jax.ShapeDtypeStruct(s4.shape, F32),
        scratch_shapes=[pltpu.VMEM((n_sc, 2 * SSM_BLOCK_GROUPS, LANE), F32),
                        pltpu.VMEM((n_sc, 2 * SSM_BLOCK_GROUPS, LANE), F32)],
        compiler_params=_params(1),
        name="ssm_scan",
    )(s4, p1, p2)
    return x4.reshape(R, W)


def _ssm_out_kernel(y_ref, x_ref, u_ref, w_ref, d_ref, o_ref):
    nr = y_ref.shape[0]
    nc = 2 * SSM_BLOCK_GROUPS
    xs = jnp.concatenate([x_ref[pl.ds(c, nr, stride=nc), :] for c in range(nc)], axis=1).astype(BF16)
    y = y_ref[...] + _dot(xs, w_ref[...])
    d = d_ref[...]
    for t in range(SSM_CHUNK):
        rows = pl.ds(t, nr, stride=SSM_CHUNK)
        o_ref[rows, :] = _gelu_tanh(y[:, t * LANE:(t + 1) * LANE] + d * u_ref[rows, :])


def ssm_chunk_out(y_intra, x_states, z, u_col, w_out_state, d_skip, *, tr):
    R = z.shape[0]
    nblk, CW, _ = w_out_state.shape
    nr = tr // SSM_CHUNK
    c0 = u_col // LANE
    return pl.pallas_call(
        _ssm_out_kernel,
        grid=(nblk, R // tr),
        in_specs=[pl.BlockSpec((nr, CW), lambda j, i: (i, j)),
                  pl.BlockSpec((tr, LANE), lambda j, i: (i, j)),
                  pl.BlockSpec((tr, LANE), lambda j, i: (i, c0 + j)),
                  pl.BlockSpec((None, CW, CW), lambda j, i: (j, 0, 0), pipeline_mode=pl.Buffered(1)),
                  pl.BlockSpec((1, LANE), lambda j, i: (0, j))],
        out_specs=pl.BlockSpec((tr, LANE), lambda j, i: (i, j)),
        out_shape=jax.ShapeDtypeStruct((R, nblk * LANE), F32),
        compiler_params=_params(2),
        name="ssm_chunk_out",
    )(y_intra, x_states, z, w_out_state, d_skip.astype(F32).reshape(1, -1))


def _ssm_tables(lam_re, lam_im, log_step, b_re, b_im, c_re, c_im):
    C, H, P = SSM_CHUNK, SSM_GROUP, SSM_STATE
    G = lam_re.shape[1]
    delta = jnp.exp(log_step.astype(F32))[..., None]
    zr = lam_re.astype(F32) * delta
    zi = lam_im.astype(F32) * delta
    k = jnp.arange(C + 1, dtype=F32)[:, None, None, None]
    mag = jnp.exp(k * zr[None])
    pw_re = mag * jnp.cos(k * zi[None])
    pw_im = mag * jnp.sin(k * zi[None])
    lb_re, lb_im = pw_re[1], pw_im[1]
    lr, li = lam_re.astype(F32), lam_im.astype(F32)
    den = lr * lr + li * li
    f_re = ((lb_re - 1.0) * lr + lb_im * li) / den
    f_im = (lb_im * lr - (lb_re - 1.0) * li) / den
    br, bi = b_re.astype(F32), b_im.astype(F32)
    bb_re = f_re[..., None] * br - f_im[..., None] * bi
    bb_im = f_re[..., None] * bi + f_im[..., None] * br
    cr, ci = c_re.astype(F32), c_im.astype(F32)
    cl_re = cr[None] * pw_re[:, :, :, None, :] - ci[None] * pw_im[:, :, :, None, :]
    cl_im = cr[None] * pw_im[:, :, :, None, :] + ci[None] * pw_re[:, :, :, None, :]
    hp = lax.Precision.HIGHEST
    kern = (jnp.einsum("kdghp,dgpj->dgkhj", cl_re[:C], bb_re, precision=hp)
            - jnp.einsum("kdghp,dgpj->dgkhj", cl_im[:C], bb_im, precision=hp))
    k_idx = jnp.arange(C)[:, None, None]
    s_idx = jnp.arange(C)[None, :, None]
    t_idx = jnp.arange(C)[None, None, :]
    sel_f = (t_idx - s_idx == k_idx).astype(F32)
    sel_b = (s_idx - t_idx == k_idx).astype(F32)
    ksum = (jnp.einsum("kst,gkhj->gsjth", sel_f, kern[0], precision=hp)
            + jnp.einsum("kst,gkhj->gsjth", sel_b, kern[1], precision=hp))

    def state_in(d, power_of_s):
        pr = pw_re[power_of_s, d]
        pi = pw_im[power_of_s, d]
        re = pr[..., None] * bb_re[d][None] - pi[..., None] * bb_im[d][None]
        im = pr[..., None] * bb_im[d][None] + pi[..., None] * bb_re[d][None]
        return jnp.concatenate([re, im], axis=2).transpose(1, 0, 3, 2)

    def state_out(d, power_of_t):
        re = cl_re[power_of_t, d]
        im = cl_im[power_of_t, d]
        return jnp.concatenate([re, -im], axis=-1).transpose(1, 3, 0, 2)

    m_sum = jnp.stack([state_in(0, C - 1 - jnp.arange(C)), state_in(1, jnp.arange(C))])
    m_out = jnp.stack([state_out(0, 1 + jnp.arange(C)), state_out(1, C - jnp.arange(C))])

    GB = SSM_BLOCK_GROUPS
    nblk = G // GB
    CW = C * GB * H
    eye = jnp.eye(GB, dtype=F32)
    w_intra = (ksum.reshape(nblk, GB, C, H, C, 1, H) * eye[None, :, None, None, None, :, None]
               ).transpose(0, 2, 1, 3, 4, 5, 6).reshape(nblk, CW, CW).astype(BF16)
    w_state = (m_sum.reshape(2, nblk, GB, C, H, 1, 2 * P) * eye[None, None, :, None, None, :, None]
               ).transpose(1, 3, 2, 4, 0, 5, 6).reshape(nblk, CW, 2 * GB * 2 * P).astype(BF16)
    w_out_state = (m_out.reshape(2, nblk, GB, 2 * P, C, 1, H) * eye[None, None, :, None, None, :, None]
                   ).transpose(1, 0, 2, 3, 4, 5, 6).reshape(nblk, 2 * GB * 2 * P, CW).astype(BF16)
    i = (C * jnp.arange(SSM_SUPER + 1, dtype=F32))[:, None, None, None]
    mag_a = jnp.exp(i * zr[None])
    pa_re = mag_a * jnp.cos(i * zi[None])
    pa_im = mag_a * jnp.sin(i * zi[None])

    def scan_table(lo, hi):
        t = jnp.concatenate([lo, hi], axis=-1).reshape(SSM_SUPER + 1, 2, nblk, GB, 2 * P)
        return t.transpose(2, 0, 1, 3, 4).reshape(nblk, SSM_SUPER + 1, 2 * GB, 2 * P)

    return w_intra, w_state, w_out_state, scan_table(pa_re, pa_re), scan_table(-pa_im, pa_im)


def s5_branch(dm, z, col, tables, d_skip):
    w_intra, w_state, w_out_state, p1, p2 = tables
    nr = _pick_tile((272, 136, 96, 64, 32, 16, 8), dm.R // SSM_CHUNK)
    tr = nr * SSM_CHUNK
    y_intra, s = ssm_chunk_in(z, col["u"], w_intra, w_state, tr=tr)
    x_states = ssm_scan(dm, s, p1, p2)
    return ssm_chunk_out(y_intra, x_states, z, col["u"], w_out_state, d_skip, tr=tr)


def _merge_kernel(ya_ref, yb_ref, yc_ref, ga_ref, gb_ref, gc_ref, wa_ref, wb_ref, wc_ref, o_ref, sa, sb, sc):
    @pl.when(pl.program_id(1) == 0)
    def _():
        for w_ref, s in ((wa_ref, sa), (wb_ref, sb), (wc_ref, sc)):
            s[...] = w_ref[...].astype(BF16)

    acc = ga_ref[...].astype(F32) * _dot(ya_ref[...], sa[...])
    acc = acc + gb_ref[...].astype(F32) * _dot(yb_ref[...], sb[...])
    acc = acc + gc_ref[...].astype(F32) * _dot(yc_ref[...], sc[...])
    o_ref[...] = acc.astype(o_ref.dtype)


def merge_branches(dm, ys, y_row0s, gates, w_branch, layer, *, row0, n_rows):
    BW = ys[0].shape[1]
    D = w_branch.shape[-1]
    tm = dm.tm
    tn = 512
    t0 = row0 // tm
    nj = D // tn
    y_specs = [pl.BlockSpec((tm, BW), functools.partial(lambda j, i, o: (i + o, 0), o=(row0 - y0) // tm))
               for y0 in y_row0s]
    g_specs = [pl.BlockSpec((tm, tn), functools.partial(lambda j, i, n: (i + t0, n * nj + j), n=n))
               for n in range(N_BRANCH)]
    w_specs = [pl.BlockSpec((None, None, BW, tn), functools.partial(lambda j, i, n: (layer, n, 0, j), n=n))
               for n in range(N_BRANCH)]
    return pl.pallas_call(
        _merge_kernel,
        grid=(nj, n_rows // tm),
        in_specs=y_specs + g_specs + w_specs,
        out_specs=pl.BlockSpec((tm, tn), lambda j, i: (i, j)),
        out_shape=jax.ShapeDtypeStruct((n_rows, D), BF16),
        scratch_shapes=[pltpu.VMEM((BW, tn), BF16)] * 3,
        compiler_params=_params(2),
        name="merge_branches",
    )(*ys, gates, gates, gates, w_branch, w_branch, w_branch)


def _router_kernel(h_ref, whi_ref, wlo_ref, b_ref, idx_ref, w_ref):
    lo, hi = _unpack_bf16_pairs(h_ref[...])
    half = lo.shape[1]
    logits = b_ref[...]
    for w in (whi_ref, wlo_ref):
        logits = logits + _dot(lo, w[:half, :]) + _dot(hi, w[half:, :])
    lane = lax.broadcasted_iota(jnp.int32, logits.shape, 1).astype(F32)
    logits = jnp.where(lane < N_EXPERTS, logits, NEG_BIG)
    m1 = jnp.max(logits, axis=-1, keepdims=True)
    i1 = jnp.min(jnp.where(logits == m1, lane, float(LANE)), axis=-1, keepdims=True)
    rest = jnp.where(lane == i1, NEG_BIG, logits)
    m2 = jnp.max(rest, axis=-1, keepdims=True)
    i2 = jnp.min(jnp.where(rest == m2, lane, float(LANE)), axis=-1, keepdims=True)
    e = jnp.exp(m2 - m1)
    w1 = 1.0 / (1.0 + e)
    w2 = e / (1.0 + e)
    idx_ref[...] = jnp.where(lane == 0.0, i1, jnp.where(lane == 1.0, i2, 0.0)).astype(jnp.int32)
    w_ref[...] = jnp.where(lane == 0.0, w1, jnp.where(lane == 1.0, w2, 0.0))


def moe_router(hp, w_router, b_router):
    M = hp.shape[0]
    D = 2 * hp.shape[1]
    tm = _pick_tile((1024, 512, 256, 128), M)
    w_pad = jnp.zeros((D, LANE), F32).at[:, :N_EXPERTS].set(w_router.astype(F32))
    w_hi = w_pad.astype(BF16)
    w_lo = (w_pad - w_hi.astype(F32)).astype(BF16)
    b_pad = jnp.zeros((1, LANE), F32).at[0, :N_EXPERTS].set(b_router.astype(F32))
    return pl.pallas_call(
        _router_kernel,
        grid=(M // tm,),
        in_specs=[pl.BlockSpec((tm, D // 2), lambda i: (i, 0)),
                  pl.BlockSpec((D, LANE), lambda i: (0, 0)),
                  pl.BlockSpec((D, LANE), lambda i: (0, 0)),
                  pl.BlockSpec((1, LANE), lambda i: (0, 0))],
        out_specs=[pl.BlockSpec((tm, LANE), lambda i: (i, 0)), pl.BlockSpec((tm, LANE), lambda i: (i, 0))],
        out_shape=[jax.ShapeDtypeStruct((M, LANE), jnp.int32), jax.ShapeDtypeStruct((M, LANE), F32)],
        compiler_params=_params(1),
        name="moe_router",
    )(hp, w_hi, w_lo, b_pad)


GATHER_UNROLL = 8


def _gather_rows_kernel(idx_ref, src_ref, o_ref, sem):
    tg = o_ref.shape[0]
    base = pl.program_id(0) * tg

    def start(r, carry):
        pltpu.make_async_copy(src_ref.at[pl.ds(idx_ref[base + r], 1)], o_ref.at[pl.ds(r, 1)], sem).start()
        return carry

    lax.fori_loop(0, tg, start, 0, unroll=GATHER_UNROLL)
    pltpu.make_async_copy(src_ref.at[pl.ds(0, tg)], o_ref, sem).wait()


def gather_rows(src, idx, *, tg=256):
    M = idx.shape[0]
    W = src.shape[1]
    return pl.pallas_call(
        _gather_rows_kernel,
        grid_spec=pltpu.PrefetchScalarGridSpec(
            num_scalar_prefetch=1,
            grid=(M // tg,),
            in_specs=[pl.BlockSpec(memory_space=pl.ANY)],
            out_specs=pl.BlockSpec((tg, W), lambda i, idx_ref: (i, 0)),
            scratch_shapes=[pltpu.SemaphoreType.DMA(())],
        ),
        out_shape=jax.ShapeDtypeStruct((M, W), src.dtype),
        compiler_params=_params(1),
        name="gather_rows",
    )(idx, src)


def _moe_w13_kernel(te_ref, tv_ref, a_ref, wg_ref, wu_ref, o_ref, sg, su):
    i = pl.program_id(1)
    prev = te_ref[jnp.maximum(i - 1, 0)]

    @pl.when(jnp.logical_or(i == 0, te_ref[i] != prev))
    def _():
        sg[...] = wg_ref[...].astype(BF16)
        su[...] = wu_ref[...].astype(BF16)

    @pl.when(tv_ref[i] == 1)
    def _():
        lo, hi = _unpack_bf16_pairs(a_ref[...])
        half = lo.shape[1]
        g = _dot(lo, sg[:half, :]) + _dot(hi, sg[half:, :])
        u = _dot(lo, su[:half, :]) + _dot(hi, su[half:, :])
        o_ref[...] = (_silu(g) * u).astype(o_ref.dtype)

    @pl.when(tv_ref[i] == 0)
    def _():
        o_ref[...] = jnp.zeros_like(o_ref)


def moe_w13(xs, w13, moe_idx, tile_expert, tile_valid, *, tn):
    P = xs.shape[0]
    D = 2 * xs.shape[1]
    F = w13.shape[-1] // 2
    tm = MOE_TM
    nj = F // tn
    return pl.pallas_call(
        _moe_w13_kernel,
        grid_spec=pltpu.PrefetchScalarGridSpec(
            num_scalar_prefetch=2,
            grid=(nj, P // tm),
            in_specs=[pl.BlockSpec((tm, D // 2), lambda j, i, te, tv: (i, 0)),
                      pl.BlockSpec((None, None, D, tn), lambda j, i, te, tv: (moe_idx, te[i], 0, j)),
                      pl.BlockSpec((None, None, D, tn), lambda j, i, te, tv: (moe_idx, te[i], 0, j + nj))],
            out_specs=pl.BlockSpec((tm, tn), lambda j, i, te, tv: (i, j)),
            scratch_shapes=[pltpu.VMEM((D, tn), BF16), pltpu.VMEM((D, tn), BF16)],
        ),
        out_shape=jax.ShapeDtypeStruct((P, F), BF16),
        compiler_params=_params(2),
        name="moe_w13",
    )(tile_expert, tile_valid, xs, w13, w13)


def _moe_w2_kernel(te_ref, tv_ref, a_ref, w_ref, o_ref, acc_ref):
    i = pl.program_id(0)
    k = pl.program_id(1)

    @pl.when(tv_ref[i] == 1)
    def _():
        @pl.when(k == 0)
        def _():
            acc_ref[...] = jnp.zeros_like(acc_ref)

        acc_ref[...] += _dot(a_ref[...], w_ref[...])

        @pl.when(k == pl.num_programs(1) - 1)
        def _():
            o_ref[...] = acc_ref[...]

    @pl.when(jnp.logical_and(tv_ref[i] == 0, k == 0))
    def _():
        o_ref[...] = jnp.zeros_like(o_ref)


def moe_w2(act, w2, tile_expert, tile_valid, *, tk):
    P, F = act.shape
    D = w2.shape[-1]
    tm = MOE_TM
    return pl.pallas_call(
        _moe_w2_kernel,
        grid_spec=pltpu.PrefetchScalarGridSpec(
            num_scalar_prefetch=2,
            grid=(P // tm, F // tk),
            in_specs=[pl.BlockSpec((tm, tk), lambda i, k, te, tv: (i, k * tv[i])),
                      pl.BlockSpec((None, tk, D), lambda i, k, te, tv: (te[i], k * tv[i], 0))],
            out_specs=pl.BlockSpec((tm, D), lambda i, k, te, tv: (i, 0)),
            scratch_shapes=[pltpu.VMEM((tm, D), F32)],
        ),
        out_shape=jax.ShapeDtypeStruct((P, D), F32),
        compiler_params=_params(2),
        name="moe_w2",
    )(tile_expert, tile_valid, act, w2)


def _moe_combine_kernel(p0_ref, p1_ref, y_ref, x_ref, gate_ref, w_ref, o_ref, b0, b1, sem):
    tc = o_ref.shape[0]
    base = pl.program_id(0) * tc

    def start(r, carry):
        pltpu.make_async_copy(y_ref.at[pl.ds(p0_ref[base + r], 1)], b0.at[pl.ds(r, 1)], sem.at[0]).start()
        pltpu.make_async_copy(y_ref.at[pl.ds(p1_ref[base + r], 1)], b1.at[pl.ds(r, 1)], sem.at[1]).start()
        return carry

    lax.fori_loop(0, tc, start, 0, unroll=GATHER_UNROLL)
    pltpu.make_async_copy(y_ref.at[pl.ds(0, tc)], b0, sem.at[0]).wait()
    pltpu.make_async_copy(y_ref.at[pl.ds(0, tc)], b1, sem.at[1]).wait()
    w = w_ref[...]
    o_ref[...] = x_ref[...] + gate_ref[0] * (w[:, 0:1] * b0[...] + w[:, 1:2] * b1[...])


def moe_combine(dm, y_sorted, pos0, pos1, top_w, x, gate, *, mod_row0):
    M, D = x.shape
    tc = 128
    t0 = mod_row0 // tc
    return pl.pallas_call(
        _moe_combine_kernel,
        grid_spec=pltpu.PrefetchScalarGridSpec(
            num_scalar_prefetch=2,
            grid=(M // tc,),
            in_specs=[pl.BlockSpec(memory_space=pl.ANY),
                      pl.BlockSpec((tc, D), lambda i, p0, p1: (i, 0)),
                      pl.BlockSpec((1, 1, D), lambda i, p0, p1: (dm.mod_row(i + t0, tc), 0, 0)),
                      pl.BlockSpec((tc, LANE), lambda i, p0, p1: (i, 0))],
            out_specs=pl.BlockSpec((tc, D), lambda i, p0, p1: (i, 0)),
            scratch_shapes=[pltpu.VMEM((tc, D), F32), pltpu.VMEM((tc, D), F32), pltpu.SemaphoreType.DMA((2,))],
        ),
        out_shape=jax.ShapeDtypeStruct((M, D), F32),
        compiler_params=_params(1),
        name="moe_combine",
    )(pos0, pos1, y_sorted, x, gate, top_w)


def moe_ffn(dm, hp, x, gate, w_router, b_router, w13, w2, moe_idx, *, mod_row0):
    M = hp.shape[0]
    E = N_EXPERTS
    tm = MOE_TM
    top_idx, top_w = moe_router(hp, w_router, b_router)
    e_flat = top_idx[:, :TOP_K].T.reshape(-1)
    onehot = (e_flat[:, None] == jnp.arange(E, dtype=jnp.int32)[None, :]).astype(jnp.int32)
    csum = jnp.cumsum(onehot, axis=0)
    counts = csum[-1]
    rank = jnp.sum((csum - onehot) * onehot, axis=1)
    padded = ((counts + tm - 1) // tm) * tm
    ends = jnp.cumsum(padded)
    starts = ends - padded
    pos = starts[e_flat] + rank
    P = TOP_K * M + E * tm
    n_tiles = P // tm
    tok = jnp.tile(jnp.arange(M, dtype=jnp.int32), TOP_K)
    gidx = jnp.zeros((P,), jnp.int32).at[pos].set(tok)
    tile_start = jnp.arange(n_tiles, dtype=jnp.int32) * tm
    tile_valid = (tile_start < ends[-1]).astype(jnp.int32)
    te = jnp.sum((tile_start[:, None] >= ends[None, :]).astype(jnp.int32), axis=1)
    last_e = jnp.sum((ends[-1] - 1 >= ends).astype(jnp.int32))
    tile_expert = jnp.minimum(te, last_e).astype(jnp.int32)

    xs = gather_rows(hp, gidx)
    F = w13.shape[-1] // 2
    act = moe_w13(xs, w13, moe_idx, tile_expert, tile_valid, tn=_pick_tile((1024, 512, 256, 128), F))
    y_sorted = moe_w2(act, w2[moe_idx].astype(BF16), tile_expert, tile_valid, tk=_pick_tile((512, 256, 128), F))
    return moe_combine(dm, y_sorted, pos[:M], pos[M:], top_w, x, gate, mod_row0=mod_row0)


def _axial_angles(seq, rot_dim):
    rows = seq // GRID_W
    t_row = jnp.repeat(jnp.arange(rows, dtype=F32), GRID_W)
    t_col = jnp.tile(jnp.arange(GRID_W, dtype=F32), rows)
    quarter = rot_dim // 4
    inv_freq = ROPE_THETA ** (-jnp.arange(quarter, dtype=F32) / quarter)
    return jnp.concatenate([t_row[:, None] * inv_freq, t_col[:, None] * inv_freq], axis=-1)


def _rope_tables(dm):
    ident = min(dm.tm, 256)
    ang = _axial_angles(dm.S, SWA_HEAD_DIM)
    cos_a = jnp.concatenate([jnp.cos(ang), jnp.cos(ang)], axis=-1)
    sin_a = jnp.concatenate([-jnp.sin(ang), jnp.sin(ang)], axis=-1)
    cos_a = jnp.concatenate([jnp.ones((ident, LANE), F32), cos_a], axis=0)
    sin_a = jnp.concatenate([jnp.zeros((ident, LANE), F32), sin_a], axis=0)
    ang = _axial_angles(dm.S, MLA_ROPE)
    half = MLA_ROPE // 2
    zeros = jnp.zeros((dm.S, half), F32)
    pad = jnp.zeros((dm.S, LANE - MLA_ROPE), F32)
    c_b = jnp.concatenate([jnp.cos(ang), jnp.cos(ang), pad], axis=-1)
    s1_b = jnp.concatenate([-jnp.sin(ang), zeros, pad], axis=-1)
    s2_b = jnp.concatenate([zeros, jnp.sin(ang), pad], axis=-1)
    c_b = jnp.concatenate([jnp.ones((ident, LANE), F32), c_b], axis=0)
    s1_b = jnp.concatenate([jnp.zeros((ident, LANE), F32), s1_b], axis=0)
    s2_b = jnp.concatenate([jnp.zeros((ident, LANE), F32), s2_b], axis=0)
    return (cos_a, sin_a), (c_b, s1_b, s2_b)


def _pad_head_vec(g):
    return jnp.zeros((1, MLA_QK_PAD), F32).at[0, :MLA_QK].set(g.astype(F32))


def _trunk(x, c, ctx, c_ctx, mod_w, mod_b, norm_mix_g, norm_ffn_g, w_in,
           swa_q_norm_g, swa_k_norm_g, swa_sink,
           mla_q_a_norm_g, mla_w_uq, mla_kv_a_norm_g, mla_w_ukv, mla_q_norm_g, mla_k_norm_g,
           ssm_lam_re, ssm_lam_im, ssm_log_step, ssm_b_re, ssm_b_im, ssm_c_re, ssm_c_im,
           ssm_d, ssm_w_glu, ssm_b_glu, w_branch, w_out,
           ffn_w13, ffn_w2, moe_w_router, moe_b_router, moe_w13, moe_w2):
    B, S, D = x.shape
    L = ctx.shape[1]
    depth = mod_w.shape[0]
    dm = Dims(B, S, L)
    tm = dm.tm
    RC, RL, R = dm.RC, dm.RL, dm.R
    q_w = SWA_HEADS * SWA_HEAD_DIM
    kv_w = SWA_KV_HEADS * SWA_HEAD_DIM
    q_rank = mla_w_uq.shape[1]
    kv_rank = mla_w_ukv.shape[1]
    ssm_w = ssm_d.shape[1]
    n_gate = N_BRANCH * D
    src = {}
    off = 0
    for name, width in (("q", q_w), ("k", kv_w), ("v", kv_w), ("c_q", q_rank), ("c_kv", kv_rank),
                        ("kpe", MLA_ROPE), ("u", ssm_w), ("gates", n_gate)):
        src[name] = (off, width)
        off += width
    order = ("q", "u", "c_q", "k", "v", "c_kv", "kpe")
    col = {}
    off = 0
    for name in order:
        col[name] = off
        off += src[name][1]
    z_tn = 512
    z_cols = -(-off // z_tn) * z_tn

    (cos_a, sin_a), tabs_b = _rope_tables(dm)
    xall = jnp.concatenate([ctx.reshape(RC, D), x.reshape(RL, D)], axis=0).astype(F32)
    cond = jnp.zeros((8, D), F32).at[0].set(c_ctx.astype(F32)).at[1:1 + B].set(c.astype(F32))

    for layer in range(depth):
        with_ctx = layer < depth - 1
        row0 = 0 if with_ctx else RC
        n_rows = R - row0
        mods = mm1(cond, [(mod_w, (layer,), 0)], _epi_bias, n_rows=8, n_cols=6 * D, tm=8, tn=512, out_dtype=F32,
                   extras=[(mod_b.reshape(depth, 1, 6 * D), (None, 1, 512), lambda j, i: (layer, 0, j))],
                   prologue=lambda a: _silu(a).astype(BF16), name="ada_mod")
        sh_m, sc_m, g_m, sh_f, sc_f, g_f = [mods[:, i * D:(i + 1) * D].reshape(8, 1, D) for i in range(6)]

        h = modulate(dm, xall, norm_mix_g[layer], sh_m, sc_m, mod_row0=0)
        w_l = w_in[layer]
        w_rest = jnp.concatenate([w_l[:, src[n][0]:src[n][0] + src[n][1]] for n in order]
                                 + [jnp.zeros((D, z_cols - off), w_l.dtype)], axis=1)
        w_gates = w_l[:, src["gates"][0]:]
        z = mm1(h, [(w_rest, (), 0)], _epi_id, n_rows=R, n_cols=z_cols, tm=tm, tn=z_tn, out_dtype=F32, name="w_in")
        gates = mm1(h, [(w_gates, (), 0)], _epi_sigmoid, n_rows=R, n_cols=n_gate, tm=tm, tn=512,
                    out_dtype=BF16, name="w_in_gates")

        qa, ka, va = swa_prep(dm, z, col, swa_q_norm_g[layer], swa_k_norm_g[layer], cos_a, sin_a)
        ya_l = swa_attention(dm, qa, ka, va, swa_sink[layer], latent=True)
        cqn, ckvn = mla_lowrank_norm(dm, z, col, q_rank, kv_rank, mla_q_a_norm_g[layer], mla_kv_a_norm_g[layer])
        w_uq = mla_w_uq[layer].reshape(q_rank, MLA_HEADS, MLA_QK)
        w_uq = jnp.pad(w_uq, ((0, 0), (0, 0), (0, MLA_QK_PAD - MLA_QK))).reshape(q_rank, MLA_HEADS * MLA_QK_PAD)
        w_ukv = mla_w_ukv[layer].reshape(kv_rank, MLA_HEADS, MLA_NOPE + MLA_V)
        w_ukv = jnp.concatenate([w_ukv[:, :, :MLA_NOPE].reshape(kv_rank, -1),
                                 w_ukv[:, :, MLA_NOPE:].reshape(kv_rank, -1)], axis=1)
        qf = mm1(cqn, [(w_uq, (), 0)], _epi_id, n_rows=R, n_cols=w_uq.shape[1], tm=tm, tn=512, out_dtype=F32,
                 name="mla_uq")
        kvf = mm1(ckvn, [(w_ukv, (), 0)], _epi_id, n_rows=R, n_cols=w_ukv.shape[1], tm=tm, tn=512, out_dtype=F32,
                  name="mla_ukv")
        gq_pad = _pad_head_vec(mla_q_norm_g[layer])
        gk_pad = _pad_head_vec(mla_k_norm_g[layer])
        qm_c, km_c, vm_c = mla_prep(dm, qf, kvf, z, col, gq_pad, gk_pad, tabs_b, row0=0, n_rows=RC)
        qm_l, km_l, vm_l = mla_prep(dm, qf, kvf, z, col, gq_pad, gk_pad, tabs_b, row0=RC, n_rows=RL)
        yb_l = mla_attention(dm, qm_l, km_c, vm_c, km_l, vm_l)
        if with_ctx:
            ya = jnp.concatenate([swa_attention(dm, qa, ka, va, swa_sink[layer], latent=False), ya_l], axis=0)
            yb = jnp.concatenate([mla_attention(dm, qm_c, km_c, vm_c), yb_l], axis=0)
            y_row0s = (0, 0, 0)
        else:
            ya, yb = ya_l, yb_l
            y_row0s = (RC, RC, 0)
        tables = _ssm_tables(ssm_lam_re[layer], ssm_lam_im[layer], ssm_log_step[layer], ssm_b_re[layer],
                             ssm_b_im[layer], ssm_c_re[layer], ssm_c_im[layer])
        yg = s5_branch(dm, z, col, tables, ssm_d[layer])
        b_glu = ssm_b_glu.reshape(depth, 1, 2 * ssm_w)
        gl_tn = 512
        yc = mm1(yg, [(ssm_w_glu, (layer,), 0), (ssm_w_glu, (layer,), ssm_w)], _epi_glu_bias,
                 n_rows=R, n_cols=ssm_w, tm=tm, tn=gl_tn, out_dtype=BF16,
                 extras=[(b_glu, (None, 1, gl_tn), lambda j, i: (layer, 0, j)),
                         (b_glu, (None, 1, gl_tn), lambda j, i: (layer, 0, j + ssm_w // gl_tn))],
                 prologue=lambda a: a.astype(BF16), name="ssm_glu")
        mixed = merge_branches(dm, (ya, yb, yc), y_row0s, gates, w_branch, layer, row0=row0, n_rows=n_rows)
        t0 = row0 // tm
        x1 = mm1(mixed, [(w_out, (layer,), 0)], _epi_residual, n_rows=n_rows, n_cols=D, tm=tm, tn=512, out_dtype=F32,
                 extras=[(xall, (tm, 512), lambda j, i: (i + t0, j)),
                         (g_m, (1, 1, 512), lambda j, i: (dm.mod_row(i + t0, tm), 0, j))],
                 name="w_out")
        is_moe = layer % 2 == 1
        h2 = modulate(dm, x1, norm_ffn_g[layer], sh_f, sc_f, mod_row0=row0, pack=is_moe)
        if not is_moe:
            F = ffn_w13.shape[-1] // 2
            f_tn = _pick_tile((512, 256, 128), F)
            act = mm1(h2, [(ffn_w13, (layer // 2,), 0), (ffn_w13, (layer // 2,), F)], _epi_swiglu, n_rows=n_rows,
                      n_cols=F, tm=tm, tn=f_tn, out_dtype=BF16, name="ffn_w13")
            x2 = mm2_residual(dm, act, ffn_w2[layer // 2].astype(BF16), x1, g_f, mod_row0=row0, tm=tm, tk=f_tn)
        else:
            if with_ctx:
                raise NotImplementedError("a mixture-of-experts layer that still feeds context rows")
            x2 = moe_ffn(dm, h2, x1, g_f, moe_w_router[layer // 2], moe_b_router[layer // 2], moe_w13, moe_w2,
                         layer // 2, mod_row0=row0)
        xall = x2
    return xall.reshape(B, S, D)


def kernel(x, c, ctx, c_ctx, mod_w, mod_b, norm_mix_g, norm_ffn_g, w_in, swa_q_norm_g, swa_k_norm_g, swa_sink, mla_q_a_norm_g, mla_w_uq, mla_kv_a_norm_g, mla_w_ukv, mla_q_norm_g, mla_k_norm_g, ssm_lam_re, ssm_lam_im, ssm_log_step, ssm_b_re, ssm_b_im, ssm_c_re, ssm_c_im, ssm_d, ssm_w_glu, ssm_b_glu, w_branch, w_out, ffn_w13, ffn_w2, moe_w_router, moe_b_router, moe_w13, moe_w2):
    return _trunk(x, c, ctx, c_ctx, mod_w, mod_b, norm_mix_g, norm_ffn_g, w_in, swa_q_norm_g, swa_k_norm_g, swa_sink,
                  mla_q_a_norm_g, mla_w_uq, mla_kv_a_norm_g, mla_w_ukv, mla_q_norm_g, mla_k_norm_g,
                  ssm_lam_re, ssm_lam_im, ssm_log_step, ssm_b_re, ssm_b_im, ssm_c_re, ssm_c_im,
                  ssm_d, ssm_w_glu, ssm_b_glu, w_branch, w_out, ffn_w13, ffn_w2, moe_w_router, moe_b_router,
                  moe_w13, moe_w2)
```

```python
import functools
import math

import jax
import jax.numpy as jnp
from jax import lax
from jax.experimental import pallas as pl
from jax.experimental.pallas import tpu as pltpu

F32 = jnp.float32
BF16 = jnp.bfloat16

GRID_W = 64
ROPE_THETA = 10000.0
EPS = 1e-6
SWA_HEADS = 8
SWA_KV_HEADS = 2
SWA_HEAD_DIM = 128
SWA_WINDOW = 128
MLA_HEADS = 8
MLA_NOPE = 128
MLA_ROPE = 64
MLA_V = 128
MLA_QK = MLA_NOPE + MLA_ROPE
MLA_QK_PAD = 256
SSM_GROUP = 16
SSM_STATE = 64
SSM_CHUNK = 16
N_BRANCH = 3
N_EXPERTS = 8
TOP_K = 2
LANE = 128
VMEM_LIMIT_BYTES = 56 * 1024 * 1024
MOE_TM = 512
NEG_BIG = -1e30


def _params(n_grid):
    return pltpu.CompilerParams(dimension_semantics=("arbitrary",) * n_grid, vmem_limit_bytes=VMEM_LIMIT_BYTES)


def _pick_tile(candidates, *sizes):
    for t in candidates:
        if all(s % t == 0 for s in sizes):
            return t
    raise ValueError(f"no tile in {candidates} divides {sizes}")


class Dims:
    def __init__(self, batch, seq, ctx_len):
        self.B, self.S, self.L = batch, seq, ctx_len
        self.RC = batch * ctx_len
        self.RL = batch * seq
        self.R = self.RC + self.RL
        self.tm = _pick_tile((1024, 512, 256, 128), ctx_len * batch, seq)

    def mod_row(self, tile, tm):
        nct = self.RC // tm
        return jnp.where(tile < nct, 0, 1 + (tile - nct) // (self.S // tm))


def _silu(x):
    return x * (1.0 / (1.0 + jnp.exp(-x)))


def _sigmoid(x):
    return 1.0 / (1.0 + jnp.exp(-x))


def _gelu_tanh(x):
    c = math.sqrt(2.0 / math.pi)
    return 0.5 * x * (1.0 + jnp.tanh(c * (x + 0.044715 * (x * x * x))))


def _dot(a, b):
    return jnp.dot(a, b, preferred_element_type=F32)


def _dot_nt(a, b):
    return lax.dot_general(a, b, (((1,), (1,)), ((), ())), preferred_element_type=F32)


def _pack_bf16_pairs(y):
    half = y.shape[1] // 2
    bits = lax.bitcast_convert_type(y.astype(BF16).astype(F32), jnp.uint32)
    return (bits[:, :half] >> 16) | (bits[:, half:] & jnp.uint32(0xFFFF0000))


def _unpack_bf16_pairs(p):
    lo = lax.bitcast_convert_type(p << 16, F32).astype(BF16)
    hi = lax.bitcast_convert_type(p & jnp.uint32(0xFFFF0000), F32).astype(BF16)
    return lo, hi


def _store_row_tiles(ref, val):
    m, w = val.shape
    n = w // LANE
    for c in range(n):
        ref[pl.ds(c, m, stride=n), :] = val[:, c * LANE:(c + 1) * LANE]


def _load_row_tiles(ref, n):
    m = ref.shape[0] // n
    return jnp.concatenate([ref[pl.ds(c, m, stride=n), :] for c in range(n)], axis=1)


def _modulate_kernel(x_ref, g_ref, sh_ref, sc_ref, o_ref, *, pack):
    x = x_ref[...]
    ms = jnp.mean(x * x, axis=-1, keepdims=True)
    y = x * lax.rsqrt(ms + EPS) * g_ref[...]
    y = y * (1.0 + sc_ref[0]) + sh_ref[0]
    if pack:
        _store_row_tiles(o_ref, _pack_bf16_pairs(y))
    else:
        o_ref[...] = y.astype(o_ref.dtype)


def modulate(dm, x, g, shift, scale, *, mod_row0, pack=False):
    n_rows, D = x.shape
    tm = min(dm.tm, 512)
    t0 = mod_row0 // tm
    if pack:
        n = D // 2 // LANE
        out_spec = pl.BlockSpec((tm * n, LANE), lambda i: (i, 0))
        out_shape = jax.ShapeDtypeStruct((n_rows * n, LANE), jnp.uint32)
    else:
        out_spec = pl.BlockSpec((tm, D), lambda i: (i, 0))
        out_shape = jax.ShapeDtypeStruct((n_rows, D), BF16)
    return pl.pallas_call(
        functools.partial(_modulate_kernel, pack=pack),
        grid=(n_rows // tm,),
        in_specs=[
            pl.BlockSpec((tm, D), lambda i: (i, 0)),
            pl.BlockSpec((1, D), lambda i: (0, 0)),
            pl.BlockSpec((1, 1, D), lambda i: (dm.mod_row(i + t0, tm), 0, 0)),
            pl.BlockSpec((1, 1, D), lambda i: (dm.mod_row(i + t0, tm), 0, 0)),
        ],
        out_specs=out_spec,
        out_shape=out_shape,
        compiler_params=_params(1),
        name="modulate_packed" if pack else "modulate",
    )(x, g.reshape(1, D), shift, scale)


def _mm1_kernel(*refs, n_w, n_extra, epilogue, prologue):
    a_ref = refs[0]
    w_refs = refs[1:1 + n_w]
    extra = refs[1 + n_w:1 + n_w + n_extra]
    o_ref = refs[1 + n_w + n_extra]
    wb = refs[2 + n_w + n_extra:]

    @pl.when(pl.program_id(1) == 0)
    def _():
        for w_ref, b in zip(w_refs, wb):
            b[...] = w_ref[...].astype(BF16)

    a = a_ref[...]
    if prologue is not None:
        a = prologue(a)
    accs = [_dot(a, b[...]) for b in wb]
    o_ref[...] = epilogue(accs, *extra).astype(o_ref.dtype)


def mm1(a, weights, epilogue, *, n_rows, n_cols, tm, tn, out_dtype, a_row0=0, extras=(), prologue=None, name):
    K = a.shape[1]
    t0 = a_row0 // tm
    in_specs = [pl.BlockSpec((tm, K), lambda j, i: (i + t0, 0))]
    operands = [a]
    for w, lead, col0 in weights:
        c0 = col0 // tn
        in_specs.append(pl.BlockSpec((None,) * len(lead) + (K, tn),
                                     functools.partial(lambda j, i, lead, c0: lead + (0, j + c0), lead=lead, c0=c0)))
        operands.append(w)
    for arr, bshape, imap in extras:
        in_specs.append(pl.BlockSpec(bshape, imap))
        operands.append(arr)
    kern = functools.partial(_mm1_kernel, n_w=len(weights), n_extra=len(extras), epilogue=epilogue,
                             prologue=prologue)
    return pl.pallas_call(
        kern,
        grid=(n_cols // tn, n_rows // tm),
        in_specs=in_specs,
        out_specs=pl.BlockSpec((tm, tn), lambda j, i: (i, j)),
        out_shape=jax.ShapeDtypeStruct((n_rows, n_cols), out_dtype),
        scratch_shapes=[pltpu.VMEM((K, tn), BF16) for _ in weights],
        compiler_params=_params(2),
        name=name,
    )(*operands)


def _epi_id(accs):
    return accs[0]


def _epi_sigmoid(accs):
    return _sigmoid(accs[0])


def _epi_swiglu(accs):
    return _silu(accs[0]) * accs[1]


def _epi_bias(accs, b_ref):
    return accs[0] + b_ref[...]


def _epi_glu_bias(accs, ba_ref, bb_ref):
    return (accs[0] + ba_ref[...]) * _sigmoid(accs[1] + bb_ref[...])


def _epi_residual(accs, x_ref, gate_ref):
    return x_ref[...] + gate_ref[0] * accs[0]


def _mm2_kernel(a_ref, w_ref, x_ref, gate_ref, o_ref, acc_ref):
    k = pl.program_id(1)

    @pl.when(k == 0)
    def _():
        acc_ref[...] = jnp.zeros_like(acc_ref)

    acc_ref[...] += _dot(a_ref[...], w_ref[...])

    @pl.when(k == pl.num_programs(1) - 1)
    def _():
        o_ref[...] = x_ref[...] + gate_ref[0] * acc_ref[...]


def mm2_residual(dm, a, w, x, gate, *, mod_row0, tm, tk):
    M, K = a.shape
    N = w.shape[1]
    t0 = mod_row0 // tm
    return pl.pallas_call(
        _mm2_kernel,
        grid=(M // tm, K // tk),
        in_specs=[
            pl.BlockSpec((tm, tk), lambda i, k: (i, k)),
            pl.BlockSpec((tk, N), lambda i, k: (k, 0)),
            pl.BlockSpec((tm, N), lambda i, k: (i, 0)),
            pl.BlockSpec((1, 1, N), lambda i, k: (dm.mod_row(i + t0, tm), 0, 0)),
        ],
        out_specs=pl.BlockSpec((tm, N), lambda i, k: (i, 0)),
        out_shape=jax.ShapeDtypeStruct((M, N), F32),
        scratch_shapes=[pltpu.VMEM((tm, N), F32)],
        compiler_params=_params(2),
        name="mm2_residual",
    )(a, w, x, gate)


def _swa_prep_kernel(q_ref, k_ref, v_ref, gq_ref, gk_ref, cos_ref, sin_ref, qo_ref, ko_ref, vo_ref):
    c = cos_ref[...]
    s = sin_ref[...]

    def norm_rope(x, g, scale):
        ms = jnp.mean(x * x, axis=-1, keepdims=True)
        y = x * lax.rsqrt(ms + EPS) * g
        return (y * c + pltpu.roll(y, SWA_HEAD_DIM // 2, 1) * s) * scale

    gq = gq_ref[...]
    gk = gk_ref[...]
    for h in range(SWA_HEADS):
        sl = slice(h * SWA_HEAD_DIM, (h + 1) * SWA_HEAD_DIM)
        qo_ref[:, sl] = norm_rope(q_ref[:, sl], gq, SWA_HEAD_DIM ** -0.5).astype(BF16)
    for h in range(SWA_KV_HEADS):
        sl = slice(h * SWA_HEAD_DIM, (h + 1) * SWA_HEAD_DIM)
        ko_ref[:, sl] = norm_rope(k_ref[:, sl], gk, 1.0).astype(BF16)
    vo_ref[...] = v_ref[...].astype(BF16)


def _rope_tile_index(dm, tm):
    nct = dm.RC // tm
    return lambda i: (jnp.where(i < nct, 0, 1 + (i - nct) % (dm.S // tm)), 0)


def swa_prep(dm, z, col, gq, gk, cos_t, sin_t):
    tm = min(dm.tm, 256)
    QW = SWA_HEADS * SWA_HEAD_DIM
    KW = SWA_KV_HEADS * SWA_HEAD_DIM
    ridx = _rope_tile_index(dm, tm)
    return pl.pallas_call(
        _swa_prep_kernel,
        grid=(dm.R // tm,),
        in_specs=[
            pl.BlockSpec((tm, QW), lambda i: (i, col["q"] // QW)),
            pl.BlockSpec((tm, KW), lambda i: (i, col["k"] // KW)),
            pl.BlockSpec((tm, KW), lambda i: (i, col["v"] // KW)),
            pl.BlockSpec((1, SWA_HEAD_DIM), lambda i: (0, 0)),
            pl.BlockSpec((1, SWA_HEAD_DIM), lambda i: (0, 0)),
            pl.BlockSpec((tm, SWA_HEAD_DIM), ridx),
            pl.BlockSpec((tm, SWA_HEAD_DIM), ridx),
        ],
        out_specs=[
            pl.BlockSpec((tm, QW), lambda i: (i, 0)),
            pl.BlockSpec((tm, KW), lambda i: (i, 0)),
            pl.BlockSpec((tm, KW), lambda i: (i, 0)),
        ],
        out_shape=[
            jax.ShapeDtypeStruct((dm.R, QW), BF16),
            jax.ShapeDtypeStruct((dm.R, KW), BF16),
            jax.ShapeDtypeStruct((dm.R, KW), BF16),
        ],
        compiler_params=_params(1),
        name="swa_prep",
    )(z, z, z, gq.reshape(1, -1), gk.reshape(1, -1), cos_t, sin_t)


def _swa_attn_kernel(*refs, windowed, nb):
    if windowed:
        q_ref, kc_ref, kp_ref, kk_ref, kn_ref, vc_ref, vp_ref, vk_ref, vn_ref, sink_ref, o_ref = refs
    else:
        q_ref, kc_ref, vc_ref, sink_ref, o_ref = refs
    G = SWA_HEADS // SWA_KV_HEADS
    blk = q_ref.shape[0]
    Dh = SWA_HEAD_DIM
    q = jnp.concatenate([q_ref[:, g * Dh:(g + 1) * Dh] for g in range(G)], axis=0)
    sink = sink_ref[0][:, 0:1]
    scores = [_dot_nt(q, kc_ref[...])]
    values = [vc_ref[...]]
    if windowed:
        n = pl.program_id(2)
        qi = lax.broadcasted_iota(jnp.int32, (G * blk, blk), 0) % blk
        kj = lax.broadcasted_iota(jnp.int32, (G * blk, blk), 1)
        s_p = _dot_nt(q, kp_ref[...])
        s_p = jnp.where(kj >= qi, s_p, NEG_BIG)
        s_p = jnp.where(n >= 1, s_p, NEG_BIG)
        s_n = _dot_nt(q, kn_ref[...])
        s_n = jnp.where(kj <= qi, s_n, NEG_BIG)
        s_n = jnp.where(n <= nb - 2, s_n, NEG_BIG)
        scores += [s_p, _dot_nt(q, kk_ref[...]), s_n]
        values += [vp_ref[...], vk_ref[...], vn_ref[...]]
    m = sink
    for s in scores:
        m = jnp.maximum(m, jnp.max(s, axis=-1, keepdims=True))
    l = jnp.exp(sink - m)
    o = None
    for s, v in zip(scores, values):
        p = jnp.exp(s - m)
        l = l + jnp.sum(p, axis=-1, keepdims=True)
        pv = _dot(p.astype(BF16), v)
        o = pv if o is None else o + pv
    o = o / l
    for g in range(G):
        o_ref[:, g * Dh:(g + 1) * Dh] = o[g * blk:(g + 1) * blk].astype(o_ref.dtype)


def swa_attention(dm, qa, ka, va, sink, *, latent):
    G = SWA_HEADS // SWA_KV_HEADS
    Dh = SWA_HEAD_DIM
    blk = SWA_WINDOW
    L = dm.L
    sink_col = jnp.broadcast_to(sink.astype(F32).reshape(SWA_KV_HEADS, G, 1, 1),
                                (SWA_KV_HEADS, G, blk, LANE)).reshape(SWA_KV_HEADS, G * blk, LANE)
    sink_spec = pl.BlockSpec((1, G * blk, LANE), lambda b, h, n: (h, 0, 0))
    ctx_spec = pl.BlockSpec((L, Dh), lambda b, h, n: (b, h))
    if latent:
        nb = dm.S // blk
        base = dm.RC // blk

        def q_map(b, h, n):
            return (base + b * nb + n, h)

        def kv_map(off):
            return lambda b, h, n: (base + b * nb + jnp.clip(n + off, 0, nb - 1), h)

        win_specs = [pl.BlockSpec((blk, Dh), kv_map(off)) for off in (-1, 0, 1)]
        in_specs = ([pl.BlockSpec((blk, G * Dh), q_map), ctx_spec] + win_specs + [ctx_spec] + win_specs
                    + [sink_spec])
        operands = (qa, ka, ka, ka, ka, va, va, va, va, sink_col)
        n_out = dm.RL
    else:
        nb = L // blk
        in_specs = [pl.BlockSpec((blk, G * Dh), lambda b, h, n: (b * nb + n, h)), ctx_spec, ctx_spec, sink_spec]
        operands = (qa, ka, va, sink_col)
        n_out = dm.RC
    return pl.pallas_call(
        functools.partial(_swa_attn_kernel, windowed=latent, nb=nb),
        grid=(dm.B, SWA_KV_HEADS, nb),
        in_specs=in_specs,
        out_specs=pl.BlockSpec((blk, G * Dh), lambda b, h, n: (b * nb + n, h)),
        out_shape=jax.ShapeDtypeStruct((n_out, SWA_HEADS * Dh), BF16),
        compiler_params=_params(3),
        name="swa_attn_lat" if latent else "swa_attn_ctx",
    )(*operands)


def _rms_rows_kernel(a_ref, b_ref, ga_ref, gb_ref, ao_ref, bo_ref):
    for x_ref, g_ref, o_ref in ((a_ref, ga_ref, ao_ref), (b_ref, gb_ref, bo_ref)):
        x = x_ref[...]
        ms = jnp.mean(x * x, axis=-1, keepdims=True)
        o_ref[...] = (x * lax.rsqrt(ms + EPS) * g_ref[...]).astype(o_ref.dtype)


def mla_lowrank_norm(dm, z, col, q_rank, kv_rank, gq, gkv):
    tm = min(dm.tm, 512)
    return pl.pallas_call(
        _rms_rows_kernel,
        grid=(dm.R // tm,),
        in_specs=[
            pl.BlockSpec((tm, q_rank), lambda i: (i, col["c_q"] // q_rank)),
            pl.BlockSpec((tm, kv_rank), lambda i: (i, col["c_kv"] // kv_rank)),
            pl.BlockSpec((1, q_rank), lambda i: (0, 0)),
            pl.BlockSpec((1, kv_rank), lambda i: (0, 0)),
        ],
        out_specs=[pl.BlockSpec((tm, q_rank), lambda i: (i, 0)), pl.BlockSpec((tm, kv_rank), lambda i: (i, 0))],
        out_shape=[jax.ShapeDtypeStruct((dm.R, q_rank), BF16), jax.ShapeDtypeStruct((dm.R, kv_rank), BF16)],
        compiler_params=_params(1),
        name="mla_lowrank_norm",
    )(z, z, gq.reshape(1, -1), gkv.reshape(1, -1))


def _mla_prep_kernel(q_ref, kn_ref, v_ref, pe_ref, gq_ref, gk_ref, c_ref, s1_ref, s2_ref, qo_ref, ko_ref, vo_ref):
    c = c_ref[...]
    s1 = s1_ref[...]
    s2 = s2_ref[...]
    gq = gq_ref[...]
    gk = gk_ref[...]
    scale = MLA_QK ** -0.5

    def rope(x):
        return x * c + pltpu.roll(x, LANE - MLA_ROPE // 2, 1) * s1 + pltpu.roll(x, MLA_ROPE // 2, 1) * s2

    pe = pe_ref[...]
    pe_ss = jnp.sum(pe * pe, axis=-1, keepdims=True)
    for h in range(MLA_HEADS):
        lo = h * MLA_QK_PAD
        qh = q_ref[:, lo:lo + MLA_QK_PAD]
        inv = lax.rsqrt(jnp.sum(qh * qh, axis=-1, keepdims=True) * (1.0 / MLA_QK) + EPS)
        qn = qh * inv * gq
        qo_ref[:, lo:lo + MLA_NOPE] = (qn[:, :MLA_NOPE] * scale).astype(BF16)
        qo_ref[:, lo + MLA_NOPE:lo + MLA_QK_PAD] = (rope(qn[:, MLA_NOPE:]) * scale).astype(BF16)
        kh = kn_ref[:, h * MLA_NOPE:(h + 1) * MLA_NOPE]
        inv = lax.rsqrt((jnp.sum(kh * kh, axis=-1, keepdims=True) + pe_ss) * (1.0 / MLA_QK) + EPS)
        ko_ref[:, lo:lo + MLA_NOPE] = (kh * inv * gk[:, :MLA_NOPE]).astype(BF16)
        ko_ref[:, lo + MLA_NOPE:lo + MLA_QK_PAD] = rope(pe * inv * gk[:, MLA_NOPE:]).astype(BF16)
    vo_ref[...] = v_ref[...].astype(BF16)


def mla_prep(dm, qf, kvf, z, col, gq_pad, gk_pad, tabs, *, row0, n_rows):
    tm = min(dm.tm, 256)
    t0 = row0 // tm
    QW = MLA_HEADS * MLA_QK_PAD
    NW = MLA_HEADS * MLA_NOPE
    ridx = _rope_tile_index(dm, tm)
    rspec = pl.BlockSpec((tm, LANE), lambda i: ridx(i + t0))
    return pl.pallas_call(
        _mla_prep_kernel,
        grid=(n_rows // tm,),
        in_specs=[
            pl.BlockSpec((tm, QW), lambda i: (i + t0, 0)),
            pl.BlockSpec((tm, NW), lambda i: (i + t0, 0)),
            pl.BlockSpec((tm, NW), lambda i: (i + t0, 1)),
            pl.BlockSpec((tm, LANE), lambda i: (i + t0, col["kpe"] // LANE)),
            pl.BlockSpec((1, MLA_QK_PAD), lambda i: (0, 0)),
            pl.BlockSpec((1, MLA_QK_PAD), lambda i: (0, 0)),
            rspec, rspec, rspec,
        ],
        out_specs=[
            pl.BlockSpec((tm, QW), lambda i: (i, 0)),
            pl.BlockSpec((tm, QW), lambda i: (i, 0)),
            pl.BlockSpec((tm, NW), lambda i: (i, 0)),
        ],
        out_shape=[
            jax.ShapeDtypeStruct((n_rows, QW), BF16),
            jax.ShapeDtypeStruct((n_rows, QW), BF16),
            jax.ShapeDtypeStruct((n_rows, NW), BF16),
        ],
        compiler_params=_params(1),
        name="mla_prep",
    )(qf, kvf, kvf, z, gq_pad, gk_pad, *tabs)


def _mla_attn_kernel(*refs, with_latent):
    if with_latent:
        q_ref, kc_ref, kl_ref, vc_ref, vl_ref, o_ref = refs
    else:
        q_ref, kc_ref, vc_ref, o_ref = refs
    q = q_ref[...]
    s_c = _dot_nt(q, kc_ref[...])
    m = jnp.max(s_c, axis=-1, keepdims=True)
    if with_latent:
        s_l = _dot_nt(q, kl_ref[...])
        m = jnp.maximum(m, jnp.max(s_l, axis=-1, keepdims=True))
    p_c = jnp.exp(s_c - m)
    l = jnp.sum(p_c, axis=-1, keepdims=True)
    o = _dot(p_c.astype(BF16), vc_ref[...])
    if with_latent:
        p_l = jnp.exp(s_l - m)
        l = l + jnp.sum(p_l, axis=-1, keepdims=True)
        o = o + _dot(p_l.astype(BF16), vl_ref[...])
    o_ref[...] = (o / l).astype(o_ref.dtype)


def mla_attention(dm, q, kc, vc, kl=None, vl=None):
    with_latent = kl is not None
    n_q = dm.S if with_latent else dm.L
    tq = min(256, n_q)
    nq = n_q // tq
    QP, V = MLA_QK_PAD, MLA_V
    in_specs = [pl.BlockSpec((tq, QP), lambda b, h, n: (b * nq + n, h)),
                pl.BlockSpec((dm.L, QP), lambda b, h, n: (b, h))]
    operands = [q, kc]
    if with_latent:
        in_specs.append(pl.BlockSpec((dm.S, QP), lambda b, h, n: (b, h)))
        operands.append(kl)
    in_specs.append(pl.BlockSpec((dm.L, V), lambda b, h, n: (b, h)))
    operands.append(vc)
    if with_latent:
        in_specs.append(pl.BlockSpec((dm.S, V), lambda b, h, n: (b, h)))
        operands.append(vl)
    return pl.pallas_call(
        functools.partial(_mla_attn_kernel, with_latent=with_latent),
        grid=(dm.B, MLA_HEADS, nq),
        in_specs=in_specs,
        out_specs=pl.BlockSpec((tq, V), lambda b, h, n: (b * nq + n, h)),
        out_shape=jax.ShapeDtypeStruct((dm.B * n_q, MLA_HEADS * V), BF16),
        compiler_params=_params(3),
        name="mla_attn_lat" if with_latent else "mla_attn_ctx",
    )(*operands)


SSM_SUPER = 16
SSM_BLOCK_GROUPS = LANE // SSM_GROUP


def _ssm_in_kernel(u_ref, wi_ref, ws_ref, y_ref, s_ref):
    a = _load_row_tiles(u_ref, SSM_CHUNK).astype(BF16)
    y_ref[...] = _dot(a, wi_ref[...])
    zs = _dot(a, ws_ref[...])
    for c in range(s_ref.shape[0]):
        s_ref[c] = zs[:, c * LANE:(c + 1) * LANE]


def ssm_chunk_in(z, u_col, w_intra, w_state, *, tr):
    R = z.shape[0]
    nblk, CW, _ = w_intra.shape
    nr = tr // SSM_CHUNK
    c0 = u_col // LANE
    w_spec = pl.BlockSpec((None, CW, CW), lambda j, i: (j, 0, 0), pipeline_mode=pl.Buffered(1))
    return pl.pallas_call(
        _ssm_in_kernel,
        grid=(nblk, R // tr),
        in_specs=[pl.BlockSpec((tr, LANE), lambda j, i: (i, c0 + j)), w_spec, w_spec],
        out_specs=[pl.BlockSpec((nr, CW), lambda j, i: (i, j)),
                   pl.BlockSpec((2 * SSM_BLOCK_GROUPS, nr, LANE), lambda j, i: (0, i, j))],
        out_shape=[jax.ShapeDtypeStruct((R // SSM_CHUNK, nblk * CW), F32),
                   jax.ShapeDtypeStruct((2 * SSM_BLOCK_GROUPS, R // SSM_CHUNK, nblk * LANE), F32)],
        compiler_params=_params(2),
        name="ssm_chunk_in",
    )(z, w_intra, w_state)


def _ssm_scan_kernel(s_ref, p1_ref, p2_ref, x_ref, t_ref, e_ref, *, batch, n_ctx_sc, n_lat_sc):
    SC = SSM_SUPER
    GB = SSM_BLOCK_GROUPS
    FWD, BWD = slice(0, GB), slice(GB, 2 * GB)
    n_sc = s_ref.shape[1] // SC

    def cmul(i, rows, x):
        swapped = jnp.concatenate([x[..., SSM_STATE:], x[..., :SSM_STATE]], axis=-1)
        return p1_ref[i, rows] * x + p2_ref[i, rows] * swapped

    def chunk(i):
        return pl.ds(i, n_sc, stride=SC)

    lf = jnp.zeros((GB, n_sc, LANE), F32)
    lb = jnp.zeros((GB, n_sc, LANE), F32)
    for i in range(SC):
        x_ref[FWD, chunk(i), :] = lf
        x_ref[BWD, chunk(SC - 1 - i), :] = lb
        lf = cmul(1, FWD, lf) + s_ref[FWD, chunk(i), :]
        lb = cmul(1, BWD, lb) + s_ref[BWD, chunk(SC - 1 - i), :]
    t_ref[FWD] = lf
    t_ref[BWD] = lb
    n_ctx = batch * n_ctx_sc
    for rows, order in ((FWD, 1), (BWD, -1)):
        e = jnp.zeros((GB, batch, LANE), F32)
        for region_start, per_batch in ((0, n_ctx_sc), (n_ctx, n_lat_sc)):
            steps = range(per_batch) if order == 1 else range(per_batch - 1, -1, -1)
            for m in steps:
                idx = pl.ds(region_start + m, batch, stride=per_batch)
                e_ref[rows, idx, :] = e
                e = cmul(SC, rows, e) + t_ref[rows, idx, :]
    ef = e_ref[FWD]
    eb = e_ref[BWD]
    for i in range(SC):
        x_ref[FWD, chunk(i), :] += cmul(i, FWD, ef)
        x_ref[BWD, chunk(SC - 1 - i), :] += cmul(i, BWD, eb)


def ssm_scan(dm, s, p1, p2):
    NS, NR, W = s.shape
    nblk = W // LANE
    n_sc = NR // SSM_SUPER
    n_ctx_sc = dm.L // (SSM_CHUNK * SSM_SUPER)
    n_lat_sc = dm.S // (SSM_CHUNK * SSM_SUPER)
    blk = pl.BlockSpec((NS, NR, LANE), lambda j: (0, 0, j))
    pspec = pl.BlockSpec((None, SSM_SUPER + 1, NS, 1, LANE), lambda j: (j, 0, 0, 0, 0))
    return pl.pallas_call(
        functools.partial(_ssm_scan_kernel, batch=dm.B, n_ctx_sc=n_ctx_sc, n_lat_sc=n_lat_sc),
        grid=(nblk,),
        in_specs=[blk, pspec, pspec],
        out_specs=blk,
        out_shape=jax.ShapeDtypeStruct(s.shape, F32),
        scratch_shapes=[pltpu.VMEM((NS, n_sc, LANE), F32), pltpu.VMEM((NS, n_sc, LANE), F32)],
        compiler_params=_params(1),
        name="ssm_scan",
    )(s, p1, p2)


def _ssm_out_kernel(y_ref, x_ref, u_ref, w_ref, d_ref, o_ref):
    nr = y_ref.shape[0]
    xs = jnp.concatenate([x_ref[c] for c in range(x_ref.shape[0])], axis=1).astype(BF16)
    y = y_ref[...] + _dot(xs, w_ref[...])
    d = d_ref[...]
    for t in range(SSM_CHUNK):
        rows = pl.ds(t, nr, stride=SSM_CHUNK)
        o_ref[rows, :] = _gelu_tanh(y[:, t * LANE:(t + 1) * LANE] + d * u_ref[rows, :])


def ssm_chunk_out(y_intra, x_states, z, u_col, w_out_state, d_skip, *, tr):
    R = z.shape[0]
    nblk, CW, _ = w_out_state.shape
    nr = tr // SSM_CHUNK
    c0 = u_col // LANE
    return pl.pallas_call(
        _ssm_out_kernel,
        grid=(nblk, R // tr),
        in_specs=[pl.BlockSpec((nr, CW), lambda j, i: (i, j)),
                  pl.BlockSpec((2 * SSM_BLOCK_GROUPS, nr, LANE), lambda j, i: (0, i, j)),
                  pl.BlockSpec((tr, LANE), lambda j, i: (i, c0 + j)),
                  pl.BlockSpec((None, CW, CW), lambda j, i: (j, 0, 0), pipeline_mode=pl.Buffered(1)),
                  pl.BlockSpec((1, LANE), lambda j, i: (0, j))],
        out_specs=pl.BlockSpec((tr, LANE), lambda j, i: (i, j)),
        out_shape=jax.ShapeDtypeStruct((R, nblk * LANE), F32),
        compiler_params=_params(2),
        name="ssm_chunk_out",
    )(y_intra, x_states, z, w_out_state, d_skip.astype(F32).reshape(1, -1))


def _ssm_expand_kernel(k_ref, o_ref, *, mode):
    C, H, GB = SSM_CHUNK, SSM_GROUP, SSM_BLOCK_GROUPS
    CH = C * H
    W = o_ref.shape[1]
    ri = lax.broadcasted_iota(jnp.int32, (CH, W), 0)
    ci = lax.broadcasted_iota(jnp.int32, (CH, W), 1)
    for gl in range(GB):
        kc = k_ref[gl].astype(BF16)
        if mode == "state":
            zero = jnp.zeros((CH, LANE), BF16)
            cols = [kc[:, d * LANE:(d + 1) * LANE] if g2 == gl else zero for d in range(2) for g2 in range(GB)]
            t = jnp.concatenate(cols, axis=1)
        else:
            sel = jnp.where(ci == (ri >> 4) * LANE + gl * H + (ri & (H - 1)), 1.0, 0.0).astype(BF16)
            t = _dot(kc, sel).astype(BF16)
        if mode == "out":
            for d in range(2):
                o_ref[d * GB * LANE + gl * LANE:d * GB * LANE + (gl + 1) * LANE, :] = t[d * LANE:(d + 1) * LANE, :]
        else:
            for s in range(C):
                o_ref[s * LANE + gl * H:s * LANE + (gl + 1) * H, :] = t[s * H:(s + 1) * H, :]


def _ssm_expand(compact, mode):
    G, CH, _ = compact.shape
    GB = SSM_BLOCK_GROUPS
    CW = CH * GB
    return pl.pallas_call(
        functools.partial(_ssm_expand_kernel, mode=mode),
        grid=(G // GB,),
        in_specs=[pl.BlockSpec((GB, CH, CH), lambda j: (j, 0, 0))],
        out_specs=pl.BlockSpec((None, CW, CW), lambda j: (j, 0, 0)),
        out_shape=jax.ShapeDtypeStruct((G // GB, CW, CW), BF16),
        compiler_params=_params(1),
        name="ssm_expand_" + mode,
    )(compact)


def _ssm_tables(lam_re, lam_im, log_step, b_re, b_im, c_re, c_im):
    C, H, P = SSM_CHUNK, SSM_GROUP, SSM_STATE
    G = lam_re.shape[1]
    delta = jnp.exp(log_step.astype(F32))[..., None]
    zr = lam_re.astype(F32) * delta
    zi = lam_im.astype(F32) * delta
    k = jnp.arange(C + 1, dtype=F32)[:, None, None, None]
    mag = jnp.exp(k * zr[None])
    pw_re = mag * jnp.cos(k * zi[None])
    pw_im = mag * jnp.sin(k * zi[None])
    lb_re, lb_im = pw_re[1], pw_im[1]
    lr, li = lam_re.astype(F32), lam_im.astype(F32)
    den = lr * lr + li * li
    f_re = ((lb_re - 1.0) * lr + lb_im * li) / den
    f_im = (lb_im * lr - (lb_re - 1.0) * li) / den
    br, bi = b_re.astype(F32), b_im.astype(F32)
    bb_re = f_re[..., None] * br - f_im[..., None] * bi
    bb_im = f_re[..., None] * bi + f_im[..., None] * br
    cr, ci = c_re.astype(F32), c_im.astype(F32)
    cl_re = cr[None] * pw_re[:, :, :, None, :] - ci[None] * pw_im[:, :, :, None, :]
    cl_im = cr[None] * pw_im[:, :, :, None, :] + ci[None] * pw_re[:, :, :, None, :]
    hp = lax.Precision.HIGHEST
    kern = (jnp.einsum("kdghp,dgpj->dgkhj", cl_re[:C], bb_re, precision=hp)
            - jnp.einsum("kdghp,dgpj->dgkhj", cl_im[:C], bb_im, precision=hp))
    k_idx = jnp.arange(C)[:, None, None]
    s_idx = jnp.arange(C)[None, :, None]
    t_idx = jnp.arange(C)[None, None, :]
    sel_f = (t_idx - s_idx == k_idx).astype(F32)
    sel_b = (s_idx - t_idx == k_idx).astype(F32)
    ksum = (jnp.einsum("kst,gkhj->gsjth", sel_f, kern[0], precision=hp)
            + jnp.einsum("kst,gkhj->gsjth", sel_b, kern[1], precision=hp))

    def state_in(d, power_of_s):
        pr = pw_re[power_of_s, d]
        pi = pw_im[power_of_s, d]
        re = pr[..., None] * bb_re[d][None] - pi[..., None] * bb_im[d][None]
        im = pr[..., None] * bb_im[d][None] + pi[..., None] * bb_re[d][None]
        return jnp.concatenate([re, im], axis=2).transpose(1, 0, 3, 2)

    def state_out(d, power_of_t):
        re = cl_re[power_of_t, d]
        im = cl_im[power_of_t, d]
        return jnp.concatenate([re, -im], axis=-1).transpose(1, 3, 0, 2)

    m_sum = jnp.stack([state_in(0, C - 1 - jnp.arange(C)), state_in(1, jnp.arange(C))])
    m_out = jnp.stack([state_out(0, 1 + jnp.arange(C)), state_out(1, C - jnp.arange(C))])

    GB = SSM_BLOCK_GROUPS
    nblk = G // GB
    w_intra = _ssm_expand(ksum.reshape(G, C * H, C * H), "intra")
    w_state = _ssm_expand(m_sum.transpose(1, 2, 3, 0, 4).reshape(G, C * H, 2 * 2 * P), "state")
    w_out_state = _ssm_expand(m_out.transpose(1, 0, 2, 3, 4).reshape(G, 2 * 2 * P, C * H), "out")
    i = (C * jnp.arange(SSM_SUPER + 1, dtype=F32))[:, None, None, None]
    mag_a = jnp.exp(i * zr[None])
    pa_re = mag_a * jnp.cos(i * zi[None])
    pa_im = mag_a * jnp.sin(i * zi[None])

    def scan_table(lo, hi):
        t = jnp.concatenate([lo, hi], axis=-1).reshape(SSM_SUPER + 1, 2, nblk, GB, 2 * P)
        return t.transpose(2, 0, 1, 3, 4).reshape(nblk, SSM_SUPER + 1, 2 * GB, 1, 2 * P)

    return w_intra, w_state, w_out_state, scan_table(pa_re, pa_re), scan_table(-pa_im, pa_im)


def s5_branch(dm, z, col, tables, d_skip):
    w_intra, w_state, w_out_state, p1, p2 = tables
    nr = _pick_tile((272, 136, 96, 64, 32, 16, 8), dm.R // SSM_CHUNK)
    tr = nr * SSM_CHUNK
    y_intra, s = ssm_chunk_in(z, col["u"], w_intra, w_state, tr=tr)
    x_states = ssm_scan(dm, s, p1, p2)
    return ssm_chunk_out(y_intra, x_states, z, col["u"], w_out_state, d_skip, tr=tr)


def _merge_kernel(ya_ref, yb_ref, yc_ref, ga_ref, gb_ref, gc_ref, wa_ref, wb_ref, wc_ref, o_ref, sa, sb, sc):
    @pl.when(pl.program_id(1) == 0)
    def _():
        for w_ref, s in ((wa_ref, sa), (wb_ref, sb), (wc_ref, sc)):
            s[...] = w_ref[...].astype(BF16)

    acc = ga_ref[...].astype(F32) * _dot(ya_ref[...], sa[...])
    acc = acc + gb_ref[...].astype(F32) * _dot(yb_ref[...], sb[...])
    acc = acc + gc_ref[...].astype(F32) * _dot(yc_ref[...], sc[...])
    o_ref[...] = acc.astype(o_ref.dtype)


def merge_branches(dm, ys, y_row0s, gates, w_branch, layer, *, row0, n_rows):
    BW = ys[0].shape[1]
    D = w_branch.shape[-1]
    tm = dm.tm
    tn = 512
    t0 = row0 // tm
    nj = D // tn
    y_specs = [pl.BlockSpec((tm, BW), functools.partial(lambda j, i, o: (i + o, 0), o=(row0 - y0) // tm))
               for y0 in y_row0s]
    g_specs = [pl.BlockSpec((tm, tn), functools.partial(lambda j, i, n: (i + t0, n * nj + j), n=n))
               for n in range(N_BRANCH)]
    w_specs = [pl.BlockSpec((None, None, BW, tn), functools.partial(lambda j, i, n: (layer, n, 0, j), n=n))
               for n in range(N_BRANCH)]
    return pl.pallas_call(
        _merge_kernel,
        grid=(nj, n_rows // tm),
        in_specs=y_specs + g_specs + w_specs,
        out_specs=pl.BlockSpec((tm, tn), lambda j, i: (i, j)),
        out_shape=jax.ShapeDtypeStruct((n_rows, D), BF16),
        scratch_shapes=[pltpu.VMEM((BW, tn), BF16)] * 3,
        compiler_params=_params(2),
        name="merge_branches",
    )(*ys, gates, gates, gates, w_branch, w_branch, w_branch)


def _router_kernel(h_ref, whi_ref, wlo_ref, b_ref, idx_ref, w_ref):
    half = whi_ref.shape[0] // 2
    lo, hi = _unpack_bf16_pairs(_load_row_tiles(h_ref, half // LANE))
    logits = b_ref[...]
    for w in (whi_ref, wlo_ref):
        logits = logits + _dot(lo, w[:half, :]) + _dot(hi, w[half:, :])
    lane = lax.broadcasted_iota(jnp.int32, logits.shape, 1).astype(F32)
    logits = jnp.where(lane < N_EXPERTS, logits, NEG_BIG)
    m1 = jnp.max(logits, axis=-1, keepdims=True)
    i1 = jnp.min(jnp.where(logits == m1, lane, float(LANE)), axis=-1, keepdims=True)
    rest = jnp.where(lane == i1, NEG_BIG, logits)
    m2 = jnp.max(rest, axis=-1, keepdims=True)
    i2 = jnp.min(jnp.where(rest == m2, lane, float(LANE)), axis=-1, keepdims=True)
    e = jnp.exp(m2 - m1)
    w1 = 1.0 / (1.0 + e)
    w2 = e / (1.0 + e)
    idx_ref[...] = jnp.where(lane == 0.0, i1, jnp.where(lane == 1.0, i2, 0.0)).astype(jnp.int32)
    w_ref[...] = jnp.where(lane == 0.0, w1, jnp.where(lane == 1.0, w2, 0.0))


def moe_router(hp, w_router, b_router):
    D = w_router.shape[0]
    n = D // 2 // LANE
    M = hp.shape[0] // n
    tm = _pick_tile((1024, 512, 256, 128), M)
    w_pad = jnp.zeros((D, LANE), F32).at[:, :N_EXPERTS].set(w_router.astype(F32))
    w_hi = w_pad.astype(BF16)
    w_lo = (w_pad - w_hi.astype(F32)).astype(BF16)
    b_pad = jnp.zeros((1, LANE), F32).at[0, :N_EXPERTS].set(b_router.astype(F32))
    return pl.pallas_call(
        _router_kernel,
        grid=(M // tm,),
        in_specs=[pl.BlockSpec((tm * n, LANE), lambda i: (i, 0)),
                  pl.BlockSpec((D, LANE), lambda i: (0, 0)),
                  pl.BlockSpec((D, LANE), lambda i: (0, 0)),
                  pl.BlockSpec((1, LANE), lambda i: (0, 0))],
        out_specs=[pl.BlockSpec((tm, LANE), lambda i: (i, 0)), pl.BlockSpec((tm, LANE), lambda i: (i, 0))],
        out_shape=[jax.ShapeDtypeStruct((M, LANE), jnp.int32), jax.ShapeDtypeStruct((M, LANE), F32)],
        compiler_params=_params(1),
        name="moe_router",
    )(hp, w_hi, w_lo, b_pad)


GATHER_UNROLL = 8


def _gather_rows_kernel(idx_ref, src_ref, o_ref, sem, *, n):
    tg = o_ref.shape[0] // n
    base = pl.program_id(0) * tg

    def start(r, carry):
        src_row = pl.multiple_of(idx_ref[base + r] * n, n)
        dst_row = pl.multiple_of(r * n, n)
        pltpu.make_async_copy(src_ref.at[pl.ds(src_row, n)], o_ref.at[pl.ds(dst_row, n)], sem).start()
        return carry

    lax.fori_loop(0, tg, start, 0, unroll=GATHER_UNROLL)
    pltpu.make_async_copy(src_ref.at[pl.ds(0, tg * n)], o_ref, sem).wait()


def gather_rows(src, idx, n, *, tg=256):
    M = idx.shape[0]
    return pl.pallas_call(
        functools.partial(_gather_rows_kernel, n=n),
        grid_spec=pltpu.PrefetchScalarGridSpec(
            num_scalar_prefetch=1,
            grid=(M // tg,),
            in_specs=[pl.BlockSpec(memory_space=pl.ANY)],
            out_specs=pl.BlockSpec((tg * n, LANE), lambda i, idx_ref: (i, 0)),
            scratch_shapes=[pltpu.SemaphoreType.DMA(())],
        ),
        out_shape=jax.ShapeDtypeStruct((M * n, LANE), src.dtype),
        compiler_params=_params(1),
        name="gather_rows",
    )(idx, src)


def _moe_w13_kernel(te_ref, tv_ref, a_ref, wg_ref, wu_ref, o_ref, sg, su):
    i = pl.program_id(1)
    prev = te_ref[jnp.maximum(i - 1, 0)]

    @pl.when(jnp.logical_or(i == 0, te_ref[i] != prev))
    def _():
        sg[...] = wg_ref[...].astype(BF16)
        su[...] = wu_ref[...].astype(BF16)

    @pl.when(tv_ref[i] == 1)
    def _():
        half = sg.shape[0] // 2
        lo, hi = _unpack_bf16_pairs(_load_row_tiles(a_ref, half // LANE))
        g = _dot(lo, sg[:half, :]) + _dot(hi, sg[half:, :])
        u = _dot(lo, su[:half, :]) + _dot(hi, su[half:, :])
        o_ref[...] = (_silu(g) * u).astype(o_ref.dtype)

    @pl.when(tv_ref[i] == 0)
    def _():
        o_ref[...] = jnp.zeros_like(o_ref)


def moe_w13(xs, w13, moe_idx, tile_expert, tile_valid, *, tn):
    D = w13.shape[2]
    n = D // 2 // LANE
    P = xs.shape[0] // n
    F = w13.shape[-1] // 2
    tm = MOE_TM
    nj = F // tn
    return pl.pallas_call(
        _moe_w13_kernel,
        grid_spec=pltpu.PrefetchScalarGridSpec(
            num_scalar_prefetch=2,
            grid=(nj, P // tm),
            in_specs=[pl.BlockSpec((tm * n, LANE), lambda j, i, te, tv: (i, 0)),
                      pl.BlockSpec((None, None, D, tn), lambda j, i, te, tv: (moe_idx, te[i], 0, j)),
                      pl.BlockSpec((None, None, D, tn), lambda j, i, te, tv: (moe_idx, te[i], 0, j + nj))],
            out_specs=pl.BlockSpec((tm, tn), lambda j, i, te, tv: (i, j)),
            scratch_shapes=[pltpu.VMEM((D, tn), BF16), pltpu.VMEM((D, tn), BF16)],
        ),
        out_shape=jax.ShapeDtypeStruct((P, F), BF16),
        compiler_params=_params(2),
        name="moe_w13",
    )(tile_expert, tile_valid, xs, w13, w13)


def _moe_w2_kernel(te_ref, tv_ref, a_ref, w_ref, o_ref, acc_ref):
    i = pl.program_id(0)
    k = pl.program_id(1)

    @pl.when(tv_ref[i] == 1)
    def _():
        @pl.when(k == 0)
        def _():
            acc_ref[...] = jnp.zeros_like(acc_ref)

        acc_ref[...] += _dot(a_ref[...], w_ref[...])

        @pl.when(k == pl.num_programs(1) - 1)
        def _():
            _store_row_tiles(o_ref, acc_ref[...])

    @pl.when(jnp.logical_and(tv_ref[i] == 0, k == 0))
    def _():
        o_ref[...] = jnp.zeros_like(o_ref)


def moe_w2(act, w2, tile_expert, tile_valid, *, tk):
    P, F = act.shape
    D = w2.shape[-1]
    tm = MOE_TM
    return pl.pallas_call(
        _moe_w2_kernel,
        grid_spec=pltpu.PrefetchScalarGridSpec(
            num_scalar_prefetch=2,
            grid=(P // tm, F // tk),
            in_specs=[pl.BlockSpec((tm, tk), lambda i, k, te, tv: (i, k * tv[i])),
                      pl.BlockSpec((None, tk, D), lambda i, k, te, tv: (te[i], k * tv[i], 0))],
            out_specs=pl.BlockSpec((tm * (D // LANE), LANE), lambda i, k, te, tv: (i, 0)),
            scratch_shapes=[pltpu.VMEM((tm, D), F32)],
        ),
        out_shape=jax.ShapeDtypeStruct((P * (D // LANE), LANE), F32),
        compiler_params=_params(2),
        name="moe_w2",
    )(tile_expert, tile_valid, act, w2)


def _moe_combine_kernel(p0_ref, p1_ref, y_ref, x_ref, gate_ref, w_ref, o_ref, b0, b1, sem):
    tc, D = o_ref.shape
    n = D // LANE
    base = pl.program_id(0) * tc

    def start(r, carry):
        dst = pl.ds(pl.multiple_of(r * n, n), n)
        src0 = pl.ds(pl.multiple_of(p0_ref[base + r] * n, n), n)
        src1 = pl.ds(pl.multiple_of(p1_ref[base + r] * n, n), n)
        pltpu.make_async_copy(y_ref.at[src0], b0.at[dst], sem.at[0]).start()
        pltpu.make_async_copy(y_ref.at[src1], b1.at[dst], sem.at[1]).start()
        return carry

    lax.fori_loop(0, tc, start, 0, unroll=GATHER_UNROLL)
    pltpu.make_async_copy(y_ref.at[pl.ds(0, tc * n)], b0, sem.at[0]).wait()
    pltpu.make_async_copy(y_ref.at[pl.ds(0, tc * n)], b1, sem.at[1]).wait()
    w = w_ref[...]
    y = w[:, 0:1] * _load_row_tiles(b0, n) + w[:, 1:2] * _load_row_tiles(b1, n)
    o_ref[...] = x_ref[...] + gate_ref[0] * y


def moe_combine(dm, y_sorted, pos0, pos1, top_w, x, gate, *, mod_row0):
    M, D = x.shape
    tc = 128
    t0 = mod_row0 // tc
    n = D // LANE
    return pl.pallas_call(
        _moe_combine_kernel,
        grid_spec=pltpu.PrefetchScalarGridSpec(
            num_scalar_prefetch=2,
            grid=(M // tc,),
            in_specs=[pl.BlockSpec(memory_space=pl.ANY),
                      pl.BlockSpec((tc, D), lambda i, p0, p1: (i, 0)),
                      pl.BlockSpec((1, 1, D), lambda i, p0, p1: (dm.mod_row(i + t0, tc), 0, 0)),
                      pl.BlockSpec((tc, LANE), lambda i, p0, p1: (i, 0))],
            out_specs=pl.BlockSpec((tc, D), lambda i, p0, p1: (i, 0)),
            scratch_shapes=[pltpu.VMEM((tc * n, LANE), F32), pltpu.VMEM((tc * n, LANE), F32),
                            pltpu.SemaphoreType.DMA((2,))],
        ),
        out_shape=jax.ShapeDtypeStruct((M, D), F32),
        compiler_params=_params(1),
        name="moe_combine",
    )(pos0, pos1, y_sorted, x, gate, top_w)


def moe_ffn(dm, hp, x, gate, w_router, b_router, w13, w2, moe_idx, *, mod_row0):
    M, D = x.shape
    E = N_EXPERTS
    tm = MOE_TM
    top_idx, top_w = moe_router(hp, w_router, b_router)
    e_flat = top_idx[:, :TOP_K].T.reshape(-1)
    onehot = (e_flat[:, None] == jnp.arange(E, dtype=jnp.int32)[None, :]).astype(jnp.int32)
    csum = jnp.cumsum(onehot, axis=0)
    counts = csum[-1]
    rank = jnp.sum((csum - onehot) * onehot, axis=1)
    padded = ((counts + tm - 1) // tm) * tm
    ends = jnp.cumsum(padded)
    starts = ends - padded
    pos = starts[e_flat] + rank
    P = TOP_K * M + E * tm
    n_tiles = P // tm
    tok = jnp.tile(jnp.arange(M, dtype=jnp.int32), TOP_K)
    gidx = jnp.zeros((P,), jnp.int32).at[pos].set(tok)
    tile_start = jnp.arange(n_tiles, dtype=jnp.int32) * tm
    tile_valid = (tile_start < ends[-1]).astype(jnp.int32)
    te = jnp.sum((tile_start[:, None] >= ends[None, :]).astype(jnp.int32), axis=1)
    last_e = jnp.sum((ends[-1] - 1 >= ends).astype(jnp.int32))
    tile_expert = jnp.minimum(te, last_e).astype(jnp.int32)

    xs = gather_rows(hp, gidx, D // 2 // LANE)
    F = w13.shape[-1] // 2
    act = moe_w13(xs, w13, moe_idx, tile_expert, tile_valid, tn=_pick_tile((1024, 512, 256, 128), F))
    y_sorted = moe_w2(act, w2[moe_idx].astype(BF16), tile_expert, tile_valid, tk=_pick_tile((512, 256, 128), F))
    return moe_combine(dm, y_sorted, pos[:M], pos[M:], top_w, x, gate, mod_row0=mod_row0)


def _axial_angles(seq, rot_dim):
    rows = seq // GRID_W
    t_row = jnp.repeat(jnp.arange(rows, dtype=F32), GRID_W)
    t_col = jnp.tile(jnp.arange(GRID_W, dtype=F32), rows)
    quarter = rot_dim // 4
    inv_freq = ROPE_THETA ** (-jnp.arange(quarter, dtype=F32) / quarter)
    return jnp.concatenate([t_row[:, None] * inv_freq, t_col[:, None] * inv_freq], axis=-1)


def _rope_tables(dm):
    ident = min(dm.tm, 256)
    ang = _axial_angles(dm.S, SWA_HEAD_DIM)
    cos_a = jnp.concatenate([jnp.cos(ang), jnp.cos(ang)], axis=-1)
    sin_a = jnp.concatenate([-jnp.sin(ang), jnp.sin(ang)], axis=-1)
    cos_a = jnp.concatenate([jnp.ones((ident, LANE), F32), cos_a], axis=0)
    sin_a = jnp.concatenate([jnp.zeros((ident, LANE), F32), sin_a], axis=0)
    ang = _axial_angles(dm.S, MLA_ROPE)
    half = MLA_ROPE // 2
    zeros = jnp.zeros((dm.S, half), F32)
    pad = jnp.zeros((dm.S, LANE - MLA_ROPE), F32)
    c_b = jnp.concatenate([jnp.cos(ang), jnp.cos(ang), pad], axis=-1)
    s1_b = jnp.concatenate([-jnp.sin(ang), zeros, pad], axis=-1)
    s2_b = jnp.concatenate([zeros, jnp.sin(ang), pad], axis=-1)
    c_b = jnp.concatenate([jnp.ones((ident, LANE), F32), c_b], axis=0)
    s1_b = jnp.concatenate([jnp.zeros((ident, LANE), F32), s1_b], axis=0)
    s2_b = jnp.concatenate([jnp.zeros((ident, LANE), F32), s2_b], axis=0)
    return (cos_a, sin_a), (c_b, s1_b, s2_b)


def _pad_head_vec(g):
    return jnp.zeros((1, MLA_QK_PAD), F32).at[0, :MLA_QK].set(g.astype(F32))


def _trunk(x, c, ctx, c_ctx, mod_w, mod_b, norm_mix_g, norm_ffn_g, w_in,
           swa_q_norm_g, swa_k_norm_g, swa_sink,
           mla_q_a_norm_g, mla_w_uq, mla_kv_a_norm_g, mla_w_ukv, mla_q_norm_g, mla_k_norm_g,
           ssm_lam_re, ssm_lam_im, ssm_log_step, ssm_b_re, ssm_b_im, ssm_c_re, ssm_c_im,
           ssm_d, ssm_w_glu, ssm_b_glu, w_branch, w_out,
           ffn_w13, ffn_w2, moe_w_router, moe_b_router, moe_w13, moe_w2):
    B, S, D = x.shape
    L = ctx.shape[1]
    depth = mod_w.shape[0]
    dm = Dims(B, S, L)
    tm = dm.tm
    RC, RL, R = dm.RC, dm.RL, dm.R
    q_w = SWA_HEADS * SWA_HEAD_DIM
    kv_w = SWA_KV_HEADS * SWA_HEAD_DIM
    q_rank = mla_w_uq.shape[1]
    kv_rank = mla_w_ukv.shape[1]
    ssm_w = ssm_d.shape[1]
    n_gate = N_BRANCH * D
    src = {}
    off = 0
    for name, width in (("q", q_w), ("k", kv_w), ("v", kv_w), ("c_q", q_rank), ("c_kv", kv_rank),
                        ("kpe", MLA_ROPE), ("u", ssm_w), ("gates", n_gate)):
        src[name] = (off, width)
        off += width
    order = ("q", "u", "c_q", "k", "v", "c_kv", "kpe")
    col = {}
    off = 0
    for name in order:
        col[name] = off
        off += src[name][1]
    z_tn = 512
    z_cols = -(-off // z_tn) * z_tn

    (cos_a, sin_a), tabs_b = _rope_tables(dm)
    xall = jnp.concatenate([ctx.reshape(RC, D), x.reshape(RL, D)], axis=0).astype(F32)
    cond = jnp.zeros((8, D), F32).at[0].set(c_ctx.astype(F32)).at[1:1 + B].set(c.astype(F32))

    for layer in range(depth):
        with_ctx = layer < depth - 1
        row0 = 0 if with_ctx else RC
        n_rows = R - row0
        mods = mm1(cond, [(mod_w, (layer,), 0)], _epi_bias, n_rows=8, n_cols=6 * D, tm=8, tn=512, out_dtype=F32,
                   extras=[(mod_b.reshape(depth, 1, 6 * D), (None, 1, 512), lambda j, i: (layer, 0, j))],
                   prologue=lambda a: _silu(a).astype(BF16), name="ada_mod")
        sh_m, sc_m, g_m, sh_f, sc_f, g_f = [mods[:, i * D:(i + 1) * D].reshape(8, 1, D) for i in range(6)]

        h = modulate(dm, xall, norm_mix_g[layer], sh_m, sc_m, mod_row0=0)
        w_l = w_in[layer]
        w_rest = jnp.concatenate([w_l[:, src[n][0]:src[n][0] + src[n][1]] for n in order]
                                 + [jnp.zeros((D, z_cols - off), w_l.dtype)], axis=1)
        w_gates = w_l[:, src["gates"][0]:]
        z = mm1(h, [(w_rest, (), 0)], _epi_id, n_rows=R, n_cols=z_cols, tm=tm, tn=z_tn, out_dtype=F32, name="w_in")
        gates = mm1(h, [(w_gates, (), 0)], _epi_sigmoid, n_rows=R, n_cols=n_gate, tm=tm, tn=512,
                    out_dtype=BF16, name="w_in_gates")

        qa, ka, va = swa_prep(dm, z, col, swa_q_norm_g[layer], swa_k_norm_g[layer], cos_a, sin_a)
        ya_l = swa_attention(dm, qa, ka, va, swa_sink[layer], latent=True)
        cqn, ckvn = mla_lowrank_norm(dm, z, col, q_rank, kv_rank, mla_q_a_norm_g[layer], mla_kv_a_norm_g[layer])
        w_uq = mla_w_uq[layer].reshape(q_rank, MLA_HEADS, MLA_QK)
        w_uq = jnp.pad(w_uq, ((0, 0), (0, 0), (0, MLA_QK_PAD - MLA_QK))).reshape(q_rank, MLA_HEADS * MLA_QK_PAD)
        w_ukv = mla_w_ukv[layer].reshape(kv_rank, MLA_HEADS, MLA_NOPE + MLA_V)
        w_ukv = jnp.concatenate([w_ukv[:, :, :MLA_NOPE].reshape(kv_rank, -1),
                                 w_ukv[:, :, MLA_NOPE:].reshape(kv_rank, -1)], axis=1)
        qf = mm1(cqn, [(w_uq, (), 0)], _epi_id, n_rows=R, n_cols=w_uq.shape[1], tm=tm, tn=512, out_dtype=F32,
                 name="mla_uq")
        kvf = mm1(ckvn, [(w_ukv, (), 0)], _epi_id, n_rows=R, n_cols=w_ukv.shape[1], tm=tm, tn=512, out_dtype=F32,
                  name="mla_ukv")
        gq_pad = _pad_head_vec(mla_q_norm_g[layer])
        gk_pad = _pad_head_vec(mla_k_norm_g[layer])
        qm_c, km_c, vm_c = mla_prep(dm, qf, kvf, z, col, gq_pad, gk_pad, tabs_b, row0=0, n_rows=RC)
        qm_l, km_l, vm_l = mla_prep(dm, qf, kvf, z, col, gq_pad, gk_pad, tabs_b, row0=RC, n_rows=RL)
        yb_l = mla_attention(dm, qm_l, km_c, vm_c, km_l, vm_l)
        if with_ctx:
            ya = jnp.concatenate([swa_attention(dm, qa, ka, va, swa_sink[layer], latent=False), ya_l], axis=0)
            yb = jnp.concatenate([mla_attention(dm, qm_c, km_c, vm_c), yb_l], axis=0)
            y_row0s = (0, 0, 0)
        else:
            ya, yb = ya_l, yb_l
            y_row0s = (RC, RC, 0)
        tables = _ssm_tables(ssm_lam_re[layer], ssm_lam_im[layer], ssm_log_step[layer], ssm_b_re[layer],
                             ssm_b_im[layer], ssm_c_re[layer], ssm_c_im[layer])
        yg = s5_branch(dm, z, col, tables, ssm_d[layer])
        b_glu = ssm_b_glu.reshape(depth, 1, 2 * ssm_w)
        gl_tn = 512
        yc = mm1(yg, [(ssm_w_glu, (layer,), 0), (ssm_w_glu, (layer,), ssm_w)], _epi_glu_bias,
                 n_rows=R, n_cols=ssm_w, tm=tm, tn=gl_tn, out_dtype=BF16,
                 extras=[(b_glu, (None, 1, gl_tn), lambda j, i: (layer, 0, j)),
                         (b_glu, (None, 1, gl_tn), lambda j, i: (layer, 0, j + ssm_w // gl_tn))],
                 prologue=lambda a: a.astype(BF16), name="ssm_glu")
        mixed = merge_branches(dm, (ya, yb, yc), y_row0s, gates, w_branch, layer, row0=row0, n_rows=n_rows)
        t0 = row0 // tm
        x1 = mm1(mixed, [(w_out, (layer,), 0)], _epi_residual, n_rows=n_rows, n_cols=D, tm=tm, tn=512, out_dtype=F32,
                 extras=[(xall, (tm, 512), lambda j, i: (i + t0, j)),
                         (g_m, (1, 1, 512), lambda j, i: (dm.mod_row(i + t0, tm), 0, j))],
                 name="w_out")
        is_moe = layer % 2 == 1
        h2 = modulate(dm, x1, norm_ffn_g[layer], sh_f, sc_f, mod_row0=row0, pack=is_moe)
        if not is_moe:
            F = ffn_w13.shape[-1] // 2
            f_tn = _pick_tile((512, 256, 128), F)
            act = mm1(h2, [(ffn_w13, (layer // 2,), 0), (ffn_w13, (layer // 2,), F)], _epi_swiglu, n_rows=n_rows,
                      n_cols=F, tm=tm, tn=f_tn, out_dtype=BF16, name="ffn_w13")
            x2 = mm2_residual(dm, act, ffn_w2[layer // 2].astype(BF16), x1, g_f, mod_row0=row0, tm=tm, tk=f_tn)
        else:
            if with_ctx:
                raise NotImplementedError("a mixture-of-experts layer that still feeds context rows")
            x2 = moe_ffn(dm, h2, x1, g_f, moe_w_router[layer // 2], moe_b_router[layer // 2], moe_w13, moe_w2,
                         layer // 2, mod_row0=row0)
        xall = x2
    return xall.reshape(B, S, D)


def kernel(x, c, ctx, c_ctx, mod_w, mod_b, norm_mix_g, norm_ffn_g, w_in, swa_q_norm_g, swa_k_norm_g, swa_sink, mla_q_a_norm_g, mla_w_uq, mla_kv_a_norm_g, mla_w_ukv, mla_q_norm_g, mla_k_norm_g, ssm_lam_re, ssm_lam_im, ssm_log_step, ssm_b_re, ssm_b_im, ssm_c_re, ssm_c_im, ssm_d, ssm_w_glu, ssm_b_glu, w_branch, w_out, ffn_w13, ffn_w2, moe_w_router, moe_b_router, moe_w13, moe_w2):
    return _trunk(x, c, ctx, c_ctx, mod_w, mod_b, norm_mix_g, norm_ffn_g, w_in, swa_q_norm_g, swa_k_norm_g, swa_sink,
                  mla_q_a_norm_g, mla_w_uq, mla_kv_a_norm_g, mla_w_ukv, mla_q_norm_g, mla_k_norm_g,
                  ssm_lam_re, ssm_lam_im, ssm_log_step, ssm_b_re, ssm_b_im, ssm_c_re, ssm_c_im,
                  ssm_d, ssm_w_glu, ssm_b_glu, w_branch, w_out, ffn_w13, ffn_w2, moe_w_router, moe_b_router,
                  moe_w13, moe_w2)
```

```python
import functools
import math

import jax
import jax.numpy as jnp
from jax import lax
from jax.experimental import pallas as pl
from jax.experimental.pallas import tpu as pltpu

F32 = jnp.float32
BF16 = jnp.bfloat16

GRID_W = 64
ROPE_THETA = 10000.0
EPS = 1e-6
SWA_HEADS = 8
SWA_KV_HEADS = 2
SWA_HEAD_DIM = 128
SWA_WINDOW = 128
MLA_HEADS = 8
MLA_NOPE = 128
MLA_ROPE = 64
MLA_V = 128
MLA_QK = MLA_NOPE + MLA_ROPE
MLA_QK_PAD = 256
SSM_GROUP = 16
SSM_STATE = 64
SSM_CHUNK = 16
N_BRANCH = 3
N_EXPERTS = 8
TOP_K = 2
LANE = 128
VMEM_LIMIT_BYTES = 56 * 1024 * 1024
MOE_TM = 512
NEG_BIG = -1e30


def _params(n_grid):
    return pltpu.CompilerParams(dimension_semantics=("arbitrary",) * n_grid, vmem_limit_bytes=VMEM_LIMIT_BYTES)


def _pick_tile(candidates, *sizes):
    for t in candidates:
        if all(s % t == 0 for s in sizes):
            return t
    raise ValueError(f"no tile in {candidates} divides {sizes}")


class Dims:
    def __init__(self, batch, seq, ctx_len):
        self.B, self.S, self.L = batch, seq, ctx_len
        self.RC = batch * ctx_len
        self.RL = batch * seq
        self.R = self.RC + self.RL
        self.tm = _pick_tile((1024, 512, 256, 128), ctx_len * batch, seq)

    def mod_row(self, tile, tm):
        nct = self.RC // tm
        return jnp.where(tile < nct, 0, 1 + (tile - nct) // (self.S // tm))


def _silu(x):
    return x * (1.0 / (1.0 + jnp.exp(-x)))


def _sigmoid(x):
    return 1.0 / (1.0 + jnp.exp(-x))


def _gelu_tanh(x):
    c = math.sqrt(2.0 / math.pi)
    return 0.5 * x * (1.0 + jnp.tanh(c * (x + 0.044715 * (x * x * x))))


def _dot(a, b):
    return jnp.dot(a, b, preferred_element_type=F32)


def _dot_nt(a, b):
    return lax.dot_general(a, b, (((1,), (1,)), ((), ())), preferred_element_type=F32)


def _pack_bf16_pairs(y):
    half = y.shape[1] // 2
    bits = lax.bitcast_convert_type(y.astype(BF16).astype(F32), jnp.uint32)
    return (bits[:, :half] >> 16) | (bits[:, half:] & jnp.uint32(0xFFFF0000))


def _unpack_bf16_pairs(p):
    lo = lax.bitcast_convert_type(p << 16, F32).astype(BF16)
    hi = lax.bitcast_convert_type(p & jnp.uint32(0xFFFF0000), F32).astype(BF16)
    return lo, hi


def _store_row_tiles(ref, val):
    m, w = val.shape
    n = w // LANE
    for c in range(n):
        ref[pl.ds(c, m, stride=n), :] = val[:, c * LANE:(c + 1) * LANE]


def _load_row_tiles(ref, n):
    if len(ref.shape) == 2:
        m = ref.shape[0] // n
        return jnp.concatenate([ref[pl.ds(c, m, stride=n), :] for c in range(n)], axis=1)
    m = ref.shape[1] // n
    return jnp.concatenate([ref[j, pl.ds(c, m, stride=n), :] for j in range(ref.shape[0]) for c in range(n)], axis=1)


def _modulate_kernel(x_ref, g_ref, sh_ref, sc_ref, o_ref, *, pack):
    x = x_ref[...]
    ms = jnp.mean(x * x, axis=-1, keepdims=True)
    y = x * lax.rsqrt(ms + EPS) * g_ref[...]
    y = y * (1.0 + sc_ref[0]) + sh_ref[0]
    if pack:
        _store_row_tiles(o_ref, _pack_bf16_pairs(y))
    else:
        o_ref[...] = y.astype(o_ref.dtype)


def modulate(dm, x, g, shift, scale, *, mod_row0, pack=False):
    n_rows, D = x.shape
    tm = min(dm.tm, 512)
    t0 = mod_row0 // tm
    if pack:
        n = D // 2 // LANE
        out_spec = pl.BlockSpec((tm * n, LANE), lambda i: (i, 0))
        out_shape = jax.ShapeDtypeStruct((n_rows * n, LANE), jnp.uint32)
    else:
        out_spec = pl.BlockSpec((tm, D), lambda i: (i, 0))
        out_shape = jax.ShapeDtypeStruct((n_rows, D), BF16)
    return pl.pallas_call(
        functools.partial(_modulate_kernel, pack=pack),
        grid=(n_rows // tm,),
        in_specs=[
            pl.BlockSpec((tm, D), lambda i: (i, 0)),
            pl.BlockSpec((1, D), lambda i: (0, 0)),
            pl.BlockSpec((1, 1, D), lambda i: (dm.mod_row(i + t0, tm), 0, 0)),
            pl.BlockSpec((1, 1, D), lambda i: (dm.mod_row(i + t0, tm), 0, 0)),
        ],
        out_specs=out_spec,
        out_shape=out_shape,
        compiler_params=_params(1),
        name="modulate_packed" if pack else "modulate",
    )(x, g.reshape(1, D), shift, scale)


def _mm1_kernel(*refs, n_w, n_extra, epilogue, prologue):
    a_ref = refs[0]
    w_refs = refs[1:1 + n_w]
    extra = refs[1 + n_w:1 + n_w + n_extra]
    o_ref = refs[1 + n_w + n_extra]
    wb = refs[2 + n_w + n_extra:]

    @pl.when(pl.program_id(1) == 0)
    def _():
        for w_ref, b in zip(w_refs, wb):
            b[...] = w_ref[...].astype(BF16)

    a = a_ref[...]
    if prologue is not None:
        a = prologue(a)
    accs = [_dot(a, b[...]) for b in wb]
    o_ref[...] = epilogue(accs, *extra).astype(o_ref.dtype)


def mm1(a, weights, epilogue, *, n_rows, n_cols, tm, tn, out_dtype, a_row0=0, extras=(), prologue=None, name):
    K = a.shape[1]
    t0 = a_row0 // tm
    in_specs = [pl.BlockSpec((tm, K), lambda j, i: (i + t0, 0))]
    operands = [a]
    for w, lead, col0 in weights:
        c0 = col0 // tn
        in_specs.append(pl.BlockSpec((None,) * len(lead) + (K, tn),
                                     functools.partial(lambda j, i, lead, c0: lead + (0, j + c0), lead=lead, c0=c0)))
        operands.append(w)
    for arr, bshape, imap in extras:
        in_specs.append(pl.BlockSpec(bshape, imap))
        operands.append(arr)
    kern = functools.partial(_mm1_kernel, n_w=len(weights), n_extra=len(extras), epilogue=epilogue,
                             prologue=prologue)
    return pl.pallas_call(
        kern,
        grid=(n_cols // tn, n_rows // tm),
        in_specs=in_specs,
        out_specs=pl.BlockSpec((tm, tn), lambda j, i: (i, j)),
        out_shape=jax.ShapeDtypeStruct((n_rows, n_cols), out_dtype),
        scratch_shapes=[pltpu.VMEM((K, tn), BF16) for _ in weights],
        compiler_params=_params(2),
        name=name,
    )(*operands)


def _epi_id(accs):
    return accs[0]


def _epi_sigmoid(accs):
    return _sigmoid(accs[0])


def _epi_swiglu(accs):
    return _silu(accs[0]) * accs[1]


def _epi_bias(accs, b_ref):
    return accs[0] + b_ref[...]


def _epi_glu_bias(accs, ba_ref, bb_ref):
    return (accs[0] + ba_ref[...]) * _sigmoid(accs[1] + bb_ref[...])


def _epi_residual(accs, x_ref, gate_ref):
    return x_ref[...] + gate_ref[0] * accs[0]


def _mm2_kernel(a_ref, w_ref, x_ref, gate_ref, o_ref, acc_ref):
    k = pl.program_id(1)

    @pl.when(k == 0)
    def _():
        acc_ref[...] = jnp.zeros_like(acc_ref)

    acc_ref[...] += _dot(a_ref[...], w_ref[...])

    @pl.when(k == pl.num_programs(1) - 1)
    def _():
        o_ref[...] = x_ref[...] + gate_ref[0] * acc_ref[...]


def mm2_residual(dm, a, w, x, gate, *, mod_row0, tm, tk):
    M, K = a.shape
    N = w.shape[1]
    t0 = mod_row0 // tm
    return pl.pallas_call(
        _mm2_kernel,
        grid=(M // tm, K // tk),
        in_specs=[
            pl.BlockSpec((tm, tk), lambda i, k: (i, k)),
            pl.BlockSpec((tk, N), lambda i, k: (k, 0)),
            pl.BlockSpec((tm, N), lambda i, k: (i, 0)),
            pl.BlockSpec((1, 1, N), lambda i, k: (dm.mod_row(i + t0, tm), 0, 0)),
        ],
        out_specs=pl.BlockSpec((tm, N), lambda i, k: (i, 0)),
        out_shape=jax.ShapeDtypeStruct((M, N), F32),
        scratch_shapes=[pltpu.VMEM((tm, N), F32)],
        compiler_params=_params(2),
        name="mm2_residual",
    )(a, w, x, gate)


def _swa_prep_kernel(q_ref, k_ref, v_ref, gq_ref, gk_ref, cos_ref, sin_ref, qo_ref, ko_ref, vo_ref):
    c = cos_ref[...]
    s = sin_ref[...]

    def norm_rope(x, g, scale):
        ms = jnp.mean(x * x, axis=-1, keepdims=True)
        y = x * lax.rsqrt(ms + EPS) * g
        return (y * c + pltpu.roll(y, SWA_HEAD_DIM // 2, 1) * s) * scale

    gq = gq_ref[...]
    gk = gk_ref[...]
    for h in range(SWA_HEADS):
        sl = slice(h * SWA_HEAD_DIM, (h + 1) * SWA_HEAD_DIM)
        qo_ref[:, sl] = norm_rope(q_ref[:, sl], gq, SWA_HEAD_DIM ** -0.5).astype(BF16)
    for h in range(SWA_KV_HEADS):
        sl = slice(h * SWA_HEAD_DIM, (h + 1) * SWA_HEAD_DIM)
        ko_ref[:, sl] = norm_rope(k_ref[:, sl], gk, 1.0).astype(BF16)
    vo_ref[...] = v_ref[...].astype(BF16)


def _rope_tile_index(dm, tm):
    nct = dm.RC // tm
    return lambda i: (jnp.where(i < nct, 0, 1 + (i - nct) % (dm.S // tm)), 0)


def swa_prep(dm, z, col, gq, gk, cos_t, sin_t):
    tm = min(dm.tm, 256)
    QW = SWA_HEADS * SWA_HEAD_DIM
    KW = SWA_KV_HEADS * SWA_HEAD_DIM
    ridx = _rope_tile_index(dm, tm)
    return pl.pallas_call(
        _swa_prep_kernel,
        grid=(dm.R // tm,),
        in_specs=[
            pl.BlockSpec((tm, QW), lambda i: (i, col["q"] // QW)),
            pl.BlockSpec((tm, KW), lambda i: (i, col["k"] // KW)),
            pl.BlockSpec((tm, KW), lambda i: (i, col["v"] // KW)),
            pl.BlockSpec((1, SWA_HEAD_DIM), lambda i: (0, 0)),
            pl.BlockSpec((1, SWA_HEAD_DIM), lambda i: (0, 0)),
            pl.BlockSpec((tm, SWA_HEAD_DIM), ridx),
            pl.BlockSpec((tm, SWA_HEAD_DIM), ridx),
        ],
        out_specs=[
            pl.BlockSpec((tm, QW), lambda i: (i, 0)),
            pl.BlockSpec((tm, KW), lambda i: (i, 0)),
            pl.BlockSpec((tm, KW), lambda i: (i, 0)),
        ],
        out_shape=[
            jax.ShapeDtypeStruct((dm.R, QW), BF16),
            jax.ShapeDtypeStruct((dm.R, KW), BF16),
            jax.ShapeDtypeStruct((dm.R, KW), BF16),
        ],
        compiler_params=_params(1),
        name="swa_prep",
    )(z, z, z, gq.reshape(1, -1), gk.reshape(1, -1), cos_t, sin_t)


def _swa_attn_kernel(*refs, windowed, nb):
    if windowed:
        q_ref, kc_ref, kp_ref, kk_ref, kn_ref, vc_ref, vp_ref, vk_ref, vn_ref, sink_ref, o_ref = refs
    else:
        q_ref, kc_ref, vc_ref, sink_ref, o_ref = refs
    G = SWA_HEADS // SWA_KV_HEADS
    blk = q_ref.shape[0]
    Dh = SWA_HEAD_DIM
    q = jnp.concatenate([q_ref[:, g * Dh:(g + 1) * Dh] for g in range(G)], axis=0)
    sink = sink_ref[0][:, 0:1]
    scores = [_dot_nt(q, kc_ref[...])]
    values = [vc_ref[...]]
    if windowed:
        n = pl.program_id(2)
        qi = lax.broadcasted_iota(jnp.int32, (G * blk, blk), 0) % blk
        kj = lax.broadcasted_iota(jnp.int32, (G * blk, blk), 1)
        s_p = _dot_nt(q, kp_ref[...])
        s_p = jnp.where(kj >= qi, s_p, NEG_BIG)
        s_p = jnp.where(n >= 1, s_p, NEG_BIG)
        s_n = _dot_nt(q, kn_ref[...])
        s_n = jnp.where(kj <= qi, s_n, NEG_BIG)
        s_n = jnp.where(n <= nb - 2, s_n, NEG_BIG)
        scores += [s_p, _dot_nt(q, kk_ref[...]), s_n]
        values += [vp_ref[...], vk_ref[...], vn_ref[...]]
    m = sink
    for s in scores:
        m = jnp.maximum(m, jnp.max(s, axis=-1, keepdims=True))
    l = jnp.exp(sink - m)
    o = None
    for s, v in zip(scores, values):
        p = jnp.exp(s - m)
        l = l + jnp.sum(p, axis=-1, keepdims=True)
        pv = _dot(p.astype(BF16), v)
        o = pv if o is None else o + pv
    o = o / l
    for g in range(G):
        o_ref[:, g * Dh:(g + 1) * Dh] = o[g * blk:(g + 1) * blk].astype(o_ref.dtype)


def swa_attention(dm, qa, ka, va, sink, *, latent):
    G = SWA_HEADS // SWA_KV_HEADS
    Dh = SWA_HEAD_DIM
    blk = SWA_WINDOW
    L = dm.L
    sink_col = jnp.broadcast_to(sink.astype(F32).reshape(SWA_KV_HEADS, G, 1, 1),
                                (SWA_KV_HEADS, G, blk, LANE)).reshape(SWA_KV_HEADS, G * blk, LANE)
    sink_spec = pl.BlockSpec((1, G * blk, LANE), lambda b, h, n: (h, 0, 0))
    ctx_spec = pl.BlockSpec((L, Dh), lambda b, h, n: (b, h))
    if latent:
        nb = dm.S // blk
        base = dm.RC // blk

        def q_map(b, h, n):
            return (base + b * nb + n, h)

        def kv_map(off):
            return lambda b, h, n: (base + b * nb + jnp.clip(n + off, 0, nb - 1), h)

        win_specs = [pl.BlockSpec((blk, Dh), kv_map(off)) for off in (-1, 0, 1)]
        in_specs = ([pl.BlockSpec((blk, G * Dh), q_map), ctx_spec] + win_specs + [ctx_spec] + win_specs
                    + [sink_spec])
        operands = (qa, ka, ka, ka, ka, va, va, va, va, sink_col)
        n_out = dm.RL
    else:
        nb = L // blk
        in_specs = [pl.BlockSpec((blk, G * Dh), lambda b, h, n: (b * nb + n, h)), ctx_spec, ctx_spec, sink_spec]
        operands = (qa, ka, va, sink_col)
        n_out = dm.RC
    return pl.pallas_call(
        functools.partial(_swa_attn_kernel, windowed=latent, nb=nb),
        grid=(dm.B, SWA_KV_HEADS, nb),
        in_specs=in_specs,
        out_specs=pl.BlockSpec((blk, G * Dh), lambda b, h, n: (b * nb + n, h)),
        out_shape=jax.ShapeDtypeStruct((n_out, SWA_HEADS * Dh), BF16),
        compiler_params=_params(3),
        name="swa_attn_lat" if latent else "swa_attn_ctx",
    )(*operands)


def _rms_rows_kernel(a_ref, b_ref, ga_ref, gb_ref, ao_ref, bo_ref):
    for x_ref, g_ref, o_ref in ((a_ref, ga_ref, ao_ref), (b_ref, gb_ref, bo_ref)):
        x = x_ref[...]
        ms = jnp.mean(x * x, axis=-1, keepdims=True)
        o_ref[...] = (x * lax.rsqrt(ms + EPS) * g_ref[...]).astype(o_ref.dtype)


def mla_lowrank_norm(dm, z, col, q_rank, kv_rank, gq, gkv):
    tm = min(dm.tm, 512)
    return pl.pallas_call(
        _rms_rows_kernel,
        grid=(dm.R // tm,),
        in_specs=[
            pl.BlockSpec((tm, q_rank), lambda i: (i, col["c_q"] // q_rank)),
            pl.BlockSpec((tm, kv_rank), lambda i: (i, col["c_kv"] // kv_rank)),
            pl.BlockSpec((1, q_rank), lambda i: (0, 0)),
            pl.BlockSpec((1, kv_rank), lambda i: (0, 0)),
        ],
        out_specs=[pl.BlockSpec((tm, q_rank), lambda i: (i, 0)), pl.BlockSpec((tm, kv_rank), lambda i: (i, 0))],
        out_shape=[jax.ShapeDtypeStruct((dm.R, q_rank), BF16), jax.ShapeDtypeStruct((dm.R, kv_rank), BF16)],
        compiler_params=_params(1),
        name="mla_lowrank_norm",
    )(z, z, gq.reshape(1, -1), gkv.reshape(1, -1))


def _mla_prep_kernel(q_ref, kn_ref, v_ref, pe_ref, gq_ref, gk_ref, c_ref, s1_ref, s2_ref, qo_ref, ko_ref, vo_ref,
                     vt_ref):
    c = c_ref[...]
    s1 = s1_ref[...]
    s2 = s2_ref[...]
    gq = gq_ref[...]
    gk = gk_ref[...]
    scale = MLA_QK ** -0.5 * math.log2(math.e)

    def rope(x):
        return x * c + pltpu.roll(x, LANE - MLA_ROPE // 2, 1) * s1 + pltpu.roll(x, MLA_ROPE // 2, 1) * s2

    pe = pe_ref[...]
    pe_ss = jnp.sum(pe * pe, axis=-1, keepdims=True)
    for h in range(MLA_HEADS):
        lo = h * MLA_QK_PAD
        qh = q_ref[:, lo:lo + MLA_QK_PAD]
        inv = lax.rsqrt(jnp.sum(qh * qh, axis=-1, keepdims=True) * (1.0 / MLA_QK) + EPS)
        qn = qh * inv * gq
        qo_ref[:, lo:lo + MLA_NOPE] = (qn[:, :MLA_NOPE] * scale).astype(BF16)
        qo_ref[:, lo + MLA_NOPE:lo + MLA_QK_PAD] = (rope(qn[:, MLA_NOPE:]) * scale).astype(BF16)
        kh = kn_ref[:, h * MLA_NOPE:(h + 1) * MLA_NOPE]
        inv = lax.rsqrt((jnp.sum(kh * kh, axis=-1, keepdims=True) + pe_ss) * (1.0 / MLA_QK) + EPS)
        ko_ref[:, lo:lo + MLA_NOPE] = (kh * inv * gk[:, :MLA_NOPE]).astype(BF16)
        ko_ref[:, lo + MLA_NOPE:lo + MLA_QK_PAD] = rope(pe * inv * gk[:, MLA_NOPE:]).astype(BF16)
    v = v_ref[...]
    vo_ref[...] = v.astype(BF16)
    vt_ref[...] = v.T.astype(BF16)


def mla_prep(dm, qf, kvf, z, col, gq_pad, gk_pad, tabs, *, row0, n_rows, rows_per_batch):
    tm = min(dm.tm, 256)
    t0 = row0 // tm
    QW = MLA_HEADS * MLA_QK_PAD
    NW = MLA_HEADS * MLA_NOPE
    ridx = _rope_tile_index(dm, tm)
    rspec = pl.BlockSpec((tm, LANE), lambda i: ridx(i + t0))
    tpb = rows_per_batch // tm
    return pl.pallas_call(
        _mla_prep_kernel,
        grid=(n_rows // tm,),
        in_specs=[
            pl.BlockSpec((tm, QW), lambda i: (i + t0, 0)),
            pl.BlockSpec((tm, NW), lambda i: (i + t0, 0)),
            pl.BlockSpec((tm, NW), lambda i: (i + t0, 1)),
            pl.BlockSpec((tm, LANE), lambda i: (i + t0, col["kpe"] // LANE)),
            pl.BlockSpec((1, MLA_QK_PAD), lambda i: (0, 0)),
            pl.BlockSpec((1, MLA_QK_PAD), lambda i: (0, 0)),
            rspec, rspec, rspec,
        ],
        out_specs=[
            pl.BlockSpec((tm, QW), lambda i: (i, 0)),
            pl.BlockSpec((tm, QW), lambda i: (i, 0)),
            pl.BlockSpec((tm, NW), lambda i: (i, 0)),
            pl.BlockSpec((NW, tm), lambda i: (i // tpb, i % tpb)),
        ],
        out_shape=[
            jax.ShapeDtypeStruct((n_rows, QW), BF16),
            jax.ShapeDtypeStruct((n_rows, QW), BF16),
            jax.ShapeDtypeStruct((n_rows, NW), BF16),
            jax.ShapeDtypeStruct((n_rows // rows_per_batch * NW, rows_per_batch), BF16),
        ],
        compiler_params=_params(1),
        name="mla_prep",
    )(qf, kvf, kvf, z, gq_pad, gk_pad, *tabs)


MLA_KEY_CHUNK = 512


def _mla_attn_ctx_kernel(q_ref, kc_ref, vc_ref, o_ref):
    s = _dot_nt(q_ref[...], kc_ref[...])
    p = jnp.exp2(s - jnp.max(s, axis=-1, keepdims=True))
    o = _dot(p.astype(BF16), vc_ref[...])
    o_ref[...] = (o / jnp.sum(p, axis=-1, keepdims=True)).astype(o_ref.dtype)


def _mla_attn_lat_kernel(q_ref, kc_ref, kl_ref, vct_ref, vlt_ref, o_ref, s_scr):
    tq = q_ref.shape[0]
    q = q_ref[...]
    L, S = kc_ref.shape[0], kl_ref.shape[0]
    tk = min(MLA_KEY_CHUNK, S)
    chunks = [(kc_ref, vct_ref, 0, L, 0)] + [(kl_ref, vlt_ref, c * tk, tk, L + c * tk) for c in range(S // tk)]
    mx = jnp.full((tq, LANE), NEG_BIG, F32)
    for k_ref, _, off, w, so in chunks:
        s = _dot_nt(q, k_ref[off:off + w, :])
        s_scr[:, so:so + w] = s
        for g in range(w // LANE):
            mx = jnp.maximum(mx, s[:, g * LANE:(g + 1) * LANE])
    m = jnp.max(mx, axis=-1, keepdims=True)
    ls = jnp.zeros((tq, LANE), F32)
    acc = jnp.zeros((vct_ref.shape[0], tq), F32)
    for _, vt_ref, off, w, so in chunks:
        p = jnp.exp2(s_scr[:, so:so + w] - m)
        for g in range(w // LANE):
            ls = ls + p[:, g * LANE:(g + 1) * LANE]
        acc = acc + _dot_nt(vt_ref[:, off:off + w], p.astype(BF16))
    l = jnp.sum(ls, axis=-1, keepdims=True)
    o_ref[...] = (acc.T / l).astype(o_ref.dtype)


def mla_attention_ctx(dm, q, kc, vc):
    L = dm.L
    QP, V = MLA_QK_PAD, MLA_V
    return pl.pallas_call(
        _mla_attn_ctx_kernel,
        grid=(dm.B, MLA_HEADS),
        in_specs=[pl.BlockSpec((L, QP), lambda b, h: (b, h)),
                  pl.BlockSpec((L, QP), lambda b, h: (b, h)),
                  pl.BlockSpec((L, V), lambda b, h: (b, h))],
        out_specs=pl.BlockSpec((L, V), lambda b, h: (b, h)),
        out_shape=jax.ShapeDtypeStruct((dm.RC, MLA_HEADS * V), BF16),
        compiler_params=_params(2),
        name="mla_attn_ctx",
    )(q, kc, vc)


def mla_attention_lat(dm, q, kc, kl, vct, vlt):
    L, S = dm.L, dm.S
    tq = min(512, S)
    nq = S // tq
    QP, V = MLA_QK_PAD, MLA_V
    H = MLA_HEADS
    return pl.pallas_call(
        _mla_attn_lat_kernel,
        grid=(dm.B, H, nq),
        in_specs=[pl.BlockSpec((tq, QP), lambda b, h, n: (b * nq + n, h)),
                  pl.BlockSpec((L, QP), lambda b, h, n: (b, h)),
                  pl.BlockSpec((S, QP), lambda b, h, n: (b, h)),
                  pl.BlockSpec((V, L), lambda b, h, n: (b * H + h, 0)),
                  pl.BlockSpec((V, S), lambda b, h, n: (b * H + h, 0))],
        out_specs=pl.BlockSpec((tq, V), lambda b, h, n: (b * nq + n, h)),
        out_shape=jax.ShapeDtypeStruct((dm.RL, H * V), BF16),
        scratch_shapes=[pltpu.VMEM((tq, L + S), F32)],
        compiler_params=_params(3),
        name="mla_attn_lat",
    )(q, kc, kl, vct, vlt)


SSM_SUPER = 16
SSM_BLOCK_GROUPS = LANE // SSM_GROUP


def _ssm_in_kernel(u_ref, wi_ref, ws_ref, y_ref, s_ref):
    a = _load_row_tiles(u_ref, SSM_CHUNK).astype(BF16)
    y_ref[...] = _dot(a, wi_ref[...])
    zs = _dot(a, ws_ref[...])
    for c in range(s_ref.shape[0]):
        s_ref[c] = zs[:, c * LANE:(c + 1) * LANE]


def ssm_chunk_in(z, u_col, w_intra, w_state, *, tr):
    R = z.shape[0]
    nblk, CW, _ = w_intra.shape
    nr = tr // SSM_CHUNK
    c0 = u_col // LANE
    w_spec = pl.BlockSpec((None, CW, CW), lambda j, i: (j, 0, 0), pipeline_mode=pl.Buffered(1))
    return pl.pallas_call(
        _ssm_in_kernel,
        grid=(nblk, R // tr),
        in_specs=[pl.BlockSpec((tr, LANE), lambda j, i: (i, c0 + j)), w_spec, w_spec],
        out_specs=[pl.BlockSpec((nr, CW), lambda j, i: (i, j)),
                   pl.BlockSpec((2 * SSM_BLOCK_GROUPS, nr, LANE), lambda j, i: (0, i, j))],
        out_shape=[jax.ShapeDtypeStruct((R // SSM_CHUNK, nblk * CW), F32),
                   jax.ShapeDtypeStruct((2 * SSM_BLOCK_GROUPS, R // SSM_CHUNK, nblk * LANE), F32)],
        compiler_params=_params(2),
        name="ssm_chunk_in",
    )(z, w_intra, w_state)


def _ssm_scan_kernel(s_ref, p1_ref, p2_ref, x_ref, t_ref, e_ref, *, batch, n_ctx_sc, n_lat_sc):
    SC = SSM_SUPER
    GB = SSM_BLOCK_GROUPS
    FWD, BWD = slice(0, GB), slice(GB, 2 * GB)
    n_sc = s_ref.shape[1] // SC

    def cmul(i, rows, x):
        swapped = jnp.concatenate([x[..., SSM_STATE:], x[..., :SSM_STATE]], axis=-1)
        return p1_ref[i, rows] * x + p2_ref[i, rows] * swapped

    def chunk(i):
        return pl.ds(i, n_sc, stride=SC)

    lf = jnp.zeros((GB, n_sc, LANE), F32)
    lb = jnp.zeros((GB, n_sc, LANE), F32)
    for i in range(SC):
        x_ref[FWD, chunk(i), :] = lf
        x_ref[BWD, chunk(SC - 1 - i), :] = lb
        lf = cmul(1, FWD, lf) + s_ref[FWD, chunk(i), :]
        lb = cmul(1, BWD, lb) + s_ref[BWD, chunk(SC - 1 - i), :]
    t_ref[FWD] = lf
    t_ref[BWD] = lb
    n_ctx = batch * n_ctx_sc
    for rows, order in ((FWD, 1), (BWD, -1)):
        e = jnp.zeros((GB, batch, LANE), F32)
        for region_start, per_batch in ((0, n_ctx_sc), (n_ctx, n_lat_sc)):
            steps = range(per_batch) if order == 1 else range(per_batch - 1, -1, -1)
            for m in steps:
                idx = pl.ds(region_start + m, batch, stride=per_batch)
                e_ref[rows, idx, :] = e
                e = cmul(SC, rows, e) + t_ref[rows, idx, :]
    ef = e_ref[FWD]
    eb = e_ref[BWD]
    for i in range(SC):
        x_ref[FWD, chunk(i), :] += cmul(i, FWD, ef)
        x_ref[BWD, chunk(SC - 1 - i), :] += cmul(i, BWD, eb)


def ssm_scan(dm, s, p1, p2):
    NS, NR, W = s.shape
    nblk = W // LANE
    n_sc = NR // SSM_SUPER
    n_ctx_sc = dm.L // (SSM_CHUNK * SSM_SUPER)
    n_lat_sc = dm.S // (SSM_CHUNK * SSM_SUPER)
    blk = pl.BlockSpec((NS, NR, LANE), lambda j: (0, 0, j))
    pspec = pl.BlockSpec((None, SSM_SUPER + 1, NS, 1, LANE), lambda j: (j, 0, 0, 0, 0))
    return pl.pallas_call(
        functools.partial(_ssm_scan_kernel, batch=dm.B, n_ctx_sc=n_ctx_sc, n_lat_sc=n_lat_sc),
        grid=(nblk,),
        in_specs=[blk, pspec, pspec],
        out_specs=blk,
        out_shape=jax.ShapeDtypeStruct(s.shape, F32),
        scratch_shapes=[pltpu.VMEM((NS, n_sc, LANE), F32), pltpu.VMEM((NS, n_sc, LANE), F32)],
        compiler_params=_params(1),
        name="ssm_scan",
    )(s, p1, p2)


def _ssm_out_kernel(y_ref, x_ref, u_ref, w_ref, d_ref, o_ref):
    nr = y_ref.shape[0]
    xs = jnp.concatenate([x_ref[c] for c in range(x_ref.shape[0])], axis=1).astype(BF16)
    y = y_ref[...] + _dot(xs, w_ref[...])
    d = d_ref[...]
    for t in range(SSM_CHUNK):
        rows = pl.ds(t, nr, stride=SSM_CHUNK)
        o_ref[rows, :] = _gelu_tanh(y[:, t * LANE:(t + 1) * LANE] + d * u_ref[rows, :])


def ssm_chunk_out(y_intra, x_states, z, u_col, w_out_state, d_skip, *, tr):
    R = z.shape[0]
    nblk, CW, _ = w_out_state.shape
    nr = tr // SSM_CHUNK
    c0 = u_col // LANE
    return pl.pallas_call(
        _ssm_out_kernel,
        grid=(nblk, R // tr),
        in_specs=[pl.BlockSpec((nr, CW), lambda j, i: (i, j)),
                  pl.BlockSpec((2 * SSM_BLOCK_GROUPS, nr, LANE), lambda j, i: (0, i, j)),
                  pl.BlockSpec((tr, LANE), lambda j, i: (i, c0 + j)),
                  pl.BlockSpec((None, CW, CW), lambda j, i: (j, 0, 0), pipeline_mode=pl.Buffered(1)),
                  pl.BlockSpec((1, LANE), lambda j, i: (0, j))],
        out_specs=pl.BlockSpec((tr, LANE), lambda j, i: (i, j)),
        out_shape=jax.ShapeDtypeStruct((R, nblk * LANE), F32),
        compiler_params=_params(2),
        name="ssm_chunk_out",
    )(y_intra, x_states, z, w_out_state, d_skip.astype(F32).reshape(1, -1))


def _ssm_expand_kernel(k_ref, o_ref, *, mode):
    C, H, GB = SSM_CHUNK, SSM_GROUP, SSM_BLOCK_GROUPS
    CH = C * H
    W = o_ref.shape[1]
    ri = lax.broadcasted_iota(jnp.int32, (CH, W), 0)
    ci = lax.broadcasted_iota(jnp.int32, (CH, W), 1)
    for gl in range(GB):
        kc = k_ref[gl].astype(BF16)
        if mode == "state":
            zero = jnp.zeros((CH, LANE), BF16)
            cols = [kc[:, d * LANE:(d + 1) * LANE] if g2 == gl else zero for d in range(2) for g2 in range(GB)]
            t = jnp.concatenate(cols, axis=1)
        else:
            sel = jnp.where(ci == (ri >> 4) * LANE + gl * H + (ri & (H - 1)), 1.0, 0.0).astype(BF16)
            t = _dot(kc, sel).astype(BF16)
        if mode == "out":
            for d in range(2):
                o_ref[d * GB * LANE + gl * LANE:d * GB * LANE + (gl + 1) * LANE, :] = t[d * LANE:(d + 1) * LANE, :]
        else:
            for s in range(C):
                o_ref[s * LANE + gl * H:s * LANE + (gl + 1) * H, :] = t[s * H:(s + 1) * H, :]


def _ssm_expand(compact, mode):
    G, CH, _ = compact.shape
    GB = SSM_BLOCK_GROUPS
    CW = CH * GB
    return pl.pallas_call(
        functools.partial(_ssm_expand_kernel, mode=mode),
        grid=(G // GB,),
        in_specs=[pl.BlockSpec((GB, CH, CH), lambda j: (j, 0, 0))],
        out_specs=pl.BlockSpec((None, CW, CW), lambda j: (j, 0, 0)),
        out_shape=jax.ShapeDtypeStruct((G // GB, CW, CW), BF16),
        compiler_params=_params(1),
        name="ssm_expand_" + mode,
    )(compact)


def _ssm_tables(lam_re, lam_im, log_step, b_re, b_im, c_re, c_im):
    C, H, P = SSM_CHUNK, SSM_GROUP, SSM_STATE
    G = lam_re.shape[1]
    delta = jnp.exp(log_step.astype(F32))[..., None]
    zr = lam_re.astype(F32) * delta
    zi = lam_im.astype(F32) * delta
    k = jnp.arange(C + 1, dtype=F32)[:, None, None, None]
    mag = jnp.exp(k * zr[None])
    pw_re = mag * jnp.cos(k * zi[None])
    pw_im = mag * jnp.sin(k * zi[None])
    lb_re, lb_im = pw_re[1], pw_im[1]
    lr, li = lam_re.astype(F32), lam_im.astype(F32)
    den = lr * lr + li * li
    f_re = ((lb_re - 1.0) * lr + lb_im * li) / den
    f_im = (lb_im * lr - (lb_re - 1.0) * li) / den
    br, bi = b_re.astype(F32), b_im.astype(F32)
    bb_re = f_re[..., None] * br - f_im[..., None] * bi
    bb_im = f_re[..., None] * bi + f_im[..., None] * br
    cr, ci = c_re.astype(F32), c_im.astype(F32)
    cl_re = cr[None] * pw_re[:, :, :, None, :] - ci[None] * pw_im[:, :, :, None, :]
    cl_im = cr[None] * pw_im[:, :, :, None, :] + ci[None] * pw_re[:, :, :, None, :]
    hp = lax.Precision.HIGHEST
    kern = (jnp.einsum("kdghp,dgpj->dgkhj", cl_re[:C], bb_re, precision=hp)
            - jnp.einsum("kdghp,dgpj->dgkhj", cl_im[:C], bb_im, precision=hp))
    k_idx = jnp.arange(C)[:, None, None]
    s_idx = jnp.arange(C)[None, :, None]
    t_idx = jnp.arange(C)[None, None, :]
    sel_f = (t_idx - s_idx == k_idx).astype(F32)
    sel_b = (s_idx - t_idx == k_idx).astype(F32)
    ksum = (jnp.einsum("kst,gkhj->gsjth", sel_f, kern[0], precision=hp)
            + jnp.einsum("kst,gkhj->gsjth", sel_b, kern[1], precision=hp))

    def state_in(d, power_of_s):
        pr = pw_re[power_of_s, d]
        pi = pw_im[power_of_s, d]
        re = pr[..., None] * bb_re[d][None] - pi[..., None] * bb_im[d][None]
        im = pr[..., None] * bb_im[d][None] + pi[..., None] * bb_re[d][None]
        return jnp.concatenate([re, im], axis=2).transpose(1, 0, 3, 2)

    def state_out(d, power_of_t):
        re = cl_re[power_of_t, d]
        im = cl_im[power_of_t, d]
        return jnp.concatenate([re, -im], axis=-1).transpose(1, 3, 0, 2)

    m_sum = jnp.stack([state_in(0, C - 1 - jnp.arange(C)), state_in(1, jnp.arange(C))])
    m_out = jnp.stack([state_out(0, 1 + jnp.arange(C)), state_out(1, C - jnp.arange(C))])

    GB = SSM_BLOCK_GROUPS
    nblk = G // GB
    w_intra = _ssm_expand(ksum.reshape(G, C * H, C * H), "intra")
    w_state = _ssm_expand(m_sum.transpose(1, 2, 3, 0, 4).reshape(G, C * H, 2 * 2 * P), "state")
    w_out_state = _ssm_expand(m_out.transpose(1, 0, 2, 3, 4).reshape(G, 2 * 2 * P, C * H), "out")
    i = (C * jnp.arange(SSM_SUPER + 1, dtype=F32))[:, None, None, None]
    mag_a = jnp.exp(i * zr[None])
    pa_re = mag_a * jnp.cos(i * zi[None])
    pa_im = mag_a * jnp.sin(i * zi[None])

    def scan_table(lo, hi):
        t = jnp.concatenate([lo, hi], axis=-1).reshape(SSM_SUPER + 1, 2, nblk, GB, 2 * P)
        return t.transpose(2, 0, 1, 3, 4).reshape(nblk, SSM_SUPER + 1, 2 * GB, 1, 2 * P)

    return w_intra, w_state, w_out_state, scan_table(pa_re, pa_re), scan_table(-pa_im, pa_im)


def s5_branch(dm, z, col, tables, d_skip):
    w_intra, w_state, w_out_state, p1, p2 = tables
    nr = _pick_tile((272, 136, 96, 64, 32, 16, 8), dm.R // SSM_CHUNK)
    tr = nr * SSM_CHUNK
    y_intra, s = ssm_chunk_in(z, col["u"], w_intra, w_state, tr=tr)
    x_states = ssm_scan(dm, s, p1, p2)
    return ssm_chunk_out(y_intra, x_states, z, col["u"], w_out_state, d_skip, tr=tr)


def _merge_kernel(ya_ref, yb_ref, yc_ref, ga_ref, gb_ref, gc_ref, wa_ref, wb_ref, wc_ref, o_ref, sa, sb, sc):
    @pl.when(pl.program_id(1) == 0)
    def _():
        for w_ref, s in ((wa_ref, sa), (wb_ref, sb), (wc_ref, sc)):
            s[...] = w_ref[...].astype(BF16)

    acc = ga_ref[...].astype(F32) * _dot(ya_ref[...], sa[...])
    acc = acc + gb_ref[...].astype(F32) * _dot(yb_ref[...], sb[...])
    acc = acc + gc_ref[...].astype(F32) * _dot(yc_ref[...], sc[...])
    o_ref[...] = acc.astype(o_ref.dtype)


def merge_branches(dm, ys, y_row0s, gates, w_branch, layer, *, row0, n_rows):
    BW = ys[0].shape[1]
    D = w_branch.shape[-1]
    tm = dm.tm
    tn = 512
    t0 = row0 // tm
    nj = D // tn
    y_specs = [pl.BlockSpec((tm, BW), functools.partial(lambda j, i, o: (i + o, 0), o=(row0 - y0) // tm))
               for y0 in y_row0s]
    g_specs = [pl.BlockSpec((tm, tn), functools.partial(lambda j, i, n: (i + t0, n * nj + j), n=n))
               for n in range(N_BRANCH)]
    w_specs = [pl.BlockSpec((None, None, BW, tn), functools.partial(lambda j, i, n: (layer, n, 0, j), n=n))
               for n in range(N_BRANCH)]
    return pl.pallas_call(
        _merge_kernel,
        grid=(nj, n_rows // tm),
        in_specs=y_specs + g_specs + w_specs,
        out_specs=pl.BlockSpec((tm, tn), lambda j, i: (i, j)),
        out_shape=jax.ShapeDtypeStruct((n_rows, D), BF16),
        scratch_shapes=[pltpu.VMEM((BW, tn), BF16)] * 3,
        compiler_params=_params(2),
        name="merge_branches",
    )(*ys, gates, gates, gates, w_branch, w_branch, w_branch)


def _router_kernel(h_ref, whi_ref, wlo_ref, b_ref, idx_ref, w_ref):
    half = whi_ref.shape[0] // 2
    lo, hi = _unpack_bf16_pairs(_load_row_tiles(h_ref, half // LANE))
    logits = b_ref[...]
    for w in (whi_ref, wlo_ref):
        logits = logits + _dot(lo, w[:half, :]) + _dot(hi, w[half:, :])
    lane = lax.broadcasted_iota(jnp.int32, logits.shape, 1).astype(F32)
    logits = jnp.where(lane < N_EXPERTS, logits, NEG_BIG)
    m1 = jnp.max(logits, axis=-1, keepdims=True)
    i1 = jnp.min(jnp.where(logits == m1, lane, float(LANE)), axis=-1, keepdims=True)
    rest = jnp.where(lane == i1, NEG_BIG, logits)
    m2 = jnp.max(rest, axis=-1, keepdims=True)
    i2 = jnp.min(jnp.where(rest == m2, lane, float(LANE)), axis=-1, keepdims=True)
    e = jnp.exp(m2 - m1)
    w1 = 1.0 / (1.0 + e)
    w2 = e / (1.0 + e)
    idx_ref[...] = jnp.where(lane == 0.0, i1, jnp.where(lane == 1.0, i2, 0.0)).astype(jnp.int32)
    w_ref[...] = jnp.where(lane == 0.0, w1, jnp.where(lane == 1.0, w2, 0.0))


def moe_router(hp, w_router, b_router):
    D = w_router.shape[0]
    n = D // 2 // LANE
    M = hp.shape[0] // n
    tm = _pick_tile((1024, 512, 256, 128), M)
    w_pad = jnp.zeros((D, LANE), F32).at[:, :N_EXPERTS].set(w_router.astype(F32))
    w_hi = w_pad.astype(BF16)
    w_lo = (w_pad - w_hi.astype(F32)).astype(BF16)
    b_pad = jnp.zeros((1, LANE), F32).at[0, :N_EXPERTS].set(b_router.astype(F32))
    return pl.pallas_call(
        _router_kernel,
        grid=(M // tm,),
        in_specs=[pl.BlockSpec((tm * n, LANE), lambda i: (i, 0)),
                  pl.BlockSpec((D, LANE), lambda i: (0, 0)),
                  pl.BlockSpec((D, LANE), lambda i: (0, 0)),
                  pl.BlockSpec((1, LANE), lambda i: (0, 0))],
        out_specs=[pl.BlockSpec((tm, LANE), lambda i: (i, 0)), pl.BlockSpec((tm, LANE), lambda i: (i, 0))],
        out_shape=[jax.ShapeDtypeStruct((M, LANE), jnp.int32), jax.ShapeDtypeStruct((M, LANE), F32)],
        compiler_params=_params(1),
        name="moe_router",
    )(hp, w_hi, w_lo, b_pad)


GATHER_UNROLL = 8


def _gather_rows_kernel(idx_ref, src_ref, o_ref, sem, *, n):
    tg = o_ref.shape[0] // n
    base = pl.program_id(0) * tg

    def start(it, carry):
        for u in range(GATHER_UNROLL):
            r = it * GATHER_UNROLL + u
            src_row = pl.multiple_of(idx_ref[base + r] * n, n)
            dst_row = pl.multiple_of(r * n, n)
            pltpu.make_async_copy(src_ref.at[pl.ds(src_row, n)], o_ref.at[pl.ds(dst_row, n)], sem).start(
                priority=u % 2)
        return carry

    lax.fori_loop(0, tg // GATHER_UNROLL, start, 0)
    pltpu.make_async_copy(src_ref.at[pl.ds(0, tg * n)], o_ref, sem).wait()


def gather_rows(src, idx, n, *, tg=256):
    M = idx.shape[0]
    return pl.pallas_call(
        functools.partial(_gather_rows_kernel, n=n),
        grid_spec=pltpu.PrefetchScalarGridSpec(
            num_scalar_prefetch=1,
            grid=(M // tg,),
            in_specs=[pl.BlockSpec(memory_space=pl.ANY)],
            out_specs=pl.BlockSpec((tg * n, LANE), lambda i, idx_ref: (i, 0)),
            scratch_shapes=[pltpu.SemaphoreType.DMA(())],
        ),
        out_shape=jax.ShapeDtypeStruct((M * n, LANE), src.dtype),
        compiler_params=_params(1),
        name="gather_rows",
    )(idx, src)


def _moe_w13_kernel(te_ref, tv_ref, a_ref, wg_ref, wu_ref, o_ref, sg, su):
    i = pl.program_id(1)
    prev = te_ref[jnp.maximum(i - 1, 0)]

    @pl.when(jnp.logical_or(i == 0, te_ref[i] != prev))
    def _():
        sg[...] = wg_ref[...].astype(BF16)
        su[...] = wu_ref[...].astype(BF16)

    @pl.when(tv_ref[i] == 1)
    def _():
        half = sg.shape[0] // 2
        lo, hi = _unpack_bf16_pairs(_load_row_tiles(a_ref, half // LANE))
        g = _dot(lo, sg[:half, :]) + _dot(hi, sg[half:, :])
        u = _dot(lo, su[:half, :]) + _dot(hi, su[half:, :])
        o_ref[...] = (_silu(g) * u).astype(o_ref.dtype)

    @pl.when(tv_ref[i] == 0)
    def _():
        o_ref[...] = jnp.zeros_like(o_ref)


def moe_w13(xs, w13, moe_idx, tile_expert, tile_valid, *, tn):
    D = w13.shape[2]
    n = D // 2 // LANE
    P = xs.shape[0] // n
    F = w13.shape[-1] // 2
    tm = MOE_TM
    nj = F // tn
    return pl.pallas_call(
        _moe_w13_kernel,
        grid_spec=pltpu.PrefetchScalarGridSpec(
            num_scalar_prefetch=2,
            grid=(nj, P // tm),
            in_specs=[pl.BlockSpec((tm * n, LANE), lambda j, i, te, tv: (i, 0)),
                      pl.BlockSpec((None, None, D, tn), lambda j, i, te, tv: (moe_idx, te[i], 0, j)),
                      pl.BlockSpec((None, None, D, tn), lambda j, i, te, tv: (moe_idx, te[i], 0, j + nj))],
            out_specs=pl.BlockSpec((tm, tn), lambda j, i, te, tv: (i, j)),
            scratch_shapes=[pltpu.VMEM((D, tn), BF16), pltpu.VMEM((D, tn), BF16)],
        ),
        out_shape=jax.ShapeDtypeStruct((P, F), BF16),
        compiler_params=_params(2),
        name="moe_w13",
    )(tile_expert, tile_valid, xs, w13, w13)


def _moe_w2_kernel(te_ref, tv_ref, ts_ref, a_ref, w_ref, o_ref):
    i = pl.program_id(1)

    @pl.when(tv_ref[i] == 1)
    def _():
        _store_row_tiles(o_ref, _dot(a_ref[...], w_ref[...]))

    @pl.when(tv_ref[i] == 0)
    def _():
        o_ref[...] = jnp.zeros_like(o_ref)


def moe_w2(act, w2, tile_expert, tile_valid, tile_src, *, tn):
    P, F = act.shape
    D = w2.shape[-1]
    tm = MOE_TM
    n = tn // LANE
    return pl.pallas_call(
        _moe_w2_kernel,
        grid_spec=pltpu.PrefetchScalarGridSpec(
            num_scalar_prefetch=3,
            grid=(D // tn, P // tm),
            in_specs=[pl.BlockSpec((tm, F), lambda j, i, te, tv, ts: (ts[i], 0)),
                      pl.BlockSpec((None, F, tn), lambda j, i, te, tv, ts: (te[i], 0, j))],
            out_specs=pl.BlockSpec((None, tm * n, LANE), lambda j, i, te, tv, ts: (j, i, 0)),
        ),
        out_shape=jax.ShapeDtypeStruct((D // tn, P * n, LANE), F32),
        compiler_params=_params(2),
        name="moe_w2",
    )(tile_expert, tile_valid, tile_src, act, w2)


def _moe_combine_kernel(p0_ref, p1_ref, y_ref, x_ref, gate_ref, w_ref, o_ref, b0, b1, sem):
    tc = o_ref.shape[0]
    n = b0.shape[1] // tc
    base = pl.program_id(0) * tc

    def start(r, carry):
        dst = pl.ds(pl.multiple_of(r * n, n), n)
        src0 = pl.ds(pl.multiple_of(p0_ref[base + r] * n, n), n)
        src1 = pl.ds(pl.multiple_of(p1_ref[base + r] * n, n), n)
        pltpu.make_async_copy(y_ref.at[:, src0], b0.at[:, dst], sem.at[0]).start(priority=0)
        pltpu.make_async_copy(y_ref.at[:, src1], b1.at[:, dst], sem.at[1]).start(priority=1)
        return carry

    lax.fori_loop(0, tc, start, 0, unroll=GATHER_UNROLL)
    pltpu.make_async_copy(y_ref.at[:, pl.ds(0, tc * n)], b0, sem.at[0]).wait()
    pltpu.make_async_copy(y_ref.at[:, pl.ds(0, tc * n)], b1, sem.at[1]).wait()
    w = w_ref[...]
    y = w[:, 0:1] * _load_row_tiles(b0, n) + w[:, 1:2] * _load_row_tiles(b1, n)
    o_ref[...] = x_ref[...] + gate_ref[0] * y


def moe_combine(dm, y_sorted, pos0, pos1, top_w, x, gate, *, mod_row0):
    M, D = x.shape
    tc = 128
    t0 = mod_row0 // tc
    J = y_sorted.shape[0]
    n = D // (J * LANE)
    return pl.pallas_call(
        _moe_combine_kernel,
        grid_spec=pltpu.PrefetchScalarGridSpec(
            num_scalar_prefetch=2,
            grid=(M // tc,),
            in_specs=[pl.BlockSpec(memory_space=pl.ANY),
                      pl.BlockSpec((tc, D), lambda i, p0, p1: (i, 0)),
                      pl.BlockSpec((1, 1, D), lambda i, p0, p1: (dm.mod_row(i + t0, tc), 0, 0)),
                      pl.BlockSpec((tc, LANE), lambda i, p0, p1: (i, 0))],
            out_specs=pl.BlockSpec((tc, D), lambda i, p0, p1: (i, 0)),
            scratch_shapes=[pltpu.VMEM((J, tc * n, LANE), F32), pltpu.VMEM((J, tc * n, LANE), F32),
                            pltpu.SemaphoreType.DMA((2,))],
        ),
        out_shape=jax.ShapeDtypeStruct((M, D), F32),
        compiler_params=_params(1),
        name="moe_combine",
    )(pos0, pos1, y_sorted, x, gate, top_w)


def moe_ffn(dm, hp, x, gate, w_router, b_router, w13, w2, moe_idx, *, mod_row0):
    M, D = x.shape
    E = N_EXPERTS
    tm = MOE_TM
    top_idx, top_w = moe_router(hp, w_router, b_router)
    e_flat = top_idx[:, :TOP_K].T.reshape(-1)
    onehot = (e_flat[:, None] == jnp.arange(E, dtype=jnp.int32)[None, :]).astype(jnp.int32)
    csum = jnp.cumsum(onehot, axis=0)
    counts = csum[-1]
    rank = jnp.sum((csum - onehot) * onehot, axis=1)
    padded = ((counts + tm - 1) // tm) * tm
    ends = jnp.cumsum(padded)
    starts = ends - padded
    pos = starts[e_flat] + rank
    P = TOP_K * M + E * tm
    n_tiles = P // tm
    tok = jnp.tile(jnp.arange(M, dtype=jnp.int32), TOP_K)
    gidx = jnp.zeros((P,), jnp.int32).at[pos].set(tok)
    tile_start = jnp.arange(n_tiles, dtype=jnp.int32) * tm
    tile_valid = (tile_start < ends[-1]).astype(jnp.int32)
    te = jnp.sum((tile_start[:, None] >= ends[None, :]).astype(jnp.int32), axis=1)
    last_e = jnp.sum((ends[-1] - 1 >= ends).astype(jnp.int32))
    tile_expert = jnp.minimum(te, last_e).astype(jnp.int32)

    xs = gather_rows(hp, gidx, D // 2 // LANE)
    F = w13.shape[-1] // 2
    act = moe_w13(xs, w13, moe_idx, tile_expert, tile_valid, tn=_pick_tile((1024, 512, 256, 128), F))
    tile_src = jnp.minimum(jnp.arange(n_tiles, dtype=jnp.int32), jnp.sum(tile_valid) - 1).astype(jnp.int32)
    y_sorted = moe_w2(act, w2[moe_idx].astype(BF16), tile_expert, tile_valid, tile_src, tn=min(1024, D))
    return moe_combine(dm, y_sorted, pos[:M], pos[M:], top_w, x, gate, mod_row0=mod_row0)


def _axial_angles(seq, rot_dim):
    rows = seq // GRID_W
    t_row = jnp.repeat(jnp.arange(rows, dtype=F32), GRID_W)
    t_col = jnp.tile(jnp.arange(GRID_W, dtype=F32), rows)
    quarter = rot_dim // 4
    inv_freq = ROPE_THETA ** (-jnp.arange(quarter, dtype=F32) / quarter)
    return jnp.concatenate([t_row[:, None] * inv_freq, t_col[:, None] * inv_freq], axis=-1)


def _rope_tables(dm):
    ident = min(dm.tm, 256)
    ang = _axial_angles(dm.S, SWA_HEAD_DIM)
    cos_a = jnp.concatenate([jnp.cos(ang), jnp.cos(ang)], axis=-1)
    sin_a = jnp.concatenate([-jnp.sin(ang), jnp.sin(ang)], axis=-1)
    cos_a = jnp.concatenate([jnp.ones((ident, LANE), F32), cos_a], axis=0)
    sin_a = jnp.concatenate([jnp.zeros((ident, LANE), F32), sin_a], axis=0)
    ang = _axial_angles(dm.S, MLA_ROPE)
    half = MLA_ROPE // 2
    zeros = jnp.zeros((dm.S, half), F32)
    pad = jnp.zeros((dm.S, LANE - MLA_ROPE), F32)
    c_b = jnp.concatenate([jnp.cos(ang), jnp.cos(ang), pad], axis=-1)
    s1_b = jnp.concatenate([-jnp.sin(ang), zeros, pad], axis=-1)
    s2_b = jnp.concatenate([zeros, jnp.sin(ang), pad], axis=-1)
    c_b = jnp.concatenate([jnp.ones((ident, LANE), F32), c_b], axis=0)
    s1_b = jnp.concatenate([jnp.zeros((ident, LANE), F32), s1_b], axis=0)
    s2_b = jnp.concatenate([jnp.zeros((ident, LANE), F32), s2_b], axis=0)
    return (cos_a, sin_a), (c_b, s1_b, s2_b)


def _pad_head_vec(g):
    return jnp.zeros((1, MLA_QK_PAD), F32).at[0, :MLA_QK].set(g.astype(F32))


def _trunk(x, c, ctx, c_ctx, mod_w, mod_b, norm_mix_g, norm_ffn_g, w_in,
           swa_q_norm_g, swa_k_norm_g, swa_sink,
           mla_q_a_norm_g, mla_w_uq, mla_kv_a_norm_g, mla_w_ukv, mla_q_norm_g, mla_k_norm_g,
           ssm_lam_re, ssm_lam_im, ssm_log_step, ssm_b_re, ssm_b_im, ssm_c_re, ssm_c_im,
           ssm_d, ssm_w_glu, ssm_b_glu, w_branch, w_out,
           ffn_w13, ffn_w2, moe_w_router, moe_b_router, moe_w13, moe_w2):
    B, S, D = x.shape
    L = ctx.shape[1]
    depth = mod_w.shape[0]
    dm = Dims(B, S, L)
    tm = dm.tm
    RC, RL, R = dm.RC, dm.RL, dm.R
    q_w = SWA_HEADS * SWA_HEAD_DIM
    kv_w = SWA_KV_HEADS * SWA_HEAD_DIM
    q_rank = mla_w_uq.shape[1]
    kv_rank = mla_w_ukv.shape[1]
    ssm_w = ssm_d.shape[1]
    n_gate = N_BRANCH * D
    src = {}
    off = 0
    for name, width in (("q", q_w), ("k", kv_w), ("v", kv_w), ("c_q", q_rank), ("c_kv", kv_rank),
                        ("kpe", MLA_ROPE), ("u", ssm_w), ("gates", n_gate)):
        src[name] = (off, width)
        off += width
    order = ("q", "u", "c_q", "k", "v", "c_kv", "kpe")
    col = {}
    off = 0
    for name in order:
        col[name] = off
        off += src[name][1]
    z_tn = 512
    z_cols = -(-off // z_tn) * z_tn

    (cos_a, sin_a), tabs_b = _rope_tables(dm)
    xall = jnp.concatenate([ctx.reshape(RC, D), x.reshape(RL, D)], axis=0).astype(F32)
    cond = jnp.zeros((8, D), F32).at[0].set(c_ctx.astype(F32)).at[1:1 + B].set(c.astype(F32))

    for layer in range(depth):
        with_ctx = layer < depth - 1
        row0 = 0 if with_ctx else RC
        n_rows = R - row0
        mods = mm1(cond, [(mod_w, (layer,), 0)], _epi_bias, n_rows=8, n_cols=6 * D, tm=8, tn=512, out_dtype=F32,
                   extras=[(mod_b.reshape(depth, 1, 6 * D), (None, 1, 512), lambda j, i: (layer, 0, j))],
                   prologue=lambda a: _silu(a).astype(BF16), name="ada_mod")
        sh_m, sc_m, g_m, sh_f, sc_f, g_f = [mods[:, i * D:(i + 1) * D].reshape(8, 1, D) for i in range(6)]

        h = modulate(dm, xall, norm_mix_g[layer], sh_m, sc_m, mod_row0=0)
        w_l = w_in[layer]
        w_rest = jnp.concatenate([w_l[:, src[n][0]:src[n][0] + src[n][1]] for n in order]
                                 + [jnp.zeros((D, z_cols - off), w_l.dtype)], axis=1)
        w_gates = w_l[:, src["gates"][0]:]
        z = mm1(h, [(w_rest, (), 0)], _epi_id, n_rows=R, n_cols=z_cols, tm=tm, tn=z_tn, out_dtype=F32, name="w_in")
        gates = mm1(h, [(w_gates, (), 0)], _epi_sigmoid, n_rows=R, n_cols=n_gate, tm=tm, tn=512,
                    out_dtype=BF16, name="w_in_gates")

        qa, ka, va = swa_prep(dm, z, col, swa_q_norm_g[layer], swa_k_norm_g[layer], cos_a, sin_a)
        ya_l = swa_attention(dm, qa, ka, va, swa_sink[layer], latent=True)
        cqn, ckvn = mla_lowrank_norm(dm, z, col, q_rank, kv_rank, mla_q_a_norm_g[layer], mla_kv_a_norm_g[layer])
        w_uq = mla_w_uq[layer].reshape(q_rank, MLA_HEADS, MLA_QK)
        w_uq = jnp.pad(w_uq, ((0, 0), (0, 0), (0, MLA_QK_PAD - MLA_QK))).reshape(q_rank, MLA_HEADS * MLA_QK_PAD)
        w_ukv = mla_w_ukv[layer].reshape(kv_rank, MLA_HEADS, MLA_NOPE + MLA_V)
        w_ukv = jnp.concatenate([w_ukv[:, :, :MLA_NOPE].reshape(kv_rank, -1),
                                 w_ukv[:, :, MLA_NOPE:].reshape(kv_rank, -1)], axis=1)
        qf = mm1(cqn, [(w_uq, (), 0)], _epi_id, n_rows=R, n_cols=w_uq.shape[1], tm=tm, tn=512, out_dtype=F32,
                 name="mla_uq")
        kvf = mm1(ckvn, [(w_ukv, (), 0)], _epi_id, n_rows=R, n_cols=w_ukv.shape[1], tm=tm, tn=512, out_dtype=F32,
                  name="mla_ukv")
        gq_pad = _pad_head_vec(mla_q_norm_g[layer])
        gk_pad = _pad_head_vec(mla_k_norm_g[layer])
        qm_c, km_c, vm_c, vt_c = mla_prep(dm, qf, kvf, z, col, gq_pad, gk_pad, tabs_b, row0=0, n_rows=RC,
                                          rows_per_batch=L)
        qm_l, km_l, _, vt_l = mla_prep(dm, qf, kvf, z, col, gq_pad, gk_pad, tabs_b, row0=RC, n_rows=RL,
                                       rows_per_batch=S)
        yb_l = mla_attention_lat(dm, qm_l, km_c, km_l, vt_c, vt_l)
        if with_ctx:
            ya = jnp.concatenate([swa_attention(dm, qa, ka, va, swa_sink[layer], latent=False), ya_l], axis=0)
            yb = jnp.concatenate([mla_attention_ctx(dm, qm_c, km_c, vm_c), yb_l], axis=0)
            y_row0s = (0, 0, 0)
        else:
            ya, yb = ya_l, yb_l
            y_row0s = (RC, RC, 0)
        tables = _ssm_tables(ssm_lam_re[layer], ssm_lam_im[layer], ssm_log_step[layer], ssm_b_re[layer],
                             ssm_b_im[layer], ssm_c_re[layer], ssm_c_im[layer])
        yg = s5_branch(dm, z, col, tables, ssm_d[layer])
        b_glu = ssm_b_glu.reshape(depth, 1, 2 * ssm_w)
        gl_tn = 512
        yc = mm1(yg, [(ssm_w_glu, (layer,), 0), (ssm_w_glu, (layer,), ssm_w)], _epi_glu_bias,
                 n_rows=R, n_cols=ssm_w, tm=tm, tn=gl_tn, out_dtype=BF16,
                 extras=[(b_glu, (None, 1, gl_tn), lambda j, i: (layer, 0, j)),
                         (b_glu, (None, 1, gl_tn), lambda j, i: (layer, 0, j + ssm_w // gl_tn))],
                 prologue=lambda a: a.astype(BF16), name="ssm_glu")
        mixed = merge_branches(dm, (ya, yb, yc), y_row0s, gates, w_branch, layer, row0=row0, n_rows=n_rows)
        t0 = row0 // tm
        x1 = mm1(mixed, [(w_out, (layer,), 0)], _epi_residual, n_rows=n_rows, n_cols=D, tm=tm, tn=512, out_dtype=F32,
                 extras=[(xall, (tm, 512), lambda j, i: (i + t0, j)),
                         (g_m, (1, 1, 512), lambda j, i: (dm.mod_row(i + t0, tm), 0, j))],
                 name="w_out")
        is_moe = layer % 2 == 1
        h2 = modulate(dm, x1, norm_ffn_g[layer], sh_f, sc_f, mod_row0=row0, pack=is_moe)
        if not is_moe:
            F = ffn_w13.shape[-1] // 2
            f_tn = _pick_tile((512, 256, 128), F)
            act = mm1(h2, [(ffn_w13, (layer // 2,), 0), (ffn_w13, (layer // 2,), F)], _epi_swiglu, n_rows=n_rows,
                      n_cols=F, tm=tm, tn=f_tn, out_dtype=BF16, name="ffn_w13")
            x2 = mm2_residual(dm, act, ffn_w2[layer // 2].astype(BF16), x1, g_f, mod_row0=row0, tm=tm, tk=f_tn)
        else:
            if with_ctx:
                raise NotImplementedError("a mixture-of-experts layer that still feeds context rows")
            x2 = moe_ffn(dm, h2, x1, g_f, moe_w_router[layer // 2], moe_b_router[layer // 2], moe_w13, moe_w2,
                         layer // 2, mod_row0=row0)
        xall = x2
    return xall.reshape(B, S, D)


def kernel(x, c, ctx, c_ctx, mod_w, mod_b, norm_mix_g, norm_ffn_g, w_in, swa_q_norm_g, swa_k_norm_g, swa_sink, mla_q_a_norm_g, mla_w_uq, mla_kv_a_norm_g, mla_w_ukv, mla_q_norm_g, mla_k_norm_g, ssm_lam_re, ssm_lam_im, ssm_log_step, ssm_b_re, ssm_b_im, ssm_c_re, ssm_c_im, ssm_d, ssm_w_glu, ssm_b_glu, w_branch, w_out, ffn_w13, ffn_w2, moe_w_router, moe_b_router, moe_w13, moe_w2):
    return _trunk(x, c, ctx, c_ctx, mod_w, mod_b, norm_mix_g, norm_ffn_g, w_in, swa_q_norm_g, swa_k_norm_g, swa_sink,
                  mla_q_a_norm_g, mla_w_uq, mla_kv_a_norm_g, mla_w_ukv, mla_q_norm_g, mla_k_norm_g,
                  ssm_lam_re, ssm_lam_im, ssm_log_step, ssm_b_re, ssm_b_im, ssm_c_re, ssm_c_im,
                  ssm_d, ssm_w_glu, ssm_b_glu, w_branch, w_out, ffn_w13, ffn_w2, moe_w_router, moe_b_router,
                  moe_w13, moe_w2)
```

```python
import functools
import math

import jax
import jax.numpy as jnp
from jax import lax
from jax.experimental import pallas as pl
from jax.experimental.pallas import tpu as pltpu

F32 = jnp.float32
BF16 = jnp.bfloat16

GRID_W = 64
ROPE_THETA = 10000.0
EPS = 1e-6
SWA_HEADS = 8
SWA_KV_HEADS = 2
SWA_HEAD_DIM = 128
SWA_WINDOW = 128
MLA_HEADS = 8
MLA_NOPE = 128
MLA_ROPE = 64
MLA_V = 128
MLA_QK = MLA_NOPE + MLA_ROPE
MLA_QK_PAD = 256
SSM_GROUP = 16
SSM_STATE = 64
SSM_CHUNK = 16
N_BRANCH = 3
N_EXPERTS = 8
TOP_K = 2
LANE = 128
VMEM_LIMIT_BYTES = 56 * 1024 * 1024
MOE_TM = 512
NEG_BIG = -1e30


def _params(n_grid):
    return pltpu.CompilerParams(dimension_semantics=("arbitrary",) * n_grid, vmem_limit_bytes=VMEM_LIMIT_BYTES)


def _pick_tile(candidates, *sizes):
    for t in candidates:
        if all(s % t == 0 for s in sizes):
            return t
    raise ValueError(f"no tile in {candidates} divides {sizes}")


class Dims:
    def __init__(self, batch, seq, ctx_len):
        self.B, self.S, self.L = batch, seq, ctx_len
        self.RC = batch * ctx_len
        self.RL = batch * seq
        self.R = self.RC + self.RL
        self.tm = _pick_tile((1024, 512, 256, 128), ctx_len * batch, seq)

    def mod_row(self, tile, tm):
        nct = self.RC // tm
        return jnp.where(tile < nct, 0, 1 + (tile - nct) // (self.S // tm))


def _silu(x):
    return x * (1.0 / (1.0 + jnp.exp(-x)))


def _sigmoid(x):
    return 1.0 / (1.0 + jnp.exp(-x))


def _gelu_tanh(x):
    c = math.sqrt(2.0 / math.pi)
    return 0.5 * x * (1.0 + jnp.tanh(c * (x + 0.044715 * (x * x * x))))


def _dot(a, b):
    return jnp.dot(a, b, preferred_element_type=F32)


def _dot_nt(a, b):
    return lax.dot_general(a, b, (((1,), (1,)), ((), ())), preferred_element_type=F32)


def _pack_bf16_pairs(y):
    half = y.shape[1] // 2
    bits = lax.bitcast_convert_type(y.astype(BF16).astype(F32), jnp.uint32)
    return (bits[:, :half] >> 16) | (bits[:, half:] & jnp.uint32(0xFFFF0000))


def _unpack_bf16_pairs(p):
    lo = lax.bitcast_convert_type(p << 16, F32).astype(BF16)
    hi = lax.bitcast_convert_type(p & jnp.uint32(0xFFFF0000), F32).astype(BF16)
    return lo, hi


def _store_row_tiles(ref, val):
    m, w = val.shape
    n = w // LANE
    for c in range(n):
        ref[pl.ds(c, m, stride=n), :] = val[:, c * LANE:(c + 1) * LANE]


def _load_row_tiles(ref, n):
    if len(ref.shape) == 2:
        m = ref.shape[0] // n
        return jnp.concatenate([ref[pl.ds(c, m, stride=n), :] for c in range(n)], axis=1)
    m = ref.shape[1] // n
    return jnp.concatenate([ref[j, pl.ds(c, m, stride=n), :] for j in range(ref.shape[0]) for c in range(n)], axis=1)


def _modulate_kernel(x_ref, g_ref, sh_ref, sc_ref, o_ref, *, pack):
    x = x_ref[...]
    ms = jnp.mean(x * x, axis=-1, keepdims=True)
    y = x * lax.rsqrt(ms + EPS) * g_ref[...]
    y = y * (1.0 + sc_ref[0]) + sh_ref[0]
    if pack:
        _store_row_tiles(o_ref, _pack_bf16_pairs(y))
    else:
        o_ref[...] = y.astype(o_ref.dtype)


def modulate(dm, x, g, shift, scale, *, mod_row0, pack=False):
    n_rows, D = x.shape
    tm = min(dm.tm, 512)
    t0 = mod_row0 // tm
    if pack:
        n = D // 2 // LANE
        out_spec = pl.BlockSpec((tm * n, LANE), lambda i: (i, 0))
        out_shape = jax.ShapeDtypeStruct((n_rows * n, LANE), jnp.uint32)
    else:
        out_spec = pl.BlockSpec((tm, D), lambda i: (i, 0))
        out_shape = jax.ShapeDtypeStruct((n_rows, D), BF16)
    return pl.pallas_call(
        functools.partial(_modulate_kernel, pack=pack),
        grid=(n_rows // tm,),
        in_specs=[
            pl.BlockSpec((tm, D), lambda i: (i, 0)),
            pl.BlockSpec((1, D), lambda i: (0, 0)),
            pl.BlockSpec((1, 1, D), lambda i: (dm.mod_row(i + t0, tm), 0, 0)),
            pl.BlockSpec((1, 1, D), lambda i: (dm.mod_row(i + t0, tm), 0, 0)),
        ],
        out_specs=out_spec,
        out_shape=out_shape,
        compiler_params=_params(1),
        name="modulate_packed" if pack else "modulate",
    )(x, g.reshape(1, D), shift, scale)


def _mm1_kernel(*refs, n_w, n_extra, epilogue, prologue):
    a_ref = refs[0]
    w_refs = refs[1:1 + n_w]
    extra = refs[1 + n_w:1 + n_w + n_extra]
    o_ref = refs[1 + n_w + n_extra]
    wb = refs[2 + n_w + n_extra:]

    @pl.when(pl.program_id(1) == 0)
    def _():
        for w_ref, b in zip(w_refs, wb):
            b[...] = w_ref[...].astype(BF16)

    a = a_ref[...]
    if prologue is not None:
        a = prologue(a)
    accs = [_dot(a, b[...]) for b in wb]
    o_ref[...] = epilogue(accs, *extra).astype(o_ref.dtype)


def mm1(a, weights, epilogue, *, n_rows, n_cols, tm, tn, out_dtype, a_row0=0, extras=(), prologue=None, name):
    K = a.shape[1]
    t0 = a_row0 // tm
    in_specs = [pl.BlockSpec((tm, K), lambda j, i: (i + t0, 0))]
    operands = [a]
    for w, lead, col0 in weights:
        c0 = col0 // tn
        in_specs.append(pl.BlockSpec((None,) * len(lead) + (K, tn),
                                     functools.partial(lambda j, i, lead, c0: lead + (0, j + c0), lead=lead, c0=c0)))
        operands.append(w)
    for arr, bshape, imap in extras:
        in_specs.append(pl.BlockSpec(bshape, imap))
        operands.append(arr)
    kern = functools.partial(_mm1_kernel, n_w=len(weights), n_extra=len(extras), epilogue=epilogue,
                             prologue=prologue)
    return pl.pallas_call(
        kern,
        grid=(n_cols // tn, n_rows // tm),
        in_specs=in_specs,
        out_specs=pl.BlockSpec((tm, tn), lambda j, i: (i, j)),
        out_shape=jax.ShapeDtypeStruct((n_rows, n_cols), out_dtype),
        scratch_shapes=[pltpu.VMEM((K, tn), BF16) for _ in weights],
        compiler_params=_params(2),
        name=name,
    )(*operands)


def _epi_id(accs):
    return accs[0]


def _epi_sigmoid(accs):
    return _sigmoid(accs[0])


def _epi_swiglu(accs):
    return _silu(accs[0]) * accs[1]


def _epi_bias(accs, b_ref):
    return accs[0] + b_ref[...]


def _epi_glu_bias(accs, ba_ref, bb_ref):
    return (accs[0] + ba_ref[...]) * _sigmoid(accs[1] + bb_ref[...])


def _epi_residual(accs, x_ref, gate_ref):
    return x_ref[...] + gate_ref[0] * accs[0]


def _mm2_kernel(a_ref, w_ref, x_ref, gate_ref, o_ref, acc_ref):
    k = pl.program_id(1)

    @pl.when(k == 0)
    def _():
        acc_ref[...] = jnp.zeros_like(acc_ref)

    acc_ref[...] += _dot(a_ref[...], w_ref[...])

    @pl.when(k == pl.num_programs(1) - 1)
    def _():
        o_ref[...] = x_ref[...] + gate_ref[0] * acc_ref[...]


def mm2_residual(dm, a, w, x, gate, *, mod_row0, tm, tk):
    M, K = a.shape
    N = w.shape[1]
    t0 = mod_row0 // tm
    return pl.pallas_call(
        _mm2_kernel,
        grid=(M // tm, K // tk),
        in_specs=[
            pl.BlockSpec((tm, tk), lambda i, k: (i, k)),
            pl.BlockSpec((tk, N), lambda i, k: (k, 0)),
            pl.BlockSpec((tm, N), lambda i, k: (i, 0)),
            pl.BlockSpec((1, 1, N), lambda i, k: (dm.mod_row(i + t0, tm), 0, 0)),
        ],
        out_specs=pl.BlockSpec((tm, N), lambda i, k: (i, 0)),
        out_shape=jax.ShapeDtypeStruct((M, N), F32),
        scratch_shapes=[pltpu.VMEM((tm, N), F32)],
        compiler_params=_params(2),
        name="mm2_residual",
    )(a, w, x, gate)


def _swa_prep_kernel(q_ref, k_ref, v_ref, gq_ref, gk_ref, cos_ref, sin_ref, qo_ref, ko_ref, vo_ref):
    c = cos_ref[...]
    s = sin_ref[...]

    def norm_rope(x, g, scale):
        ms = jnp.mean(x * x, axis=-1, keepdims=True)
        y = x * lax.rsqrt(ms + EPS) * g
        return (y * c + pltpu.roll(y, SWA_HEAD_DIM // 2, 1) * s) * scale

    gq = gq_ref[...]
    gk = gk_ref[...]
    for h in range(SWA_HEADS):
        sl = slice(h * SWA_HEAD_DIM, (h + 1) * SWA_HEAD_DIM)
        qo_ref[:, sl] = norm_rope(q_ref[:, sl], gq, SWA_HEAD_DIM ** -0.5).astype(BF16)
    for h in range(SWA_KV_HEADS):
        sl = slice(h * SWA_HEAD_DIM, (h + 1) * SWA_HEAD_DIM)
        ko_ref[:, sl] = norm_rope(k_ref[:, sl], gk, 1.0).astype(BF16)
    vo_ref[...] = v_ref[...].astype(BF16)


def _rope_tile_index(dm, tm):
    nct = dm.RC // tm
    return lambda i: (jnp.where(i < nct, 0, 1 + (i - nct) % (dm.S // tm)), 0)


def swa_prep(dm, z, col, gq, gk, cos_t, sin_t):
    tm = min(dm.tm, 256)
    QW = SWA_HEADS * SWA_HEAD_DIM
    KW = SWA_KV_HEADS * SWA_HEAD_DIM
    ridx = _rope_tile_index(dm, tm)
    return pl.pallas_call(
        _swa_prep_kernel,
        grid=(dm.R // tm,),
        in_specs=[
            pl.BlockSpec((tm, QW), lambda i: (i, col["q"] // QW)),
            pl.BlockSpec((tm, KW), lambda i: (i, col["k"] // KW)),
            pl.BlockSpec((tm, KW), lambda i: (i, col["v"] // KW)),
            pl.BlockSpec((1, SWA_HEAD_DIM), lambda i: (0, 0)),
            pl.BlockSpec((1, SWA_HEAD_DIM), lambda i: (0, 0)),
            pl.BlockSpec((tm, SWA_HEAD_DIM), ridx),
            pl.BlockSpec((tm, SWA_HEAD_DIM), ridx),
        ],
        out_specs=[
            pl.BlockSpec((tm, QW), lambda i: (i, 0)),
            pl.BlockSpec((tm, KW), lambda i: (i, 0)),
            pl.BlockSpec((tm, KW), lambda i: (i, 0)),
        ],
        out_shape=[
            jax.ShapeDtypeStruct((dm.R, QW), BF16),
            jax.ShapeDtypeStruct((dm.R, KW), BF16),
            jax.ShapeDtypeStruct((dm.R, KW), BF16),
        ],
        compiler_params=_params(1),
        name="swa_prep",
    )(z, z, z, gq.reshape(1, -1), gk.reshape(1, -1), cos_t, sin_t)


def _swa_attn_kernel(*refs, windowed, nb):
    if windowed:
        q_ref, kc_ref, kp_ref, kk_ref, kn_ref, vc_ref, vp_ref, vk_ref, vn_ref, sink_ref, o_ref = refs
    else:
        q_ref, kc_ref, vc_ref, sink_ref, o_ref = refs
    G = SWA_HEADS // SWA_KV_HEADS
    blk = q_ref.shape[0]
    Dh = SWA_HEAD_DIM
    q = jnp.concatenate([q_ref[:, g * Dh:(g + 1) * Dh] for g in range(G)], axis=0)
    sink = sink_ref[0][:, 0:1]
    scores = [_dot_nt(q, kc_ref[...])]
    values = [vc_ref[...]]
    if windowed:
        n = pl.program_id(2)
        qi = lax.broadcasted_iota(jnp.int32, (G * blk, blk), 0) % blk
        kj = lax.broadcasted_iota(jnp.int32, (G * blk, blk), 1)
        s_p = _dot_nt(q, kp_ref[...])
        s_p = jnp.where(kj >= qi, s_p, NEG_BIG)
        s_p = jnp.where(n >= 1, s_p, NEG_BIG)
        s_n = _dot_nt(q, kn_ref[...])
        s_n = jnp.where(kj <= qi, s_n, NEG_BIG)
        s_n = jnp.where(n <= nb - 2, s_n, NEG_BIG)
        scores += [s_p, _dot_nt(q, kk_ref[...]), s_n]
        values += [vp_ref[...], vk_ref[...], vn_ref[...]]
    m = sink
    for s in scores:
        m = jnp.maximum(m, jnp.max(s, axis=-1, keepdims=True))
    l = jnp.exp(sink - m)
    o = None
    for s, v in zip(scores, values):
        p = jnp.exp(s - m)
        l = l + jnp.sum(p, axis=-1, keepdims=True)
        pv = _dot(p.astype(BF16), v)
        o = pv if o is None else o + pv
    o = o / l
    for g in range(G):
        o_ref[:, g * Dh:(g + 1) * Dh] = o[g * blk:(g + 1) * blk].astype(o_ref.dtype)


def swa_attention(dm, qa, ka, va, sink, *, latent):
    G = SWA_HEADS // SWA_KV_HEADS
    Dh = SWA_HEAD_DIM
    blk = SWA_WINDOW
    L = dm.L
    sink_col = jnp.broadcast_to(sink.astype(F32).reshape(SWA_KV_HEADS, G, 1, 1),
                                (SWA_KV_HEADS, G, blk, LANE)).reshape(SWA_KV_HEADS, G * blk, LANE)
    sink_spec = pl.BlockSpec((1, G * blk, LANE), lambda b, h, n: (h, 0, 0))
    ctx_spec = pl.BlockSpec((L, Dh), lambda b, h, n: (b, h))
    if latent:
        nb = dm.S // blk
        base = dm.RC // blk

        def q_map(b, h, n):
            return (base + b * nb + n, h)

        def kv_map(off):
            return lambda b, h, n: (base + b * nb + jnp.clip(n + off, 0, nb - 1), h)

        win_specs = [pl.BlockSpec((blk, Dh), kv_map(off)) for off in (-1, 0, 1)]
        in_specs = ([pl.BlockSpec((blk, G * Dh), q_map), ctx_spec] + win_specs + [ctx_spec] + win_specs
                    + [sink_spec])
        operands = (qa, ka, ka, ka, ka, va, va, va, va, sink_col)
        n_out = dm.RL
    else:
        nb = L // blk
        in_specs = [pl.BlockSpec((blk, G * Dh), lambda b, h, n: (b * nb + n, h)), ctx_spec, ctx_spec, sink_spec]
        operands = (qa, ka, va, sink_col)
        n_out = dm.RC
    return pl.pallas_call(
        functools.partial(_swa_attn_kernel, windowed=latent, nb=nb),
        grid=(dm.B, SWA_KV_HEADS, nb),
        in_specs=in_specs,
        out_specs=pl.BlockSpec((blk, G * Dh), lambda b, h, n: (b * nb + n, h)),
        out_shape=jax.ShapeDtypeStruct((n_out, SWA_HEADS * Dh), BF16),
        compiler_params=_params(3),
        name="swa_attn_lat" if latent else "swa_attn_ctx",
    )(*operands)


def _rms_rows_kernel(a_ref, b_ref, ga_ref, gb_ref, ao_ref, bo_ref):
    for x_ref, g_ref, o_ref in ((a_ref, ga_ref, ao_ref), (b_ref, gb_ref, bo_ref)):
        x = x_ref[...]
        ms = jnp.mean(x * x, axis=-1, keepdims=True)
        o_ref[...] = (x * lax.rsqrt(ms + EPS) * g_ref[...]).astype(o_ref.dtype)


def mla_lowrank_norm(dm, z, col, q_rank, kv_rank, gq, gkv):
    tm = min(dm.tm, 512)
    return pl.pallas_call(
        _rms_rows_kernel,
        grid=(dm.R // tm,),
        in_specs=[
            pl.BlockSpec((tm, q_rank), lambda i: (i, col["c_q"] // q_rank)),
            pl.BlockSpec((tm, kv_rank), lambda i: (i, col["c_kv"] // kv_rank)),
            pl.BlockSpec((1, q_rank), lambda i: (0, 0)),
            pl.BlockSpec((1, kv_rank), lambda i: (0, 0)),
        ],
        out_specs=[pl.BlockSpec((tm, q_rank), lambda i: (i, 0)), pl.BlockSpec((tm, kv_rank), lambda i: (i, 0))],
        out_shape=[jax.ShapeDtypeStruct((dm.R, q_rank), BF16), jax.ShapeDtypeStruct((dm.R, kv_rank), BF16)],
        compiler_params=_params(1),
        name="mla_lowrank_norm",
    )(z, z, gq.reshape(1, -1), gkv.reshape(1, -1))


def _mla_prep_kernel(q_ref, kn_ref, v_ref, pe_ref, gq_ref, gk_ref, c_ref, s1_ref, s2_ref, qo_ref, ko_ref, vo_ref,
                     vt_ref):
    c = c_ref[...]
    s1 = s1_ref[...]
    s2 = s2_ref[...]
    gq = gq_ref[...]
    gk = gk_ref[...]
    scale = MLA_QK ** -0.5 * math.log2(math.e)

    def rope(x):
        return x * c + pltpu.roll(x, LANE - MLA_ROPE // 2, 1) * s1 + pltpu.roll(x, MLA_ROPE // 2, 1) * s2

    pe = pe_ref[...]
    pe_ss = jnp.sum(pe * pe, axis=-1, keepdims=True)
    for h in range(MLA_HEADS):
        lo = h * MLA_QK_PAD
        qh = q_ref[:, lo:lo + MLA_QK_PAD]
        inv = lax.rsqrt(jnp.sum(qh * qh, axis=-1, keepdims=True) * (1.0 / MLA_QK) + EPS)
        qn = qh * inv * gq
        qo_ref[:, lo:lo + MLA_NOPE] = (qn[:, :MLA_NOPE] * scale).astype(BF16)
        qo_ref[:, lo + MLA_NOPE:lo + MLA_QK_PAD] = (rope(qn[:, MLA_NOPE:]) * scale).astype(BF16)
        kh = kn_ref[:, h * MLA_NOPE:(h + 1) * MLA_NOPE]
        inv = lax.rsqrt((jnp.sum(kh * kh, axis=-1, keepdims=True) + pe_ss) * (1.0 / MLA_QK) + EPS)
        ko_ref[:, lo:lo + MLA_NOPE] = (kh * inv * gk[:, :MLA_NOPE]).astype(BF16)
        ko_ref[:, lo + MLA_NOPE:lo + MLA_QK_PAD] = rope(pe * inv * gk[:, MLA_NOPE:]).astype(BF16)
    v = v_ref[...]
    vo_ref[...] = v.astype(BF16)
    vt_ref[...] = v.T.astype(BF16)


def mla_prep(dm, qf, kvf, z, col, gq_pad, gk_pad, tabs, *, row0, n_rows, rows_per_batch):
    tm = min(dm.tm, 256)
    t0 = row0 // tm
    QW = MLA_HEADS * MLA_QK_PAD
    NW = MLA_HEADS * MLA_NOPE
    ridx = _rope_tile_index(dm, tm)
    rspec = pl.BlockSpec((tm, LANE), lambda i: ridx(i + t0))
    tpb = rows_per_batch // tm
    return pl.pallas_call(
        _mla_prep_kernel,
        grid=(n_rows // tm,),
        in_specs=[
            pl.BlockSpec((tm, QW), lambda i: (i + t0, 0)),
            pl.BlockSpec((tm, NW), lambda i: (i + t0, 0)),
            pl.BlockSpec((tm, NW), lambda i: (i + t0, 1)),
            pl.BlockSpec((tm, LANE), lambda i: (i + t0, col["kpe"] // LANE)),
            pl.BlockSpec((1, MLA_QK_PAD), lambda i: (0, 0)),
            pl.BlockSpec((1, MLA_QK_PAD), lambda i: (0, 0)),
            rspec, rspec, rspec,
        ],
        out_specs=[
            pl.BlockSpec((tm, QW), lambda i: (i, 0)),
            pl.BlockSpec((tm, QW), lambda i: (i, 0)),
            pl.BlockSpec((tm, NW), lambda i: (i, 0)),
            pl.BlockSpec((NW, tm), lambda i: (i // tpb, i % tpb)),
        ],
        out_shape=[
            jax.ShapeDtypeStruct((n_rows, QW), BF16),
            jax.ShapeDtypeStruct((n_rows, QW), BF16),
            jax.ShapeDtypeStruct((n_rows, NW), BF16),
            jax.ShapeDtypeStruct((n_rows // rows_per_batch * NW, rows_per_batch), BF16),
        ],
        compiler_params=_params(1),
        name="mla_prep",
    )(qf, kvf, kvf, z, gq_pad, gk_pad, *tabs)


MLA_KEY_CHUNK = 512


def _mla_attn_ctx_kernel(q_ref, kc_ref, vc_ref, o_ref):
    s = _dot_nt(q_ref[...], kc_ref[...])
    p = jnp.exp2(s - jnp.max(s, axis=-1, keepdims=True))
    o = _dot(p.astype(BF16), vc_ref[...])
    o_ref[...] = (o / jnp.sum(p, axis=-1, keepdims=True)).astype(o_ref.dtype)


def _mla_attn_lat_kernel(q_ref, kc_ref, kl_ref, vct_ref, vlt_ref, o_ref, s_scr):
    tq = q_ref.shape[0]
    q = q_ref[...]
    L, S = kc_ref.shape[0], kl_ref.shape[0]
    tk = min(MLA_KEY_CHUNK, S)
    chunks = [(kc_ref, vct_ref, 0, L, 0)] + [(kl_ref, vlt_ref, c * tk, tk, L + c * tk) for c in range(S // tk)]
    mx = jnp.full((tq, LANE), NEG_BIG, F32)
    for k_ref, _, off, w, so in chunks:
        s = _dot_nt(q, k_ref[off:off + w, :])
        s_scr[:, so:so + w] = s
        for g in range(w // LANE):
            mx = jnp.maximum(mx, s[:, g * LANE:(g + 1) * LANE])
    m = jnp.max(mx, axis=-1, keepdims=True)
    ls = jnp.zeros((tq, LANE), F32)
    acc = jnp.zeros((vct_ref.shape[0], tq), F32)
    for _, vt_ref, off, w, so in chunks:
        p = jnp.exp2(s_scr[:, so:so + w] - m)
        for g in range(w // LANE):
            ls = ls + p[:, g * LANE:(g + 1) * LANE]
        acc = acc + _dot_nt(vt_ref[:, off:off + w], p.astype(BF16))
    l = jnp.sum(ls, axis=-1, keepdims=True)
    o_ref[...] = (acc.T / l).astype(o_ref.dtype)


def mla_attention_ctx(dm, q, kc, vc):
    L = dm.L
    QP, V = MLA_QK_PAD, MLA_V
    return pl.pallas_call(
        _mla_attn_ctx_kernel,
        grid=(dm.B, MLA_HEADS),
        in_specs=[pl.BlockSpec((L, QP), lambda b, h: (b, h)),
                  pl.BlockSpec((L, QP), lambda b, h: (b, h)),
                  pl.BlockSpec((L, V), lambda b, h: (b, h))],
        out_specs=pl.BlockSpec((L, V), lambda b, h: (b, h)),
        out_shape=jax.ShapeDtypeStruct((dm.RC, MLA_HEADS * V), BF16),
        compiler_params=_params(2),
        name="mla_attn_ctx",
    )(q, kc, vc)


def mla_attention_lat(dm, q, kc, kl, vct, vlt):
    L, S = dm.L, dm.S
    tq = min(512, S)
    nq = S // tq
    QP, V = MLA_QK_PAD, MLA_V
    H = MLA_HEADS
    return pl.pallas_call(
        _mla_attn_lat_kernel,
        grid=(dm.B, H, nq),
        in_specs=[pl.BlockSpec((tq, QP), lambda b, h, n: (b * nq + n, h)),
                  pl.BlockSpec((L, QP), lambda b, h, n: (b, h)),
                  pl.BlockSpec((S, QP), lambda b, h, n: (b, h)),
                  pl.BlockSpec((V, L), lambda b, h, n: (b * H + h, 0)),
                  pl.BlockSpec((V, S), lambda b, h, n: (b * H + h, 0))],
        out_specs=pl.BlockSpec((tq, V), lambda b, h, n: (b * nq + n, h)),
        out_shape=jax.ShapeDtypeStruct((dm.RL, H * V), BF16),
        scratch_shapes=[pltpu.VMEM((tq, L + S), F32)],
        compiler_params=_params(3),
        name="mla_attn_lat",
    )(q, kc, kl, vct, vlt)


SSM_SUPER = 16
SSM_BLOCK_GROUPS = LANE // SSM_GROUP


def _ssm_in_kernel(u_ref, wi_ref, ws_ref, y_ref, s_ref):
    a = _load_row_tiles(u_ref, SSM_CHUNK).astype(BF16)
    y_ref[...] = _dot(a, wi_ref[...])
    zs = _dot(a, ws_ref[...])
    for c in range(s_ref.shape[0]):
        s_ref[c] = zs[:, c * LANE:(c + 1) * LANE]


def ssm_chunk_in(z, u_col, w_intra, w_state, *, tr):
    R = z.shape[0]
    nblk, CW, _ = w_intra.shape
    nr = tr // SSM_CHUNK
    c0 = u_col // LANE
    w_spec = pl.BlockSpec((None, CW, CW), lambda j, i: (j, 0, 0), pipeline_mode=pl.Buffered(1))
    return pl.pallas_call(
        _ssm_in_kernel,
        grid=(nblk, R // tr),
        in_specs=[pl.BlockSpec((tr, LANE), lambda j, i: (i, c0 + j)), w_spec, w_spec],
        out_specs=[pl.BlockSpec((nr, CW), lambda j, i: (i, j)),
                   pl.BlockSpec((2 * SSM_BLOCK_GROUPS, nr, LANE), lambda j, i: (0, i, j))],
        out_shape=[jax.ShapeDtypeStruct((R // SSM_CHUNK, nblk * CW), F32),
                   jax.ShapeDtypeStruct((2 * SSM_BLOCK_GROUPS, R // SSM_CHUNK, nblk * LANE), F32)],
        compiler_params=_params(2),
        name="ssm_chunk_in",
    )(z, w_intra, w_state)


def _ssm_scan_kernel(s_ref, p1_ref, p2_ref, x_ref, t_ref, e_ref, *, batch, n_ctx_sc, n_lat_sc):
    SC = SSM_SUPER
    GB = SSM_BLOCK_GROUPS
    FWD, BWD = slice(0, GB), slice(GB, 2 * GB)
    n_sc = s_ref.shape[1] // SC

    def cmul(i, rows, x):
        swapped = jnp.concatenate([x[..., SSM_STATE:], x[..., :SSM_STATE]], axis=-1)
        return p1_ref[i, rows] * x + p2_ref[i, rows] * swapped

    def chunk(i):
        return pl.ds(i, n_sc, stride=SC)

    lf = jnp.zeros((GB, n_sc, LANE), F32)
    lb = jnp.zeros((GB, n_sc, LANE), F32)
    for i in range(SC):
        x_ref[FWD, chunk(i), :] = lf
        x_ref[BWD, chunk(SC - 1 - i), :] = lb
        lf = cmul(1, FWD, lf) + s_ref[FWD, chunk(i), :]
        lb = cmul(1, BWD, lb) + s_ref[BWD, chunk(SC - 1 - i), :]
    t_ref[FWD] = lf
    t_ref[BWD] = lb
    n_ctx = batch * n_ctx_sc
    for rows, order in ((FWD, 1), (BWD, -1)):
        e = jnp.zeros((GB, batch, LANE), F32)
        for region_start, per_batch in ((0, n_ctx_sc), (n_ctx, n_lat_sc)):
            steps = range(per_batch) if order == 1 else range(per_batch - 1, -1, -1)
            for m in steps:
                idx = pl.ds(region_start + m, batch, stride=per_batch)
                e_ref[rows, idx, :] = e
                e = cmul(SC, rows, e) + t_ref[rows, idx, :]
    ef = e_ref[FWD]
    eb = e_ref[BWD]
    for i in range(SC):
        x_ref[FWD, chunk(i), :] += cmul(i, FWD, ef)
        x_ref[BWD, chunk(SC - 1 - i), :] += cmul(i, BWD, eb)


def ssm_scan(dm, s, p1, p2):
    NS, NR, W = s.shape
    nblk = W // LANE
    n_sc = NR // SSM_SUPER
    n_ctx_sc = dm.L // (SSM_CHUNK * SSM_SUPER)
    n_lat_sc = dm.S // (SSM_CHUNK * SSM_SUPER)
    blk = pl.BlockSpec((NS, NR, LANE), lambda j: (0, 0, j))
    pspec = pl.BlockSpec((None, SSM_SUPER + 1, NS, 1, LANE), lambda j: (j, 0, 0, 0, 0))
    return pl.pallas_call(
        functools.partial(_ssm_scan_kernel, batch=dm.B, n_ctx_sc=n_ctx_sc, n_lat_sc=n_lat_sc),
        grid=(nblk,),
        in_specs=[blk, pspec, pspec],
        out_specs=blk,
        out_shape=jax.ShapeDtypeStruct(s.shape, F32),
        scratch_shapes=[pltpu.VMEM((NS, n_sc, LANE), F32), pltpu.VMEM((NS, n_sc, LANE), F32)],
        compiler_params=_params(1),
        name="ssm_scan",
    )(s, p1, p2)


def _ssm_out_kernel(y_ref, x_ref, u_ref, w_ref, d_ref, o_ref):
    nr = y_ref.shape[0]
    xs = jnp.concatenate([x_ref[c] for c in range(x_ref.shape[0])], axis=1).astype(BF16)
    y = y_ref[...] + _dot(xs, w_ref[...])
    d = d_ref[...]
    for t in range(SSM_CHUNK):
        rows = pl.ds(t, nr, stride=SSM_CHUNK)
        o_ref[rows, :] = _gelu_tanh(y[:, t * LANE:(t + 1) * LANE] + d * u_ref[rows, :])


def ssm_chunk_out(y_intra, x_states, z, u_col, w_out_state, d_skip, *, tr):
    R = z.shape[0]
    nblk, CW, _ = w_out_state.shape
    nr = tr // SSM_CHUNK
    c0 = u_col // LANE
    return pl.pallas_call(
        _ssm_out_kernel,
        grid=(nblk, R // tr),
        in_specs=[pl.BlockSpec((nr, CW), lambda j, i: (i, j)),
                  pl.BlockSpec((2 * SSM_BLOCK_GROUPS, nr, LANE), lambda j, i: (0, i, j)),
                  pl.BlockSpec((tr, LANE), lambda j, i: (i, c0 + j)),
                  pl.BlockSpec((None, CW, CW), lambda j, i: (j, 0, 0), pipeline_mode=pl.Buffered(1)),
                  pl.BlockSpec((1, LANE), lambda j, i: (0, j))],
        out_specs=pl.BlockSpec((tr, LANE), lambda j, i: (i, j)),
        out_shape=jax.ShapeDtypeStruct((R, nblk * LANE), F32),
        compiler_params=_params(2),
        name="ssm_chunk_out",
    )(y_intra, x_states, z, w_out_state, d_skip.astype(F32).reshape(1, -1))


def _ssm_expand_kernel(k_ref, o_ref, *, mode):
    C, H, GB = SSM_CHUNK, SSM_GROUP, SSM_BLOCK_GROUPS
    CH = C * H
    W = o_ref.shape[1]
    ri = lax.broadcasted_iota(jnp.int32, (CH, W), 0)
    ci = lax.broadcasted_iota(jnp.int32, (CH, W), 1)
    for gl in range(GB):
        kc = k_ref[gl].astype(BF16)
        if mode == "state":
            zero = jnp.zeros((CH, LANE), BF16)
            cols = [kc[:, d * LANE:(d + 1) * LANE] if g2 == gl else zero for d in range(2) for g2 in range(GB)]
            t = jnp.concatenate(cols, axis=1)
        else:
            sel = jnp.where(ci == (ri >> 4) * LANE + gl * H + (ri & (H - 1)), 1.0, 0.0).astype(BF16)
            t = _dot(kc, sel).astype(BF16)
        if mode == "out":
            for d in range(2):
                o_ref[d * GB * LANE + gl * LANE:d * GB * LANE + (gl + 1) * LANE, :] = t[d * LANE:(d + 1) * LANE, :]
        else:
            for s in range(C):
                o_ref[s * LANE + gl * H:s * LANE + (gl + 1) * H, :] = t[s * H:(s + 1) * H, :]


def _ssm_expand(compact, mode):
    G, CH, _ = compact.shape
    GB = SSM_BLOCK_GROUPS
    CW = CH * GB
    return pl.pallas_call(
        functools.partial(_ssm_expand_kernel, mode=mode),
        grid=(G // GB,),
        in_specs=[pl.BlockSpec((GB, CH, CH), lambda j: (j, 0, 0))],
        out_specs=pl.BlockSpec((None, CW, CW), lambda j: (j, 0, 0)),
        out_shape=jax.ShapeDtypeStruct((G // GB, CW, CW), BF16),
        compiler_params=_params(1),
        name="ssm_expand_" + mode,
    )(compact)


def _ssm_tables(lam_re, lam_im, log_step, b_re, b_im, c_re, c_im):
    C, H, P = SSM_CHUNK, SSM_GROUP, SSM_STATE
    G = lam_re.shape[1]
    delta = jnp.exp(log_step.astype(F32))[..., None]
    zr = lam_re.astype(F32) * delta
    zi = lam_im.astype(F32) * delta
    k = jnp.arange(C + 1, dtype=F32)[:, None, None, None]
    mag = jnp.exp(k * zr[None])
    pw_re = mag * jnp.cos(k * zi[None])
    pw_im = mag * jnp.sin(k * zi[None])
    lb_re, lb_im = pw_re[1], pw_im[1]
    lr, li = lam_re.astype(F32), lam_im.astype(F32)
    den = lr * lr + li * li
    f_re = ((lb_re - 1.0) * lr + lb_im * li) / den
    f_im = (lb_im * lr - (lb_re - 1.0) * li) / den
    br, bi = b_re.astype(F32), b_im.astype(F32)
    bb_re = f_re[..., None] * br - f_im[..., None] * bi
    bb_im = f_re[..., None] * bi + f_im[..., None] * br
    cr, ci = c_re.astype(F32), c_im.astype(F32)
    cl_re = cr[None] * pw_re[:, :, :, None, :] - ci[None] * pw_im[:, :, :, None, :]
    cl_im = cr[None] * pw_im[:, :, :, None, :] + ci[None] * pw_re[:, :, :, None, :]
    hp = lax.Precision.HIGHEST
    kern = (jnp.einsum("kdghp,dgpj->dgkhj", cl_re[:C], bb_re, precision=hp)
            - jnp.einsum("kdghp,dgpj->dgkhj", cl_im[:C], bb_im, precision=hp))
    k_idx = jnp.arange(C)[:, None, None]
    s_idx = jnp.arange(C)[None, :, None]
    t_idx = jnp.arange(C)[None, None, :]
    sel_f = (t_idx - s_idx == k_idx).astype(F32)
    sel_b = (s_idx - t_idx == k_idx).astype(F32)
    ksum = (jnp.einsum("kst,gkhj->gsjth", sel_f, kern[0], precision=hp)
            + jnp.einsum("kst,gkhj->gsjth", sel_b, kern[1], precision=hp))

    def state_in(d, power_of_s):
        pr = pw_re[power_of_s, d]
        pi = pw_im[power_of_s, d]
        re = pr[..., None] * bb_re[d][None] - pi[..., None] * bb_im[d][None]
        im = pr[..., None] * bb_im[d][None] + pi[..., None] * bb_re[d][None]
        return jnp.concatenate([re, im], axis=2).transpose(1, 0, 3, 2)

    def state_out(d, power_of_t):
        re = cl_re[power_of_t, d]
        im = cl_im[power_of_t, d]
        return jnp.concatenate([re, -im], axis=-1).transpose(1, 3, 0, 2)

    m_sum = jnp.stack([state_in(0, C - 1 - jnp.arange(C)), state_in(1, jnp.arange(C))])
    m_out = jnp.stack([state_out(0, 1 + jnp.arange(C)), state_out(1, C - jnp.arange(C))])

    GB = SSM_BLOCK_GROUPS
    nblk = G // GB
    w_intra = _ssm_expand(ksum.reshape(G, C * H, C * H), "intra")
    w_state = _ssm_expand(m_sum.transpose(1, 2, 3, 0, 4).reshape(G, C * H, 2 * 2 * P), "state")
    w_out_state = _ssm_expand(m_out.transpose(1, 0, 2, 3, 4).reshape(G, 2 * 2 * P, C * H), "out")
    i = (C * jnp.arange(SSM_SUPER + 1, dtype=F32))[:, None, None, None]
    mag_a = jnp.exp(i * zr[None])
    pa_re = mag_a * jnp.cos(i * zi[None])
    pa_im = mag_a * jnp.sin(i * zi[None])

    def scan_table(lo, hi):
        t = jnp.concatenate([lo, hi], axis=-1).reshape(SSM_SUPER + 1, 2, nblk, GB, 2 * P)
        return t.transpose(2, 0, 1, 3, 4).reshape(nblk, SSM_SUPER + 1, 2 * GB, 1, 2 * P)

    return w_intra, w_state, w_out_state, scan_table(pa_re, pa_re), scan_table(-pa_im, pa_im)


def s5_branch(dm, z, col, tables, d_skip):
    w_intra, w_state, w_out_state, p1, p2 = tables
    nr = _pick_tile((272, 136, 96, 64, 32, 16, 8), dm.R // SSM_CHUNK)
    tr = nr * SSM_CHUNK
    y_intra, s = ssm_chunk_in(z, col["u"], w_intra, w_state, tr=tr)
    x_states = ssm_scan(dm, s, p1, p2)
    return ssm_chunk_out(y_intra, x_states, z, col["u"], w_out_state, d_skip, tr=tr)


def _merge_kernel(ya_ref, yb_ref, yc_ref, ga_ref, gb_ref, gc_ref, wa_ref, wb_ref, wc_ref, o_ref, sa, sb, sc):
    @pl.when(pl.program_id(1) == 0)
    def _():
        for w_ref, s in ((wa_ref, sa), (wb_ref, sb), (wc_ref, sc)):
            s[...] = w_ref[...].astype(BF16)

    acc = ga_ref[...].astype(F32) * _dot(ya_ref[...], sa[...])
    acc = acc + gb_ref[...].astype(F32) * _dot(yb_ref[...], sb[...])
    acc = acc + gc_ref[...].astype(F32) * _dot(yc_ref[...], sc[...])
    o_ref[...] = acc.astype(o_ref.dtype)


def merge_branches(dm, ys, y_row0s, gates, w_branch, layer, *, row0, n_rows):
    BW = ys[0].shape[1]
    D = w_branch.shape[-1]
    tm = min(dm.tm, 512)
    tn = 1024
    t0 = row0 // tm
    nj = D // tn
    y_specs = [pl.BlockSpec((tm, BW), functools.partial(lambda j, i, o: (i + o, 0), o=(row0 - y0) // tm))
               for y0 in y_row0s]
    g_specs = [pl.BlockSpec((tm, tn), functools.partial(lambda j, i, n: (i + t0, n * nj + j), n=n))
               for n in range(N_BRANCH)]
    w_specs = [pl.BlockSpec((None, None, BW, tn), functools.partial(lambda j, i, n: (layer, n, 0, j), n=n))
               for n in range(N_BRANCH)]
    return pl.pallas_call(
        _merge_kernel,
        grid=(nj, n_rows // tm),
        in_specs=y_specs + g_specs + w_specs,
        out_specs=pl.BlockSpec((tm, tn), lambda j, i: (i, j)),
        out_shape=jax.ShapeDtypeStruct((n_rows, D), BF16),
        scratch_shapes=[pltpu.VMEM((BW, tn), BF16)] * 3,
        compiler_params=_params(2),
        name="merge_branches",
    )(*ys, gates, gates, gates, w_branch, w_branch, w_branch)


def _router_kernel(h_ref, whi_ref, wlo_ref, b_ref, idx_ref, w_ref):
    half = whi_ref.shape[0] // 2
    lo, hi = _unpack_bf16_pairs(_load_row_tiles(h_ref, half // LANE))
    logits = b_ref[...]
    for w in (whi_ref, wlo_ref):
        logits = logits + _dot(lo, w[:half, :]) + _dot(hi, w[half:, :])
    lane = lax.broadcasted_iota(jnp.int32, logits.shape, 1).astype(F32)
    logits = jnp.where(lane < N_EXPERTS, logits, NEG_BIG)
    m1 = jnp.max(logits, axis=-1, keepdims=True)
    i1 = jnp.min(jnp.where(logits == m1, lane, float(LANE)), axis=-1, keepdims=True)
    rest = jnp.where(lane == i1, NEG_BIG, logits)
    m2 = jnp.max(rest, axis=-1, keepdims=True)
    i2 = jnp.min(jnp.where(rest == m2, lane, float(LANE)), axis=-1, keepdims=True)
    e = jnp.exp(m2 - m1)
    w1 = 1.0 / (1.0 + e)
    w2 = e / (1.0 + e)
    idx_ref[...] = jnp.where(lane == 0.0, i1, jnp.where(lane == 1.0, i2, 0.0)).astype(jnp.int32)
    w_ref[...] = jnp.where(lane == 0.0, w1, jnp.where(lane == 1.0, w2, 0.0))


def moe_router(hp, w_router, b_router):
    D = w_router.shape[0]
    n = D // 2 // LANE
    M = hp.shape[0] // n
    tm = _pick_tile((1024, 512, 256, 128), M)
    w_pad = jnp.zeros((D, LANE), F32).at[:, :N_EXPERTS].set(w_router.astype(F32))
    w_hi = w_pad.astype(BF16)
    w_lo = (w_pad - w_hi.astype(F32)).astype(BF16)
    b_pad = jnp.zeros((1, LANE), F32).at[0, :N_EXPERTS].set(b_router.astype(F32))
    return pl.pallas_call(
        _router_kernel,
        grid=(M // tm,),
        in_specs=[pl.BlockSpec((tm * n, LANE), lambda i: (i, 0)),
                  pl.BlockSpec((D, LANE), lambda i: (0, 0)),
                  pl.BlockSpec((D, LANE), lambda i: (0, 0)),
                  pl.BlockSpec((1, LANE), lambda i: (0, 0))],
        out_specs=[pl.BlockSpec((tm, LANE), lambda i: (i, 0)), pl.BlockSpec((tm, LANE), lambda i: (i, 0))],
        out_shape=[jax.ShapeDtypeStruct((M, LANE), jnp.int32), jax.ShapeDtypeStruct((M, LANE), F32)],
        compiler_params=_params(1),
        name="moe_router",
    )(hp, w_hi, w_lo, b_pad)


GATHER_UNROLL = 8


def _gather_rows_kernel(idx_ref, src_ref, o_ref, buf, sem, *, n):
    tg = o_ref.shape[0]
    base = pl.program_id(0) * tg

    def start(it, carry):
        for u in range(GATHER_UNROLL):
            r = it * GATHER_UNROLL + u
            src_row = pl.multiple_of(idx_ref[base + r] * n, n)
            dst_row = pl.multiple_of(r * n, n)
            pltpu.make_async_copy(src_ref.at[pl.ds(src_row, n)], buf.at[pl.ds(dst_row, n)], sem).start(
                priority=u % 2)
        return carry

    lax.fori_loop(0, tg // GATHER_UNROLL, start, 0)
    pltpu.make_async_copy(src_ref.at[pl.ds(0, tg * n)], buf, sem).wait()
    lo, hi = _unpack_bf16_pairs(_load_row_tiles(buf, n))
    half = lo.shape[1]
    o_ref[:, :half] = lo
    o_ref[:, half:] = hi


def gather_rows(src, idx, n, *, tg=256):
    M = idx.shape[0]
    return pl.pallas_call(
        functools.partial(_gather_rows_kernel, n=n),
        grid_spec=pltpu.PrefetchScalarGridSpec(
            num_scalar_prefetch=1,
            grid=(M // tg,),
            in_specs=[pl.BlockSpec(memory_space=pl.ANY)],
            out_specs=pl.BlockSpec((tg, 2 * n * LANE), lambda i, idx_ref: (i, 0)),
            scratch_shapes=[pltpu.VMEM((tg * n, LANE), src.dtype), pltpu.SemaphoreType.DMA(())],
        ),
        out_shape=jax.ShapeDtypeStruct((M, 2 * n * LANE), BF16),
        compiler_params=_params(1),
        name="gather_rows",
    )(idx, src)


def _moe_w13_kernel(te_ref, tv_ref, a_ref, wg_ref, wu_ref, o_ref, sg, su):
    i = pl.program_id(1)
    prev = te_ref[jnp.maximum(i - 1, 0)]

    @pl.when(jnp.logical_or(i == 0, te_ref[i] != prev))
    def _():
        sg[...] = wg_ref[...].astype(BF16)
        su[...] = wu_ref[...].astype(BF16)

    @pl.when(tv_ref[i] == 1)
    def _():
        a = a_ref[...]
        o_ref[...] = (_silu(_dot(a, sg[...])) * _dot(a, su[...])).astype(o_ref.dtype)

    @pl.when(tv_ref[i] == 0)
    def _():
        o_ref[...] = jnp.zeros_like(o_ref)


def moe_w13(xs, w13, moe_idx, tile_expert, tile_valid, *, tn):
    P, D = xs.shape
    F = w13.shape[-1] // 2
    tm = MOE_TM
    nj = F // tn
    return pl.pallas_call(
        _moe_w13_kernel,
        grid_spec=pltpu.PrefetchScalarGridSpec(
            num_scalar_prefetch=2,
            grid=(nj, P // tm),
            in_specs=[pl.BlockSpec((tm, D), lambda j, i, te, tv: (i, 0)),
                      pl.BlockSpec((None, None, D, tn), lambda j, i, te, tv: (moe_idx, te[i], 0, j)),
                      pl.BlockSpec((None, None, D, tn), lambda j, i, te, tv: (moe_idx, te[i], 0, j + nj))],
            out_specs=pl.BlockSpec((tm, tn), lambda j, i, te, tv: (i, j)),
            scratch_shapes=[pltpu.VMEM((D, tn), BF16), pltpu.VMEM((D, tn), BF16)],
        ),
        out_shape=jax.ShapeDtypeStruct((P, F), BF16),
        compiler_params=_params(2),
        name="moe_w13",
    )(tile_expert, tile_valid, xs, w13, w13)


def _moe_w2_kernel(te_ref, tv_ref, ts_ref, a_ref, w_ref, o_ref):
    i = pl.program_id(1)

    @pl.when(tv_ref[i] == 1)
    def _():
        _store_row_tiles(o_ref, _dot(a_ref[...], w_ref[...]))

    @pl.when(tv_ref[i] == 0)
    def _():
        o_ref[...] = jnp.zeros_like(o_ref)


def moe_w2(act, w2, tile_expert, tile_valid, tile_src, *, tn):
    P, F = act.shape
    D = w2.shape[-1]
    tm = MOE_TM
    n = tn // LANE
    return pl.pallas_call(
        _moe_w2_kernel,
        grid_spec=pltpu.PrefetchScalarGridSpec(
            num_scalar_prefetch=3,
            grid=(D // tn, P // tm),
            in_specs=[pl.BlockSpec((tm, F), lambda j, i, te, tv, ts: (ts[i], 0)),
                      pl.BlockSpec((None, F, tn), lambda j, i, te, tv, ts: (te[i], 0, j))],
            out_specs=pl.BlockSpec((None, tm * n, LANE), lambda j, i, te, tv, ts: (j, i, 0)),
        ),
        out_shape=jax.ShapeDtypeStruct((D // tn, P * n, LANE), F32),
        compiler_params=_params(2),
        name="moe_w2",
    )(tile_expert, tile_valid, tile_src, act, w2)


def _moe_combine_kernel(p0_ref, p1_ref, y_ref, x_ref, gate_ref, w_ref, o_ref, b0, b1, sem):
    tc = o_ref.shape[0]
    n = b0.shape[1] // tc
    base = pl.program_id(0) * tc

    def start(r, carry):
        dst = pl.ds(pl.multiple_of(r * n, n), n)
        src0 = pl.ds(pl.multiple_of(p0_ref[base + r] * n, n), n)
        src1 = pl.ds(pl.multiple_of(p1_ref[base + r] * n, n), n)
        pltpu.make_async_copy(y_ref.at[:, src0], b0.at[:, dst], sem.at[0]).start(priority=0)
        pltpu.make_async_copy(y_ref.at[:, src1], b1.at[:, dst], sem.at[1]).start(priority=1)
        return carry

    lax.fori_loop(0, tc, start, 0, unroll=GATHER_UNROLL)
    pltpu.make_async_copy(y_ref.at[:, pl.ds(0, tc * n)], b0, sem.at[0]).wait()
    pltpu.make_async_copy(y_ref.at[:, pl.ds(0, tc * n)], b1, sem.at[1]).wait()
    w = w_ref[...]
    y = w[:, 0:1] * _load_row_tiles(b0, n) + w[:, 1:2] * _load_row_tiles(b1, n)
    o_ref[...] = x_ref[...] + gate_ref[0] * y


def moe_combine(dm, y_sorted, pos0, pos1, top_w, x, gate, *, mod_row0):
    M, D = x.shape
    tc = 128
    t0 = mod_row0 // tc
    J = y_sorted.shape[0]
    n = D // (J * LANE)
    return pl.pallas_call(
        _moe_combine_kernel,
        grid_spec=pltpu.PrefetchScalarGridSpec(
            num_scalar_prefetch=2,
            grid=(M // tc,),
            in_specs=[pl.BlockSpec(memory_space=pl.ANY),
                      pl.BlockSpec((tc, D), lambda i, p0, p1: (i, 0)),
                      pl.BlockSpec((1, 1, D), lambda i, p0, p1: (dm.mod_row(i + t0, tc), 0, 0)),
                      pl.BlockSpec((tc, LANE), lambda i, p0, p1: (i, 0))],
            out_specs=pl.BlockSpec((tc, D), lambda i, p0, p1: (i, 0)),
            scratch_shapes=[pltpu.VMEM((J, tc * n, LANE), F32), pltpu.VMEM((J, tc * n, LANE), F32),
                            pltpu.SemaphoreType.DMA((2,))],
        ),
        out_shape=jax.ShapeDtypeStruct((M, D), F32),
        compiler_params=_params(1),
        name="moe_combine",
    )(pos0, pos1, y_sorted, x, gate, top_w)


def moe_ffn(dm, hp, x, gate, w_router, b_router, w13, w2, moe_idx, *, mod_row0):
    M, D = x.shape
    E = N_EXPERTS
    tm = MOE_TM
    top_idx, top_w = moe_router(hp, w_router, b_router)
    e_flat = top_idx[:, :TOP_K].T.reshape(-1)
    onehot = (e_flat[:, None] == jnp.arange(E, dtype=jnp.int32)[None, :]).astype(jnp.int32)
    csum = jnp.cumsum(onehot, axis=0)
    counts = csum[-1]
    rank = jnp.sum((csum - onehot) * onehot, axis=1)
    padded = ((counts + tm - 1) // tm) * tm
    ends = jnp.cumsum(padded)
    starts = ends - padded
    pos = starts[e_flat] + rank
    P = TOP_K * M + E * tm
    n_tiles = P // tm
    tok = jnp.tile(jnp.arange(M, dtype=jnp.int32), TOP_K)
    gidx = jnp.zeros((P,), jnp.int32).at[pos].set(tok)
    tile_start = jnp.arange(n_tiles, dtype=jnp.int32) * tm
    tile_valid = (tile_start < ends[-1]).astype(jnp.int32)
    te = jnp.sum((tile_start[:, None] >= ends[None, :]).astype(jnp.int32), axis=1)
    last_e = jnp.sum((ends[-1] - 1 >= ends).astype(jnp.int32))
    tile_expert = jnp.minimum(te, last_e).astype(jnp.int32)

    xs = gather_rows(hp, gidx, D // 2 // LANE)
    F = w13.shape[-1] // 2
    act = moe_w13(xs, w13, moe_idx, tile_expert, tile_valid, tn=_pick_tile((1024, 512, 256, 128), F))
    tile_src = jnp.minimum(jnp.arange(n_tiles, dtype=jnp.int32), jnp.sum(tile_valid) - 1).astype(jnp.int32)
    y_sorted = moe_w2(act, w2[moe_idx].astype(BF16), tile_expert, tile_valid, tile_src, tn=min(1024, D))
    return moe_combine(dm, y_sorted, pos[:M], pos[M:], top_w, x, gate, mod_row0=mod_row0)


def _axial_angles(seq, rot_dim):
    rows = seq // GRID_W
    t_row = jnp.repeat(jnp.arange(rows, dtype=F32), GRID_W)
    t_col = jnp.tile(jnp.arange(GRID_W, dtype=F32), rows)
    quarter = rot_dim // 4
    inv_freq = ROPE_THETA ** (-jnp.arange(quarter, dtype=F32) / quarter)
    return jnp.concatenate([t_row[:, None] * inv_freq, t_col[:, None] * inv_freq], axis=-1)


def _rope_tables(dm):
    ident = min(dm.tm, 256)
    ang = _axial_angles(dm.S, SWA_HEAD_DIM)
    cos_a = jnp.concatenate([jnp.cos(ang), jnp.cos(ang)], axis=-1)
    sin_a = jnp.concatenate([-jnp.sin(ang), jnp.sin(ang)], axis=-1)
    cos_a = jnp.concatenate([jnp.ones((ident, LANE), F32), cos_a], axis=0)
    sin_a = jnp.concatenate([jnp.zeros((ident, LANE), F32), sin_a], axis=0)
    ang = _axial_angles(dm.S, MLA_ROPE)
    half = MLA_ROPE // 2
    zeros = jnp.zeros((dm.S, half), F32)
    pad = jnp.zeros((dm.S, LANE - MLA_ROPE), F32)
    c_b = jnp.concatenate([jnp.cos(ang), jnp.cos(ang), pad], axis=-1)
    s1_b = jnp.concatenate([-jnp.sin(ang), zeros, pad], axis=-1)
    s2_b = jnp.concatenate([zeros, jnp.sin(ang), pad], axis=-1)
    c_b = jnp.concatenate([jnp.ones((ident, LANE), F32), c_b], axis=0)
    s1_b = jnp.concatenate([jnp.zeros((ident, LANE), F32), s1_b], axis=0)
    s2_b = jnp.concatenate([jnp.zeros((ident, LANE), F32), s2_b], axis=0)
    return (cos_a, sin_a), (c_b, s1_b, s2_b)


def _pad_head_vec(g):
    return jnp.zeros((1, MLA_QK_PAD), F32).at[0, :MLA_QK].set(g.astype(F32))


def _trunk(x, c, ctx, c_ctx, mod_w, mod_b, norm_mix_g, norm_ffn_g, w_in,
           swa_q_norm_g, swa_k_norm_g, swa_sink,
           mla_q_a_norm_g, mla_w_uq, mla_kv_a_norm_g, mla_w_ukv, mla_q_norm_g, mla_k_norm_g,
           ssm_lam_re, ssm_lam_im, ssm_log_step, ssm_b_re, ssm_b_im, ssm_c_re, ssm_c_im,
           ssm_d, ssm_w_glu, ssm_b_glu, w_branch, w_out,
           ffn_w13, ffn_w2, moe_w_router, moe_b_router, moe_w13, moe_w2):
    B, S, D = x.shape
    L = ctx.shape[1]
    depth = mod_w.shape[0]
    dm = Dims(B, S, L)
    tm = dm.tm
    RC, RL, R = dm.RC, dm.RL, dm.R
    q_w = SWA_HEADS * SWA_HEAD_DIM
    kv_w = SWA_KV_HEADS * SWA_HEAD_DIM
    q_rank = mla_w_uq.shape[1]
    kv_rank = mla_w_ukv.shape[1]
    ssm_w = ssm_d.shape[1]
    n_gate = N_BRANCH * D
    src = {}
    off = 0
    for name, width in (("q", q_w), ("k", kv_w), ("v", kv_w), ("c_q", q_rank), ("c_kv", kv_rank),
                        ("kpe", MLA_ROPE), ("u", ssm_w), ("gates", n_gate)):
        src[name] = (off, width)
        off += width
    order = ("q", "u", "c_q", "k", "v", "c_kv", "kpe")
    col = {}
    off = 0
    for name in order:
        col[name] = off
        off += src[name][1]
    z_tn = 1792
    z_cols = -(-off // z_tn) * z_tn

    (cos_a, sin_a), tabs_b = _rope_tables(dm)
    xall = jnp.concatenate([ctx.reshape(RC, D), x.reshape(RL, D)], axis=0).astype(F32)
    cond = jnp.zeros((8, D), F32).at[0].set(c_ctx.astype(F32)).at[1:1 + B].set(c.astype(F32))

    for layer in range(depth):
        with_ctx = layer < depth - 1
        row0 = 0 if with_ctx else RC
        n_rows = R - row0
        mods = mm1(cond, [(mod_w, (layer,), 0)], _epi_bias, n_rows=8, n_cols=6 * D, tm=8, tn=512, out_dtype=F32,
                   extras=[(mod_b.reshape(depth, 1, 6 * D), (None, 1, 512), lambda j, i: (layer, 0, j))],
                   prologue=lambda a: _silu(a).astype(BF16), name="ada_mod")
        sh_m, sc_m, g_m, sh_f, sc_f, g_f = [mods[:, i * D:(i + 1) * D].reshape(8, 1, D) for i in range(6)]

        h = modulate(dm, xall, norm_mix_g[layer], sh_m, sc_m, mod_row0=0)
        w_l = w_in[layer]
        w_rest = jnp.concatenate([w_l[:, src[n][0]:src[n][0] + src[n][1]] for n in order]
                                 + [jnp.zeros((D, z_cols - off), w_l.dtype)], axis=1)
        w_gates = w_l[:, src["gates"][0]:]
        z = mm1(h, [(w_rest, (), 0)], _epi_id, n_rows=R, n_cols=z_cols, tm=min(tm, 512), tn=z_tn, out_dtype=F32,
                name="w_in")
        gates = mm1(h, [(w_gates, (), 0)], _epi_sigmoid, n_rows=R, n_cols=n_gate, tm=tm, tn=1024,
                    out_dtype=BF16, name="w_in_gates")

        qa, ka, va = swa_prep(dm, z, col, swa_q_norm_g[layer], swa_k_norm_g[layer], cos_a, sin_a)
        ya_l = swa_attention(dm, qa, ka, va, swa_sink[layer], latent=True)
        cqn, ckvn = mla_lowrank_norm(dm, z, col, q_rank, kv_rank, mla_q_a_norm_g[layer], mla_kv_a_norm_g[layer])
        w_uq = mla_w_uq[layer].reshape(q_rank, MLA_HEADS, MLA_QK)
        w_uq = jnp.pad(w_uq, ((0, 0), (0, 0), (0, MLA_QK_PAD - MLA_QK))).reshape(q_rank, MLA_HEADS * MLA_QK_PAD)
        w_ukv = mla_w_ukv[layer].reshape(kv_rank, MLA_HEADS, MLA_NOPE + MLA_V)
        w_ukv = jnp.concatenate([w_ukv[:, :, :MLA_NOPE].reshape(kv_rank, -1),
                                 w_ukv[:, :, MLA_NOPE:].reshape(kv_rank, -1)], axis=1)
        qf = mm1(cqn, [(w_uq, (), 0)], _epi_id, n_rows=R, n_cols=w_uq.shape[1], tm=tm, tn=512, out_dtype=F32,
                 name="mla_uq")
        kvf = mm1(ckvn, [(w_ukv, (), 0)], _epi_id, n_rows=R, n_cols=w_ukv.shape[1], tm=tm, tn=512, out_dtype=F32,
                  name="mla_ukv")
        gq_pad = _pad_head_vec(mla_q_norm_g[layer])
        gk_pad = _pad_head_vec(mla_k_norm_g[layer])
        qm_c, km_c, vm_c, vt_c = mla_prep(dm, qf, kvf, z, col, gq_pad, gk_pad, tabs_b, row0=0, n_rows=RC,
                                          rows_per_batch=L)
        qm_l, km_l, _, vt_l = mla_prep(dm, qf, kvf, z, col, gq_pad, gk_pad, tabs_b, row0=RC, n_rows=RL,
                                       rows_per_batch=S)
        yb_l = mla_attention_lat(dm, qm_l, km_c, km_l, vt_c, vt_l)
        if with_ctx:
            ya = jnp.concatenate([swa_attention(dm, qa, ka, va, swa_sink[layer], latent=False), ya_l], axis=0)
            yb = jnp.concatenate([mla_attention_ctx(dm, qm_c, km_c, vm_c), yb_l], axis=0)
            y_row0s = (0, 0, 0)
        else:
            ya, yb = ya_l, yb_l
            y_row0s = (RC, RC, 0)
        tables = _ssm_tables(ssm_lam_re[layer], ssm_lam_im[layer], ssm_log_step[layer], ssm_b_re[layer],
                             ssm_b_im[layer], ssm_c_re[layer], ssm_c_im[layer])
        yg = s5_branch(dm, z, col, tables, ssm_d[layer])
        b_glu = ssm_b_glu.reshape(depth, 1, 2 * ssm_w)
        gl_tn = 512
        yc = mm1(yg, [(ssm_w_glu, (layer,), 0), (ssm_w_glu, (layer,), ssm_w)], _epi_glu_bias,
                 n_rows=R, n_cols=ssm_w, tm=tm, tn=gl_tn, out_dtype=BF16,
                 extras=[(b_glu, (None, 1, gl_tn), lambda j, i: (layer, 0, j)),
                         (b_glu, (None, 1, gl_tn), lambda j, i: (layer, 0, j + ssm_w // gl_tn))],
                 prologue=lambda a: a.astype(BF16), name="ssm_glu")
        mixed = merge_branches(dm, (ya, yb, yc), y_row0s, gates, w_branch, layer, row0=row0, n_rows=n_rows)
        t0 = row0 // tm
        x1 = mm1(mixed, [(w_out, (layer,), 0)], _epi_residual, n_rows=n_rows, n_cols=D, tm=tm, tn=1024, out_dtype=F32,
                 extras=[(xall, (tm, 1024), lambda j, i: (i + t0, j)),
                         (g_m, (1, 1, 1024), lambda j, i: (dm.mod_row(i + t0, tm), 0, j))],
                 name="w_out")
        is_moe = layer % 2 == 1
        h2 = modulate(dm, x1, norm_ffn_g[layer], sh_f, sc_f, mod_row0=row0, pack=is_moe)
        if not is_moe:
            F = ffn_w13.shape[-1] // 2
            f_tn = _pick_tile((512, 256, 128), F)
            act = mm1(h2, [(ffn_w13, (layer // 2,), 0), (ffn_w13, (layer // 2,), F)], _epi_swiglu, n_rows=n_rows,
                      n_cols=F, tm=tm, tn=f_tn, out_dtype=BF16, name="ffn_w13")
            x2 = mm2_residual(dm, act, ffn_w2[layer // 2].astype(BF16), x1, g_f, mod_row0=row0, tm=tm, tk=f_tn)
        else:
            if with_ctx:
                raise NotImplementedError("a mixture-of-experts layer that still feeds context rows")
            x2 = moe_ffn(dm, h2, x1, g_f, moe_w_router[layer // 2], moe_b_router[layer // 2], moe_w13, moe_w2,
                         layer // 2, mod_row0=row0)
        xall = x2
    return xall.reshape(B, S, D)


def kernel(x, c, ctx, c_ctx, mod_w, mod_b, norm_mix_g, norm_ffn_g, w_in, swa_q_norm_g, swa_k_norm_g, swa_sink, mla_q_a_norm_g, mla_w_uq, mla_kv_a_norm_g, mla_w_ukv, mla_q_norm_g, mla_k_norm_g, ssm_lam_re, ssm_lam_im, ssm_log_step, ssm_b_re, ssm_b_im, ssm_c_re, ssm_c_im, ssm_d, ssm_w_glu, ssm_b_glu, w_branch, w_out, ffn_w13, ffn_w2, moe_w_router, moe_b_router, moe_w13, moe_w2):
    return _trunk(x, c, ctx, c_ctx, mod_w, mod_b, norm_mix_g, norm_ffn_g, w_in, swa_q_norm_g, swa_k_norm_g, swa_sink,
                  mla_q_a_norm_g, mla_w_uq, mla_kv_a_norm_g, mla_w_ukv, mla_q_norm_g, mla_k_norm_g,
                  ssm_lam_re, ssm_lam_im, ssm_log_step, ssm_b_re, ssm_b_im, ssm_c_re, ssm_c_im,
                  ssm_d, ssm_w_glu, ssm_b_glu, w_branch, w_out, ffn_w13, ffn_w2, moe_w_router, moe_b_router,
                  moe_w13, moe_w2)
```

```python
import functools
import math

import jax
import jax.numpy as jnp
from jax import lax
from jax.experimental import pallas as pl
from jax.experimental.pallas import tpu as pltpu

F32 = jnp.float32
BF16 = jnp.bfloat16

GRID_W = 64
ROPE_THETA = 10000.0
EPS = 1e-6
SWA_HEADS = 8
SWA_KV_HEADS = 2
SWA_HEAD_DIM = 128
SWA_WINDOW = 128
MLA_HEADS = 8
MLA_NOPE = 128
MLA_ROPE = 64
MLA_V = 128
MLA_QK = MLA_NOPE + MLA_ROPE
MLA_QK_PAD = 256
SSM_GROUP = 16
SSM_STATE = 64
SSM_CHUNK = 16
N_BRANCH = 3
N_EXPERTS = 8
TOP_K = 2
LANE = 128
VMEM_LIMIT_BYTES = 56 * 1024 * 1024
MOE_TM = 512
NEG_BIG = -1e30


def _params(n_grid):
    return pltpu.CompilerParams(dimension_semantics=("arbitrary",) * n_grid, vmem_limit_bytes=VMEM_LIMIT_BYTES)


def _pick_tile(candidates, *sizes):
    for t in candidates:
        if all(s % t == 0 for s in sizes):
            return t
    raise ValueError(f"no tile in {candidates} divides {sizes}")


class Dims:
    def __init__(self, batch, seq, ctx_len):
        self.B, self.S, self.L = batch, seq, ctx_len
        self.RC = batch * ctx_len
        self.RL = batch * seq
        self.R = self.RC + self.RL
        self.tm = _pick_tile((1024, 512, 256, 128), ctx_len * batch, seq)

    def mod_row(self, tile, tm):
        nct = self.RC // tm
        return jnp.where(tile < nct, 0, 1 + (tile - nct) // (self.S // tm))


def _silu(x):
    return x * (1.0 / (1.0 + jnp.exp(-x)))


def _sigmoid(x):
    return 1.0 / (1.0 + jnp.exp(-x))


def _gelu_tanh(x):
    c = math.sqrt(2.0 / math.pi)
    return 0.5 * x * (1.0 + jnp.tanh(c * (x + 0.044715 * (x * x * x))))


def _dot(a, b):
    return jnp.dot(a, b, preferred_element_type=F32)


def _dot_nt(a, b):
    return lax.dot_general(a, b, (((1,), (1,)), ((), ())), preferred_element_type=F32)


def _pack_bf16_pairs(y):
    half = y.shape[1] // 2
    bits = lax.bitcast_convert_type(y.astype(BF16).astype(F32), jnp.uint32)
    return (bits[:, :half] >> 16) | (bits[:, half:] & jnp.uint32(0xFFFF0000))


def _unpack_bf16_pairs(p):
    lo = lax.bitcast_convert_type(p << 16, F32).astype(BF16)
    hi = lax.bitcast_convert_type(p & jnp.uint32(0xFFFF0000), F32).astype(BF16)
    return lo, hi


def _store_row_tiles(ref, val):
    m, w = val.shape
    n = w // LANE
    for c in range(n):
        ref[pl.ds(c, m, stride=n), :] = val[:, c * LANE:(c + 1) * LANE]


def _load_row_tiles(ref, n):
    if len(ref.shape) == 2:
        m = ref.shape[0] // n
        return jnp.concatenate([ref[pl.ds(c, m, stride=n), :] for c in range(n)], axis=1)
    m = ref.shape[1] // n
    return jnp.concatenate([ref[j, pl.ds(c, m, stride=n), :] for j in range(ref.shape[0]) for c in range(n)], axis=1)


def _modulate_kernel(x_ref, g_ref, sh_ref, sc_ref, o_ref, *, pack):
    x = x_ref[...]
    ms = jnp.mean(x * x, axis=-1, keepdims=True)
    y = x * lax.rsqrt(ms + EPS) * g_ref[...]
    y = y * (1.0 + sc_ref[0]) + sh_ref[0]
    if pack:
        _store_row_tiles(o_ref, _pack_bf16_pairs(y))
    else:
        o_ref[...] = y.astype(o_ref.dtype)


def modulate(dm, x, g, shift, scale, *, mod_row0, pack=False):
    n_rows, D = x.shape
    tm = min(dm.tm, 512)
    t0 = mod_row0 // tm
    if pack:
        n = D // 2 // LANE
        out_spec = pl.BlockSpec((tm * n, LANE), lambda i: (i, 0))
        out_shape = jax.ShapeDtypeStruct((n_rows * n, LANE), jnp.uint32)
    else:
        out_spec = pl.BlockSpec((tm, D), lambda i: (i, 0))
        out_shape = jax.ShapeDtypeStruct((n_rows, D), BF16)
    return pl.pallas_call(
        functools.partial(_modulate_kernel, pack=pack),
        grid=(n_rows // tm,),
        in_specs=[
            pl.BlockSpec((tm, D), lambda i: (i, 0)),
            pl.BlockSpec((1, D), lambda i: (0, 0)),
            pl.BlockSpec((1, 1, D), lambda i: (dm.mod_row(i + t0, tm), 0, 0)),
            pl.BlockSpec((1, 1, D), lambda i: (dm.mod_row(i + t0, tm), 0, 0)),
        ],
        out_specs=out_spec,
        out_shape=out_shape,
        compiler_params=_params(1),
        name="modulate_packed" if pack else "modulate",
    )(x, g.reshape(1, D), shift, scale)


def _mm1_kernel(*refs, n_w, n_extra, epilogue, prologue):
    a_ref = refs[0]
    w_refs = refs[1:1 + n_w]
    extra = refs[1 + n_w:1 + n_w + n_extra]
    o_ref = refs[1 + n_w + n_extra]
    wb = refs[2 + n_w + n_extra:]

    @pl.when(pl.program_id(1) == 0)
    def _():
        for w_ref, b in zip(w_refs, wb):
            b[...] = w_ref[...].astype(BF16)

    a = a_ref[...]
    if prologue is not None:
        a = prologue(a)
    accs = [_dot(a, b[...]) for b in wb]
    o_ref[...] = epilogue(accs, *extra).astype(o_ref.dtype)


def mm1(a, weights, epilogue, *, n_rows, n_cols, tm, tn, out_dtype, a_row0=0, extras=(), prologue=None, name):
    K = a.shape[1]
    t0 = a_row0 // tm
    in_specs = [pl.BlockSpec((tm, K), lambda j, i: (i + t0, 0))]
    operands = [a]
    for w, lead, col0 in weights:
        c0 = col0 // tn
        in_specs.append(pl.BlockSpec((None,) * len(lead) + (K, tn),
                                     functools.partial(lambda j, i, lead, c0: lead + (0, j + c0), lead=lead, c0=c0)))
        operands.append(w)
    for arr, bshape, imap in extras:
        in_specs.append(pl.BlockSpec(bshape, imap))
        operands.append(arr)
    kern = functools.partial(_mm1_kernel, n_w=len(weights), n_extra=len(extras), epilogue=epilogue,
                             prologue=prologue)
    return pl.pallas_call(
        kern,
        grid=(n_cols // tn, n_rows // tm),
        in_specs=in_specs,
        out_specs=pl.BlockSpec((tm, tn), lambda j, i: (i, j)),
        out_shape=jax.ShapeDtypeStruct((n_rows, n_cols), out_dtype),
        scratch_shapes=[pltpu.VMEM((K, tn), BF16) for _ in weights],
        compiler_params=_params(2),
        name=name,
    )(*operands)


def _epi_id(accs):
    return accs[0]


def _epi_sigmoid(accs):
    return _sigmoid(accs[0])


def _epi_swiglu(accs):
    return _silu(accs[0]) * accs[1]


def _epi_bias(accs, b_ref):
    return accs[0] + b_ref[...]


def _epi_glu_bias(accs, ba_ref, bb_ref):
    return (accs[0] + ba_ref[...]) * _sigmoid(accs[1] + bb_ref[...])


def _epi_residual(accs, x_ref, gate_ref):
    return x_ref[...] + gate_ref[0] * accs[0]


def _mm2_kernel(a_ref, w_ref, x_ref, gate_ref, o_ref, acc_ref):
    k = pl.program_id(1)

    @pl.when(k == 0)
    def _():
        acc_ref[...] = jnp.zeros_like(acc_ref)

    acc_ref[...] += _dot(a_ref[...], w_ref[...])

    @pl.when(k == pl.num_programs(1) - 1)
    def _():
        o_ref[...] = x_ref[...] + gate_ref[0] * acc_ref[...]


def mm2_residual(dm, a, w, x, gate, *, mod_row0, tm, tk):
    M, K = a.shape
    N = w.shape[1]
    t0 = mod_row0 // tm
    return pl.pallas_call(
        _mm2_kernel,
        grid=(M // tm, K // tk),
        in_specs=[
            pl.BlockSpec((tm, tk), lambda i, k: (i, k)),
            pl.BlockSpec((tk, N), lambda i, k: (k, 0)),
            pl.BlockSpec((tm, N), lambda i, k: (i, 0)),
            pl.BlockSpec((1, 1, N), lambda i, k: (dm.mod_row(i + t0, tm), 0, 0)),
        ],
        out_specs=pl.BlockSpec((tm, N), lambda i, k: (i, 0)),
        out_shape=jax.ShapeDtypeStruct((M, N), F32),
        scratch_shapes=[pltpu.VMEM((tm, N), F32)],
        compiler_params=_params(2),
        name="mm2_residual",
    )(a, w, x, gate)


def _swa_prep_kernel(q_ref, k_ref, v_ref, gq_ref, gk_ref, cos_ref, sin_ref, qo_ref, ko_ref, vo_ref):
    c = cos_ref[...]
    s = sin_ref[...]

    def norm_rope(x, g, scale):
        ms = jnp.mean(x * x, axis=-1, keepdims=True)
        y = x * lax.rsqrt(ms + EPS) * g
        return (y * c + pltpu.roll(y, SWA_HEAD_DIM // 2, 1) * s) * scale

    gq = gq_ref[...]
    gk = gk_ref[...]
    for h in range(SWA_HEADS):
        sl = slice(h * SWA_HEAD_DIM, (h + 1) * SWA_HEAD_DIM)
        qo_ref[:, sl] = norm_rope(q_ref[:, sl], gq, SWA_HEAD_DIM ** -0.5).astype(BF16)
    for h in range(SWA_KV_HEADS):
        sl = slice(h * SWA_HEAD_DIM, (h + 1) * SWA_HEAD_DIM)
        ko_ref[:, sl] = norm_rope(k_ref[:, sl], gk, 1.0).astype(BF16)
    vo_ref[...] = v_ref[...].astype(BF16)


def _rope_tile_index(dm, tm):
    nct = dm.RC // tm
    return lambda i: (jnp.where(i < nct, 0, 1 + (i - nct) % (dm.S // tm)), 0)


def swa_prep(dm, z, col, gq, gk, cos_t, sin_t):
    tm = min(dm.tm, 256)
    QW = SWA_HEADS * SWA_HEAD_DIM
    KW = SWA_KV_HEADS * SWA_HEAD_DIM
    ridx = _rope_tile_index(dm, tm)
    return pl.pallas_call(
        _swa_prep_kernel,
        grid=(dm.R // tm,),
        in_specs=[
            pl.BlockSpec((tm, QW), lambda i: (i, col["q"] // QW)),
            pl.BlockSpec((tm, KW), lambda i: (i, col["k"] // KW)),
            pl.BlockSpec((tm, KW), lambda i: (i, col["v"] // KW)),
            pl.BlockSpec((1, SWA_HEAD_DIM), lambda i: (0, 0)),
            pl.BlockSpec((1, SWA_HEAD_DIM), lambda i: (0, 0)),
            pl.BlockSpec((tm, SWA_HEAD_DIM), ridx),
            pl.BlockSpec((tm, SWA_HEAD_DIM), ridx),
        ],
        out_specs=[
            pl.BlockSpec((tm, QW), lambda i: (i, 0)),
            pl.BlockSpec((tm, KW), lambda i: (i, 0)),
            pl.BlockSpec((tm, KW), lambda i: (i, 0)),
        ],
        out_shape=[
            jax.ShapeDtypeStruct((dm.R, QW), BF16),
            jax.ShapeDtypeStruct((dm.R, KW), BF16),
            jax.ShapeDtypeStruct((dm.R, KW), BF16),
        ],
        compiler_params=_params(1),
        name="swa_prep",
    )(z, z, z, gq.reshape(1, -1), gk.reshape(1, -1), cos_t, sin_t)


def _swa_attn_kernel(*refs, windowed, nb):
    if windowed:
        q_ref, kc_ref, kp_ref, kk_ref, kn_ref, vc_ref, vp_ref, vk_ref, vn_ref, sink_ref, o_ref = refs
    else:
        q_ref, kc_ref, vc_ref, sink_ref, o_ref = refs
    G = SWA_HEADS // SWA_KV_HEADS
    blk = q_ref.shape[0]
    Dh = SWA_HEAD_DIM
    q = jnp.concatenate([q_ref[:, g * Dh:(g + 1) * Dh] for g in range(G)], axis=0)
    sink = sink_ref[0][:, 0:1]
    scores = [_dot_nt(q, kc_ref[...])]
    values = [vc_ref[...]]
    if windowed:
        n = pl.program_id(2)
        qi = lax.broadcasted_iota(jnp.int32, (G * blk, blk), 0) % blk
        kj = lax.broadcasted_iota(jnp.int32, (G * blk, blk), 1)
        s_p = _dot_nt(q, kp_ref[...])
        s_p = jnp.where(kj >= qi, s_p, NEG_BIG)
        s_p = jnp.where(n >= 1, s_p, NEG_BIG)
        s_n = _dot_nt(q, kn_ref[...])
        s_n = jnp.where(kj <= qi, s_n, NEG_BIG)
        s_n = jnp.where(n <= nb - 2, s_n, NEG_BIG)
        scores += [s_p, _dot_nt(q, kk_ref[...]), s_n]
        values += [vp_ref[...], vk_ref[...], vn_ref[...]]
    m = sink
    for s in scores:
        m = jnp.maximum(m, jnp.max(s, axis=-1, keepdims=True))
    l = jnp.exp(sink - m)
    o = None
    for s, v in zip(scores, values):
        p = jnp.exp(s - m)
        l = l + jnp.sum(p, axis=-1, keepdims=True)
        pv = _dot(p.astype(BF16), v)
        o = pv if o is None else o + pv
    o = o / l
    for g in range(G):
        o_ref[:, g * Dh:(g + 1) * Dh] = o[g * blk:(g + 1) * blk].astype(o_ref.dtype)


def swa_attention(dm, qa, ka, va, sink, *, latent):
    G = SWA_HEADS // SWA_KV_HEADS
    Dh = SWA_HEAD_DIM
    blk = SWA_WINDOW
    L = dm.L
    sink_col = jnp.broadcast_to(sink.astype(F32).reshape(SWA_KV_HEADS, G, 1, 1),
                                (SWA_KV_HEADS, G, blk, LANE)).reshape(SWA_KV_HEADS, G * blk, LANE)
    sink_spec = pl.BlockSpec((1, G * blk, LANE), lambda b, h, n: (h, 0, 0))
    ctx_spec = pl.BlockSpec((L, Dh), lambda b, h, n: (b, h))
    if latent:
        nb = dm.S // blk
        base = dm.RC // blk

        def q_map(b, h, n):
            return (base + b * nb + n, h)

        def kv_map(off):
            return lambda b, h, n: (base + b * nb + jnp.clip(n + off, 0, nb - 1), h)

        win_specs = [pl.BlockSpec((blk, Dh), kv_map(off)) for off in (-1, 0, 1)]
        in_specs = ([pl.BlockSpec((blk, G * Dh), q_map), ctx_spec] + win_specs + [ctx_spec] + win_specs
                    + [sink_spec])
        operands = (qa, ka, ka, ka, ka, va, va, va, va, sink_col)
        n_out = dm.RL
    else:
        nb = L // blk
        in_specs = [pl.BlockSpec((blk, G * Dh), lambda b, h, n: (b * nb + n, h)), ctx_spec, ctx_spec, sink_spec]
        operands = (qa, ka, va, sink_col)
        n_out = dm.RC
    return pl.pallas_call(
        functools.partial(_swa_attn_kernel, windowed=latent, nb=nb),
        grid=(dm.B, SWA_KV_HEADS, nb),
        in_specs=in_specs,
        out_specs=pl.BlockSpec((blk, G * Dh), lambda b, h, n: (b * nb + n, h)),
        out_shape=jax.ShapeDtypeStruct((n_out, SWA_HEADS * Dh), BF16),
        compiler_params=_params(3),
        name="swa_attn_lat" if latent else "swa_attn_ctx",
    )(*operands)


def _mla_prep_kernel(cq_ref, ckv_ref, pe_ref, gqa_ref, gkva_ref, wq_ref, wkv_ref, gq_ref, gk_ref, c_ref, s1_ref,
                     s2_ref, qo_ref, ko_ref, vo_ref, vt_ref, wq_s, wkv_s):
    @pl.when(pl.program_id(0) == 0)
    def _():
        wq_s[...] = wq_ref[...].astype(BF16)
        wkv_s[...] = wkv_ref[...].astype(BF16)

    def rms(x_ref, g_ref):
        x = x_ref[...]
        ms = jnp.mean(x * x, axis=-1, keepdims=True)
        return (x * lax.rsqrt(ms + EPS) * g_ref[...]).astype(BF16)

    qf = _dot(rms(cq_ref, gqa_ref), wq_s[...])
    kvf = _dot(rms(ckv_ref, gkva_ref), wkv_s[...])
    c = c_ref[...]
    s1 = s1_ref[...]
    s2 = s2_ref[...]
    gq = gq_ref[...]
    gk = gk_ref[...]
    scale = MLA_QK ** -0.5 * math.log2(math.e)

    def rope(x):
        return x * c + pltpu.roll(x, LANE - MLA_ROPE // 2, 1) * s1 + pltpu.roll(x, MLA_ROPE // 2, 1) * s2

    pe = pe_ref[...]
    pe_ss = jnp.sum(pe * pe, axis=-1, keepdims=True)
    NW = MLA_HEADS * MLA_NOPE
    for h in range(MLA_HEADS):
        lo = h * MLA_QK_PAD
        qh = qf[:, lo:lo + MLA_QK_PAD]
        inv = lax.rsqrt(jnp.sum(qh * qh, axis=-1, keepdims=True) * (1.0 / MLA_QK) + EPS)
        qn = qh * inv * gq
        qo_ref[:, lo:lo + MLA_NOPE] = (qn[:, :MLA_NOPE] * scale).astype(BF16)
        qo_ref[:, lo + MLA_NOPE:lo + MLA_QK_PAD] = (rope(qn[:, MLA_NOPE:]) * scale).astype(BF16)
        kh = kvf[:, h * MLA_NOPE:(h + 1) * MLA_NOPE]
        inv = lax.rsqrt((jnp.sum(kh * kh, axis=-1, keepdims=True) + pe_ss) * (1.0 / MLA_QK) + EPS)
        ko_ref[:, lo:lo + MLA_NOPE] = (kh * inv * gk[:, :MLA_NOPE]).astype(BF16)
        ko_ref[:, lo + MLA_NOPE:lo + MLA_QK_PAD] = rope(pe * inv * gk[:, MLA_NOPE:]).astype(BF16)
    v = kvf[:, NW:]
    vo_ref[...] = v.astype(BF16)
    vt_ref[...] = v.T.astype(BF16)


def mla_prep(dm, z, col, gqa, gkva, w_uq, w_ukv, gq_pad, gk_pad, tabs, *, row0, n_rows, rows_per_batch):
    tm = min(dm.tm, 256)
    t0 = row0 // tm
    QW = MLA_HEADS * MLA_QK_PAD
    NW = MLA_HEADS * MLA_NOPE
    q_rank, kv_rank = w_uq.shape[0], w_ukv.shape[0]
    ridx = _rope_tile_index(dm, tm)
    rspec = pl.BlockSpec((tm, LANE), lambda i: ridx(i + t0))
    tpb = rows_per_batch // tm
    const = lambda i: (0, 0)
    return pl.pallas_call(
        _mla_prep_kernel,
        grid=(n_rows // tm,),
        in_specs=[
            pl.BlockSpec((tm, q_rank), lambda i: (i + t0, col["c_q"] // q_rank)),
            pl.BlockSpec((tm, kv_rank), lambda i: (i + t0, col["c_kv"] // kv_rank)),
            pl.BlockSpec((tm, LANE), lambda i: (i + t0, col["kpe"] // LANE)),
            pl.BlockSpec((1, q_rank), const),
            pl.BlockSpec((1, kv_rank), const),
            pl.BlockSpec((q_rank, QW), const),
            pl.BlockSpec((kv_rank, 2 * NW), const),
            pl.BlockSpec((1, MLA_QK_PAD), const),
            pl.BlockSpec((1, MLA_QK_PAD), const),
            rspec, rspec, rspec,
        ],
        out_specs=[
            pl.BlockSpec((tm, QW), lambda i: (i, 0)),
            pl.BlockSpec((tm, QW), lambda i: (i, 0)),
            pl.BlockSpec((tm, NW), lambda i: (i, 0)),
            pl.BlockSpec((NW, tm), lambda i: (i // tpb, i % tpb)),
        ],
        out_shape=[
            jax.ShapeDtypeStruct((n_rows, QW), BF16),
            jax.ShapeDtypeStruct((n_rows, QW), BF16),
            jax.ShapeDtypeStruct((n_rows, NW), BF16),
            jax.ShapeDtypeStruct((n_rows // rows_per_batch * NW, rows_per_batch), BF16),
        ],
        scratch_shapes=[pltpu.VMEM((q_rank, QW), BF16), pltpu.VMEM((kv_rank, 2 * NW), BF16)],
        compiler_params=_params(1),
        name="mla_prep",
    )(z, z, z, gqa.reshape(1, -1), gkva.reshape(1, -1), w_uq, w_ukv, gq_pad, gk_pad, *tabs)


MLA_KEY_CHUNK = 512


def _mla_attn_ctx_kernel(q_ref, kc_ref, vc_ref, o_ref):
    s = _dot_nt(q_ref[...], kc_ref[...])
    p = jnp.exp2(s - jnp.max(s, axis=-1, keepdims=True))
    o = _dot(p.astype(BF16), vc_ref[...])
    o_ref[...] = (o / jnp.sum(p, axis=-1, keepdims=True)).astype(o_ref.dtype)


def _mla_attn_lat_kernel(q_ref, kc_ref, kl_ref, vct_ref, vlt_ref, o_ref, s_scr):
    tq = q_ref.shape[0]
    q = q_ref[...]
    L, S = kc_ref.shape[0], kl_ref.shape[0]
    tk = min(MLA_KEY_CHUNK, S)
    chunks = [(kc_ref, vct_ref, 0, L, 0)] + [(kl_ref, vlt_ref, c * tk, tk, L + c * tk) for c in range(S // tk)]
    mx = jnp.full((tq, LANE), NEG_BIG, F32)
    for k_ref, _, off, w, so in chunks:
        s = _dot_nt(q, k_ref[off:off + w, :])
        s_scr[:, so:so + w] = s
        for g in range(w // LANE):
            mx = jnp.maximum(mx, s[:, g * LANE:(g + 1) * LANE])
    m = jnp.max(mx, axis=-1, keepdims=True)
    ls = jnp.zeros((tq, LANE), F32)
    acc = jnp.zeros((vct_ref.shape[0], tq), F32)
    for _, vt_ref, off, w, so in chunks:
        p = jnp.exp2(s_scr[:, so:so + w] - m)
        for g in range(w // LANE):
            ls = ls + p[:, g * LANE:(g + 1) * LANE]
        acc = acc + _dot_nt(vt_ref[:, off:off + w], p.astype(BF16))
    l = jnp.sum(ls, axis=-1, keepdims=True)
    o_ref[...] = (acc.T / l).astype(o_ref.dtype)


def mla_attention_ctx(dm, q, kc, vc):
    L = dm.L
    QP, V = MLA_QK_PAD, MLA_V
    return pl.pallas_call(
        _mla_attn_ctx_kernel,
        grid=(dm.B, MLA_HEADS),
        in_specs=[pl.BlockSpec((L, QP), lambda b, h: (b, h)),
                  pl.BlockSpec((L, QP), lambda b, h: (b, h)),
                  pl.BlockSpec((L, V), lambda b, h: (b, h))],
        out_specs=pl.BlockSpec((L, V), lambda b, h: (b, h)),
        out_shape=jax.ShapeDtypeStruct((dm.RC, MLA_HEADS * V), BF16),
        compiler_params=_params(2),
        name="mla_attn_ctx",
    )(q, kc, vc)


def mla_attention_lat(dm, q, kc, kl, vct, vlt):
    L, S = dm.L, dm.S
    tq = min(512, S)
    nq = S // tq
    QP, V = MLA_QK_PAD, MLA_V
    H = MLA_HEADS
    return pl.pallas_call(
        _mla_attn_lat_kernel,
        grid=(dm.B, H, nq),
        in_specs=[pl.BlockSpec((tq, QP), lambda b, h, n: (b * nq + n, h)),
                  pl.BlockSpec((L, QP), lambda b, h, n: (b, h)),
                  pl.BlockSpec((S, QP), lambda b, h, n: (b, h)),
                  pl.BlockSpec((V, L), lambda b, h, n: (b * H + h, 0)),
                  pl.BlockSpec((V, S), lambda b, h, n: (b * H + h, 0))],
        out_specs=pl.BlockSpec((tq, V), lambda b, h, n: (b * nq + n, h)),
        out_shape=jax.ShapeDtypeStruct((dm.RL, H * V), BF16),
        scratch_shapes=[pltpu.VMEM((tq, L + S), F32)],
        compiler_params=_params(3),
        name="mla_attn_lat",
    )(q, kc, kl, vct, vlt)


SSM_SUPER = 16
SSM_BLOCK_GROUPS = LANE // SSM_GROUP


def _ssm_in_kernel(u_ref, wi_ref, ws_ref, y_ref, s_ref):
    a = _load_row_tiles(u_ref, SSM_CHUNK).astype(BF16)
    y_ref[...] = _dot(a, wi_ref[...])
    zs = _dot(a, ws_ref[...])
    for c in range(s_ref.shape[0]):
        s_ref[c] = zs[:, c * LANE:(c + 1) * LANE]


def ssm_chunk_in(z, u_col, w_intra, w_state, *, tr):
    R = z.shape[0]
    nblk, CW, _ = w_intra.shape
    nr = tr // SSM_CHUNK
    c0 = u_col // LANE
    w_spec = pl.BlockSpec((None, CW, CW), lambda j, i: (j, 0, 0), pipeline_mode=pl.Buffered(1))
    return pl.pallas_call(
        _ssm_in_kernel,
        grid=(nblk, R // tr),
        in_specs=[pl.BlockSpec((tr, LANE), lambda j, i: (i, c0 + j)), w_spec, w_spec],
        out_specs=[pl.BlockSpec((nr, CW), lambda j, i: (i, j)),
                   pl.BlockSpec((2 * SSM_BLOCK_GROUPS, nr, LANE), lambda j, i: (0, i, j))],
        out_shape=[jax.ShapeDtypeStruct((R // SSM_CHUNK, nblk * CW), F32),
                   jax.ShapeDtypeStruct((2 * SSM_BLOCK_GROUPS, R // SSM_CHUNK, nblk * LANE), F32)],
        compiler_params=_params(2),
        name="ssm_chunk_in",
    )(z, w_intra, w_state)


def _ssm_scan_kernel(s_ref, p1_ref, p2_ref, x_ref, t_ref, e_ref, *, batch, n_ctx_sc, n_lat_sc):
    SC = SSM_SUPER
    GB = SSM_BLOCK_GROUPS
    FWD, BWD = slice(0, GB), slice(GB, 2 * GB)
    n_sc = s_ref.shape[1] // SC

    def cmul(i, rows, x):
        swapped = jnp.concatenate([x[..., SSM_STATE:], x[..., :SSM_STATE]], axis=-1)
        return p1_ref[i, rows] * x + p2_ref[i, rows] * swapped

    def chunk(i):
        return pl.ds(i, n_sc, stride=SC)

    lf = jnp.zeros((GB, n_sc, LANE), F32)
    lb = jnp.zeros((GB, n_sc, LANE), F32)
    for i in range(SC):
        x_ref[FWD, chunk(i), :] = lf
        x_ref[BWD, chunk(SC - 1 - i), :] = lb
        lf = cmul(1, FWD, lf) + s_ref[FWD, chunk(i), :]
        lb = cmul(1, BWD, lb) + s_ref[BWD, chunk(SC - 1 - i), :]
    t_ref[FWD] = lf
    t_ref[BWD] = lb
    n_ctx = batch * n_ctx_sc
    for rows, order in ((FWD, 1), (BWD, -1)):
        e = jnp.zeros((GB, batch, LANE), F32)
        for region_start, per_batch in ((0, n_ctx_sc), (n_ctx, n_lat_sc)):
            steps = range(per_batch) if order == 1 else range(per_batch - 1, -1, -1)
            for m in steps:
                idx = pl.ds(region_start + m, batch, stride=per_batch)
                e_ref[rows, idx, :] = e
                e = cmul(SC, rows, e) + t_ref[rows, idx, :]
    ef = e_ref[FWD]
    eb = e_ref[BWD]
    for i in range(SC):
        x_ref[FWD, chunk(i), :] += cmul(i, FWD, ef)
        x_ref[BWD, chunk(SC - 1 - i), :] += cmul(i, BWD, eb)


def ssm_scan(dm, s, p1, p2, layer):
    NS, NR, W = s.shape
    nblk = W // LANE
    n_sc = NR // SSM_SUPER
    n_ctx_sc = dm.L // (SSM_CHUNK * SSM_SUPER)
    n_lat_sc = dm.S // (SSM_CHUNK * SSM_SUPER)
    blk = pl.BlockSpec((NS, NR, LANE), lambda j: (0, 0, j))
    pspec = pl.BlockSpec((None, None, SSM_SUPER + 1, NS, 1, LANE), lambda j: (layer, j, 0, 0, 0, 0))
    return pl.pallas_call(
        functools.partial(_ssm_scan_kernel, batch=dm.B, n_ctx_sc=n_ctx_sc, n_lat_sc=n_lat_sc),
        grid=(nblk,),
        in_specs=[blk, pspec, pspec],
        out_specs=blk,
        out_shape=jax.ShapeDtypeStruct(s.shape, F32),
        scratch_shapes=[pltpu.VMEM((NS, n_sc, LANE), F32), pltpu.VMEM((NS, n_sc, LANE), F32)],
        compiler_params=_params(1),
        name="ssm_scan",
    )(s, p1, p2)


def _ssm_out_kernel(y_ref, x_ref, u_ref, w_ref, d_ref, o_ref):
    nr = y_ref.shape[0]
    xs = jnp.concatenate([x_ref[c] for c in range(x_ref.shape[0])], axis=1).astype(BF16)
    y = y_ref[...] + _dot(xs, w_ref[...])
    d = d_ref[...]
    for t in range(SSM_CHUNK):
        rows = pl.ds(t, nr, stride=SSM_CHUNK)
        o_ref[rows, :] = _gelu_tanh(y[:, t * LANE:(t + 1) * LANE] + d * u_ref[rows, :])


def ssm_chunk_out(y_intra, x_states, z, u_col, w_out_state, d_skip, *, tr):
    R = z.shape[0]
    nblk, CW, _ = w_out_state.shape
    nr = tr // SSM_CHUNK
    c0 = u_col // LANE
    return pl.pallas_call(
        _ssm_out_kernel,
        grid=(nblk, R // tr),
        in_specs=[pl.BlockSpec((nr, CW), lambda j, i: (i, j)),
                  pl.BlockSpec((2 * SSM_BLOCK_GROUPS, nr, LANE), lambda j, i: (0, i, j)),
                  pl.BlockSpec((tr, LANE), lambda j, i: (i, c0 + j)),
                  pl.BlockSpec((None, CW, CW), lambda j, i: (j, 0, 0), pipeline_mode=pl.Buffered(1)),
                  pl.BlockSpec((1, LANE), lambda j, i: (0, j))],
        out_specs=pl.BlockSpec((tr, LANE), lambda j, i: (i, j)),
        out_shape=jax.ShapeDtypeStruct((R, nblk * LANE), F32),
        compiler_params=_params(2),
        name="ssm_chunk_out",
    )(y_intra, x_states, z, w_out_state, d_skip.astype(F32).reshape(1, -1))


def _ssm_expand_kernel(k_ref, o_ref, *, mode):
    C, H, GB = SSM_CHUNK, SSM_GROUP, SSM_BLOCK_GROUPS
    CH = C * H
    W = o_ref.shape[1]
    ri = lax.broadcasted_iota(jnp.int32, (CH, W), 0)
    ci = lax.broadcasted_iota(jnp.int32, (CH, W), 1)
    for gl in range(GB):
        kc = k_ref[gl].astype(BF16)
        if mode == "state":
            zero = jnp.zeros((CH, LANE), BF16)
            cols = [kc[:, d * LANE:(d + 1) * LANE] if g2 == gl else zero for d in range(2) for g2 in range(GB)]
            t = jnp.concatenate(cols, axis=1)
        else:
            sel = jnp.where(ci == (ri >> 4) * LANE + gl * H + (ri & (H - 1)), 1.0, 0.0).astype(BF16)
            t = _dot(kc, sel).astype(BF16)
        if mode == "out":
            for d in range(2):
                o_ref[d * GB * LANE + gl * LANE:d * GB * LANE + (gl + 1) * LANE, :] = t[d * LANE:(d + 1) * LANE, :]
        else:
            for s in range(C):
                o_ref[s * LANE + gl * H:s * LANE + (gl + 1) * H, :] = t[s * H:(s + 1) * H, :]


def _ssm_expand(compact, layer, mode):
    _, G, CH, _ = compact.shape
    GB = SSM_BLOCK_GROUPS
    CW = CH * GB
    return pl.pallas_call(
        functools.partial(_ssm_expand_kernel, mode=mode),
        grid=(G // GB,),
        in_specs=[pl.BlockSpec((None, GB, CH, CH), lambda j: (layer, j, 0, 0))],
        out_specs=pl.BlockSpec((None, CW, CW), lambda j: (j, 0, 0)),
        out_shape=jax.ShapeDtypeStruct((G // GB, CW, CW), BF16),
        compiler_params=_params(1),
        name="ssm_expand_" + mode,
    )(compact)


def _ssm_compact_tables(lam_re, lam_im, log_step, b_re, b_im, c_re, c_im):
    C, H, P = SSM_CHUNK, SSM_GROUP, SSM_STATE
    G = lam_re.shape[1]
    delta = jnp.exp(log_step.astype(F32))[..., None]
    zr = lam_re.astype(F32) * delta
    zi = lam_im.astype(F32) * delta
    k = jnp.arange(C + 1, dtype=F32)[:, None, None, None]
    mag = jnp.exp(k * zr[None])
    pw_re = mag * jnp.cos(k * zi[None])
    pw_im = mag * jnp.sin(k * zi[None])
    lb_re, lb_im = pw_re[1], pw_im[1]
    lr, li = lam_re.astype(F32), lam_im.astype(F32)
    den = lr * lr + li * li
    f_re = ((lb_re - 1.0) * lr + lb_im * li) / den
    f_im = (lb_im * lr - (lb_re - 1.0) * li) / den
    br, bi = b_re.astype(F32), b_im.astype(F32)
    bb_re = f_re[..., None] * br - f_im[..., None] * bi
    bb_im = f_re[..., None] * bi + f_im[..., None] * br
    cr, ci = c_re.astype(F32), c_im.astype(F32)
    cl_re = cr[None] * pw_re[:, :, :, None, :] - ci[None] * pw_im[:, :, :, None, :]
    cl_im = cr[None] * pw_im[:, :, :, None, :] + ci[None] * pw_re[:, :, :, None, :]
    hp = lax.Precision.HIGHEST
    kern = (jnp.einsum("kdghp,dgpj->dgkhj", cl_re[:C], bb_re, precision=hp)
            - jnp.einsum("kdghp,dgpj->dgkhj", cl_im[:C], bb_im, precision=hp))
    k_idx = jnp.arange(C)[:, None, None]
    s_idx = jnp.arange(C)[None, :, None]
    t_idx = jnp.arange(C)[None, None, :]
    sel_f = (t_idx - s_idx == k_idx).astype(F32)
    sel_b = (s_idx - t_idx == k_idx).astype(F32)
    ksum = (jnp.einsum("kst,gkhj->gsjth", sel_f, kern[0], precision=hp)
            + jnp.einsum("kst,gkhj->gsjth", sel_b, kern[1], precision=hp))

    def state_in(d, power_of_s):
        pr = pw_re[power_of_s, d]
        pi = pw_im[power_of_s, d]
        re = pr[..., None] * bb_re[d][None] - pi[..., None] * bb_im[d][None]
        im = pr[..., None] * bb_im[d][None] + pi[..., None] * bb_re[d][None]
        return jnp.concatenate([re, im], axis=2).transpose(1, 0, 3, 2)

    def state_out(d, power_of_t):
        re = cl_re[power_of_t, d]
        im = cl_im[power_of_t, d]
        return jnp.concatenate([re, -im], axis=-1).transpose(1, 3, 0, 2)

    m_sum = jnp.stack([state_in(0, C - 1 - jnp.arange(C)), state_in(1, jnp.arange(C))])
    m_out = jnp.stack([state_out(0, 1 + jnp.arange(C)), state_out(1, C - jnp.arange(C))])

    GB = SSM_BLOCK_GROUPS
    nblk = G // GB
    ksum = ksum.reshape(G, C * H, C * H)
    m_sum = m_sum.transpose(1, 2, 3, 0, 4).reshape(G, C * H, 2 * 2 * P)
    m_out = m_out.transpose(1, 0, 2, 3, 4).reshape(G, 2 * 2 * P, C * H)
    i = (C * jnp.arange(SSM_SUPER + 1, dtype=F32))[:, None, None, None]
    mag_a = jnp.exp(i * zr[None])
    pa_re = mag_a * jnp.cos(i * zi[None])
    pa_im = mag_a * jnp.sin(i * zi[None])

    def scan_table(lo, hi):
        t = jnp.concatenate([lo, hi], axis=-1).reshape(SSM_SUPER + 1, 2, nblk, GB, 2 * P)
        return t.transpose(2, 0, 1, 3, 4).reshape(nblk, SSM_SUPER + 1, 2 * GB, 1, 2 * P)

    return ksum, m_sum, m_out, scan_table(pa_re, pa_re), scan_table(-pa_im, pa_im)


def s5_branch(dm, z, col, compact, layer, d_skip):
    ksum, m_sum, m_out, p1, p2 = compact
    w_intra = _ssm_expand(ksum, layer, "intra")
    w_state = _ssm_expand(m_sum, layer, "state")
    w_out_state = _ssm_expand(m_out, layer, "out")
    nr = _pick_tile((272, 136, 96, 64, 32, 16, 8), dm.R // SSM_CHUNK)
    tr = nr * SSM_CHUNK
    y_intra, s = ssm_chunk_in(z, col["u"], w_intra, w_state, tr=tr)
    x_states = ssm_scan(dm, s, p1, p2, layer)
    return ssm_chunk_out(y_intra, x_states, z, col["u"], w_out_state, d_skip, tr=tr)


def _merge_kernel(ya_ref, yb_ref, yc_ref, ga_ref, gb_ref, gc_ref, wa_ref, wb_ref, wc_ref, o_ref, sa, sb, sc):
    @pl.when(pl.program_id(1) == 0)
    def _():
        for w_ref, s in ((wa_ref, sa), (wb_ref, sb), (wc_ref, sc)):
            s[...] = w_ref[...].astype(BF16)

    acc = ga_ref[...].astype(F32) * _dot(ya_ref[...], sa[...])
    acc = acc + gb_ref[...].astype(F32) * _dot(yb_ref[...], sb[...])
    acc = acc + gc_ref[...].astype(F32) * _dot(yc_ref[...], sc[...])
    o_ref[...] = acc.astype(o_ref.dtype)


def merge_branches(dm, ys, y_row0s, gates, w_branch, layer, *, row0, n_rows):
    BW = ys[0].shape[1]
    D = w_branch.shape[-1]
    tm = min(dm.tm, 512)
    tn = 1024
    t0 = row0 // tm
    nj = D // tn
    y_specs = [pl.BlockSpec((tm, BW), functools.partial(lambda j, i, o: (i + o, 0), o=(row0 - y0) // tm))
               for y0 in y_row0s]
    g_specs = [pl.BlockSpec((tm, tn), functools.partial(lambda j, i, n: (i + t0, n * nj + j), n=n))
               for n in range(N_BRANCH)]
    w_specs = [pl.BlockSpec((None, None, BW, tn), functools.partial(lambda j, i, n: (layer, n, 0, j), n=n))
               for n in range(N_BRANCH)]
    return pl.pallas_call(
        _merge_kernel,
        grid=(nj, n_rows // tm),
        in_specs=y_specs + g_specs + w_specs,
        out_specs=pl.BlockSpec((tm, tn), lambda j, i: (i, j)),
        out_shape=jax.ShapeDtypeStruct((n_rows, D), BF16),
        scratch_shapes=[pltpu.VMEM((BW, tn), BF16)] * 3,
        compiler_params=_params(2),
        name="merge_branches",
    )(*ys, gates, gates, gates, w_branch, w_branch, w_branch)


def _router_kernel(h_ref, whi_ref, wlo_ref, b_ref, idx_ref, w_ref):
    half = whi_ref.shape[0] // 2
    lo, hi = _unpack_bf16_pairs(_load_row_tiles(h_ref, half // LANE))
    logits = b_ref[...]
    for w in (whi_ref, wlo_ref):
        logits = logits + _dot(lo, w[:half, :]) + _dot(hi, w[half:, :])
    lane = lax.broadcasted_iota(jnp.int32, logits.shape, 1).astype(F32)
    logits = jnp.where(lane < N_EXPERTS, logits, NEG_BIG)
    m1 = jnp.max(logits, axis=-1, keepdims=True)
    i1 = jnp.min(jnp.where(logits == m1, lane, float(LANE)), axis=-1, keepdims=True)
    rest = jnp.where(lane == i1, NEG_BIG, logits)
    m2 = jnp.max(rest, axis=-1, keepdims=True)
    i2 = jnp.min(jnp.where(rest == m2, lane, float(LANE)), axis=-1, keepdims=True)
    e = jnp.exp(m2 - m1)
    w1 = 1.0 / (1.0 + e)
    w2 = e / (1.0 + e)
    idx_ref[...] = jnp.where(lane == 0.0, i1, jnp.where(lane == 1.0, i2, 0.0)).astype(jnp.int32)
    w_ref[...] = jnp.where(lane == 0.0, w1, jnp.where(lane == 1.0, w2, 0.0))


def moe_router(hp, w_router, b_router):
    D = w_router.shape[0]
    n = D // 2 // LANE
    M = hp.shape[0] // n
    tm = _pick_tile((1024, 512, 256, 128), M)
    w_pad = jnp.zeros((D, LANE), F32).at[:, :N_EXPERTS].set(w_router.astype(F32))
    w_hi = w_pad.astype(BF16)
    w_lo = (w_pad - w_hi.astype(F32)).astype(BF16)
    b_pad = jnp.zeros((1, LANE), F32).at[0, :N_EXPERTS].set(b_router.astype(F32))
    return pl.pallas_call(
        _router_kernel,
        grid=(M // tm,),
        in_specs=[pl.BlockSpec((tm * n, LANE), lambda i: (i, 0)),
                  pl.BlockSpec((D, LANE), lambda i: (0, 0)),
                  pl.BlockSpec((D, LANE), lambda i: (0, 0)),
                  pl.BlockSpec((1, LANE), lambda i: (0, 0))],
        out_specs=[pl.BlockSpec((tm, LANE), lambda i: (i, 0)), pl.BlockSpec((tm, LANE), lambda i: (i, 0))],
        out_shape=[jax.ShapeDtypeStruct((M, LANE), jnp.int32), jax.ShapeDtypeStruct((M, LANE), F32)],
        compiler_params=_params(1),
        name="moe_router",
    )(hp, w_hi, w_lo, b_pad)


GATHER_UNROLL = 8


def _gather_rows_kernel(idx_ref, src_ref, o_ref, buf, sem, *, n):
    tg = o_ref.shape[0]
    base = pl.program_id(0) * tg

    def start(it, carry):
        for u in range(GATHER_UNROLL):
            r = it * GATHER_UNROLL + u
            src_row = pl.multiple_of(idx_ref[base + r] * n, n)
            dst_row = pl.multiple_of(r * n, n)
            pltpu.make_async_copy(src_ref.at[pl.ds(src_row, n)], buf.at[pl.ds(dst_row, n)], sem).start(
                priority=u % 2)
        return carry

    lax.fori_loop(0, tg // GATHER_UNROLL, start, 0)
    pltpu.make_async_copy(src_ref.at[pl.ds(0, tg * n)], buf, sem).wait()
    lo, hi = _unpack_bf16_pairs(_load_row_tiles(buf, n))
    half = lo.shape[1]
    o_ref[:, :half] = lo
    o_ref[:, half:] = hi


def gather_rows(src, idx, n, *, tg=256):
    M = idx.shape[0]
    return pl.pallas_call(
        functools.partial(_gather_rows_kernel, n=n),
        grid_spec=pltpu.PrefetchScalarGridSpec(
            num_scalar_prefetch=1,
            grid=(M // tg,),
            in_specs=[pl.BlockSpec(memory_space=pl.ANY)],
            out_specs=pl.BlockSpec((tg, 2 * n * LANE), lambda i, idx_ref: (i, 0)),
            scratch_shapes=[pltpu.VMEM((tg * n, LANE), src.dtype), pltpu.SemaphoreType.DMA(())],
        ),
        out_shape=jax.ShapeDtypeStruct((M, 2 * n * LANE), BF16),
        compiler_params=_params(1),
        name="gather_rows",
    )(idx, src)


def _moe_w13_kernel(te_ref, tv_ref, a_ref, wg_ref, wu_ref, o_ref, sg, su):
    i = pl.program_id(1)
    prev = te_ref[jnp.maximum(i - 1, 0)]

    @pl.when(jnp.logical_or(i == 0, te_ref[i] != prev))
    def _():
        sg[...] = wg_ref[...].astype(BF16)
        su[...] = wu_ref[...].astype(BF16)

    @pl.when(tv_ref[i] == 1)
    def _():
        a = a_ref[...]
        o_ref[...] = (_silu(_dot(a, sg[...])) * _dot(a, su[...])).astype(o_ref.dtype)

    @pl.when(tv_ref[i] == 0)
    def _():
        o_ref[...] = jnp.zeros_like(o_ref)


def moe_w13(xs, w13, moe_idx, tile_expert, tile_valid, *, tn):
    P, D = xs.shape
    F = w13.shape[-1] // 2
    tm = MOE_TM
    nj = F // tn
    return pl.pallas_call(
        _moe_w13_kernel,
        grid_spec=pltpu.PrefetchScalarGridSpec(
            num_scalar_prefetch=2,
            grid=(nj, P // tm),
            in_specs=[pl.BlockSpec((tm, D), lambda j, i, te, tv: (i, 0)),
                      pl.BlockSpec((None, None, D, tn), lambda j, i, te, tv: (moe_idx, te[i], 0, j)),
                      pl.BlockSpec((None, None, D, tn), lambda j, i, te, tv: (moe_idx, te[i], 0, j + nj))],
            out_specs=pl.BlockSpec((tm, tn), lambda j, i, te, tv: (i, j)),
            scratch_shapes=[pltpu.VMEM((D, tn), BF16), pltpu.VMEM((D, tn), BF16)],
        ),
        out_shape=jax.ShapeDtypeStruct((P, F), BF16),
        compiler_params=_params(2),
        name="moe_w13",
    )(tile_expert, tile_valid, xs, w13, w13)


def _moe_w2_kernel(te_ref, tv_ref, ts_ref, a_ref, w_ref, o_ref):
    i = pl.program_id(1)

    @pl.when(tv_ref[i] == 1)
    def _():
        _store_row_tiles(o_ref, _dot(a_ref[...], w_ref[...]))

    @pl.when(tv_ref[i] == 0)
    def _():
        o_ref[...] = jnp.zeros_like(o_ref)


def moe_w2(act, w2, tile_expert, tile_valid, tile_src, *, tn):
    P, F = act.shape
    D = w2.shape[-1]
    tm = MOE_TM
    n = tn // LANE
    return pl.pallas_call(
        _moe_w2_kernel,
        grid_spec=pltpu.PrefetchScalarGridSpec(
            num_scalar_prefetch=3,
            grid=(D // tn, P // tm),
            in_specs=[pl.BlockSpec((tm, F), lambda j, i, te, tv, ts: (ts[i], 0)),
                      pl.BlockSpec((None, F, tn), lambda j, i, te, tv, ts: (te[i], 0, j))],
            out_specs=pl.BlockSpec((None, tm * n, LANE), lambda j, i, te, tv, ts: (j, i, 0)),
        ),
        out_shape=jax.ShapeDtypeStruct((D // tn, P * n, LANE), F32),
        compiler_params=_params(2),
        name="moe_w2",
    )(tile_expert, tile_valid, tile_src, act, w2)


def _moe_combine_kernel(p0_ref, p1_ref, y_ref, x_ref, gate_ref, w_ref, o_ref, b0, b1, sem):
    tc = o_ref.shape[0]
    n = b0.shape[1] // tc
    base = pl.program_id(0) * tc

    def start(r, carry):
        dst = pl.ds(pl.multiple_of(r * n, n), n)
        src0 = pl.ds(pl.multiple_of(p0_ref[base + r] * n, n), n)
        src1 = pl.ds(pl.multiple_of(p1_ref[base + r] * n, n), n)
        pltpu.make_async_copy(y_ref.at[:, src0], b0.at[:, dst], sem.at[0]).start(priority=0)
        pltpu.make_async_copy(y_ref.at[:, src1], b1.at[:, dst], sem.at[1]).start(priority=1)
        return carry

    lax.fori_loop(0, tc, start, 0, unroll=GATHER_UNROLL)
    pltpu.make_async_copy(y_ref.at[:, pl.ds(0, tc * n)], b0, sem.at[0]).wait()
    pltpu.make_async_copy(y_ref.at[:, pl.ds(0, tc * n)], b1, sem.at[1]).wait()
    w = w_ref[...]
    y = w[:, 0:1] * _load_row_tiles(b0, n) + w[:, 1:2] * _load_row_tiles(b1, n)
    o_ref[...] = x_ref[...] + gate_ref[0] * y


def moe_combine(dm, y_sorted, pos0, pos1, top_w, x, gate, *, mod_row0):
    M, D = x.shape
    tc = 128
    t0 = mod_row0 // tc
    J = y_sorted.shape[0]
    n = D // (J * LANE)
    return pl.pallas_call(
        _moe_combine_kernel,
        grid_spec=pltpu.PrefetchScalarGridSpec(
            num_scalar_prefetch=2,
            grid=(M // tc,),
            in_specs=[pl.BlockSpec(memory_space=pl.ANY),
                      pl.BlockSpec((tc, D), lambda i, p0, p1: (i, 0)),
                      pl.BlockSpec((1, 1, D), lambda i, p0, p1: (dm.mod_row(i + t0, tc), 0, 0)),
                      pl.BlockSpec((tc, LANE), lambda i, p0, p1: (i, 0))],
            out_specs=pl.BlockSpec((tc, D), lambda i, p0, p1: (i, 0)),
            scratch_shapes=[pltpu.VMEM((J, tc * n, LANE), F32), pltpu.VMEM((J, tc * n, LANE), F32),
                            pltpu.SemaphoreType.DMA((2,))],
        ),
        out_shape=jax.ShapeDtypeStruct((M, D), F32),
        compiler_params=_params(1),
        name="moe_combine",
    )(pos0, pos1, y_sorted, x, gate, top_w)


def moe_ffn(dm, hp, x, gate, w_router, b_router, w13, w2, moe_idx, *, mod_row0):
    M, D = x.shape
    E = N_EXPERTS
    tm = MOE_TM
    top_idx, top_w = moe_router(hp, w_router, b_router)
    e_flat = top_idx[:, :TOP_K].T.reshape(-1)
    onehot = (e_flat[:, None] == jnp.arange(E, dtype=jnp.int32)[None, :]).astype(F32)
    blk = LANE
    nb = onehot.shape[0] // blk
    oh3 = onehot.reshape(nb, blk, E)
    exact = lax.Precision.HIGHEST
    within = jnp.einsum("ij,bjk->bik", jnp.tril(jnp.ones((blk, blk), F32), -1), oh3, precision=exact)
    before = jnp.einsum("ab,bk->ak", jnp.tril(jnp.ones((nb, nb), F32), -1), jnp.sum(oh3, axis=1), precision=exact)
    rank = jnp.sum((within + before[:, None, :]) * oh3, axis=-1).reshape(-1).astype(jnp.int32)
    counts = jnp.sum(onehot, axis=0).astype(jnp.int32)
    padded = ((counts + tm - 1) // tm) * tm
    ends = jnp.cumsum(padded)
    starts = ends - padded
    pos = starts[e_flat] + rank
    P = TOP_K * M + E * tm
    n_tiles = P // tm
    tok = jnp.tile(jnp.arange(M, dtype=jnp.int32), TOP_K)
    gidx = jnp.zeros((P,), jnp.int32).at[pos].set(tok)
    tile_start = jnp.arange(n_tiles, dtype=jnp.int32) * tm
    tile_valid = (tile_start < ends[-1]).astype(jnp.int32)
    te = jnp.sum((tile_start[:, None] >= ends[None, :]).astype(jnp.int32), axis=1)
    last_e = jnp.sum((ends[-1] - 1 >= ends).astype(jnp.int32))
    tile_expert = jnp.minimum(te, last_e).astype(jnp.int32)

    xs = gather_rows(hp, gidx, D // 2 // LANE)
    F = w13.shape[-1] // 2
    act = moe_w13(xs, w13, moe_idx, tile_expert, tile_valid, tn=_pick_tile((1024, 512, 256, 128), F))
    tile_src = jnp.minimum(jnp.arange(n_tiles, dtype=jnp.int32), jnp.sum(tile_valid) - 1).astype(jnp.int32)
    y_sorted = moe_w2(act, w2[moe_idx].astype(BF16), tile_expert, tile_valid, tile_src, tn=min(1024, D))
    return moe_combine(dm, y_sorted, pos[:M], pos[M:], top_w, x, gate, mod_row0=mod_row0)


def _axial_angles(seq, rot_dim):
    rows = seq // GRID_W
    t_row = jnp.repeat(jnp.arange(rows, dtype=F32), GRID_W)
    t_col = jnp.tile(jnp.arange(GRID_W, dtype=F32), rows)
    quarter = rot_dim // 4
    inv_freq = ROPE_THETA ** (-jnp.arange(quarter, dtype=F32) / quarter)
    return jnp.concatenate([t_row[:, None] * inv_freq, t_col[:, None] * inv_freq], axis=-1)


def _rope_tables(dm):
    ident = min(dm.tm, 256)
    ang = _axial_angles(dm.S, SWA_HEAD_DIM)
    cos_a = jnp.concatenate([jnp.cos(ang), jnp.cos(ang)], axis=-1)
    sin_a = jnp.concatenate([-jnp.sin(ang), jnp.sin(ang)], axis=-1)
    cos_a = jnp.concatenate([jnp.ones((ident, LANE), F32), cos_a], axis=0)
    sin_a = jnp.concatenate([jnp.zeros((ident, LANE), F32), sin_a], axis=0)
    ang = _axial_angles(dm.S, MLA_ROPE)
    half = MLA_ROPE // 2
    zeros = jnp.zeros((dm.S, half), F32)
    pad = jnp.zeros((dm.S, LANE - MLA_ROPE), F32)
    c_b = jnp.concatenate([jnp.cos(ang), jnp.cos(ang), pad], axis=-1)
    s1_b = jnp.concatenate([-jnp.sin(ang), zeros, pad], axis=-1)
    s2_b = jnp.concatenate([zeros, jnp.sin(ang), pad], axis=-1)
    c_b = jnp.concatenate([jnp.ones((ident, LANE), F32), c_b], axis=0)
    s1_b = jnp.concatenate([jnp.zeros((ident, LANE), F32), s1_b], axis=0)
    s2_b = jnp.concatenate([jnp.zeros((ident, LANE), F32), s2_b], axis=0)
    return (cos_a, sin_a), (c_b, s1_b, s2_b)


def _pad_head_vec(g):
    return jnp.zeros((1, MLA_QK_PAD), F32).at[0, :MLA_QK].set(g.astype(F32))


def _trunk(x, c, ctx, c_ctx, mod_w, mod_b, norm_mix_g, norm_ffn_g, w_in,
           swa_q_norm_g, swa_k_norm_g, swa_sink,
           mla_q_a_norm_g, mla_w_uq, mla_kv_a_norm_g, mla_w_ukv, mla_q_norm_g, mla_k_norm_g,
           ssm_lam_re, ssm_lam_im, ssm_log_step, ssm_b_re, ssm_b_im, ssm_c_re, ssm_c_im,
           ssm_d, ssm_w_glu, ssm_b_glu, w_branch, w_out,
           ffn_w13, ffn_w2, moe_w_router, moe_b_router, moe_w13, moe_w2):
    B, S, D = x.shape
    L = ctx.shape[1]
    depth = mod_w.shape[0]
    dm = Dims(B, S, L)
    tm = dm.tm
    RC, RL, R = dm.RC, dm.RL, dm.R
    q_w = SWA_HEADS * SWA_HEAD_DIM
    kv_w = SWA_KV_HEADS * SWA_HEAD_DIM
    q_rank = mla_w_uq.shape[1]
    kv_rank = mla_w_ukv.shape[1]
    ssm_w = ssm_d.shape[1]
    n_gate = N_BRANCH * D
    src = {}
    off = 0
    for name, width in (("q", q_w), ("k", kv_w), ("v", kv_w), ("c_q", q_rank), ("c_kv", kv_rank),
                        ("kpe", MLA_ROPE), ("u", ssm_w), ("gates", n_gate)):
        src[name] = (off, width)
        off += width
    order = ("q", "u", "c_q", "k", "v", "c_kv", "kpe")
    col = {}
    off = 0
    for name in order:
        col[name] = off
        off += src[name][1]
    z_tn = 1792
    z_cols = -(-off // z_tn) * z_tn

    (cos_a, sin_a), tabs_b = _rope_tables(dm)
    ssm_compact = jax.vmap(_ssm_compact_tables)(ssm_lam_re, ssm_lam_im, ssm_log_step, ssm_b_re, ssm_b_im,
                                                ssm_c_re, ssm_c_im)
    xall = jnp.concatenate([ctx.reshape(RC, D), x.reshape(RL, D)], axis=0).astype(F32)
    cond = jnp.zeros((8, D), F32).at[0].set(c_ctx.astype(F32)).at[1:1 + B].set(c.astype(F32))

    for layer in range(depth):
        with_ctx = layer < depth - 1
        row0 = 0 if with_ctx else RC
        n_rows = R - row0
        mods = mm1(cond, [(mod_w, (layer,), 0)], _epi_bias, n_rows=8, n_cols=6 * D, tm=8, tn=512, out_dtype=F32,
                   extras=[(mod_b.reshape(depth, 1, 6 * D), (None, 1, 512), lambda j, i: (layer, 0, j))],
                   prologue=lambda a: _silu(a).astype(BF16), name="ada_mod")
        sh_m, sc_m, g_m, sh_f, sc_f, g_f = [mods[:, i * D:(i + 1) * D].reshape(8, 1, D) for i in range(6)]

        h = modulate(dm, xall, norm_mix_g[layer], sh_m, sc_m, mod_row0=0)
        w_l = w_in[layer]
        w_rest = jnp.concatenate([w_l[:, src[n][0]:src[n][0] + src[n][1]] for n in order]
                                 + [jnp.zeros((D, z_cols - off), w_l.dtype)], axis=1)
        w_gates = w_l[:, src["gates"][0]:]
        z = mm1(h, [(w_rest, (), 0)], _epi_id, n_rows=R, n_cols=z_cols, tm=min(tm, 512), tn=z_tn, out_dtype=F32,
                name="w_in")
        gates = mm1(h, [(w_gates, (), 0)], _epi_sigmoid, n_rows=R, n_cols=n_gate, tm=tm, tn=1024,
                    out_dtype=BF16, name="w_in_gates")

        qa, ka, va = swa_prep(dm, z, col, swa_q_norm_g[layer], swa_k_norm_g[layer], cos_a, sin_a)
        ya_l = swa_attention(dm, qa, ka, va, swa_sink[layer], latent=True)
        w_uq = mla_w_uq[layer].reshape(q_rank, MLA_HEADS, MLA_QK)
        w_uq = jnp.pad(w_uq, ((0, 0), (0, 0), (0, MLA_QK_PAD - MLA_QK))).reshape(q_rank, MLA_HEADS * MLA_QK_PAD)
        w_ukv = mla_w_ukv[layer].reshape(kv_rank, MLA_HEADS, MLA_NOPE + MLA_V)
        w_ukv = jnp.concatenate([w_ukv[:, :, :MLA_NOPE].reshape(kv_rank, -1),
                                 w_ukv[:, :, MLA_NOPE:].reshape(kv_rank, -1)], axis=1)
        mla_args = (dm, z, col, mla_q_a_norm_g[layer], mla_kv_a_norm_g[layer], w_uq, w_ukv,
                    _pad_head_vec(mla_q_norm_g[layer]), _pad_head_vec(mla_k_norm_g[layer]), tabs_b)
        qm_c, km_c, vm_c, vt_c = mla_prep(*mla_args, row0=0, n_rows=RC, rows_per_batch=L)
        qm_l, km_l, _, vt_l = mla_prep(*mla_args, row0=RC, n_rows=RL, rows_per_batch=S)
        yb_l = mla_attention_lat(dm, qm_l, km_c, km_l, vt_c, vt_l)
        if with_ctx:
            ya = jnp.concatenate([swa_attention(dm, qa, ka, va, swa_sink[layer], latent=False), ya_l], axis=0)
            yb = jnp.concatenate([mla_attention_ctx(dm, qm_c, km_c, vm_c), yb_l], axis=0)
            y_row0s = (0, 0, 0)
        else:
            ya, yb = ya_l, yb_l
            y_row0s = (RC, RC, 0)
        yg = s5_branch(dm, z, col, ssm_compact, layer, ssm_d[layer])
        b_glu = ssm_b_glu.reshape(depth, 1, 2 * ssm_w)
        gl_tn = 512
        yc = mm1(yg, [(ssm_w_glu, (layer,), 0), (ssm_w_glu, (layer,), ssm_w)], _epi_glu_bias,
                 n_rows=R, n_cols=ssm_w, tm=tm, tn=gl_tn, out_dtype=BF16,
                 extras=[(b_glu, (None, 1, gl_tn), lambda j, i: (layer, 0, j)),
                         (b_glu, (None, 1, gl_tn), lambda j, i: (layer, 0, j + ssm_w // gl_tn))],
                 prologue=lambda a: a.astype(BF16), name="ssm_glu")
        mixed = merge_branches(dm, (ya, yb, yc), y_row0s, gates, w_branch, layer, row0=row0, n_rows=n_rows)
        t0 = row0 // tm
        x1 = mm1(mixed, [(w_out, (layer,), 0)], _epi_residual, n_rows=n_rows, n_cols=D, tm=tm, tn=1024, out_dtype=F32,
                 extras=[(xall, (tm, 1024), lambda j, i: (i + t0, j)),
                         (g_m, (1, 1, 1024), lambda j, i: (dm.mod_row(i + t0, tm), 0, j))],
                 name="w_out")
        is_moe = layer % 2 == 1
        h2 = modulate(dm, x1, norm_ffn_g[layer], sh_f, sc_f, mod_row0=row0, pack=is_moe)
        if not is_moe:
            F = ffn_w13.shape[-1] // 2
            f_tn = _pick_tile((512, 256, 128), F)
            act = mm1(h2, [(ffn_w13, (layer // 2,), 0), (ffn_w13, (layer // 2,), F)], _epi_swiglu, n_rows=n_rows,
                      n_cols=F, tm=tm, tn=f_tn, out_dtype=BF16, name="ffn_w13")
            x2 = mm2_residual(dm, act, ffn_w2[layer // 2].astype(BF16), x1, g_f, mod_row0=row0, tm=tm, tk=f_tn)
        else:
            if with_ctx:
                raise NotImplementedError("a mixture-of-experts layer that still feeds context rows")
            x2 = moe_ffn(dm, h2, x1, g_f, moe_w_router[layer // 2], moe_b_router[layer // 2], moe_w13, moe_w2,
                         layer // 2, mod_row0=row0)
        xall = x2
    return xall.reshape(B, S, D)


def kernel(x, c, ctx, c_ctx, mod_w, mod_b, norm_mix_g, norm_ffn_g, w_in, swa_q_norm_g, swa_k_norm_g, swa_sink, mla_q_a_norm_g, mla_w_uq, mla_kv_a_norm_g, mla_w_ukv, mla_q_norm_g, mla_k_norm_g, ssm_lam_re, ssm_lam_im, ssm_log_step, ssm_b_re, ssm_b_im, ssm_c_re, ssm_c_im, ssm_d, ssm_w_glu, ssm_b_glu, w_branch, w_out, ffn_w13, ffn_w2, moe_w_router, moe_b_router, moe_w13, moe_w2):
    return _trunk(x, c, ctx, c_ctx, mod_w, mod_b, norm_mix_g, norm_ffn_g, w_in, swa_q_norm_g, swa_k_norm_g, swa_sink,
                  mla_q_a_norm_g, mla_w_uq, mla_kv_a_norm_g, mla_w_ukv, mla_q_norm_g, mla_k_norm_g,
                  ssm_lam_re, ssm_lam_im, ssm_log_step, ssm_b_re, ssm_b_im, ssm_c_re, ssm_c_im,
                  ssm_d, ssm_w_glu, ssm_b_glu, w_branch, w_out, ffn_w13, ffn_w2, moe_w_router, moe_b_router,
                  moe_w13, moe_w2)
```

```python
import functools
import math

import jax
import jax.numpy as jnp
from jax import lax
from jax.experimental import pallas as pl
from jax.experimental.pallas import tpu as pltpu

F32 = jnp.float32
BF16 = jnp.bfloat16

GRID_W = 64
ROPE_THETA = 10000.0
EPS = 1e-6
SWA_HEADS = 8
SWA_KV_HEADS = 2
SWA_HEAD_DIM = 128
SWA_WINDOW = 128
MLA_HEADS = 8
MLA_NOPE = 128
MLA_ROPE = 64
MLA_V = 128
MLA_QK = MLA_NOPE + MLA_ROPE
MLA_QK_PAD = 256
SSM_GROUP = 16
SSM_STATE = 64
SSM_CHUNK = 16
N_BRANCH = 3
N_EXPERTS = 8
TOP_K = 2
LANE = 128
VMEM_LIMIT_BYTES = 56 * 1024 * 1024
MOE_TM = 512
NEG_BIG = -1e30


def _params(n_grid):
    return pltpu.CompilerParams(dimension_semantics=("arbitrary",) * n_grid, vmem_limit_bytes=VMEM_LIMIT_BYTES)


def _pick_tile(candidates, *sizes):
    for t in candidates:
        if all(s % t == 0 for s in sizes):
            return t
    raise ValueError(f"no tile in {candidates} divides {sizes}")


class Dims:
    def __init__(self, batch, seq, ctx_len):
        self.B, self.S, self.L = batch, seq, ctx_len
        self.RC = batch * ctx_len
        self.RL = batch * seq
        self.R = self.RC + self.RL
        self.tm = _pick_tile((1024, 512, 256, 128), ctx_len * batch, seq)

    def mod_row(self, tile, tm):
        nct = self.RC // tm
        return jnp.where(tile < nct, 0, 1 + (tile - nct) // (self.S // tm))


def _silu(x):
    return x * (1.0 / (1.0 + jnp.exp(-x)))


def _sigmoid(x):
    return 1.0 / (1.0 + jnp.exp(-x))


def _gelu_tanh(x):
    c = math.sqrt(2.0 / math.pi)
    return 0.5 * x * (1.0 + jnp.tanh(c * (x + 0.044715 * (x * x * x))))


def _dot(a, b):
    return jnp.dot(a, b, preferred_element_type=F32)


def _dot_nt(a, b):
    return lax.dot_general(a, b, (((1,), (1,)), ((), ())), preferred_element_type=F32)


def _pack_bf16_pairs(y):
    half = y.shape[1] // 2
    bits = lax.bitcast_convert_type(y.astype(BF16).astype(F32), jnp.uint32)
    return (bits[:, :half] >> 16) | (bits[:, half:] & jnp.uint32(0xFFFF0000))


def _unpack_bf16_pairs(p):
    lo = lax.bitcast_convert_type(p << 16, F32).astype(BF16)
    hi = lax.bitcast_convert_type(p & jnp.uint32(0xFFFF0000), F32).astype(BF16)
    return lo, hi


def _store_row_tiles(ref, val):
    m, w = val.shape
    n = w // LANE
    for c in range(n):
        ref[pl.ds(c, m, stride=n), :] = val[:, c * LANE:(c + 1) * LANE]


def _load_row_tiles(ref, n):
    if len(ref.shape) == 2:
        m = ref.shape[0] // n
        return jnp.concatenate([ref[pl.ds(c, m, stride=n), :] for c in range(n)], axis=1)
    m = ref.shape[1] // n
    return jnp.concatenate([ref[j, pl.ds(c, m, stride=n), :] for j in range(ref.shape[0]) for c in range(n)], axis=1)


def _modulate_kernel(*refs, pack, n_ctx_tiles):
    if n_ctx_tiles is None:
        x_ref, g_ref, sh_ref, sc_ref, o_ref = refs
        x = x_ref[...]
    else:
        xc_ref, xl_ref, g_ref, sh_ref, sc_ref, o_ref = refs
        x = jnp.where(pl.program_id(0) < n_ctx_tiles, xc_ref[...], xl_ref[...])
    ms = jnp.mean(x * x, axis=-1, keepdims=True)
    y = x * lax.rsqrt(ms + EPS) * g_ref[...]
    y = y * (1.0 + sc_ref[0]) + sh_ref[0]
    if pack:
        _store_row_tiles(o_ref, _pack_bf16_pairs(y))
    else:
        o_ref[...] = y.astype(o_ref.dtype)


def modulate(dm, x, g, shift, scale, *, mod_row0, pack=False, x_ctx=None):
    D = x.shape[1]
    tm = min(dm.tm, 512)
    t0 = mod_row0 // tm
    if x_ctx is None:
        n_rows = x.shape[0]
        n_ctx_tiles = None
        x_specs = [pl.BlockSpec((tm, D), lambda i: (i, 0))]
        xs = (x,)
    else:
        n_rows = x_ctx.shape[0] + x.shape[0]
        n_ctx_tiles = x_ctx.shape[0] // tm
        x_specs = [pl.BlockSpec((tm, D), lambda i: (jnp.minimum(i, n_ctx_tiles - 1), 0)),
                   pl.BlockSpec((tm, D), lambda i: (jnp.maximum(i - n_ctx_tiles, 0), 0))]
        xs = (x_ctx, x)
    if pack:
        n = D // 2 // LANE
        out_spec = pl.BlockSpec((tm * n, LANE), lambda i: (i, 0))
        out_shape = jax.ShapeDtypeStruct((n_rows * n, LANE), jnp.uint32)
    else:
        out_spec = pl.BlockSpec((tm, D), lambda i: (i, 0))
        out_shape = jax.ShapeDtypeStruct((n_rows, D), BF16)
    return pl.pallas_call(
        functools.partial(_modulate_kernel, pack=pack, n_ctx_tiles=n_ctx_tiles),
        grid=(n_rows // tm,),
        in_specs=x_specs + [
            pl.BlockSpec((1, D), lambda i: (0, 0)),
            pl.BlockSpec((1, 1, D), lambda i: (dm.mod_row(i + t0, tm), 0, 0)),
            pl.BlockSpec((1, 1, D), lambda i: (dm.mod_row(i + t0, tm), 0, 0)),
        ],
        out_specs=out_spec,
        out_shape=out_shape,
        compiler_params=_params(1),
        name="modulate_packed" if pack else "modulate",
    )(*xs, g.reshape(1, D), shift, scale)


def _mm1_kernel(*refs, n_w, n_extra, epilogue, prologue):
    a_ref = refs[0]
    w_refs = refs[1:1 + n_w]
    extra = refs[1 + n_w:1 + n_w + n_extra]
    o_ref = refs[1 + n_w + n_extra]
    wb = refs[2 + n_w + n_extra:]

    @pl.when(pl.program_id(1) == 0)
    def _():
        for w_ref, b in zip(w_refs, wb):
            b[...] = w_ref[...].astype(BF16)

    a = a_ref[...]
    if prologue is not None:
        a = prologue(a)
    accs = [_dot(a, b[...]) for b in wb]
    o_ref[...] = epilogue(accs, *extra).astype(o_ref.dtype)


def mm1(a, weights, epilogue, *, n_rows, n_cols, tm, tn, out_dtype, a_row0=0, extras=(), prologue=None, name):
    K = a.shape[1]
    t0 = a_row0 // tm
    in_specs = [pl.BlockSpec((tm, K), lambda j, i: (i + t0, 0))]
    operands = [a]
    for w, lead, col0 in weights:
        c0 = col0 // tn
        in_specs.append(pl.BlockSpec((None,) * len(lead) + (K, tn),
                                     functools.partial(lambda j, i, lead, c0: lead + (0, j + c0), lead=lead, c0=c0)))
        operands.append(w)
    for arr, bshape, imap in extras:
        in_specs.append(pl.BlockSpec(bshape, imap))
        operands.append(arr)
    kern = functools.partial(_mm1_kernel, n_w=len(weights), n_extra=len(extras), epilogue=epilogue,
                             prologue=prologue)
    return pl.pallas_call(
        kern,
        grid=(n_cols // tn, n_rows // tm),
        in_specs=in_specs,
        out_specs=pl.BlockSpec((tm, tn), lambda j, i: (i, j)),
        out_shape=jax.ShapeDtypeStruct((n_rows, n_cols), out_dtype),
        scratch_shapes=[pltpu.VMEM((K, tn), BF16) for _ in weights],
        compiler_params=_params(2),
        name=name,
    )(*operands)


def _epi_id(accs):
    return accs[0]


def _epi_sigmoid(accs):
    return _sigmoid(accs[0])


def _epi_swiglu(accs):
    return _silu(accs[0]) * accs[1]


def _epi_bias(accs, b_ref):
    return accs[0] + b_ref[...]


def _epi_glu_bias(accs, ba_ref, bb_ref):
    return (accs[0] + ba_ref[...]) * _sigmoid(accs[1] + bb_ref[...])


def _epi_residual(accs, x_ref, gate_ref):
    return x_ref[...] + gate_ref[0] * accs[0]


def _epi_residual_two_sources(accs, xc_ref, xl_ref, gate_ref, *, n_ctx_tiles):
    x = jnp.where(pl.program_id(1) < n_ctx_tiles, xc_ref[...], xl_ref[...])
    return x + gate_ref[0] * accs[0]


def _mm2_kernel(a_ref, w_ref, x_ref, gate_ref, o_ref, acc_ref):
    k = pl.program_id(1)

    @pl.when(k == 0)
    def _():
        acc_ref[...] = jnp.zeros_like(acc_ref)

    acc_ref[...] += _dot(a_ref[...], w_ref[...])

    @pl.when(k == pl.num_programs(1) - 1)
    def _():
        o_ref[...] = x_ref[...] + gate_ref[0] * acc_ref[...]


def mm2_residual(dm, a, w, x, gate, *, mod_row0, tm, tk):
    M, K = a.shape
    N = w.shape[1]
    t0 = mod_row0 // tm
    return pl.pallas_call(
        _mm2_kernel,
        grid=(M // tm, K // tk),
        in_specs=[
            pl.BlockSpec((tm, tk), lambda i, k: (i, k)),
            pl.BlockSpec((tk, N), lambda i, k: (k, 0)),
            pl.BlockSpec((tm, N), lambda i, k: (i, 0)),
            pl.BlockSpec((1, 1, N), lambda i, k: (dm.mod_row(i + t0, tm), 0, 0)),
        ],
        out_specs=pl.BlockSpec((tm, N), lambda i, k: (i, 0)),
        out_shape=jax.ShapeDtypeStruct((M, N), F32),
        scratch_shapes=[pltpu.VMEM((tm, N), F32)],
        compiler_params=_params(2),
        name="mm2_residual",
    )(a, w, x, gate)


def _swa_prep_kernel(q_ref, k_ref, v_ref, gq_ref, gk_ref, cos_ref, sin_ref, qo_ref, ko_ref, vo_ref):
    c = cos_ref[...]
    s = sin_ref[...]

    def norm_rope(x, g, scale):
        ms = jnp.mean(x * x, axis=-1, keepdims=True)
        y = x * lax.rsqrt(ms + EPS) * g
        return (y * c + pltpu.roll(y, SWA_HEAD_DIM // 2, 1) * s) * scale

    gq = gq_ref[...]
    gk = gk_ref[...]
    for h in range(SWA_HEADS):
        sl = slice(h * SWA_HEAD_DIM, (h + 1) * SWA_HEAD_DIM)
        qo_ref[:, sl] = norm_rope(q_ref[:, sl], gq, SWA_HEAD_DIM ** -0.5).astype(BF16)
    for h in range(SWA_KV_HEADS):
        sl = slice(h * SWA_HEAD_DIM, (h + 1) * SWA_HEAD_DIM)
        ko_ref[:, sl] = norm_rope(k_ref[:, sl], gk, 1.0).astype(BF16)
    vo_ref[...] = v_ref[...].astype(BF16)


def _rope_tile_index(dm, tm):
    nct = dm.RC // tm
    return lambda i: (jnp.where(i < nct, 0, 1 + (i - nct) % (dm.S // tm)), 0)


def swa_prep(dm, z, col, gq, gk, cos_t, sin_t):
    tm = min(dm.tm, 256)
    QW = SWA_HEADS * SWA_HEAD_DIM
    KW = SWA_KV_HEADS * SWA_HEAD_DIM
    ridx = _rope_tile_index(dm, tm)
    return pl.pallas_call(
        _swa_prep_kernel,
        grid=(dm.R // tm,),
        in_specs=[
            pl.BlockSpec((tm, QW), lambda i: (i, col["q"] // QW)),
            pl.BlockSpec((tm, KW), lambda i: (i, col["k"] // KW)),
            pl.BlockSpec((tm, KW), lambda i: (i, col["v"] // KW)),
            pl.BlockSpec((1, SWA_HEAD_DIM), lambda i: (0, 0)),
            pl.BlockSpec((1, SWA_HEAD_DIM), lambda i: (0, 0)),
            pl.BlockSpec((tm, SWA_HEAD_DIM), ridx),
            pl.BlockSpec((tm, SWA_HEAD_DIM), ridx),
        ],
        out_specs=[
            pl.BlockSpec((tm, QW), lambda i: (i, 0)),
            pl.BlockSpec((tm, KW), lambda i: (i, 0)),
            pl.BlockSpec((tm, KW), lambda i: (i, 0)),
        ],
        out_shape=[
            jax.ShapeDtypeStruct((dm.R, QW), BF16),
            jax.ShapeDtypeStruct((dm.R, KW), BF16),
            jax.ShapeDtypeStruct((dm.R, KW), BF16),
        ],
        compiler_params=_params(1),
        name="swa_prep",
    )(z, z, z, gq.reshape(1, -1), gk.reshape(1, -1), cos_t, sin_t)


def _swa_attn_kernel(*refs, windowed, nb):
    if windowed:
        q_ref, kc_ref, kp_ref, kk_ref, kn_ref, vc_ref, vp_ref, vk_ref, vn_ref, sink_ref, o_ref = refs
    else:
        q_ref, kc_ref, vc_ref, sink_ref, o_ref = refs
    G = SWA_HEADS // SWA_KV_HEADS
    blk = q_ref.shape[0]
    Dh = SWA_HEAD_DIM
    q = jnp.concatenate([q_ref[:, g * Dh:(g + 1) * Dh] for g in range(G)], axis=0)
    sink = sink_ref[0][:, 0:1]
    scores = [_dot_nt(q, kc_ref[...])]
    values = [vc_ref[...]]
    if windowed:
        n = pl.program_id(2)
        qi = lax.broadcasted_iota(jnp.int32, (G * blk, blk), 0) % blk
        kj = lax.broadcasted_iota(jnp.int32, (G * blk, blk), 1)
        s_p = _dot_nt(q, kp_ref[...])
        s_p = jnp.where(kj >= qi, s_p, NEG_BIG)
        s_p = jnp.where(n >= 1, s_p, NEG_BIG)
        s_n = _dot_nt(q, kn_ref[...])
        s_n = jnp.where(kj <= qi, s_n, NEG_BIG)
        s_n = jnp.where(n <= nb - 2, s_n, NEG_BIG)
        scores += [s_p, _dot_nt(q, kk_ref[...]), s_n]
        values += [vp_ref[...], vk_ref[...], vn_ref[...]]
    m = sink
    for s in scores:
        m = jnp.maximum(m, jnp.max(s, axis=-1, keepdims=True))
    l = jnp.exp(sink - m)
    o = None
    for s, v in zip(scores, values):
        p = jnp.exp(s - m)
        l = l + jnp.sum(p, axis=-1, keepdims=True)
        pv = _dot(p.astype(BF16), v)
        o = pv if o is None else o + pv
    o = o / l
    for g in range(G):
        o_ref[:, g * Dh:(g + 1) * Dh] = o[g * blk:(g + 1) * blk].astype(o_ref.dtype)


def swa_attention(dm, qa, ka, va, sink, *, latent):
    G = SWA_HEADS // SWA_KV_HEADS
    Dh = SWA_HEAD_DIM
    blk = SWA_WINDOW
    L = dm.L
    sink_col = jnp.broadcast_to(sink.astype(F32).reshape(SWA_KV_HEADS, G, 1, 1),
                                (SWA_KV_HEADS, G, blk, LANE)).reshape(SWA_KV_HEADS, G * blk, LANE)
    sink_spec = pl.BlockSpec((1, G * blk, LANE), lambda b, h, n: (h, 0, 0))
    ctx_spec = pl.BlockSpec((L, Dh), lambda b, h, n: (b, h))
    if latent:
        nb = dm.S // blk
        base = dm.RC // blk

        def q_map(b, h, n):
            return (base + b * nb + n, h)

        def kv_map(off):
            return lambda b, h, n: (base + b * nb + jnp.clip(n + off, 0, nb - 1), h)

        win_specs = [pl.BlockSpec((blk, Dh), kv_map(off)) for off in (-1, 0, 1)]
        in_specs = ([pl.BlockSpec((blk, G * Dh), q_map), ctx_spec] + win_specs + [ctx_spec] + win_specs
                    + [sink_spec])
        operands = (qa, ka, ka, ka, ka, va, va, va, va, sink_col)
        n_out = dm.RL
    else:
        nb = L // blk
        in_specs = [pl.BlockSpec((blk, G * Dh), lambda b, h, n: (b * nb + n, h)), ctx_spec, ctx_spec, sink_spec]
        operands = (qa, ka, va, sink_col)
        n_out = dm.RC
    return pl.pallas_call(
        functools.partial(_swa_attn_kernel, windowed=latent, nb=nb),
        grid=(dm.B, SWA_KV_HEADS, nb),
        in_specs=in_specs,
        out_specs=pl.BlockSpec((blk, G * Dh), lambda b, h, n: (b * nb + n, h)),
        out_shape=jax.ShapeDtypeStruct((n_out, SWA_HEADS * Dh), BF16),
        compiler_params=_params(3),
        name="swa_attn_lat" if latent else "swa_attn_ctx",
    )(*operands)


def _mla_prep_kernel(cq_ref, ckv_ref, pe_ref, gqa_ref, gkva_ref, wq_ref, wkv_ref, gq_ref, gk_ref, c_ref, s1_ref,
                     s2_ref, qo_ref, ko_ref, vo_ref, vt_ref, wq_s, wkv_s):
    @pl.when(pl.program_id(0) == 0)
    def _():
        wq_s[...] = wq_ref[...].astype(BF16)
        wkv_s[...] = wkv_ref[...].astype(BF16)

    def rms(x_ref, g_ref):
        x = x_ref[...]
        ms = jnp.mean(x * x, axis=-1, keepdims=True)
        return (x * lax.rsqrt(ms + EPS) * g_ref[...]).astype(BF16)

    qf = _dot(rms(cq_ref, gqa_ref), wq_s[...])
    kvf = _dot(rms(ckv_ref, gkva_ref), wkv_s[...])
    c = c_ref[...]
    s1 = s1_ref[...]
    s2 = s2_ref[...]
    gq = gq_ref[...]
    gk = gk_ref[...]
    scale = MLA_QK ** -0.5 * math.log2(math.e)

    def rope(x):
        return x * c + pltpu.roll(x, LANE - MLA_ROPE // 2, 1) * s1 + pltpu.roll(x, MLA_ROPE // 2, 1) * s2

    pe = pe_ref[...]
    pe_ss = jnp.sum(pe * pe, axis=-1, keepdims=True)
    NW = MLA_HEADS * MLA_NOPE
    for h in range(MLA_HEADS):
        lo = h * MLA_QK_PAD
        qh = qf[:, lo:lo + MLA_QK_PAD]
        inv = lax.rsqrt(jnp.sum(qh * qh, axis=-1, keepdims=True) * (1.0 / MLA_QK) + EPS)
        qn = qh * inv * gq
        qo_ref[:, lo:lo + MLA_NOPE] = (qn[:, :MLA_NOPE] * scale).astype(BF16)
        qo_ref[:, lo + MLA_NOPE:lo + MLA_QK_PAD] = (rope(qn[:, MLA_NOPE:]) * scale).astype(BF16)
        kh = kvf[:, h * MLA_NOPE:(h + 1) * MLA_NOPE]
        inv = lax.rsqrt((jnp.sum(kh * kh, axis=-1, keepdims=True) + pe_ss) * (1.0 / MLA_QK) + EPS)
        ko_ref[:, lo:lo + MLA_NOPE] = (kh * inv * gk[:, :MLA_NOPE]).astype(BF16)
        ko_ref[:, lo + MLA_NOPE:lo + MLA_QK_PAD] = rope(pe * inv * gk[:, MLA_NOPE:]).astype(BF16)
    v = kvf[:, NW:]
    vo_ref[...] = v.astype(BF16)
    vt_ref[...] = v.T.astype(BF16)


def mla_prep(dm, z, col, gqa, gkva, w_uq, w_ukv, gq_pad, gk_pad, tabs, *, row0, n_rows, rows_per_batch):
    tm = min(dm.tm, 256)
    t0 = row0 // tm
    QW = MLA_HEADS * MLA_QK_PAD
    NW = MLA_HEADS * MLA_NOPE
    q_rank, kv_rank = w_uq.shape[0], w_ukv.shape[0]
    ridx = _rope_tile_index(dm, tm)
    rspec = pl.BlockSpec((tm, LANE), lambda i: ridx(i + t0))
    tpb = rows_per_batch // tm
    const = lambda i: (0, 0)
    return pl.pallas_call(
        _mla_prep_kernel,
        grid=(n_rows // tm,),
        in_specs=[
            pl.BlockSpec((tm, q_rank), lambda i: (i + t0, col["c_q"] // q_rank)),
            pl.BlockSpec((tm, kv_rank), lambda i: (i + t0, col["c_kv"] // kv_rank)),
            pl.BlockSpec((tm, LANE), lambda i: (i + t0, col["kpe"] // LANE)),
            pl.BlockSpec((1, q_rank), const),
            pl.BlockSpec((1, kv_rank), const),
            pl.BlockSpec((q_rank, QW), const),
            pl.BlockSpec((kv_rank, 2 * NW), const),
            pl.BlockSpec((1, MLA_QK_PAD), const),
            pl.BlockSpec((1, MLA_QK_PAD), const),
            rspec, rspec, rspec,
        ],
        out_specs=[
            pl.BlockSpec((tm, QW), lambda i: (i, 0)),
            pl.BlockSpec((tm, QW), lambda i: (i, 0)),
            pl.BlockSpec((tm, NW), lambda i: (i, 0)),
            pl.BlockSpec((NW, tm), lambda i: (i // tpb, i % tpb)),
        ],
        out_shape=[
            jax.ShapeDtypeStruct((n_rows, QW), BF16),
            jax.ShapeDtypeStruct((n_rows, QW), BF16),
            jax.ShapeDtypeStruct((n_rows, NW), BF16),
            jax.ShapeDtypeStruct((n_rows // rows_per_batch * NW, rows_per_batch), BF16),
        ],
        scratch_shapes=[pltpu.VMEM((q_rank, QW), BF16), pltpu.VMEM((kv_rank, 2 * NW), BF16)],
        compiler_params=_params(1),
        name="mla_prep",
    )(z, z, z, gqa.reshape(1, -1), gkva.reshape(1, -1), w_uq, w_ukv, gq_pad, gk_pad, *tabs)


MLA_KEY_CHUNK = 512


def _mla_attn_ctx_kernel(q_ref, kc_ref, vc_ref, o_ref):
    s = _dot_nt(q_ref[...], kc_ref[...])
    p = jnp.exp2(s - jnp.max(s, axis=-1, keepdims=True))
    o = _dot(p.astype(BF16), vc_ref[...])
    o_ref[...] = (o / jnp.sum(p, axis=-1, keepdims=True)).astype(o_ref.dtype)


def _mla_attn_lat_kernel(q_ref, kc_ref, kl_ref, vct_ref, vlt_ref, o_ref, s_scr):
    tq = q_ref.shape[0]
    q = q_ref[...]
    L, S = kc_ref.shape[0], kl_ref.shape[0]
    tk = min(MLA_KEY_CHUNK, S)
    chunks = [(kc_ref, vct_ref, 0, L, 0)] + [(kl_ref, vlt_ref, c * tk, tk, L + c * tk) for c in range(S // tk)]
    mx = jnp.full((tq, LANE), NEG_BIG, F32)
    for k_ref, _, off, w, so in chunks:
        s = _dot_nt(q, k_ref[off:off + w, :])
        s_scr[:, so:so + w] = s
        for g in range(w // LANE):
            mx = jnp.maximum(mx, s[:, g * LANE:(g + 1) * LANE])
    m = jnp.max(mx, axis=-1, keepdims=True)
    ls = jnp.zeros((tq, LANE), F32)
    acc = jnp.zeros((vct_ref.shape[0], tq), F32)
    for _, vt_ref, off, w, so in chunks:
        p = jnp.exp2(s_scr[:, so:so + w] - m)
        for g in range(w // LANE):
            ls = ls + p[:, g * LANE:(g + 1) * LANE]
        acc = acc + _dot_nt(vt_ref[:, off:off + w], p.astype(BF16))
    l = jnp.sum(ls, axis=-1, keepdims=True)
    o_ref[...] = (acc.T / l).astype(o_ref.dtype)


def mla_attention_ctx(dm, q, kc, vc):
    L = dm.L
    QP, V = MLA_QK_PAD, MLA_V
    return pl.pallas_call(
        _mla_attn_ctx_kernel,
        grid=(dm.B, MLA_HEADS),
        in_specs=[pl.BlockSpec((L, QP), lambda b, h: (b, h)),
                  pl.BlockSpec((L, QP), lambda b, h: (b, h)),
                  pl.BlockSpec((L, V), lambda b, h: (b, h))],
        out_specs=pl.BlockSpec((L, V), lambda b, h: (b, h)),
        out_shape=jax.ShapeDtypeStruct((dm.RC, MLA_HEADS * V), BF16),
        compiler_params=_params(2),
        name="mla_attn_ctx",
    )(q, kc, vc)


def mla_attention_lat(dm, q, kc, kl, vct, vlt):
    L, S = dm.L, dm.S
    tq = min(512, S)
    nq = S // tq
    QP, V = MLA_QK_PAD, MLA_V
    H = MLA_HEADS
    return pl.pallas_call(
        _mla_attn_lat_kernel,
        grid=(dm.B, H, nq),
        in_specs=[pl.BlockSpec((tq, QP), lambda b, h, n: (b * nq + n, h)),
                  pl.BlockSpec((L, QP), lambda b, h, n: (b, h)),
                  pl.BlockSpec((S, QP), lambda b, h, n: (b, h)),
                  pl.BlockSpec((V, L), lambda b, h, n: (b * H + h, 0)),
                  pl.BlockSpec((V, S), lambda b, h, n: (b * H + h, 0))],
        out_specs=pl.BlockSpec((tq, V), lambda b, h, n: (b * nq + n, h)),
        out_shape=jax.ShapeDtypeStruct((dm.RL, H * V), BF16),
        scratch_shapes=[pltpu.VMEM((tq, L + S), F32)],
        compiler_params=_params(3),
        name="mla_attn_lat",
    )(q, kc, kl, vct, vlt)


SSM_SUPER = 16
SSM_BLOCK_GROUPS = LANE // SSM_GROUP


def _ssm_in_kernel(u_ref, wi_ref, ws_ref, y_ref, s_ref):
    a = _load_row_tiles(u_ref, SSM_CHUNK).astype(BF16)
    y_ref[...] = _dot(a, wi_ref[...])
    zs = _dot(a, ws_ref[...])
    for c in range(s_ref.shape[0]):
        s_ref[c] = zs[:, c * LANE:(c + 1) * LANE]


def ssm_chunk_in(z, u_col, w_intra, w_state, *, tr):
    R = z.shape[0]
    nblk, CW, _ = w_intra.shape
    nr = tr // SSM_CHUNK
    c0 = u_col // LANE
    w_spec = pl.BlockSpec((None, CW, CW), lambda j, i: (j, 0, 0), pipeline_mode=pl.Buffered(1))
    return pl.pallas_call(
        _ssm_in_kernel,
        grid=(nblk, R // tr),
        in_specs=[pl.BlockSpec((tr, LANE), lambda j, i: (i, c0 + j)), w_spec, w_spec],
        out_specs=[pl.BlockSpec((nr, CW), lambda j, i: (i, j)),
                   pl.BlockSpec((2 * SSM_BLOCK_GROUPS, nr, LANE), lambda j, i: (0, i, j))],
        out_shape=[jax.ShapeDtypeStruct((R // SSM_CHUNK, nblk * CW), F32),
                   jax.ShapeDtypeStruct((2 * SSM_BLOCK_GROUPS, R // SSM_CHUNK, nblk * LANE), F32)],
        compiler_params=_params(2),
        name="ssm_chunk_in",
    )(z, w_intra, w_state)


def _ssm_scan_kernel(s_ref, p1_ref, p2_ref, x_ref, t_ref, e_ref, *, batch, n_ctx_sc, n_lat_sc):
    SC = SSM_SUPER
    GB = SSM_BLOCK_GROUPS
    FWD, BWD = slice(0, GB), slice(GB, 2 * GB)
    n_sc = s_ref.shape[1] // SC

    def cmul(i, rows, x):
        swapped = jnp.concatenate([x[..., SSM_STATE:], x[..., :SSM_STATE]], axis=-1)
        return p1_ref[i, rows] * x + p2_ref[i, rows] * swapped

    def chunk(i):
        return pl.ds(i, n_sc, stride=SC)

    lf = jnp.zeros((GB, n_sc, LANE), F32)
    lb = jnp.zeros((GB, n_sc, LANE), F32)
    for i in range(SC):
        x_ref[FWD, chunk(i), :] = lf
        x_ref[BWD, chunk(SC - 1 - i), :] = lb
        lf = cmul(1, FWD, lf) + s_ref[FWD, chunk(i), :]
        lb = cmul(1, BWD, lb) + s_ref[BWD, chunk(SC - 1 - i), :]
    t_ref[FWD] = lf
    t_ref[BWD] = lb
    n_ctx = batch * n_ctx_sc
    for rows, order in ((FWD, 1), (BWD, -1)):
        e = jnp.zeros((GB, batch, LANE), F32)
        for region_start, per_batch in ((0, n_ctx_sc), (n_ctx, n_lat_sc)):
            steps = range(per_batch) if order == 1 else range(per_batch - 1, -1, -1)
            for m in steps:
                idx = pl.ds(region_start + m, batch, stride=per_batch)
                e_ref[rows, idx, :] = e
                e = cmul(SC, rows, e) + t_ref[rows, idx, :]
    ef = e_ref[FWD]
    eb = e_ref[BWD]
    for i in range(SC):
        x_ref[FWD, chunk(i), :] += cmul(i, FWD, ef)
        x_ref[BWD, chunk(SC - 1 - i), :] += cmul(i, BWD, eb)


def ssm_scan(dm, s, p1, p2, layer):
    NS, NR, W = s.shape
    nblk = W // LANE
    n_sc = NR // SSM_SUPER
    n_ctx_sc = dm.L // (SSM_CHUNK * SSM_SUPER)
    n_lat_sc = dm.S // (SSM_CHUNK * SSM_SUPER)
    blk = pl.BlockSpec((NS, NR, LANE), lambda j: (0, 0, j))
    pspec = pl.BlockSpec((None, None, SSM_SUPER + 1, NS, 1, LANE), lambda j: (layer, j, 0, 0, 0, 0))
    return pl.pallas_call(
        functools.partial(_ssm_scan_kernel, batch=dm.B, n_ctx_sc=n_ctx_sc, n_lat_sc=n_lat_sc),
        grid=(nblk,),
        in_specs=[blk, pspec, pspec],
        out_specs=blk,
        out_shape=jax.ShapeDtypeStruct(s.shape, F32),
        scratch_shapes=[pltpu.VMEM((NS, n_sc, LANE), F32), pltpu.VMEM((NS, n_sc, LANE), F32)],
        compiler_params=_params(1),
        name="ssm_scan",
    )(s, p1, p2)


def _ssm_out_kernel(y_ref, x_ref, u_ref, w_ref, d_ref, o_ref):
    nr = y_ref.shape[0]
    xs = jnp.concatenate([x_ref[c] for c in range(x_ref.shape[0])], axis=1).astype(BF16)
    y = y_ref[...] + _dot(xs, w_ref[...])
    d = d_ref[...]
    for t in range(SSM_CHUNK):
        rows = pl.ds(t, nr, stride=SSM_CHUNK)
        o_ref[rows, :] = _gelu_tanh(y[:, t * LANE:(t + 1) * LANE] + d * u_ref[rows, :])


def ssm_chunk_out(y_intra, x_states, z, u_col, w_out_state, d_skip, *, tr):
    R = z.shape[0]
    nblk, CW, _ = w_out_state.shape
    nr = tr // SSM_CHUNK
    c0 = u_col // LANE
    return pl.pallas_call(
        _ssm_out_kernel,
        grid=(nblk, R // tr),
        in_specs=[pl.BlockSpec((nr, CW), lambda j, i: (i, j)),
                  pl.BlockSpec((2 * SSM_BLOCK_GROUPS, nr, LANE), lambda j, i: (0, i, j)),
                  pl.BlockSpec((tr, LANE), lambda j, i: (i, c0 + j)),
                  pl.BlockSpec((None, CW, CW), lambda j, i: (j, 0, 0), pipeline_mode=pl.Buffered(1)),
                  pl.BlockSpec((1, LANE), lambda j, i: (0, j))],
        out_specs=pl.BlockSpec((tr, LANE), lambda j, i: (i, j)),
        out_shape=jax.ShapeDtypeStruct((R, nblk * LANE), F32),
        compiler_params=_params(2),
        name="ssm_chunk_out",
    )(y_intra, x_states, z, w_out_state, d_skip.astype(F32).reshape(1, -1))


def _ssm_expand_kernel(k_ref, o_ref, *, mode):
    C, H, GB = SSM_CHUNK, SSM_GROUP, SSM_BLOCK_GROUPS
    CH = C * H
    W = o_ref.shape[1]
    ri = lax.broadcasted_iota(jnp.int32, (CH, W), 0)
    ci = lax.broadcasted_iota(jnp.int32, (CH, W), 1)
    for gl in range(GB):
        kc = k_ref[gl].astype(BF16)
        if mode == "state":
            zero = jnp.zeros((CH, LANE), BF16)
            cols = [kc[:, d * LANE:(d + 1) * LANE] if g2 == gl else zero for d in range(2) for g2 in range(GB)]
            t = jnp.concatenate(cols, axis=1)
        else:
            sel = jnp.where(ci == (ri >> 4) * LANE + gl * H + (ri & (H - 1)), 1.0, 0.0).astype(BF16)
            t = _dot(kc, sel).astype(BF16)
        if mode == "out":
            for d in range(2):
                o_ref[d * GB * LANE + gl * LANE:d * GB * LANE + (gl + 1) * LANE, :] = t[d * LANE:(d + 1) * LANE, :]
        else:
            for s in range(C):
                o_ref[s * LANE + gl * H:s * LANE + (gl + 1) * H, :] = t[s * H:(s + 1) * H, :]


def _ssm_expand(compact, layer, mode):
    _, G, CH, _ = compact.shape
    GB = SSM_BLOCK_GROUPS
    CW = CH * GB
    return pl.pallas_call(
        functools.partial(_ssm_expand_kernel, mode=mode),
        grid=(G // GB,),
        in_specs=[pl.BlockSpec((None, GB, CH, CH), lambda j: (layer, j, 0, 0))],
        out_specs=pl.BlockSpec((None, CW, CW), lambda j: (j, 0, 0)),
        out_shape=jax.ShapeDtypeStruct((G // GB, CW, CW), BF16),
        compiler_params=_params(1),
        name="ssm_expand_" + mode,
    )(compact)


def _ssm_compact_tables(lam_re, lam_im, log_step, b_re, b_im, c_re, c_im):
    C, H, P = SSM_CHUNK, SSM_GROUP, SSM_STATE
    G = lam_re.shape[1]
    delta = jnp.exp(log_step.astype(F32))[..., None]
    zr = lam_re.astype(F32) * delta
    zi = lam_im.astype(F32) * delta
    k = jnp.arange(C + 1, dtype=F32)[:, None, None, None]
    mag = jnp.exp(k * zr[None])
    pw_re = mag * jnp.cos(k * zi[None])
    pw_im = mag * jnp.sin(k * zi[None])
    lb_re, lb_im = pw_re[1], pw_im[1]
    lr, li = lam_re.astype(F32), lam_im.astype(F32)
    den = lr * lr + li * li
    f_re = ((lb_re - 1.0) * lr + lb_im * li) / den
    f_im = (lb_im * lr - (lb_re - 1.0) * li) / den
    br, bi = b_re.astype(F32), b_im.astype(F32)
    bb_re = f_re[..., None] * br - f_im[..., None] * bi
    bb_im = f_re[..., None] * bi + f_im[..., None] * br
    cr, ci = c_re.astype(F32), c_im.astype(F32)
    cl_re = cr[None] * pw_re[:, :, :, None, :] - ci[None] * pw_im[:, :, :, None, :]
    cl_im = cr[None] * pw_im[:, :, :, None, :] + ci[None] * pw_re[:, :, :, None, :]
    hp = lax.Precision.HIGHEST
    kern = (jnp.einsum("kdghp,dgpj->dgkhj", cl_re[:C], bb_re, precision=hp)
            - jnp.einsum("kdghp,dgpj->dgkhj", cl_im[:C], bb_im, precision=hp))
    k_idx = jnp.arange(C)[:, None, None]
    s_idx = jnp.arange(C)[None, :, None]
    t_idx = jnp.arange(C)[None, None, :]
    sel_f = (t_idx - s_idx == k_idx).astype(F32)
    sel_b = (s_idx - t_idx == k_idx).astype(F32)
    ksum = (jnp.einsum("kst,gkhj->gsjth", sel_f, kern[0], precision=hp)
            + jnp.einsum("kst,gkhj->gsjth", sel_b, kern[1], precision=hp))

    def state_in(d, power_of_s):
        pr = pw_re[power_of_s, d]
        pi = pw_im[power_of_s, d]
        re = pr[..., None] * bb_re[d][None] - pi[..., None] * bb_im[d][None]
        im = pr[..., None] * bb_im[d][None] + pi[..., None] * bb_re[d][None]
        return jnp.concatenate([re, im], axis=2).transpose(1, 0, 3, 2)

    def state_out(d, power_of_t):
        re = cl_re[power_of_t, d]
        im = cl_im[power_of_t, d]
        return jnp.concatenate([re, -im], axis=-1).transpose(1, 3, 0, 2)

    m_sum = jnp.stack([state_in(0, C - 1 - jnp.arange(C)), state_in(1, jnp.arange(C))])
    m_out = jnp.stack([state_out(0, 1 + jnp.arange(C)), state_out(1, C - jnp.arange(C))])

    GB = SSM_BLOCK_GROUPS
    nblk = G // GB
    ksum = ksum.reshape(G, C * H, C * H)
    m_sum = m_sum.transpose(1, 2, 3, 0, 4).reshape(G, C * H, 2 * 2 * P)
    m_out = m_out.transpose(1, 0, 2, 3, 4).reshape(G, 2 * 2 * P, C * H)
    i = (C * jnp.arange(SSM_SUPER + 1, dtype=F32))[:, None, None, None]
    mag_a = jnp.exp(i * zr[None])
    pa_re = mag_a * jnp.cos(i * zi[None])
    pa_im = mag_a * jnp.sin(i * zi[None])

    def scan_table(lo, hi):
        t = jnp.concatenate([lo, hi], axis=-1).reshape(SSM_SUPER + 1, 2, nblk, GB, 2 * P)
        return t.transpose(2, 0, 1, 3, 4).reshape(nblk, SSM_SUPER + 1, 2 * GB, 1, 2 * P)

    return ksum, m_sum, m_out, scan_table(pa_re, pa_re), scan_table(-pa_im, pa_im)


def s5_branch(dm, z, col, compact, layer, d_skip):
    ksum, m_sum, m_out, p1, p2 = compact
    w_intra = _ssm_expand(ksum, layer, "intra")
    w_state = _ssm_expand(m_sum, layer, "state")
    w_out_state = _ssm_expand(m_out, layer, "out")
    nr = _pick_tile((272, 136, 96, 64, 32, 16, 8), dm.R // SSM_CHUNK)
    tr = nr * SSM_CHUNK
    y_intra, s = ssm_chunk_in(z, col["u"], w_intra, w_state, tr=tr)
    x_states = ssm_scan(dm, s, p1, p2, layer)
    return ssm_chunk_out(y_intra, x_states, z, col["u"], w_out_state, d_skip, tr=tr)


def _merge_kernel(*refs, n_ctx_tiles):
    if n_ctx_tiles is None:
        ya_ref, yb_ref, yc_ref, ga_ref, gb_ref, gc_ref, wa_ref, wb_ref, wc_ref, o_ref, sa, sb, sc = refs
        ya, yb = ya_ref[...], yb_ref[...]
    else:
        (yac_ref, ybc_ref, ya_ref, yb_ref, yc_ref, ga_ref, gb_ref, gc_ref, wa_ref, wb_ref, wc_ref, o_ref,
         sa, sb, sc) = refs
        is_ctx = pl.program_id(1) < n_ctx_tiles
        ya = jnp.where(is_ctx, yac_ref[...], ya_ref[...])
        yb = jnp.where(is_ctx, ybc_ref[...], yb_ref[...])

    @pl.when(pl.program_id(1) == 0)
    def _():
        for w_ref, s in ((wa_ref, sa), (wb_ref, sb), (wc_ref, sc)):
            s[...] = w_ref[...].astype(BF16)

    acc = ga_ref[...].astype(F32) * _dot(ya, sa[...])
    acc = acc + gb_ref[...].astype(F32) * _dot(yb, sb[...])
    acc = acc + gc_ref[...].astype(F32) * _dot(yc_ref[...], sc[...])
    o_ref[...] = acc.astype(o_ref.dtype)


def merge_branches(dm, ys, y_row0s, gates, w_branch, layer, *, row0, n_rows, ys_ctx=None):
    BW = ys[0].shape[1]
    D = w_branch.shape[-1]
    tm = min(dm.tm, 512)
    tn = 1024
    t0 = row0 // tm
    nj = D // tn
    n_ctx_tiles = None if ys_ctx is None else dm.RC // tm

    def y_index(y0):
        off = (row0 - y0) // tm
        return lambda j, i: (jnp.maximum(i + off, 0), 0)

    y_specs = [pl.BlockSpec((tm, BW), y_index(y0)) for y0 in y_row0s]
    ctx_arrays = ()
    if ys_ctx is not None:
        ctx_spec = pl.BlockSpec((tm, BW), lambda j, i: (jnp.minimum(i, n_ctx_tiles - 1), 0))
        y_specs = [ctx_spec, ctx_spec] + y_specs
        ctx_arrays = tuple(ys_ctx)
    g_specs = [pl.BlockSpec((tm, tn), functools.partial(lambda j, i, n: (i + t0, n * nj + j), n=n))
               for n in range(N_BRANCH)]
    w_specs = [pl.BlockSpec((None, None, BW, tn), functools.partial(lambda j, i, n: (layer, n, 0, j), n=n))
               for n in range(N_BRANCH)]
    return pl.pallas_call(
        functools.partial(_merge_kernel, n_ctx_tiles=n_ctx_tiles),
        grid=(nj, n_rows // tm),
        in_specs=y_specs + g_specs + w_specs,
        out_specs=pl.BlockSpec((tm, tn), lambda j, i: (i, j)),
        out_shape=jax.ShapeDtypeStruct((n_rows, D), BF16),
        scratch_shapes=[pltpu.VMEM((BW, tn), BF16)] * 3,
        compiler_params=_params(2),
        name="merge_branches",
    )(*ctx_arrays, *ys, gates, gates, gates, w_branch, w_branch, w_branch)


def _router_kernel(h_ref, whi_ref, wlo_ref, b_ref, idx_ref, w_ref):
    half = whi_ref.shape[0] // 2
    lo, hi = _unpack_bf16_pairs(_load_row_tiles(h_ref, half // LANE))
    logits = b_ref[...]
    for w in (whi_ref, wlo_ref):
        logits = logits + _dot(lo, w[:half, :]) + _dot(hi, w[half:, :])
    lane = lax.broadcasted_iota(jnp.int32, logits.shape, 1).astype(F32)
    logits = jnp.where(lane < N_EXPERTS, logits, NEG_BIG)
    m1 = jnp.max(logits, axis=-1, keepdims=True)
    i1 = jnp.min(jnp.where(logits == m1, lane, float(LANE)), axis=-1, keepdims=True)
    rest = jnp.where(lane == i1, NEG_BIG, logits)
    m2 = jnp.max(rest, axis=-1, keepdims=True)
    i2 = jnp.min(jnp.where(rest == m2, lane, float(LANE)), axis=-1, keepdims=True)
    e = jnp.exp(m2 - m1)
    w1 = 1.0 / (1.0 + e)
    w2 = e / (1.0 + e)
    idx_ref[...] = jnp.where(lane == 0.0, i1, jnp.where(lane == 1.0, i2, 0.0)).astype(jnp.int32)
    w_ref[...] = jnp.where(lane == 0.0, w1, jnp.where(lane == 1.0, w2, 0.0))


def moe_router(hp, w_router, b_router):
    D = w_router.shape[0]
    n = D // 2 // LANE
    M = hp.shape[0] // n
    tm = _pick_tile((1024, 512, 256, 128), M)
    w_pad = jnp.zeros((D, LANE), F32).at[:, :N_EXPERTS].set(w_router.astype(F32))
    w_hi = w_pad.astype(BF16)
    w_lo = (w_pad - w_hi.astype(F32)).astype(BF16)
    b_pad = jnp.zeros((1, LANE), F32).at[0, :N_EXPERTS].set(b_router.astype(F32))
    return pl.pallas_call(
        _router_kernel,
        grid=(M // tm,),
        in_specs=[pl.BlockSpec((tm * n, LANE), lambda i: (i, 0)),
                  pl.BlockSpec((D, LANE), lambda i: (0, 0)),
                  pl.BlockSpec((D, LANE), lambda i: (0, 0)),
                  pl.BlockSpec((1, LANE), lambda i: (0, 0))],
        out_specs=[pl.BlockSpec((tm, LANE), lambda i: (i, 0)), pl.BlockSpec((tm, LANE), lambda i: (i, 0))],
        out_shape=[jax.ShapeDtypeStruct((M, LANE), jnp.int32), jax.ShapeDtypeStruct((M, LANE), F32)],
        compiler_params=_params(1),
        name="moe_router",
    )(hp, w_hi, w_lo, b_pad)


GATHER_UNROLL = 8


def _gather_rows_kernel(idx_ref, src_ref, o_ref, buf, sem, *, n):
    tg = o_ref.shape[0]
    base = pl.program_id(0) * tg

    def start(it, carry):
        for u in range(GATHER_UNROLL):
            r = it * GATHER_UNROLL + u
            src_row = pl.multiple_of(idx_ref[base + r] * n, n)
            dst_row = pl.multiple_of(r * n, n)
            pltpu.make_async_copy(src_ref.at[pl.ds(src_row, n)], buf.at[pl.ds(dst_row, n)], sem).start(
                priority=u % 2)
        return carry

    lax.fori_loop(0, tg // GATHER_UNROLL, start, 0)
    pltpu.make_async_copy(src_ref.at[pl.ds(0, tg * n)], buf, sem).wait()
    lo, hi = _unpack_bf16_pairs(_load_row_tiles(buf, n))
    half = lo.shape[1]
    o_ref[:, :half] = lo
    o_ref[:, half:] = hi


def gather_rows(src, idx, n, *, tg=256):
    M = idx.shape[0]
    return pl.pallas_call(
        functools.partial(_gather_rows_kernel, n=n),
        grid_spec=pltpu.PrefetchScalarGridSpec(
            num_scalar_prefetch=1,
            grid=(M // tg,),
            in_specs=[pl.BlockSpec(memory_space=pl.ANY)],
            out_specs=pl.BlockSpec((tg, 2 * n * LANE), lambda i, idx_ref: (i, 0)),
            scratch_shapes=[pltpu.VMEM((tg * n, LANE), src.dtype), pltpu.SemaphoreType.DMA(())],
        ),
        out_shape=jax.ShapeDtypeStruct((M, 2 * n * LANE), BF16),
        compiler_params=_params(1),
        name="gather_rows",
    )(idx, src)


def _moe_w13_kernel(te_ref, tv_ref, a_ref, wg_ref, wu_ref, o_ref, sg, su):
    i = pl.program_id(1)
    prev = te_ref[jnp.maximum(i - 1, 0)]

    @pl.when(jnp.logical_or(i == 0, te_ref[i] != prev))
    def _():
        sg[...] = wg_ref[...].astype(BF16)
        su[...] = wu_ref[...].astype(BF16)

    @pl.when(tv_ref[i] == 1)
    def _():
        a = a_ref[...]
        o_ref[...] = (_silu(_dot(a, sg[...])) * _dot(a, su[...])).astype(o_ref.dtype)

    @pl.when(tv_ref[i] == 0)
    def _():
        o_ref[...] = jnp.zeros_like(o_ref)


def moe_w13(xs, w13, moe_idx, tile_expert, tile_valid, *, tn):
    P, D = xs.shape
    F = w13.shape[-1] // 2
    tm = MOE_TM
    nj = F // tn
    return pl.pallas_call(
        _moe_w13_kernel,
        grid_spec=pltpu.PrefetchScalarGridSpec(
            num_scalar_prefetch=2,
            grid=(nj, P // tm),
            in_specs=[pl.BlockSpec((tm, D), lambda j, i, te, tv: (i, 0)),
                      pl.BlockSpec((None, None, D, tn), lambda j, i, te, tv: (moe_idx, te[i], 0, j)),
                      pl.BlockSpec((None, None, D, tn), lambda j, i, te, tv: (moe_idx, te[i], 0, j + nj))],
            out_specs=pl.BlockSpec((tm, tn), lambda j, i, te, tv: (i, j)),
            scratch_shapes=[pltpu.VMEM((D, tn), BF16), pltpu.VMEM((D, tn), BF16)],
        ),
        out_shape=jax.ShapeDtypeStruct((P, F), BF16),
        compiler_params=_params(2),
        name="moe_w13",
    )(tile_expert, tile_valid, xs, w13, w13)


def _moe_w2_kernel(te_ref, tv_ref, ts_ref, a_ref, w_ref, o_ref):
    i = pl.program_id(1)

    @pl.when(tv_ref[i] == 1)
    def _():
        _store_row_tiles(o_ref, _dot(a_ref[...], w_ref[...]))

    @pl.when(tv_ref[i] == 0)
    def _():
        o_ref[...] = jnp.zeros_like(o_ref)


def moe_w2(act, w2, tile_expert, tile_valid, tile_src, *, tn):
    P, F = act.shape
    D = w2.shape[-1]
    tm = MOE_TM
    n = tn // LANE
    return pl.pallas_call(
        _moe_w2_kernel,
        grid_spec=pltpu.PrefetchScalarGridSpec(
            num_scalar_prefetch=3,
            grid=(D // tn, P // tm),
            in_specs=[pl.BlockSpec((tm, F), lambda j, i, te, tv, ts: (ts[i], 0)),
                      pl.BlockSpec((None, F, tn), lambda j, i, te, tv, ts: (te[i], 0, j))],
            out_specs=pl.BlockSpec((None, tm * n, LANE), lambda j, i, te, tv, ts: (j, i, 0)),
        ),
        out_shape=jax.ShapeDtypeStruct((D // tn, P * n, LANE), F32),
        compiler_params=_params(2),
        name="moe_w2",
    )(tile_expert, tile_valid, tile_src, act, w2)


def _moe_combine_kernel(p0_ref, p1_ref, y_ref, x_ref, gate_ref, w_ref, o_ref, b0, b1, sem):
    tc = o_ref.shape[0]
    n = b0.shape[1] // tc
    base = pl.program_id(0) * tc

    def start(r, carry):
        dst = pl.ds(pl.multiple_of(r * n, n), n)
        src0 = pl.ds(pl.multiple_of(p0_ref[base + r] * n, n), n)
        src1 = pl.ds(pl.multiple_of(p1_ref[base + r] * n, n), n)
        pltpu.make_async_copy(y_ref.at[:, src0], b0.at[:, dst], sem.at[0]).start(priority=0)
        pltpu.make_async_copy(y_ref.at[:, src1], b1.at[:, dst], sem.at[1]).start(priority=1)
        return carry

    lax.fori_loop(0, tc, start, 0, unroll=GATHER_UNROLL)
    pltpu.make_async_copy(y_ref.at[:, pl.ds(0, tc * n)], b0, sem.at[0]).wait()
    pltpu.make_async_copy(y_ref.at[:, pl.ds(0, tc * n)], b1, sem.at[1]).wait()
    w = w_ref[...]
    y = w[:, 0:1] * _load_row_tiles(b0, n) + w[:, 1:2] * _load_row_tiles(b1, n)
    o_ref[...] = x_ref[...] + gate_ref[0] * y


def moe_combine(dm, y_sorted, pos0, pos1, top_w, x, gate, *, mod_row0):
    M, D = x.shape
    tc = 128
    t0 = mod_row0 // tc
    J = y_sorted.shape[0]
    n = D // (J * LANE)
    return pl.pallas_call(
        _moe_combine_kernel,
        grid_spec=pltpu.PrefetchScalarGridSpec(
            num_scalar_prefetch=2,
            grid=(M // tc,),
            in_specs=[pl.BlockSpec(memory_space=pl.ANY),
                      pl.BlockSpec((tc, D), lambda i, p0, p1: (i, 0)),
                      pl.BlockSpec((1, 1, D), lambda i, p0, p1: (dm.mod_row(i + t0, tc), 0, 0)),
                      pl.BlockSpec((tc, LANE), lambda i, p0, p1: (i, 0))],
            out_specs=pl.BlockSpec((tc, D), lambda i, p0, p1: (i, 0)),
            scratch_shapes=[pltpu.VMEM((J, tc * n, LANE), F32), pltpu.VMEM((J, tc * n, LANE), F32),
                            pltpu.SemaphoreType.DMA((2,))],
        ),
        out_shape=jax.ShapeDtypeStruct((M, D), F32),
        compiler_params=_params(1),
        name="moe_combine",
    )(pos0, pos1, y_sorted, x, gate, top_w)


def moe_ffn(dm, hp, x, gate, w_router, b_router, w13, w2, moe_idx, *, mod_row0):
    M, D = x.shape
    E = N_EXPERTS
    tm = MOE_TM
    top_idx, top_w = moe_router(hp, w_router, b_router)
    e_flat = top_idx[:, :TOP_K].T.reshape(-1)
    onehot = (e_flat[:, None] == jnp.arange(E, dtype=jnp.int32)[None, :]).astype(F32)
    blk = LANE
    nb = onehot.shape[0] // blk
    oh3 = onehot.reshape(nb, blk, E)
    exact = lax.Precision.HIGHEST
    within = jnp.einsum("ij,bjk->bik", jnp.tril(jnp.ones((blk, blk), F32), -1), oh3, precision=exact)
    before = jnp.einsum("ab,bk->ak", jnp.tril(jnp.ones((nb, nb), F32), -1), jnp.sum(oh3, axis=1), precision=exact)
    rank = jnp.sum((within + before[:, None, :]) * oh3, axis=-1).reshape(-1).astype(jnp.int32)
    counts = jnp.sum(onehot, axis=0).astype(jnp.int32)
    padded = ((counts + tm - 1) // tm) * tm
    ends = jnp.cumsum(padded)
    starts = ends - padded
    pos = starts[e_flat] + rank
    P = TOP_K * M + E * tm
    n_tiles = P // tm
    tok = jnp.tile(jnp.arange(M, dtype=jnp.int32), TOP_K)
    gidx = jnp.zeros((P,), jnp.int32).at[pos].set(tok)
    tile_start = jnp.arange(n_tiles, dtype=jnp.int32) * tm
    tile_valid = (tile_start < ends[-1]).astype(jnp.int32)
    te = jnp.sum((tile_start[:, None] >= ends[None, :]).astype(jnp.int32), axis=1)
    last_e = jnp.sum((ends[-1] - 1 >= ends).astype(jnp.int32))
    tile_expert = jnp.minimum(te, last_e).astype(jnp.int32)

    xs = gather_rows(hp, gidx, D // 2 // LANE)
    F = w13.shape[-1] // 2
    act = moe_w13(xs, w13, moe_idx, tile_expert, tile_valid, tn=_pick_tile((1024, 512, 256, 128), F))
    tile_src = jnp.minimum(jnp.arange(n_tiles, dtype=jnp.int32), jnp.sum(tile_valid) - 1).astype(jnp.int32)
    y_sorted = moe_w2(act, w2[moe_idx].astype(BF16), tile_expert, tile_valid, tile_src, tn=min(1024, D))
    return moe_combine(dm, y_sorted, pos[:M], pos[M:], top_w, x, gate, mod_row0=mod_row0)


def _axial_angles(seq, rot_dim):
    rows = seq // GRID_W
    t_row = jnp.repeat(jnp.arange(rows, dtype=F32), GRID_W)
    t_col = jnp.tile(jnp.arange(GRID_W, dtype=F32), rows)
    quarter = rot_dim // 4
    inv_freq = ROPE_THETA ** (-jnp.arange(quarter, dtype=F32) / quarter)
    return jnp.concatenate([t_row[:, None] * inv_freq, t_col[:, None] * inv_freq], axis=-1)


def _rope_tables(dm):
    ident = min(dm.tm, 256)
    ang = _axial_angles(dm.S, SWA_HEAD_DIM)
    cos_a = jnp.concatenate([jnp.cos(ang), jnp.cos(ang)], axis=-1)
    sin_a = jnp.concatenate([-jnp.sin(ang), jnp.sin(ang)], axis=-1)
    cos_a = jnp.concatenate([jnp.ones((ident, LANE), F32), cos_a], axis=0)
    sin_a = jnp.concatenate([jnp.zeros((ident, LANE), F32), sin_a], axis=0)
    ang = _axial_angles(dm.S, MLA_ROPE)
    half = MLA_ROPE // 2
    zeros = jnp.zeros((dm.S, half), F32)
    pad = jnp.zeros((dm.S, LANE - MLA_ROPE), F32)
    c_b = jnp.concatenate([jnp.cos(ang), jnp.cos(ang), pad], axis=-1)
    s1_b = jnp.concatenate([-jnp.sin(ang), zeros, pad], axis=-1)
    s2_b = jnp.concatenate([zeros, jnp.sin(ang), pad], axis=-1)
    c_b = jnp.concatenate([jnp.ones((ident, LANE), F32), c_b], axis=0)
    s1_b = jnp.concatenate([jnp.zeros((ident, LANE), F32), s1_b], axis=0)
    s2_b = jnp.concatenate([jnp.zeros((ident, LANE), F32), s2_b], axis=0)
    return (cos_a, sin_a), (c_b, s1_b, s2_b)


def _pad_head_vec(g):
    return jnp.zeros((1, MLA_QK_PAD), F32).at[0, :MLA_QK].set(g.astype(F32))


def _trunk(x, c, ctx, c_ctx, mod_w, mod_b, norm_mix_g, norm_ffn_g, w_in,
           swa_q_norm_g, swa_k_norm_g, swa_sink,
           mla_q_a_norm_g, mla_w_uq, mla_kv_a_norm_g, mla_w_ukv, mla_q_norm_g, mla_k_norm_g,
           ssm_lam_re, ssm_lam_im, ssm_log_step, ssm_b_re, ssm_b_im, ssm_c_re, ssm_c_im,
           ssm_d, ssm_w_glu, ssm_b_glu, w_branch, w_out,
           ffn_w13, ffn_w2, moe_w_router, moe_b_router, moe_w13, moe_w2):
    B, S, D = x.shape
    L = ctx.shape[1]
    depth = mod_w.shape[0]
    dm = Dims(B, S, L)
    tm = dm.tm
    RC, RL, R = dm.RC, dm.RL, dm.R
    q_w = SWA_HEADS * SWA_HEAD_DIM
    kv_w = SWA_KV_HEADS * SWA_HEAD_DIM
    q_rank = mla_w_uq.shape[1]
    kv_rank = mla_w_ukv.shape[1]
    ssm_w = ssm_d.shape[1]
    n_gate = N_BRANCH * D
    src = {}
    off = 0
    for name, width in (("q", q_w), ("k", kv_w), ("v", kv_w), ("c_q", q_rank), ("c_kv", kv_rank),
                        ("kpe", MLA_ROPE), ("u", ssm_w), ("gates", n_gate)):
        src[name] = (off, width)
        off += width
    order = ("q", "u", "c_q", "k", "v", "c_kv", "kpe")
    col = {}
    off = 0
    for name in order:
        col[name] = off
        off += src[name][1]
    z_tn = 1792
    z_cols = -(-off // z_tn) * z_tn

    (cos_a, sin_a), tabs_b = _rope_tables(dm)
    ssm_compact = jax.vmap(_ssm_compact_tables)(ssm_lam_re, ssm_lam_im, ssm_log_step, ssm_b_re, ssm_b_im,
                                                ssm_c_re, ssm_c_im)
    x_ctx0 = ctx.reshape(RC, D).astype(F32)
    xall = x.reshape(RL, D).astype(F32)
    cond = jnp.zeros((8, D), F32).at[0].set(c_ctx.astype(F32)).at[1:1 + B].set(c.astype(F32))

    for layer in range(depth):
        with_ctx = layer < depth - 1
        row0 = 0 if with_ctx else RC
        n_rows = R - row0
        mods = mm1(cond, [(mod_w, (layer,), 0)], _epi_bias, n_rows=8, n_cols=6 * D, tm=8, tn=512, out_dtype=F32,
                   extras=[(mod_b.reshape(depth, 1, 6 * D), (None, 1, 512), lambda j, i: (layer, 0, j))],
                   prologue=lambda a: _silu(a).astype(BF16), name="ada_mod")
        sh_m, sc_m, g_m, sh_f, sc_f, g_f = [mods[:, i * D:(i + 1) * D].reshape(8, 1, D) for i in range(6)]

        split_input = layer == 0
        h = modulate(dm, xall, norm_mix_g[layer], sh_m, sc_m, mod_row0=0, x_ctx=x_ctx0 if split_input else None)
        w_l = w_in[layer]
        w_rest = jnp.concatenate([w_l[:, src[n][0]:src[n][0] + src[n][1]] for n in order]
                                 + [jnp.zeros((D, z_cols - off), w_l.dtype)], axis=1)
        w_gates = w_l[:, src["gates"][0]:]
        z = mm1(h, [(w_rest, (), 0)], _epi_id, n_rows=R, n_cols=z_cols, tm=min(tm, 512), tn=z_tn, out_dtype=F32,
                name="w_in")
        gates = mm1(h, [(w_gates, (), 0)], _epi_sigmoid, n_rows=R, n_cols=n_gate, tm=tm, tn=1024,
                    out_dtype=BF16, name="w_in_gates")

        qa, ka, va = swa_prep(dm, z, col, swa_q_norm_g[layer], swa_k_norm_g[layer], cos_a, sin_a)
        ya_l = swa_attention(dm, qa, ka, va, swa_sink[layer], latent=True)
        w_uq = mla_w_uq[layer].reshape(q_rank, MLA_HEADS, MLA_QK)
        w_uq = jnp.pad(w_uq, ((0, 0), (0, 0), (0, MLA_QK_PAD - MLA_QK))).reshape(q_rank, MLA_HEADS * MLA_QK_PAD)
        w_ukv = mla_w_ukv[layer].reshape(kv_rank, MLA_HEADS, MLA_NOPE + MLA_V)
        w_ukv = jnp.concatenate([w_ukv[:, :, :MLA_NOPE].reshape(kv_rank, -1),
                                 w_ukv[:, :, MLA_NOPE:].reshape(kv_rank, -1)], axis=1)
        mla_args = (dm, z, col, mla_q_a_norm_g[layer], mla_kv_a_norm_g[layer], w_uq, w_ukv,
                    _pad_head_vec(mla_q_norm_g[layer]), _pad_head_vec(mla_k_norm_g[layer]), tabs_b)
        qm_c, km_c, vm_c, vt_c = mla_prep(*mla_args, row0=0, n_rows=RC, rows_per_batch=L)
        qm_l, km_l, _, vt_l = mla_prep(*mla_args, row0=RC, n_rows=RL, rows_per_batch=S)
        yb_l = mla_attention_lat(dm, qm_l, km_c, km_l, vt_c, vt_l)
        if with_ctx:
            ys_ctx = (swa_attention(dm, qa, ka, va, swa_sink[layer], latent=False),
                      mla_attention_ctx(dm, qm_c, km_c, vm_c))
        else:
            ys_ctx = None
        yg = s5_branch(dm, z, col, ssm_compact, layer, ssm_d[layer])
        b_glu = ssm_b_glu.reshape(depth, 1, 2 * ssm_w)
        gl_tn = 512
        yc = mm1(yg, [(ssm_w_glu, (layer,), 0), (ssm_w_glu, (layer,), ssm_w)], _epi_glu_bias,
                 n_rows=R, n_cols=ssm_w, tm=tm, tn=gl_tn, out_dtype=BF16,
                 extras=[(b_glu, (None, 1, gl_tn), lambda j, i: (layer, 0, j)),
                         (b_glu, (None, 1, gl_tn), lambda j, i: (layer, 0, j + ssm_w // gl_tn))],
                 prologue=lambda a: a.astype(BF16), name="ssm_glu")
        mixed = merge_branches(dm, (ya_l, yb_l, yc), (RC, RC, 0), gates, w_branch, layer, row0=row0, n_rows=n_rows,
                               ys_ctx=ys_ctx)
        t0 = row0 // tm
        gate_extra = (g_m, (1, 1, 1024), lambda j, i: (dm.mod_row(i + t0, tm), 0, j))
        if split_input and with_ctx:
            nct = RC // tm
            res_epi = functools.partial(_epi_residual_two_sources, n_ctx_tiles=nct)
            res_extras = [(x_ctx0, (tm, 1024), lambda j, i: (jnp.minimum(i, nct - 1), j)),
                          (xall, (tm, 1024), lambda j, i: (jnp.maximum(i - nct, 0), j)), gate_extra]
        elif split_input:
            res_epi = _epi_residual
            res_extras = [(xall, (tm, 1024), lambda j, i: (i, j)), gate_extra]
        else:
            res_epi = _epi_residual
            res_extras = [(xall, (tm, 1024), lambda j, i: (i + t0, j)), gate_extra]
        x1 = mm1(mixed, [(w_out, (layer,), 0)], res_epi, n_rows=n_rows, n_cols=D, tm=tm, tn=1024, out_dtype=F32,
                 extras=res_extras, name="w_out")
        is_moe = layer % 2 == 1
        h2 = modulate(dm, x1, norm_ffn_g[layer], sh_f, sc_f, mod_row0=row0, pack=is_moe)
        if not is_moe:
            F = ffn_w13.shape[-1] // 2
            f_tn = _pick_tile((512, 256, 128), F)
            act = mm1(h2, [(ffn_w13, (layer // 2,), 0), (ffn_w13, (layer // 2,), F)], _epi_swiglu, n_rows=n_rows,
                      n_cols=F, tm=tm, tn=f_tn, out_dtype=BF16, name="ffn_w13")
            x2 = mm2_residual(dm, act, ffn_w2[layer // 2].astype(BF16), x1, g_f, mod_row0=row0, tm=tm, tk=f_tn)
        else:
            if with_ctx:
                raise NotImplementedError("a mixture-of-experts layer that still feeds context rows")
            x2 = moe_ffn(dm, h2, x1, g_f, moe_w_router[layer // 2], moe_b_router[layer // 2], moe_w13, moe_w2,
                         layer // 2, mod_row0=row0)
        xall = x2
    return xall.reshape(B, S, D)


def kernel(x, c, ctx, c_ctx, mod_w, mod_b, norm_mix_g, norm_ffn_g, w_in, swa_q_norm_g, swa_k_norm_g, swa_sink, mla_q_a_norm_g, mla_w_uq, mla_kv_a_norm_g, mla_w_ukv, mla_q_norm_g, mla_k_norm_g, ssm_lam_re, ssm_lam_im, ssm_log_step, ssm_b_re, ssm_b_im, ssm_c_re, ssm_c_im, ssm_d, ssm_w_glu, ssm_b_glu, w_branch, w_out, ffn_w13, ffn_w2, moe_w_router, moe_b_router, moe_w13, moe_w2):
    return _trunk(x, c, ctx, c_ctx, mod_w, mod_b, norm_mix_g, norm_ffn_g, w_in, swa_q_norm_g, swa_k_norm_g, swa_sink,
                  mla_q_a_norm_g, mla_w_uq, mla_kv_a_norm_g, mla_w_ukv, mla_q_norm_g, mla_k_norm_g,
                  ssm_lam_re, ssm_lam_im, ssm_log_step, ssm_b_re, ssm_b_im, ssm_c_re, ssm_c_im,
                  ssm_d, ssm_w_glu, ssm_b_glu, w_branch, w_out, ffn_w13, ffn_w2, moe_w_router, moe_b_router,
                  moe_w13, moe_w2)
```

```python
import functools
import math

import jax
import jax.numpy as jnp
from jax import lax
from jax.experimental import pallas as pl
from jax.experimental.pallas import tpu as pltpu

F32 = jnp.float32
BF16 = jnp.bfloat16

GRID_W = 64
ROPE_THETA = 10000.0
EPS = 1e-6
SWA_HEADS = 8
SWA_KV_HEADS = 2
SWA_HEAD_DIM = 128
SWA_WINDOW = 128
MLA_HEADS = 8
MLA_NOPE = 128
MLA_ROPE = 64
MLA_V = 128
MLA_QK = MLA_NOPE + MLA_ROPE
MLA_QK_PAD = 256
SSM_GROUP = 16
SSM_STATE = 64
SSM_CHUNK = 16
N_BRANCH = 3
N_EXPERTS = 8
TOP_K = 2
LANE = 128
VMEM_LIMIT_BYTES = 56 * 1024 * 1024
MOE_TM = 512
NEG_BIG = -1e30


def _params(n_grid):
    return pltpu.CompilerParams(dimension_semantics=("arbitrary",) * n_grid, vmem_limit_bytes=VMEM_LIMIT_BYTES)


def _pick_tile(candidates, *sizes):
    for t in candidates:
        if all(s % t == 0 for s in sizes):
            return t
    raise ValueError(f"no tile in {candidates} divides {sizes}")


class Dims:
    def __init__(self, batch, seq, ctx_len):
        self.B, self.S, self.L = batch, seq, ctx_len
        self.RC = batch * ctx_len
        self.RL = batch * seq
        self.R = self.RC + self.RL
        self.tm = _pick_tile((1024, 512, 256, 128), ctx_len * batch, seq)

    def mod_row(self, tile, tm):
        nct = self.RC // tm
        return jnp.where(tile < nct, 0, 1 + (tile - nct) // (self.S // tm))


def _silu(x):
    return x * (1.0 / (1.0 + jnp.exp(-x)))


def _sigmoid(x):
    return 1.0 / (1.0 + jnp.exp(-x))


def _gelu_tanh(x):
    c = math.sqrt(2.0 / math.pi)
    return 0.5 * x * (1.0 + jnp.tanh(c * (x + 0.044715 * (x * x * x))))


def _dot(a, b):
    return jnp.dot(a, b, preferred_element_type=F32)


def _dot_nt(a, b):
    return lax.dot_general(a, b, (((1,), (1,)), ((), ())), preferred_element_type=F32)


def _pack_bf16_pairs(y):
    half = y.shape[1] // 2
    bits = lax.bitcast_convert_type(y.astype(BF16).astype(F32), jnp.uint32)
    return (bits[:, :half] >> 16) | (bits[:, half:] & jnp.uint32(0xFFFF0000))


def _unpack_bf16_pairs(p):
    lo = lax.bitcast_convert_type(p << 16, F32).astype(BF16)
    hi = lax.bitcast_convert_type(p & jnp.uint32(0xFFFF0000), F32).astype(BF16)
    return lo, hi


def _store_row_tiles(ref, val):
    m, w = val.shape
    n = w // LANE
    for c in range(n):
        ref[pl.ds(c, m, stride=n), :] = val[:, c * LANE:(c + 1) * LANE]


def _load_row_tiles(ref, n):
    if len(ref.shape) == 2:
        m = ref.shape[0] // n
        return jnp.concatenate([ref[pl.ds(c, m, stride=n), :] for c in range(n)], axis=1)
    m = ref.shape[1] // n
    return jnp.concatenate([ref[j, pl.ds(c, m, stride=n), :] for j in range(ref.shape[0]) for c in range(n)], axis=1)


def _modulate_kernel(*refs, pack, n_ctx_tiles):
    if n_ctx_tiles is None:
        x_ref, g_ref, sh_ref, sc_ref, o_ref = refs
        x = x_ref[...]
    else:
        xc_ref, xl_ref, g_ref, sh_ref, sc_ref, o_ref = refs
        x = jnp.where(pl.program_id(0) < n_ctx_tiles, xc_ref[...], xl_ref[...])
    ms = jnp.mean(x * x, axis=-1, keepdims=True)
    y = x * lax.rsqrt(ms + EPS) * g_ref[...]
    y = y * (1.0 + sc_ref[0]) + sh_ref[0]
    if pack:
        _store_row_tiles(o_ref, _pack_bf16_pairs(y))
    else:
        o_ref[...] = y.astype(o_ref.dtype)


def modulate(dm, x, g, shift, scale, *, mod_row0, pack=False, x_ctx=None):
    D = x.shape[1]
    tm = min(dm.tm, 512)
    t0 = mod_row0 // tm
    if x_ctx is None:
        n_rows = x.shape[0]
        n_ctx_tiles = None
        x_specs = [pl.BlockSpec((tm, D), lambda i: (i, 0))]
        xs = (x,)
    else:
        n_rows = x_ctx.shape[0] + x.shape[0]
        n_ctx_tiles = x_ctx.shape[0] // tm
        x_specs = [pl.BlockSpec((tm, D), lambda i: (jnp.minimum(i, n_ctx_tiles - 1), 0)),
                   pl.BlockSpec((tm, D), lambda i: (jnp.maximum(i - n_ctx_tiles, 0), 0))]
        xs = (x_ctx, x)
    if pack:
        n = D // 2 // LANE
        out_spec = pl.BlockSpec((tm * n, LANE), lambda i: (i, 0))
        out_shape = jax.ShapeDtypeStruct((n_rows * n, LANE), jnp.uint32)
    else:
        out_spec = pl.BlockSpec((tm, D), lambda i: (i, 0))
        out_shape = jax.ShapeDtypeStruct((n_rows, D), BF16)
    return pl.pallas_call(
        functools.partial(_modulate_kernel, pack=pack, n_ctx_tiles=n_ctx_tiles),
        grid=(n_rows // tm,),
        in_specs=x_specs + [
            pl.BlockSpec((1, D), lambda i: (0, 0)),
            pl.BlockSpec((1, 1, D), lambda i: (dm.mod_row(i + t0, tm), 0, 0)),
            pl.BlockSpec((1, 1, D), lambda i: (dm.mod_row(i + t0, tm), 0, 0)),
        ],
        out_specs=out_spec,
        out_shape=out_shape,
        compiler_params=_params(1),
        name="modulate_packed" if pack else "modulate",
    )(*xs, g.reshape(1, D), shift, scale)


def _mm1_kernel(*refs, n_w, n_extra, epilogue, prologue):
    a_ref = refs[0]
    w_refs = refs[1:1 + n_w]
    extra = refs[1 + n_w:1 + n_w + n_extra]
    o_ref = refs[1 + n_w + n_extra]
    wb = refs[2 + n_w + n_extra:]

    @pl.when(pl.program_id(1) == 0)
    def _():
        for w_ref, b in zip(w_refs, wb):
            b[...] = w_ref[...].astype(BF16)

    a = a_ref[...]
    if prologue is not None:
        a = prologue(a)
    accs = [_dot(a, b[...]) for b in wb]
    o_ref[...] = epilogue(accs, *extra).astype(o_ref.dtype)


def mm1(a, weights, epilogue, *, n_rows, n_cols, tm, tn, out_dtype, a_row0=0, extras=(), prologue=None, name):
    K = a.shape[1]
    t0 = a_row0 // tm
    in_specs = [pl.BlockSpec((tm, K), lambda j, i: (i + t0, 0))]
    operands = [a]
    for w, lead, col0 in weights:
        c0 = col0 // tn
        in_specs.append(pl.BlockSpec((None,) * len(lead) + (K, tn),
                                     functools.partial(lambda j, i, lead, c0: lead + (0, j + c0), lead=lead, c0=c0)))
        operands.append(w)
    for arr, bshape, imap in extras:
        in_specs.append(pl.BlockSpec(bshape, imap))
        operands.append(arr)
    kern = functools.partial(_mm1_kernel, n_w=len(weights), n_extra=len(extras), epilogue=epilogue,
                             prologue=prologue)
    return pl.pallas_call(
        kern,
        grid=(n_cols // tn, n_rows // tm),
        in_specs=in_specs,
        out_specs=pl.BlockSpec((tm, tn), lambda j, i: (i, j)),
        out_shape=jax.ShapeDtypeStruct((n_rows, n_cols), out_dtype),
        scratch_shapes=[pltpu.VMEM((K, tn), BF16) for _ in weights],
        compiler_params=_params(2),
        name=name,
    )(*operands)


def _epi_id(accs):
    return accs[0]


def _epi_sigmoid(accs):
    return _sigmoid(accs[0])


def _epi_swiglu(accs):
    return _silu(accs[0]) * accs[1]


def _epi_bias(accs, b_ref):
    return accs[0] + b_ref[...]


def _epi_glu_bias(accs, ba_ref, bb_ref):
    return (accs[0] + ba_ref[...]) * _sigmoid(accs[1] + bb_ref[...])


def _epi_residual(accs, x_ref, gate_ref):
    return x_ref[...] + gate_ref[0] * accs[0]


def _epi_residual_two_sources(accs, xc_ref, xl_ref, gate_ref, *, n_ctx_tiles):
    x = jnp.where(pl.program_id(1) < n_ctx_tiles, xc_ref[...], xl_ref[...])
    return x + gate_ref[0] * accs[0]


def _mm2_kernel(a_ref, w_ref, x_ref, gate_ref, o_ref, acc_ref):
    k = pl.program_id(1)

    @pl.when(k == 0)
    def _():
        acc_ref[...] = jnp.zeros_like(acc_ref)

    acc_ref[...] += _dot(a_ref[...], w_ref[...])

    @pl.when(k == pl.num_programs(1) - 1)
    def _():
        o_ref[...] = x_ref[...] + gate_ref[0] * acc_ref[...]


def mm2_residual(dm, a, w, x, gate, *, mod_row0, tm, tk):
    M, K = a.shape
    N = w.shape[1]
    t0 = mod_row0 // tm
    return pl.pallas_call(
        _mm2_kernel,
        grid=(M // tm, K // tk),
        in_specs=[
            pl.BlockSpec((tm, tk), lambda i, k: (i, k)),
            pl.BlockSpec((tk, N), lambda i, k: (k, 0)),
            pl.BlockSpec((tm, N), lambda i, k: (i, 0)),
            pl.BlockSpec((1, 1, N), lambda i, k: (dm.mod_row(i + t0, tm), 0, 0)),
        ],
        out_specs=pl.BlockSpec((tm, N), lambda i, k: (i, 0)),
        out_shape=jax.ShapeDtypeStruct((M, N), F32),
        scratch_shapes=[pltpu.VMEM((tm, N), F32)],
        compiler_params=_params(2),
        name="mm2_residual",
    )(a, w, x, gate)


def _swa_prep_kernel(q_ref, k_ref, v_ref, gq_ref, gk_ref, cos_ref, sin_ref, qo_ref, ko_ref, vo_ref):
    c = cos_ref[...]
    s = sin_ref[...]

    def norm_rope(x, g, scale):
        ms = jnp.mean(x * x, axis=-1, keepdims=True)
        y = x * lax.rsqrt(ms + EPS) * g
        return (y * c + pltpu.roll(y, SWA_HEAD_DIM // 2, 1) * s) * scale

    gq = gq_ref[...]
    gk = gk_ref[...]
    for h in range(SWA_HEADS):
        sl = slice(h * SWA_HEAD_DIM, (h + 1) * SWA_HEAD_DIM)
        qo_ref[:, sl] = norm_rope(q_ref[:, sl], gq, SWA_HEAD_DIM ** -0.5).astype(BF16)
    for h in range(SWA_KV_HEADS):
        sl = slice(h * SWA_HEAD_DIM, (h + 1) * SWA_HEAD_DIM)
        ko_ref[:, sl] = norm_rope(k_ref[:, sl], gk, 1.0).astype(BF16)
    vo_ref[...] = v_ref[...].astype(BF16)


def _rope_tile_index(dm, tm):
    nct = dm.RC // tm
    return lambda i: (jnp.where(i < nct, 0, 1 + (i - nct) % (dm.S // tm)), 0)


def swa_prep(dm, z, col, gq, gk, cos_t, sin_t):
    tm = min(dm.tm, 256)
    QW = SWA_HEADS * SWA_HEAD_DIM
    KW = SWA_KV_HEADS * SWA_HEAD_DIM
    ridx = _rope_tile_index(dm, tm)
    return pl.pallas_call(
        _swa_prep_kernel,
        grid=(dm.R // tm,),
        in_specs=[
            pl.BlockSpec((tm, QW), lambda i: (i, col["q"] // QW)),
            pl.BlockSpec((tm, KW), lambda i: (i, col["k"] // KW)),
            pl.BlockSpec((tm, KW), lambda i: (i, col["v"] // KW)),
            pl.BlockSpec((1, SWA_HEAD_DIM), lambda i: (0, 0)),
            pl.BlockSpec((1, SWA_HEAD_DIM), lambda i: (0, 0)),
            pl.BlockSpec((tm, SWA_HEAD_DIM), ridx),
            pl.BlockSpec((tm, SWA_HEAD_DIM), ridx),
        ],
        out_specs=[
            pl.BlockSpec((tm, QW), lambda i: (i, 0)),
            pl.BlockSpec((tm, KW), lambda i: (i, 0)),
            pl.BlockSpec((tm, KW), lambda i: (i, 0)),
        ],
        out_shape=[
            jax.ShapeDtypeStruct((dm.R, QW), BF16),
            jax.ShapeDtypeStruct((dm.R, KW), BF16),
            jax.ShapeDtypeStruct((dm.R, KW), BF16),
        ],
        compiler_params=_params(1),
        name="swa_prep",
    )(z, z, z, gq.reshape(1, -1), gk.reshape(1, -1), cos_t, sin_t)


def _swa_attn_kernel(*refs, windowed, nb):
    if windowed:
        q_ref, kc_ref, kp_ref, kk_ref, kn_ref, vc_ref, vp_ref, vk_ref, vn_ref, sink_ref, o_ref = refs
    else:
        q_ref, kc_ref, vc_ref, sink_ref, o_ref = refs
    G = SWA_HEADS // SWA_KV_HEADS
    blk = q_ref.shape[0]
    Dh = SWA_HEAD_DIM
    q = jnp.concatenate([q_ref[:, g * Dh:(g + 1) * Dh] for g in range(G)], axis=0)
    sink = sink_ref[0][:, 0:1]
    scores = [_dot_nt(q, kc_ref[...])]
    values = [vc_ref[...]]
    if windowed:
        n = pl.program_id(2)
        qi = lax.broadcasted_iota(jnp.int32, (G * blk, blk), 0) % blk
        kj = lax.broadcasted_iota(jnp.int32, (G * blk, blk), 1)
        s_p = _dot_nt(q, kp_ref[...])
        s_p = jnp.where(kj >= qi, s_p, NEG_BIG)
        s_p = jnp.where(n >= 1, s_p, NEG_BIG)
        s_n = _dot_nt(q, kn_ref[...])
        s_n = jnp.where(kj <= qi, s_n, NEG_BIG)
        s_n = jnp.where(n <= nb - 2, s_n, NEG_BIG)
        scores += [s_p, _dot_nt(q, kk_ref[...]), s_n]
        values += [vp_ref[...], vk_ref[...], vn_ref[...]]
    m = sink
    for s in scores:
        m = jnp.maximum(m, jnp.max(s, axis=-1, keepdims=True))
    l = jnp.exp(sink - m)
    o = None
    for s, v in zip(scores, values):
        p = jnp.exp(s - m)
        l = l + jnp.sum(p, axis=-1, keepdims=True)
        pv = _dot(p.astype(BF16), v)
        o = pv if o is None else o + pv
    o = o / l
    for g in range(G):
        o_ref[:, g * Dh:(g + 1) * Dh] = o[g * blk:(g + 1) * blk].astype(o_ref.dtype)


def swa_attention(dm, qa, ka, va, sink, *, latent):
    G = SWA_HEADS // SWA_KV_HEADS
    Dh = SWA_HEAD_DIM
    blk = SWA_WINDOW
    L = dm.L
    sink_col = jnp.broadcast_to(sink.astype(F32).reshape(SWA_KV_HEADS, G, 1, 1),
                                (SWA_KV_HEADS, G, blk, LANE)).reshape(SWA_KV_HEADS, G * blk, LANE)
    sink_spec = pl.BlockSpec((1, G * blk, LANE), lambda b, h, n: (h, 0, 0))
    ctx_spec = pl.BlockSpec((L, Dh), lambda b, h, n: (b, h))
    if latent:
        nb = dm.S // blk
        base = dm.RC // blk

        def q_map(b, h, n):
            return (base + b * nb + n, h)

        def kv_map(off):
            return lambda b, h, n: (base + b * nb + jnp.clip(n + off, 0, nb - 1), h)

        win_specs = [pl.BlockSpec((blk, Dh), kv_map(off)) for off in (-1, 0, 1)]
        in_specs = ([pl.BlockSpec((blk, G * Dh), q_map), ctx_spec] + win_specs + [ctx_spec] + win_specs
                    + [sink_spec])
        operands = (qa, ka, ka, ka, ka, va, va, va, va, sink_col)
        n_out = dm.RL
    else:
        nb = L // blk
        in_specs = [pl.BlockSpec((blk, G * Dh), lambda b, h, n: (b * nb + n, h)), ctx_spec, ctx_spec, sink_spec]
        operands = (qa, ka, va, sink_col)
        n_out = dm.RC
    return pl.pallas_call(
        functools.partial(_swa_attn_kernel, windowed=latent, nb=nb),
        grid=(dm.B, SWA_KV_HEADS, nb),
        in_specs=in_specs,
        out_specs=pl.BlockSpec((blk, G * Dh), lambda b, h, n: (b * nb + n, h)),
        out_shape=jax.ShapeDtypeStruct((n_out, SWA_HEADS * Dh), BF16),
        compiler_params=_params(3),
        name="swa_attn_lat" if latent else "swa_attn_ctx",
    )(*operands)


def _mla_prep_kernel(cq_ref, ckv_ref, pe_ref, gqa_ref, gkva_ref, wq_ref, wkv_ref, gq_ref, gk_ref, c_ref, s1_ref,
                     s2_ref, qo_ref, ko_ref, vo_ref, vt_ref, wq_s, wkv_s):
    @pl.when(pl.program_id(0) == 0)
    def _():
        wq_s[...] = wq_ref[...].astype(BF16)
        wkv_s[...] = wkv_ref[...].astype(BF16)

    def rms(x_ref, g_ref):
        x = x_ref[...]
        ms = jnp.mean(x * x, axis=-1, keepdims=True)
        return (x * lax.rsqrt(ms + EPS) * g_ref[...]).astype(BF16)

    qf = _dot(rms(cq_ref, gqa_ref), wq_s[...])
    kvf = _dot(rms(ckv_ref, gkva_ref), wkv_s[...])
    c = c_ref[...]
    s1 = s1_ref[...]
    s2 = s2_ref[...]
    gq = gq_ref[...]
    gk = gk_ref[...]
    scale = MLA_QK ** -0.5 * math.log2(math.e)

    def rope(x):
        return x * c + pltpu.roll(x, LANE - MLA_ROPE // 2, 1) * s1 + pltpu.roll(x, MLA_ROPE // 2, 1) * s2

    pe = pe_ref[...]
    pe_ss = jnp.sum(pe * pe, axis=-1, keepdims=True)
    NW = MLA_HEADS * MLA_NOPE
    for h in range(MLA_HEADS):
        lo = h * MLA_QK_PAD
        qh = qf[:, lo:lo + MLA_QK_PAD]
        inv = lax.rsqrt(jnp.sum(qh * qh, axis=-1, keepdims=True) * (1.0 / MLA_QK) + EPS)
        qn = qh * inv * gq
        qo_ref[:, lo:lo + MLA_NOPE] = (qn[:, :MLA_NOPE] * scale).astype(BF16)
        qo_ref[:, lo + MLA_NOPE:lo + MLA_QK_PAD] = (rope(qn[:, MLA_NOPE:]) * scale).astype(BF16)
        kh = kvf[:, h * MLA_NOPE:(h + 1) * MLA_NOPE]
        inv = lax.rsqrt((jnp.sum(kh * kh, axis=-1, keepdims=True) + pe_ss) * (1.0 / MLA_QK) + EPS)
        ko_ref[:, lo:lo + MLA_NOPE] = (kh * inv * gk[:, :MLA_NOPE]).astype(BF16)
        ko_ref[:, lo + MLA_NOPE:lo + MLA_QK_PAD] = rope(pe * inv * gk[:, MLA_NOPE:]).astype(BF16)
    v = kvf[:, NW:]
    vo_ref[...] = v.astype(BF16)
    vt_ref[...] = v.T.astype(BF16)


def mla_prep(dm, z, col, gqa, gkva, w_uq, w_ukv, gq_pad, gk_pad, tabs, *, row0, n_rows, rows_per_batch):
    tm = min(dm.tm, 256)
    t0 = row0 // tm
    QW = MLA_HEADS * MLA_QK_PAD
    NW = MLA_HEADS * MLA_NOPE
    q_rank, kv_rank = w_uq.shape[0], w_ukv.shape[0]
    ridx = _rope_tile_index(dm, tm)
    rspec = pl.BlockSpec((tm, LANE), lambda i: ridx(i + t0))
    tpb = rows_per_batch // tm
    const = lambda i: (0, 0)
    return pl.pallas_call(
        _mla_prep_kernel,
        grid=(n_rows // tm,),
        in_specs=[
            pl.BlockSpec((tm, q_rank), lambda i: (i + t0, col["c_q"] // q_rank)),
            pl.BlockSpec((tm, kv_rank), lambda i: (i + t0, col["c_kv"] // kv_rank)),
            pl.BlockSpec((tm, LANE), lambda i: (i + t0, col["kpe"] // LANE)),
            pl.BlockSpec((1, q_rank), const),
            pl.BlockSpec((1, kv_rank), const),
            pl.BlockSpec((q_rank, QW), const),
            pl.BlockSpec((kv_rank, 2 * NW), const),
            pl.BlockSpec((1, MLA_QK_PAD), const),
            pl.BlockSpec((1, MLA_QK_PAD), const),
            rspec, rspec, rspec,
        ],
        out_specs=[
            pl.BlockSpec((tm, QW), lambda i: (i, 0)),
            pl.BlockSpec((tm, QW), lambda i: (i, 0)),
            pl.BlockSpec((tm, NW), lambda i: (i, 0)),
            pl.BlockSpec((NW, tm), lambda i: (i // tpb, i % tpb)),
        ],
        out_shape=[
            jax.ShapeDtypeStruct((n_rows, QW), BF16),
            jax.ShapeDtypeStruct((n_rows, QW), BF16),
            jax.ShapeDtypeStruct((n_rows, NW), BF16),
            jax.ShapeDtypeStruct((n_rows // rows_per_batch * NW, rows_per_batch), BF16),
        ],
        scratch_shapes=[pltpu.VMEM((q_rank, QW), BF16), pltpu.VMEM((kv_rank, 2 * NW), BF16)],
        compiler_params=_params(1),
        name="mla_prep",
    )(z, z, z, gqa.reshape(1, -1), gkva.reshape(1, -1), w_uq, w_ukv, gq_pad, gk_pad, *tabs)


MLA_KEY_CHUNK = 512


def _mla_attn_ctx_kernel(q_ref, kc_ref, vc_ref, o_ref):
    s = _dot_nt(q_ref[...], kc_ref[...])
    p = jnp.exp2(s - jnp.max(s, axis=-1, keepdims=True))
    o = _dot(p.astype(BF16), vc_ref[...])
    o_ref[...] = (o / jnp.sum(p, axis=-1, keepdims=True)).astype(o_ref.dtype)


def _mla_attn_lat_kernel(q_ref, kc_ref, kl_ref, vct_ref, vlt_ref, o_ref, s_scr):
    tq = q_ref.shape[0]
    q = q_ref[...]
    L, S = kc_ref.shape[0], kl_ref.shape[0]
    tk = min(MLA_KEY_CHUNK, S)
    chunks = [(kc_ref, vct_ref, 0, L, 0)] + [(kl_ref, vlt_ref, c * tk, tk, L + c * tk) for c in range(S // tk)]
    mx = jnp.full((tq, LANE), NEG_BIG, F32)
    for k_ref, _, off, w, so in chunks:
        s = _dot_nt(q, k_ref[off:off + w, :])
        s_scr[:, so:so + w] = s
        for g in range(w // LANE):
            mx = jnp.maximum(mx, s[:, g * LANE:(g + 1) * LANE])
    m = jnp.max(mx, axis=-1, keepdims=True)
    ls = jnp.zeros((tq, LANE), F32)
    acc = jnp.zeros((vct_ref.shape[0], tq), F32)
    for _, vt_ref, off, w, so in chunks:
        p = jnp.exp2(s_scr[:, so:so + w] - m)
        for g in range(w // LANE):
            ls = ls + p[:, g * LANE:(g + 1) * LANE]
        acc = acc + _dot_nt(vt_ref[:, off:off + w], p.astype(BF16))
    l = jnp.sum(ls, axis=-1, keepdims=True)
    o_ref[...] = (acc.T / l).astype(o_ref.dtype)


def mla_attention_ctx(dm, q, kc, vc):
    L = dm.L
    QP, V = MLA_QK_PAD, MLA_V
    return pl.pallas_call(
        _mla_attn_ctx_kernel,
        grid=(dm.B, MLA_HEADS),
        in_specs=[pl.BlockSpec((L, QP), lambda b, h: (b, h)),
                  pl.BlockSpec((L, QP), lambda b, h: (b, h)),
                  pl.BlockSpec((L, V), lambda b, h: (b, h))],
        out_specs=pl.BlockSpec((L, V), lambda b, h: (b, h)),
        out_shape=jax.ShapeDtypeStruct((dm.RC, MLA_HEADS * V), BF16),
        compiler_params=_params(2),
        name="mla_attn_ctx",
    )(q, kc, vc)


def mla_attention_lat(dm, q, kc, kl, vct, vlt):
    L, S = dm.L, dm.S
    tq = min(512, S)
    nq = S // tq
    QP, V = MLA_QK_PAD, MLA_V
    H = MLA_HEADS
    return pl.pallas_call(
        _mla_attn_lat_kernel,
        grid=(dm.B, H, nq),
        in_specs=[pl.BlockSpec((tq, QP), lambda b, h, n: (b * nq + n, h)),
                  pl.BlockSpec((L, QP), lambda b, h, n: (b, h)),
                  pl.BlockSpec((S, QP), lambda b, h, n: (b, h)),
                  pl.BlockSpec((V, L), lambda b, h, n: (b * H + h, 0)),
                  pl.BlockSpec((V, S), lambda b, h, n: (b * H + h, 0))],
        out_specs=pl.BlockSpec((tq, V), lambda b, h, n: (b * nq + n, h)),
        out_shape=jax.ShapeDtypeStruct((dm.RL, H * V), BF16),
        scratch_shapes=[pltpu.VMEM((tq, L + S), F32)],
        compiler_params=_params(3),
        name="mla_attn_lat",
    )(q, kc, kl, vct, vlt)


SSM_SUPER = 16
SSM_BLOCK_GROUPS = LANE // SSM_GROUP


def _ssm_in_kernel(u_ref, wi_ref, ws_ref, y_ref, s_ref):
    a = _load_row_tiles(u_ref, SSM_CHUNK).astype(BF16)
    y_ref[...] = _dot(a, wi_ref[...])
    zs = _dot(a, ws_ref[...])
    for c in range(s_ref.shape[0]):
        s_ref[c] = zs[:, c * LANE:(c + 1) * LANE]


def ssm_chunk_in(z, u_col, w_intra, w_state, *, tr):
    R = z.shape[0]
    nblk, CW, _ = w_intra.shape
    nr = tr // SSM_CHUNK
    c0 = u_col // LANE
    w_spec = pl.BlockSpec((None, CW, CW), lambda j, i: (j, 0, 0), pipeline_mode=pl.Buffered(1))
    return pl.pallas_call(
        _ssm_in_kernel,
        grid=(nblk, R // tr),
        in_specs=[pl.BlockSpec((tr, LANE), lambda j, i: (i, c0 + j)), w_spec, w_spec],
        out_specs=[pl.BlockSpec((nr, CW), lambda j, i: (i, j)),
                   pl.BlockSpec((2 * SSM_BLOCK_GROUPS, nr, LANE), lambda j, i: (0, i, j))],
        out_shape=[jax.ShapeDtypeStruct((R // SSM_CHUNK, nblk * CW), F32),
                   jax.ShapeDtypeStruct((2 * SSM_BLOCK_GROUPS, R // SSM_CHUNK, nblk * LANE), F32)],
        compiler_params=_params(2),
        name="ssm_chunk_in",
    )(z, w_intra, w_state)


def _ssm_scan_kernel(s_ref, p1_ref, p2_ref, x_ref, t_ref, e_ref, *, batch, n_ctx_sc, n_lat_sc):
    SC = SSM_SUPER
    GB = SSM_BLOCK_GROUPS
    FWD, BWD = slice(0, GB), slice(GB, 2 * GB)
    n_sc = s_ref.shape[1] // SC

    def cmul(i, rows, x):
        swapped = jnp.concatenate([x[..., SSM_STATE:], x[..., :SSM_STATE]], axis=-1)
        return p1_ref[i, rows] * x + p2_ref[i, rows] * swapped

    def chunk(i):
        return pl.ds(i, n_sc, stride=SC)

    lf = jnp.zeros((GB, n_sc, LANE), F32)
    lb = jnp.zeros((GB, n_sc, LANE), F32)
    for i in range(SC):
        x_ref[FWD, chunk(i), :] = lf
        x_ref[BWD, chunk(SC - 1 - i), :] = lb
        lf = cmul(1, FWD, lf) + s_ref[FWD, chunk(i), :]
        lb = cmul(1, BWD, lb) + s_ref[BWD, chunk(SC - 1 - i), :]
    t_ref[FWD] = lf
    t_ref[BWD] = lb
    n_ctx = batch * n_ctx_sc
    for rows, order in ((FWD, 1), (BWD, -1)):
        e = jnp.zeros((GB, batch, LANE), F32)
        for region_start, per_batch in ((0, n_ctx_sc), (n_ctx, n_lat_sc)):
            steps = range(per_batch) if order == 1 else range(per_batch - 1, -1, -1)
            for m in steps:
                idx = pl.ds(region_start + m, batch, stride=per_batch)
                e_ref[rows, idx, :] = e
                e = cmul(SC, rows, e) + t_ref[rows, idx, :]
    ef = e_ref[FWD]
    eb = e_ref[BWD]
    for i in range(SC):
        x_ref[FWD, chunk(i), :] += cmul(i, FWD, ef)
        x_ref[BWD, chunk(SC - 1 - i), :] += cmul(i, BWD, eb)


def ssm_scan(dm, s, p1, p2, layer):
    NS, NR, W = s.shape
    nblk = W // LANE
    n_sc = NR // SSM_SUPER
    n_ctx_sc = dm.L // (SSM_CHUNK * SSM_SUPER)
    n_lat_sc = dm.S // (SSM_CHUNK * SSM_SUPER)
    blk = pl.BlockSpec((NS, NR, LANE), lambda j: (0, 0, j))
    pspec = pl.BlockSpec((None, None, SSM_SUPER + 1, NS, 1, LANE), lambda j: (layer, j, 0, 0, 0, 0))
    return pl.pallas_call(
        functools.partial(_ssm_scan_kernel, batch=dm.B, n_ctx_sc=n_ctx_sc, n_lat_sc=n_lat_sc),
        grid=(nblk,),
        in_specs=[blk, pspec, pspec],
        out_specs=blk,
        out_shape=jax.ShapeDtypeStruct(s.shape, F32),
        scratch_shapes=[pltpu.VMEM((NS, n_sc, LANE), F32), pltpu.VMEM((NS, n_sc, LANE), F32)],
        compiler_params=_params(1),
        name="ssm_scan",
    )(s, p1, p2)


def _ssm_out_kernel(y_ref, x_ref, u_ref, w_ref, d_ref, o_ref):
    nr = y_ref.shape[0]
    xs = jnp.concatenate([x_ref[c] for c in range(x_ref.shape[0])], axis=1).astype(BF16)
    y = y_ref[...] + _dot(xs, w_ref[...])
    d = d_ref[...]
    for t in range(SSM_CHUNK):
        rows = pl.ds(t, nr, stride=SSM_CHUNK)
        o_ref[rows, :] = _gelu_tanh(y[:, t * LANE:(t + 1) * LANE] + d * u_ref[rows, :])


def ssm_chunk_out(y_intra, x_states, z, u_col, w_out_state, d_skip, *, tr):
    R = z.shape[0]
    nblk, CW, _ = w_out_state.shape
    nr = tr // SSM_CHUNK
    c0 = u_col // LANE
    return pl.pallas_call(
        _ssm_out_kernel,
        grid=(nblk, R // tr),
        in_specs=[pl.BlockSpec((nr, CW), lambda j, i: (i, j)),
                  pl.BlockSpec((2 * SSM_BLOCK_GROUPS, nr, LANE), lambda j, i: (0, i, j)),
                  pl.BlockSpec((tr, LANE), lambda j, i: (i, c0 + j)),
                  pl.BlockSpec((None, CW, CW), lambda j, i: (j, 0, 0), pipeline_mode=pl.Buffered(1)),
                  pl.BlockSpec((1, LANE), lambda j, i: (0, j))],
        out_specs=pl.BlockSpec((tr, LANE), lambda j, i: (i, j)),
        out_shape=jax.ShapeDtypeStruct((R, nblk * LANE), F32),
        compiler_params=_params(2),
        name="ssm_chunk_out",
    )(y_intra, x_states, z, w_out_state, d_skip.astype(F32).reshape(1, -1))


def _ssm_expand_kernel(k_ref, o_ref, *, mode):
    C, H, GB = SSM_CHUNK, SSM_GROUP, SSM_BLOCK_GROUPS
    CH = C * H
    W = o_ref.shape[1]
    ri = lax.broadcasted_iota(jnp.int32, (CH, W), 0)
    ci = lax.broadcasted_iota(jnp.int32, (CH, W), 1)
    for gl in range(GB):
        kc = k_ref[gl].astype(BF16)
        if mode == "state":
            zero = jnp.zeros((CH, LANE), BF16)
            cols = [kc[:, d * LANE:(d + 1) * LANE] if g2 == gl else zero for d in range(2) for g2 in range(GB)]
            t = jnp.concatenate(cols, axis=1)
        else:
            sel = jnp.where(ci == (ri >> 4) * LANE + gl * H + (ri & (H - 1)), 1.0, 0.0).astype(BF16)
            t = _dot(kc, sel).astype(BF16)
        if mode == "out":
            for d in range(2):
                o_ref[d * GB * LANE + gl * LANE:d * GB * LANE + (gl + 1) * LANE, :] = t[d * LANE:(d + 1) * LANE, :]
        else:
            for s in range(C):
                o_ref[s * LANE + gl * H:s * LANE + (gl + 1) * H, :] = t[s * H:(s + 1) * H, :]


def _ssm_expand(compact, layer, mode):
    _, G, CH, _ = compact.shape
    GB = SSM_BLOCK_GROUPS
    CW = CH * GB
    return pl.pallas_call(
        functools.partial(_ssm_expand_kernel, mode=mode),
        grid=(G // GB,),
        in_specs=[pl.BlockSpec((None, GB, CH, CH), lambda j: (layer, j, 0, 0))],
        out_specs=pl.BlockSpec((None, CW, CW), lambda j: (j, 0, 0)),
        out_shape=jax.ShapeDtypeStruct((G // GB, CW, CW), BF16),
        compiler_params=_params(1),
        name="ssm_expand_" + mode,
    )(compact)


def _ssm_compact_tables(lam_re, lam_im, log_step, b_re, b_im, c_re, c_im):
    C, H, P = SSM_CHUNK, SSM_GROUP, SSM_STATE
    G = lam_re.shape[1]
    delta = jnp.exp(log_step.astype(F32))[..., None]
    zr = lam_re.astype(F32) * delta
    zi = lam_im.astype(F32) * delta
    k = jnp.arange(C + 1, dtype=F32)[:, None, None, None]
    mag = jnp.exp(k * zr[None])
    pw_re = mag * jnp.cos(k * zi[None])
    pw_im = mag * jnp.sin(k * zi[None])
    lb_re, lb_im = pw_re[1], pw_im[1]
    lr, li = lam_re.astype(F32), lam_im.astype(F32)
    den = lr * lr + li * li
    f_re = ((lb_re - 1.0) * lr + lb_im * li) / den
    f_im = (lb_im * lr - (lb_re - 1.0) * li) / den
    br, bi = b_re.astype(F32), b_im.astype(F32)
    bb_re = f_re[..., None] * br - f_im[..., None] * bi
    bb_im = f_re[..., None] * bi + f_im[..., None] * br
    cr, ci = c_re.astype(F32), c_im.astype(F32)
    cl_re = cr[None] * pw_re[:, :, :, None, :] - ci[None] * pw_im[:, :, :, None, :]
    cl_im = cr[None] * pw_im[:, :, :, None, :] + ci[None] * pw_re[:, :, :, None, :]
    hp = lax.Precision.HIGHEST
    kern = (jnp.einsum("kdghp,dgpj->dgkhj", cl_re[:C], bb_re, precision=hp)
            - jnp.einsum("kdghp,dgpj->dgkhj", cl_im[:C], bb_im, precision=hp))
    k_idx = jnp.arange(C)[:, None, None]
    s_idx = jnp.arange(C)[None, :, None]
    t_idx = jnp.arange(C)[None, None, :]
    sel_f = (t_idx - s_idx == k_idx).astype(F32)
    sel_b = (s_idx - t_idx == k_idx).astype(F32)
    ksum = (jnp.einsum("kst,gkhj->gsjth", sel_f, kern[0], precision=hp)
            + jnp.einsum("kst,gkhj->gsjth", sel_b, kern[1], precision=hp))

    def state_in(d, power_of_s):
        pr = pw_re[power_of_s, d]
        pi = pw_im[power_of_s, d]
        re = pr[..., None] * bb_re[d][None] - pi[..., None] * bb_im[d][None]
        im = pr[..., None] * bb_im[d][None] + pi[..., None] * bb_re[d][None]
        return jnp.concatenate([re, im], axis=2).transpose(1, 0, 3, 2)

    def state_out(d, power_of_t):
        re = cl_re[power_of_t, d]
        im = cl_im[power_of_t, d]
        return jnp.concatenate([re, -im], axis=-1).transpose(1, 3, 0, 2)

    m_sum = jnp.stack([state_in(0, C - 1 - jnp.arange(C)), state_in(1, jnp.arange(C))])
    m_out = jnp.stack([state_out(0, 1 + jnp.arange(C)), state_out(1, C - jnp.arange(C))])

    GB = SSM_BLOCK_GROUPS
    nblk = G // GB
    ksum = ksum.reshape(G, C * H, C * H)
    m_sum = m_sum.transpose(1, 2, 3, 0, 4).reshape(G, C * H, 2 * 2 * P)
    m_out = m_out.transpose(1, 0, 2, 3, 4).reshape(G, 2 * 2 * P, C * H)
    i = (C * jnp.arange(SSM_SUPER + 1, dtype=F32))[:, None, None, None]
    mag_a = jnp.exp(i * zr[None])
    pa_re = mag_a * jnp.cos(i * zi[None])
    pa_im = mag_a * jnp.sin(i * zi[None])

    def scan_table(lo, hi):
        t = jnp.concatenate([lo, hi], axis=-1).reshape(SSM_SUPER + 1, 2, nblk, GB, 2 * P)
        return t.transpose(2, 0, 1, 3, 4).reshape(nblk, SSM_SUPER + 1, 2 * GB, 1, 2 * P)

    return ksum, m_sum, m_out, scan_table(pa_re, pa_re), scan_table(-pa_im, pa_im)


def s5_branch(dm, z, col, compact, layer, d_skip):
    ksum, m_sum, m_out, p1, p2 = compact
    w_intra = _ssm_expand(ksum, layer, "intra")
    w_state = _ssm_expand(m_sum, layer, "state")
    w_out_state = _ssm_expand(m_out, layer, "out")
    nr = _pick_tile((272, 136, 96, 64, 32, 16, 8), dm.R // SSM_CHUNK)
    tr = nr * SSM_CHUNK
    y_intra, s = ssm_chunk_in(z, col["u"], w_intra, w_state, tr=tr)
    x_states = ssm_scan(dm, s, p1, p2, layer)
    return ssm_chunk_out(y_intra, x_states, z, col["u"], w_out_state, d_skip, tr=tr)


def _merge_kernel(*refs, n_ctx_tiles):
    if n_ctx_tiles is None:
        ya_ref, yb_ref, yc_ref, ga_ref, gb_ref, gc_ref, wa_ref, wb_ref, wc_ref, o_ref, sa, sb, sc = refs
        ya, yb = ya_ref[...], yb_ref[...]
    else:
        (yac_ref, ybc_ref, ya_ref, yb_ref, yc_ref, ga_ref, gb_ref, gc_ref, wa_ref, wb_ref, wc_ref, o_ref,
         sa, sb, sc) = refs
        is_ctx = pl.program_id(1) < n_ctx_tiles
        ya = jnp.where(is_ctx, yac_ref[...], ya_ref[...])
        yb = jnp.where(is_ctx, ybc_ref[...], yb_ref[...])

    @pl.when(pl.program_id(1) == 0)
    def _():
        for w_ref, s in ((wa_ref, sa), (wb_ref, sb), (wc_ref, sc)):
            s[...] = w_ref[...].astype(BF16)

    acc = ga_ref[...].astype(F32) * _dot(ya, sa[...])
    acc = acc + gb_ref[...].astype(F32) * _dot(yb, sb[...])
    acc = acc + gc_ref[...].astype(F32) * _dot(yc_ref[...], sc[...])
    o_ref[...] = acc.astype(o_ref.dtype)


def merge_branches(dm, ys, y_row0s, gates, w_branch, layer, *, row0, n_rows, ys_ctx=None):
    BW = ys[0].shape[1]
    D = w_branch.shape[-1]
    tm = min(dm.tm, 512)
    tn = 1024
    t0 = row0 // tm
    nj = D // tn
    n_ctx_tiles = None if ys_ctx is None else dm.RC // tm

    def y_index(y0):
        off = (row0 - y0) // tm
        return lambda j, i: (jnp.maximum(i + off, 0), 0)

    y_specs = [pl.BlockSpec((tm, BW), y_index(y0)) for y0 in y_row0s]
    ctx_arrays = ()
    if ys_ctx is not None:
        ctx_spec = pl.BlockSpec((tm, BW), lambda j, i: (jnp.minimum(i, n_ctx_tiles - 1), 0))
        y_specs = [ctx_spec, ctx_spec] + y_specs
        ctx_arrays = tuple(ys_ctx)
    g_specs = [pl.BlockSpec((tm, tn), functools.partial(lambda j, i, n: (i + t0, n * nj + j), n=n))
               for n in range(N_BRANCH)]
    w_specs = [pl.BlockSpec((None, None, BW, tn), functools.partial(lambda j, i, n: (layer, n, 0, j), n=n))
               for n in range(N_BRANCH)]
    return pl.pallas_call(
        functools.partial(_merge_kernel, n_ctx_tiles=n_ctx_tiles),
        grid=(nj, n_rows // tm),
        in_specs=y_specs + g_specs + w_specs,
        out_specs=pl.BlockSpec((tm, tn), lambda j, i: (i, j)),
        out_shape=jax.ShapeDtypeStruct((n_rows, D), BF16),
        scratch_shapes=[pltpu.VMEM((BW, tn), BF16)] * 3,
        compiler_params=_params(2),
        name="merge_branches",
    )(*ctx_arrays, *ys, gates, gates, gates, w_branch, w_branch, w_branch)


def _router_kernel(h_ref, whi_ref, wlo_ref, b_ref, idx_ref, w_ref):
    half = whi_ref.shape[0] // 2
    lo, hi = _unpack_bf16_pairs(_load_row_tiles(h_ref, half // LANE))
    logits = b_ref[...]
    for w in (whi_ref, wlo_ref):
        logits = logits + _dot(lo, w[:half, :]) + _dot(hi, w[half:, :])
    lane = lax.broadcasted_iota(jnp.int32, logits.shape, 1).astype(F32)
    logits = jnp.where(lane < N_EXPERTS, logits, NEG_BIG)
    m1 = jnp.max(logits, axis=-1, keepdims=True)
    i1 = jnp.min(jnp.where(logits == m1, lane, float(LANE)), axis=-1, keepdims=True)
    rest = jnp.where(lane == i1, NEG_BIG, logits)
    m2 = jnp.max(rest, axis=-1, keepdims=True)
    i2 = jnp.min(jnp.where(rest == m2, lane, float(LANE)), axis=-1, keepdims=True)
    e = jnp.exp(m2 - m1)
    w1 = 1.0 / (1.0 + e)
    w2 = e / (1.0 + e)
    idx_ref[...] = jnp.where(lane == 0.0, i1, jnp.where(lane == 1.0, i2, 0.0)).astype(jnp.int32)
    w_ref[...] = jnp.where(lane == 0.0, w1, jnp.where(lane == 1.0, w2, 0.0))


def moe_router(hp, w_router, b_router):
    D = w_router.shape[0]
    n = D // 2 // LANE
    M = hp.shape[0] // n
    tm = _pick_tile((1024, 512, 256, 128), M)
    w_pad = jnp.zeros((D, LANE), F32).at[:, :N_EXPERTS].set(w_router.astype(F32))
    w_hi = w_pad.astype(BF16)
    w_lo = (w_pad - w_hi.astype(F32)).astype(BF16)
    b_pad = jnp.zeros((1, LANE), F32).at[0, :N_EXPERTS].set(b_router.astype(F32))
    return pl.pallas_call(
        _router_kernel,
        grid=(M // tm,),
        in_specs=[pl.BlockSpec((tm * n, LANE), lambda i: (i, 0)),
                  pl.BlockSpec((D, LANE), lambda i: (0, 0)),
                  pl.BlockSpec((D, LANE), lambda i: (0, 0)),
                  pl.BlockSpec((1, LANE), lambda i: (0, 0))],
        out_specs=[pl.BlockSpec((tm, LANE), lambda i: (i, 0)), pl.BlockSpec((tm, LANE), lambda i: (i, 0))],
        out_shape=[jax.ShapeDtypeStruct((M, LANE), jnp.int32), jax.ShapeDtypeStruct((M, LANE), F32)],
        compiler_params=_params(1),
        name="moe_router",
    )(hp, w_hi, w_lo, b_pad)


GATHER_UNROLL = 8


def _gather_rows_kernel(idx_ref, src_ref, o_ref, buf, sem, *, n):
    tg = o_ref.shape[0]
    base = pl.program_id(0) * tg

    def start(it, carry):
        for u in range(GATHER_UNROLL):
            r = it * GATHER_UNROLL + u
            src_row = pl.multiple_of(idx_ref[base + r] * n, n)
            dst_row = pl.multiple_of(r * n, n)
            pltpu.make_async_copy(src_ref.at[pl.ds(src_row, n)], buf.at[pl.ds(dst_row, n)], sem).start(
                priority=u % 2)
        return carry

    lax.fori_loop(0, tg // GATHER_UNROLL, start, 0)
    pltpu.make_async_copy(src_ref.at[pl.ds(0, tg * n)], buf, sem).wait()
    lo, hi = _unpack_bf16_pairs(_load_row_tiles(buf, n))
    half = lo.shape[1]
    o_ref[:, :half] = lo
    o_ref[:, half:] = hi


def gather_rows(src, idx, n, *, tg=512):
    M = idx.shape[0]
    return pl.pallas_call(
        functools.partial(_gather_rows_kernel, n=n),
        grid_spec=pltpu.PrefetchScalarGridSpec(
            num_scalar_prefetch=1,
            grid=(M // tg,),
            in_specs=[pl.BlockSpec(memory_space=pl.ANY)],
            out_specs=pl.BlockSpec((tg, 2 * n * LANE), lambda i, idx_ref: (i, 0)),
            scratch_shapes=[pltpu.VMEM((tg * n, LANE), src.dtype), pltpu.SemaphoreType.DMA(())],
        ),
        out_shape=jax.ShapeDtypeStruct((M, 2 * n * LANE), BF16),
        compiler_params=_params(1),
        name="gather_rows",
    )(idx, src)


def _moe_w13_kernel(te_ref, tv_ref, a_ref, wg_ref, wu_ref, o_ref, sg, su):
    i = pl.program_id(1)
    prev = te_ref[jnp.maximum(i - 1, 0)]

    @pl.when(jnp.logical_or(i == 0, te_ref[i] != prev))
    def _():
        sg[...] = wg_ref[...].astype(BF16)
        su[...] = wu_ref[...].astype(BF16)

    @pl.when(tv_ref[i] == 1)
    def _():
        a = a_ref[...]
        o_ref[...] = (_silu(_dot(a, sg[...])) * _dot(a, su[...])).astype(o_ref.dtype)

    @pl.when(tv_ref[i] == 0)
    def _():
        o_ref[...] = jnp.zeros_like(o_ref)


def moe_w13(xs, w13, moe_idx, tile_expert, tile_valid, *, tn):
    P, D = xs.shape
    F = w13.shape[-1] // 2
    tm = MOE_TM
    nj = F // tn
    return pl.pallas_call(
        _moe_w13_kernel,
        grid_spec=pltpu.PrefetchScalarGridSpec(
            num_scalar_prefetch=2,
            grid=(nj, P // tm),
            in_specs=[pl.BlockSpec((tm, D), lambda j, i, te, tv: (i, 0)),
                      pl.BlockSpec((None, None, D, tn), lambda j, i, te, tv: (moe_idx, te[i], 0, j)),
                      pl.BlockSpec((None, None, D, tn), lambda j, i, te, tv: (moe_idx, te[i], 0, j + nj))],
            out_specs=pl.BlockSpec((tm, tn), lambda j, i, te, tv: (i, j)),
            scratch_shapes=[pltpu.VMEM((D, tn), BF16), pltpu.VMEM((D, tn), BF16)],
        ),
        out_shape=jax.ShapeDtypeStruct((P, F), BF16),
        compiler_params=_params(2),
        name="moe_w13",
    )(tile_expert, tile_valid, xs, w13, w13)


def _moe_w2_kernel(te_ref, tv_ref, ts_ref, a_ref, w_ref, o_ref):
    i = pl.program_id(1)

    @pl.when(tv_ref[i] == 1)
    def _():
        _store_row_tiles(o_ref, _dot(a_ref[...], w_ref[...]))

    @pl.when(tv_ref[i] == 0)
    def _():
        o_ref[...] = jnp.zeros_like(o_ref)


def moe_w2(act, w2, tile_expert, tile_valid, tile_src, *, tn):
    P, F = act.shape
    D = w2.shape[-1]
    tm = MOE_TM
    n = tn // LANE
    return pl.pallas_call(
        _moe_w2_kernel,
        grid_spec=pltpu.PrefetchScalarGridSpec(
            num_scalar_prefetch=3,
            grid=(D // tn, P // tm),
            in_specs=[pl.BlockSpec((tm, F), lambda j, i, te, tv, ts: (ts[i], 0)),
                      pl.BlockSpec((None, F, tn), lambda j, i, te, tv, ts: (te[i], 0, j))],
            out_specs=pl.BlockSpec((None, tm * n, LANE), lambda j, i, te, tv, ts: (j, i, 0)),
        ),
        out_shape=jax.ShapeDtypeStruct((D // tn, P * n, LANE), F32),
        compiler_params=_params(2),
        name="moe_w2",
    )(tile_expert, tile_valid, tile_src, act, w2)


def _moe_combine_kernel(p0_ref, p1_ref, y_ref, x_ref, gate_ref, w_ref, o_ref, b0, b1, sem):
    tc = o_ref.shape[0]
    n = b0.shape[1] // tc
    base = pl.program_id(0) * tc

    def start(r, carry):
        dst = pl.ds(pl.multiple_of(r * n, n), n)
        src0 = pl.ds(pl.multiple_of(p0_ref[base + r] * n, n), n)
        src1 = pl.ds(pl.multiple_of(p1_ref[base + r] * n, n), n)
        pltpu.make_async_copy(y_ref.at[:, src0], b0.at[:, dst], sem.at[0]).start(priority=0)
        pltpu.make_async_copy(y_ref.at[:, src1], b1.at[:, dst], sem.at[1]).start(priority=1)
        return carry

    lax.fori_loop(0, tc, start, 0, unroll=GATHER_UNROLL)
    pltpu.make_async_copy(y_ref.at[:, pl.ds(0, tc * n)], b0, sem.at[0]).wait()
    pltpu.make_async_copy(y_ref.at[:, pl.ds(0, tc * n)], b1, sem.at[1]).wait()
    w = w_ref[...]
    y = w[:, 0:1] * _load_row_tiles(b0, n) + w[:, 1:2] * _load_row_tiles(b1, n)
    o_ref[...] = x_ref[...] + gate_ref[0] * y


def moe_combine(dm, y_sorted, pos0, pos1, top_w, x, gate, *, mod_row0):
    M, D = x.shape
    tc = 256
    t0 = mod_row0 // tc
    J = y_sorted.shape[0]
    n = D // (J * LANE)
    return pl.pallas_call(
        _moe_combine_kernel,
        grid_spec=pltpu.PrefetchScalarGridSpec(
            num_scalar_prefetch=2,
            grid=(M // tc,),
            in_specs=[pl.BlockSpec(memory_space=pl.ANY),
                      pl.BlockSpec((tc, D), lambda i, p0, p1: (i, 0)),
                      pl.BlockSpec((1, 1, D), lambda i, p0, p1: (dm.mod_row(i + t0, tc), 0, 0)),
                      pl.BlockSpec((tc, LANE), lambda i, p0, p1: (i, 0))],
            out_specs=pl.BlockSpec((tc, D), lambda i, p0, p1: (i, 0)),
            scratch_shapes=[pltpu.VMEM((J, tc * n, LANE), F32), pltpu.VMEM((J, tc * n, LANE), F32),
                            pltpu.SemaphoreType.DMA((2,))],
        ),
        out_shape=jax.ShapeDtypeStruct((M, D), F32),
        compiler_params=_params(1),
        name="moe_combine",
    )(pos0, pos1, y_sorted, x, gate, top_w)


def moe_ffn(dm, hp, x, gate, w_router, b_router, w13, w2, moe_idx, *, mod_row0):
    M, D = x.shape
    E = N_EXPERTS
    tm = MOE_TM
    top_idx, top_w = moe_router(hp, w_router, b_router)
    e_flat = top_idx[:, :TOP_K].T.reshape(-1)
    onehot = (e_flat[:, None] == jnp.arange(E, dtype=jnp.int32)[None, :]).astype(F32)
    blk = LANE
    nb = onehot.shape[0] // blk
    oh3 = onehot.reshape(nb, blk, E)
    exact = lax.Precision.HIGHEST
    within = jnp.einsum("ij,bjk->bik", jnp.tril(jnp.ones((blk, blk), F32), -1), oh3, precision=exact)
    before = jnp.einsum("ab,bk->ak", jnp.tril(jnp.ones((nb, nb), F32), -1), jnp.sum(oh3, axis=1), precision=exact)
    rank = jnp.sum((within + before[:, None, :]) * oh3, axis=-1).reshape(-1).astype(jnp.int32)
    counts = jnp.sum(onehot, axis=0).astype(jnp.int32)
    padded = ((counts + tm - 1) // tm) * tm
    ends = jnp.cumsum(padded)
    starts = ends - padded
    pos = starts[e_flat] + rank
    P = TOP_K * M + E * tm
    n_tiles = P // tm
    tok = jnp.tile(jnp.arange(M, dtype=jnp.int32), TOP_K)
    gidx = jnp.zeros((P,), jnp.int32).at[pos].set(tok)
    tile_start = jnp.arange(n_tiles, dtype=jnp.int32) * tm
    tile_valid = (tile_start < ends[-1]).astype(jnp.int32)
    te = jnp.sum((tile_start[:, None] >= ends[None, :]).astype(jnp.int32), axis=1)
    last_e = jnp.sum((ends[-1] - 1 >= ends).astype(jnp.int32))
    tile_expert = jnp.minimum(te, last_e).astype(jnp.int32)

    xs = gather_rows(hp, gidx, D // 2 // LANE)
    F = w13.shape[-1] // 2
    act = moe_w13(xs, w13, moe_idx, tile_expert, tile_valid, tn=_pick_tile((1024, 512, 256, 128), F))
    tile_src = jnp.minimum(jnp.arange(n_tiles, dtype=jnp.int32), jnp.sum(tile_valid) - 1).astype(jnp.int32)
    y_sorted = moe_w2(act, w2[moe_idx].astype(BF16), tile_expert, tile_valid, tile_src, tn=min(1024, D))
    return moe_combine(dm, y_sorted, pos[:M], pos[M:], top_w, x, gate, mod_row0=mod_row0)


def _axial_angles(seq, rot_dim):
    rows = seq // GRID_W
    t_row = jnp.repeat(jnp.arange(rows, dtype=F32), GRID_W)
    t_col = jnp.tile(jnp.arange(GRID_W, dtype=F32), rows)
    quarter = rot_dim // 4
    inv_freq = ROPE_THETA ** (-jnp.arange(quarter, dtype=F32) / quarter)
    return jnp.concatenate([t_row[:, None] * inv_freq, t_col[:, None] * inv_freq], axis=-1)


def _rope_tables(dm):
    ident = min(dm.tm, 256)
    ang = _axial_angles(dm.S, SWA_HEAD_DIM)
    cos_a = jnp.concatenate([jnp.cos(ang), jnp.cos(ang)], axis=-1)
    sin_a = jnp.concatenate([-jnp.sin(ang), jnp.sin(ang)], axis=-1)
    cos_a = jnp.concatenate([jnp.ones((ident, LANE), F32), cos_a], axis=0)
    sin_a = jnp.concatenate([jnp.zeros((ident, LANE), F32), sin_a], axis=0)
    ang = _axial_angles(dm.S, MLA_ROPE)
    half = MLA_ROPE // 2
    zeros = jnp.zeros((dm.S, half), F32)
    pad = jnp.zeros((dm.S, LANE - MLA_ROPE), F32)
    c_b = jnp.concatenate([jnp.cos(ang), jnp.cos(ang), pad], axis=-1)
    s1_b = jnp.concatenate([-jnp.sin(ang), zeros, pad], axis=-1)
    s2_b = jnp.concatenate([zeros, jnp.sin(ang), pad], axis=-1)
    c_b = jnp.concatenate([jnp.ones((ident, LANE), F32), c_b], axis=0)
    s1_b = jnp.concatenate([jnp.zeros((ident, LANE), F32), s1_b], axis=0)
    s2_b = jnp.concatenate([jnp.zeros((ident, LANE), F32), s2_b], axis=0)
    return (cos_a, sin_a), (c_b, s1_b, s2_b)


def _pad_head_vec(g):
    return jnp.zeros((1, MLA_QK_PAD), F32).at[0, :MLA_QK].set(g.astype(F32))


def _trunk(x, c, ctx, c_ctx, mod_w, mod_b, norm_mix_g, norm_ffn_g, w_in,
           swa_q_norm_g, swa_k_norm_g, swa_sink,
           mla_q_a_norm_g, mla_w_uq, mla_kv_a_norm_g, mla_w_ukv, mla_q_norm_g, mla_k_norm_g,
           ssm_lam_re, ssm_lam_im, ssm_log_step, ssm_b_re, ssm_b_im, ssm_c_re, ssm_c_im,
           ssm_d, ssm_w_glu, ssm_b_glu, w_branch, w_out,
           ffn_w13, ffn_w2, moe_w_router, moe_b_router, moe_w13, moe_w2):
    B, S, D = x.shape
    L = ctx.shape[1]
    depth = mod_w.shape[0]
    dm = Dims(B, S, L)
    tm = dm.tm
    RC, RL, R = dm.RC, dm.RL, dm.R
    q_w = SWA_HEADS * SWA_HEAD_DIM
    kv_w = SWA_KV_HEADS * SWA_HEAD_DIM
    q_rank = mla_w_uq.shape[1]
    kv_rank = mla_w_ukv.shape[1]
    ssm_w = ssm_d.shape[1]
    n_gate = N_BRANCH * D
    src = {}
    off = 0
    for name, width in (("q", q_w), ("k", kv_w), ("v", kv_w), ("c_q", q_rank), ("c_kv", kv_rank),
                        ("kpe", MLA_ROPE), ("u", ssm_w), ("gates", n_gate)):
        src[name] = (off, width)
        off += width
    order = ("q", "u", "c_q", "k", "v", "c_kv", "kpe")
    col = {}
    off = 0
    for name in order:
        col[name] = off
        off += src[name][1]
    z_tn = 1792
    z_cols = -(-off // z_tn) * z_tn

    (cos_a, sin_a), tabs_b = _rope_tables(dm)
    ssm_compact = jax.vmap(_ssm_compact_tables)(ssm_lam_re, ssm_lam_im, ssm_log_step, ssm_b_re, ssm_b_im,
                                                ssm_c_re, ssm_c_im)
    x_ctx0 = ctx.reshape(RC, D).astype(F32)
    xall = x.reshape(RL, D).astype(F32)
    cond = jnp.zeros((8, D), F32).at[0].set(c_ctx.astype(F32)).at[1:1 + B].set(c.astype(F32))

    for layer in range(depth):
        with_ctx = layer < depth - 1
        row0 = 0 if with_ctx else RC
        n_rows = R - row0
        mods = mm1(cond, [(mod_w, (layer,), 0)], _epi_bias, n_rows=8, n_cols=6 * D, tm=8, tn=512, out_dtype=F32,
                   extras=[(mod_b.reshape(depth, 1, 6 * D), (None, 1, 512), lambda j, i: (layer, 0, j))],
                   prologue=lambda a: _silu(a).astype(BF16), name="ada_mod")
        sh_m, sc_m, g_m, sh_f, sc_f, g_f = [mods[:, i * D:(i + 1) * D].reshape(8, 1, D) for i in range(6)]

        split_input = layer == 0
        h = modulate(dm, xall, norm_mix_g[layer], sh_m, sc_m, mod_row0=0, x_ctx=x_ctx0 if split_input else None)
        w_l = w_in[layer]
        w_rest = jnp.concatenate([w_l[:, src[n][0]:src[n][0] + src[n][1]] for n in order]
                                 + [jnp.zeros((D, z_cols - off), w_l.dtype)], axis=1)
        w_gates = w_l[:, src["gates"][0]:]
        z = mm1(h, [(w_rest, (), 0)], _epi_id, n_rows=R, n_cols=z_cols, tm=min(tm, 512), tn=z_tn, out_dtype=F32,
                name="w_in")
        gates = mm1(h, [(w_gates, (), 0)], _epi_sigmoid, n_rows=R, n_cols=n_gate, tm=tm, tn=1024,
                    out_dtype=BF16, name="w_in_gates")

        qa, ka, va = swa_prep(dm, z, col, swa_q_norm_g[layer], swa_k_norm_g[layer], cos_a, sin_a)
        ya_l = swa_attention(dm, qa, ka, va, swa_sink[layer], latent=True)
        w_uq = mla_w_uq[layer].reshape(q_rank, MLA_HEADS, MLA_QK)
        w_uq = jnp.pad(w_uq, ((0, 0), (0, 0), (0, MLA_QK_PAD - MLA_QK))).reshape(q_rank, MLA_HEADS * MLA_QK_PAD)
        w_ukv = mla_w_ukv[layer].reshape(kv_rank, MLA_HEADS, MLA_NOPE + MLA_V)
        w_ukv = jnp.concatenate([w_ukv[:, :, :MLA_NOPE].reshape(kv_rank, -1),
                                 w_ukv[:, :, MLA_NOPE:].reshape(kv_rank, -1)], axis=1)
        mla_args = (dm, z, col, mla_q_a_norm_g[layer], mla_kv_a_norm_g[layer], w_uq, w_ukv,
                    _pad_head_vec(mla_q_norm_g[layer]), _pad_head_vec(mla_k_norm_g[layer]), tabs_b)
        qm_c, km_c, vm_c, vt_c = mla_prep(*mla_args, row0=0, n_rows=RC, rows_per_batch=L)
        qm_l, km_l, _, vt_l = mla_prep(*mla_args, row0=RC, n_rows=RL, rows_per_batch=S)
        yb_l = mla_attention_lat(dm, qm_l, km_c, km_l, vt_c, vt_l)
        if with_ctx:
            ys_ctx = (swa_attention(dm, qa, ka, va, swa_sink[layer], latent=False),
                      mla_attention_ctx(dm, qm_c, km_c, vm_c))
        else:
            ys_ctx = None
        yg = s5_branch(dm, z, col, ssm_compact, layer, ssm_d[layer])
        b_glu = ssm_b_glu.reshape(depth, 1, 2 * ssm_w)
        gl_tn = 512
        yc = mm1(yg, [(ssm_w_glu, (layer,), 0), (ssm_w_glu, (layer,), ssm_w)], _epi_glu_bias,
                 n_rows=R, n_cols=ssm_w, tm=tm, tn=gl_tn, out_dtype=BF16,
                 extras=[(b_glu, (None, 1, gl_tn), lambda j, i: (layer, 0, j)),
                         (b_glu, (None, 1, gl_tn), lambda j, i: (layer, 0, j + ssm_w // gl_tn))],
                 prologue=lambda a: a.astype(BF16), name="ssm_glu")
        mixed = merge_branches(dm, (ya_l, yb_l, yc), (RC, RC, 0), gates, w_branch, layer, row0=row0, n_rows=n_rows,
                               ys_ctx=ys_ctx)
        t0 = row0 // tm
        gate_extra = (g_m, (1, 1, 1024), lambda j, i: (dm.mod_row(i + t0, tm), 0, j))
        if split_input and with_ctx:
            nct = RC // tm
            res_epi = functools.partial(_epi_residual_two_sources, n_ctx_tiles=nct)
            res_extras = [(x_ctx0, (tm, 1024), lambda j, i: (jnp.minimum(i, nct - 1), j)),
                          (xall, (tm, 1024), lambda j, i: (jnp.maximum(i - nct, 0), j)), gate_extra]
        elif split_input:
            res_epi = _epi_residual
            res_extras = [(xall, (tm, 1024), lambda j, i: (i, j)), gate_extra]
        else:
            res_epi = _epi_residual
            res_extras = [(xall, (tm, 1024), lambda j, i: (i + t0, j)), gate_extra]
        x1 = mm1(mixed, [(w_out, (layer,), 0)], res_epi, n_rows=n_rows, n_cols=D, tm=tm, tn=1024, out_dtype=F32,
                 extras=res_extras, name="w_out")
        is_moe = layer % 2 == 1
        h2 = modulate(dm, x1, norm_ffn_g[layer], sh_f, sc_f, mod_row0=row0, pack=is_moe)
        if not is_moe:
            F = ffn_w13.shape[-1] // 2
            f_tn = _pick_tile((512, 256, 128), F)
            act = mm1(h2, [(ffn_w13, (layer // 2,), 0), (ffn_w13, (layer // 2,), F)], _epi_swiglu, n_rows=n_rows,
                      n_cols=F, tm=tm, tn=f_tn, out_dtype=BF16, name="ffn_w13")
            x2 = mm2_residual(dm, act, ffn_w2[layer // 2].astype(BF16), x1, g_f, mod_row0=row0, tm=tm, tk=f_tn)
        else:
            if with_ctx:
                raise NotImplementedError("a mixture-of-experts layer that still feeds context rows")
            x2 = moe_ffn(dm, h2, x1, g_f, moe_w_router[layer // 2], moe_b_router[layer // 2], moe_w13, moe_w2,
                         layer // 2, mod_row0=row0)
        xall = x2
    return xall.reshape(B, S, D)


def kernel(x, c, ctx, c_ctx, mod_w, mod_b, norm_mix_g, norm_ffn_g, w_in, swa_q_norm_g, swa_k_norm_g, swa_sink, mla_q_a_norm_g, mla_w_uq, mla_kv_a_norm_g, mla_w_ukv, mla_q_norm_g, mla_k_norm_g, ssm_lam_re, ssm_lam_im, ssm_log_step, ssm_b_re, ssm_b_im, ssm_c_re, ssm_c_im, ssm_d, ssm_w_glu, ssm_b_glu, w_branch, w_out, ffn_w13, ffn_w2, moe_w_router, moe_b_router, moe_w13, moe_w2):
    return _trunk(x, c, ctx, c_ctx, mod_w, mod_b, norm_mix_g, norm_ffn_g, w_in, swa_q_norm_g, swa_k_norm_g, swa_sink,
                  mla_q_a_norm_g, mla_w_uq, mla_kv_a_norm_g, mla_w_ukv, mla_q_norm_g, mla_k_norm_g,
                  ssm_lam_re, ssm_lam_im, ssm_log_step, ssm_b_re, ssm_b_im, ssm_c_re, ssm_c_im,
                  ssm_d, ssm_w_glu, ssm_b_glu, w_branch, w_out, ffn_w13, ffn_w2, moe_w_router, moe_b_router,
                  moe_w13, moe_w2)
```

```python
import functools
import math

import jax
import jax.numpy as jnp
from jax import lax
from jax.experimental import pallas as pl
from jax.experimental.pallas import tpu as pltpu

F32 = jnp.float32
BF16 = jnp.bfloat16

GRID_W = 64
ROPE_THETA = 10000.0
EPS = 1e-6
SWA_HEADS = 8
SWA_KV_HEADS = 2
SWA_HEAD_DIM = 128
SWA_WINDOW = 128
MLA_HEADS = 8
MLA_NOPE = 128
MLA_ROPE = 64
MLA_V = 128
MLA_QK = MLA_NOPE + MLA_ROPE
MLA_QK_PAD = 256
SSM_GROUP = 16
SSM_STATE = 64
SSM_CHUNK = 16
N_BRANCH = 3
N_EXPERTS = 8
TOP_K = 2
LANE = 128
VMEM_LIMIT_BYTES = 56 * 1024 * 1024
MOE_TM = 512
NEG_BIG = -1e30


def _params(n_grid):
    return pltpu.CompilerParams(dimension_semantics=("arbitrary",) * n_grid, vmem_limit_bytes=VMEM_LIMIT_BYTES)


def _pick_tile(candidates, *sizes):
    for t in candidates:
        if all(s % t == 0 for s in sizes):
            return t
    raise ValueError(f"no tile in {candidates} divides {sizes}")


class Dims:
    def __init__(self, batch, seq, ctx_len):
        self.B, self.S, self.L = batch, seq, ctx_len
        self.RC = batch * ctx_len
        self.RL = batch * seq
        self.R = self.RC + self.RL
        self.tm = _pick_tile((1024, 512, 256, 128), ctx_len * batch, seq)

    def mod_row(self, tile, tm):
        nct = self.RC // tm
        return jnp.where(tile < nct, 0, 1 + (tile - nct) // (self.S // tm))


def _silu(x):
    return x * (1.0 / (1.0 + jnp.exp(-x)))


def _sigmoid(x):
    return 1.0 / (1.0 + jnp.exp(-x))


def _gelu_tanh(x):
    c = math.sqrt(2.0 / math.pi)
    return 0.5 * x * (1.0 + jnp.tanh(c * (x + 0.044715 * (x * x * x))))


def _dot(a, b):
    return jnp.dot(a, b, preferred_element_type=F32)


def _dot_nt(a, b):
    return lax.dot_general(a, b, (((1,), (1,)), ((), ())), preferred_element_type=F32)


def _pack_bf16_pairs(y):
    half = y.shape[1] // 2
    bits = lax.bitcast_convert_type(y.astype(BF16).astype(F32), jnp.uint32)
    return (bits[:, :half] >> 16) | (bits[:, half:] & jnp.uint32(0xFFFF0000))


def _unpack_bf16_pairs(p):
    lo = lax.bitcast_convert_type(p << 16, F32).astype(BF16)
    hi = lax.bitcast_convert_type(p & jnp.uint32(0xFFFF0000), F32).astype(BF16)
    return lo, hi


def _store_row_tiles(ref, val):
    m, w = val.shape
    n = w // LANE
    for c in range(n):
        ref[pl.ds(c, m, stride=n), :] = val[:, c * LANE:(c + 1) * LANE]


def _load_row_tiles(ref, n):
    if len(ref.shape) == 2:
        m = ref.shape[0] // n
        return jnp.concatenate([ref[pl.ds(c, m, stride=n), :] for c in range(n)], axis=1)
    m = ref.shape[1] // n
    return jnp.concatenate([ref[j, pl.ds(c, m, stride=n), :] for j in range(ref.shape[0]) for c in range(n)], axis=1)


def _modulate_kernel(*refs, pack, n_ctx_tiles):
    if n_ctx_tiles is None:
        x_ref, g_ref, sh_ref, sc_ref, o_ref = refs
        x = x_ref[...]
    else:
        xc_ref, xl_ref, g_ref, sh_ref, sc_ref, o_ref = refs
        x = jnp.where(pl.program_id(0) < n_ctx_tiles, xc_ref[...], xl_ref[...])
    ms = jnp.mean(x * x, axis=-1, keepdims=True)
    y = x * lax.rsqrt(ms + EPS) * g_ref[...]
    y = y * (1.0 + sc_ref[0]) + sh_ref[0]
    if pack:
        _store_row_tiles(o_ref, _pack_bf16_pairs(y))
    else:
        o_ref[...] = y.astype(o_ref.dtype)


def modulate(dm, x, g, shift, scale, *, mod_row0, pack=False, x_ctx=None):
    D = x.shape[1]
    tm = min(dm.tm, 512)
    t0 = mod_row0 // tm
    if x_ctx is None:
        n_rows = x.shape[0]
        n_ctx_tiles = None
        x_specs = [pl.BlockSpec((tm, D), lambda i: (i, 0))]
        xs = (x,)
    else:
        n_rows = x_ctx.shape[0] + x.shape[0]
        n_ctx_tiles = x_ctx.shape[0] // tm
        x_specs = [pl.BlockSpec((tm, D), lambda i: (jnp.minimum(i, n_ctx_tiles - 1), 0)),
                   pl.BlockSpec((tm, D), lambda i: (jnp.maximum(i - n_ctx_tiles, 0), 0))]
        xs = (x_ctx, x)
    if pack:
        n = D // 2 // LANE
        out_spec = pl.BlockSpec((tm * n, LANE), lambda i: (i, 0))
        out_shape = jax.ShapeDtypeStruct((n_rows * n, LANE), jnp.uint32)
    else:
        out_spec = pl.BlockSpec((tm, D), lambda i: (i, 0))
        out_shape = jax.ShapeDtypeStruct((n_rows, D), BF16)
    return pl.pallas_call(
        functools.partial(_modulate_kernel, pack=pack, n_ctx_tiles=n_ctx_tiles),
        grid=(n_rows // tm,),
        in_specs=x_specs + [
            pl.BlockSpec((1, D), lambda i: (0, 0)),
            pl.BlockSpec((1, 1, D), lambda i: (dm.mod_row(i + t0, tm), 0, 0)),
            pl.BlockSpec((1, 1, D), lambda i: (dm.mod_row(i + t0, tm), 0, 0)),
        ],
        out_specs=out_spec,
        out_shape=out_shape,
        compiler_params=_params(1),
        name="modulate_packed" if pack else "modulate",
    )(*xs, g.reshape(1, D), shift, scale)


def _mm1_kernel(*refs, n_w, n_extra, epilogue, prologue):
    a_ref = refs[0]
    w_refs = refs[1:1 + n_w]
    extra = refs[1 + n_w:1 + n_w + n_extra]
    o_ref = refs[1 + n_w + n_extra]
    wb = refs[2 + n_w + n_extra:]

    @pl.when(pl.program_id(1) == 0)
    def _():
        for w_ref, b in zip(w_refs, wb):
            b[...] = w_ref[...].astype(BF16)

    a = a_ref[...]
    if prologue is not None:
        a = prologue(a)
    accs = [_dot(a, b[...]) for b in wb]
    o_ref[...] = epilogue(accs, *extra).astype(o_ref.dtype)


def mm1(a, weights, epilogue, *, n_rows, n_cols, tm, tn, out_dtype, a_row0=0, extras=(), prologue=None, name):
    K = a.shape[1]
    t0 = a_row0 // tm
    in_specs = [pl.BlockSpec((tm, K), lambda j, i: (i + t0, 0))]
    operands = [a]
    for w, lead, col0 in weights:
        c0 = col0 // tn
        in_specs.append(pl.BlockSpec((None,) * len(lead) + (K, tn),
                                     functools.partial(lambda j, i, lead, c0: lead + (0, j + c0), lead=lead, c0=c0)))
        operands.append(w)
    for arr, bshape, imap in extras:
        in_specs.append(pl.BlockSpec(bshape, imap))
        operands.append(arr)
    kern = functools.partial(_mm1_kernel, n_w=len(weights), n_extra=len(extras), epilogue=epilogue,
                             prologue=prologue)
    return pl.pallas_call(
        kern,
        grid=(n_cols // tn, n_rows // tm),
        in_specs=in_specs,
        out_specs=pl.BlockSpec((tm, tn), lambda j, i: (i, j)),
        out_shape=jax.ShapeDtypeStruct((n_rows, n_cols), out_dtype),
        scratch_shapes=[pltpu.VMEM((K, tn), BF16) for _ in weights],
        compiler_params=_params(2),
        name=name,
    )(*operands)


def _epi_id(accs):
    return accs[0]


def _epi_sigmoid(accs):
    return _sigmoid(accs[0])


def _epi_swiglu(accs):
    return _silu(accs[0]) * accs[1]


def _epi_bias(accs, b_ref):
    return accs[0] + b_ref[...]


def _epi_glu_bias(accs, ba_ref, bb_ref):
    return (accs[0] + ba_ref[...]) * _sigmoid(accs[1] + bb_ref[...])


def _epi_residual(accs, x_ref, gate_ref):
    return x_ref[...] + gate_ref[0] * accs[0]


def _epi_residual_two_sources(accs, xc_ref, xl_ref, gate_ref, *, n_ctx_tiles):
    x = jnp.where(pl.program_id(1) < n_ctx_tiles, xc_ref[...], xl_ref[...])
    return x + gate_ref[0] * accs[0]


def _mm2_kernel(a_ref, w_ref, x_ref, gate_ref, o_ref, acc_ref):
    k = pl.program_id(1)

    @pl.when(k == 0)
    def _():
        acc_ref[...] = jnp.zeros_like(acc_ref)

    acc_ref[...] += _dot(a_ref[...], w_ref[...])

    @pl.when(k == pl.num_programs(1) - 1)
    def _():
        o_ref[...] = x_ref[...] + gate_ref[0] * acc_ref[...]


def mm2_residual(dm, a, w, x, gate, *, mod_row0, tm, tk):
    M, K = a.shape
    N = w.shape[1]
    t0 = mod_row0 // tm
    return pl.pallas_call(
        _mm2_kernel,
        grid=(M // tm, K // tk),
        in_specs=[
            pl.BlockSpec((tm, tk), lambda i, k: (i, k)),
            pl.BlockSpec((tk, N), lambda i, k: (k, 0)),
            pl.BlockSpec((tm, N), lambda i, k: (i, 0)),
            pl.BlockSpec((1, 1, N), lambda i, k: (dm.mod_row(i + t0, tm), 0, 0)),
        ],
        out_specs=pl.BlockSpec((tm, N), lambda i, k: (i, 0)),
        out_shape=jax.ShapeDtypeStruct((M, N), F32),
        scratch_shapes=[pltpu.VMEM((tm, N), F32)],
        compiler_params=_params(2),
        name="mm2_residual",
    )(a, w, x, gate)


def _swa_prep_kernel(q_ref, k_ref, v_ref, gq_ref, gk_ref, cos_ref, sin_ref, qo_ref, ko_ref, vo_ref):
    c = cos_ref[...]
    s = sin_ref[...]

    def norm_rope(x, g, scale):
        ms = jnp.mean(x * x, axis=-1, keepdims=True)
        y = x * lax.rsqrt(ms + EPS) * g
        return (y * c + pltpu.roll(y, SWA_HEAD_DIM // 2, 1) * s) * scale

    gq = gq_ref[...]
    gk = gk_ref[...]
    for h in range(SWA_HEADS):
        sl = slice(h * SWA_HEAD_DIM, (h + 1) * SWA_HEAD_DIM)
        qo_ref[:, sl] = norm_rope(q_ref[:, sl], gq, SWA_HEAD_DIM ** -0.5).astype(BF16)
    for h in range(SWA_KV_HEADS):
        sl = slice(h * SWA_HEAD_DIM, (h + 1) * SWA_HEAD_DIM)
        ko_ref[:, sl] = norm_rope(k_ref[:, sl], gk, 1.0).astype(BF16)
    vo_ref[...] = v_ref[...].astype(BF16)


def _rope_tile_index(dm, tm):
    nct = dm.RC // tm
    return lambda i: (jnp.where(i < nct, 0, 1 + (i - nct) % (dm.S // tm)), 0)


def swa_prep(dm, z, col, gq, gk, cos_t, sin_t):
    tm = min(dm.tm, 256)
    QW = SWA_HEADS * SWA_HEAD_DIM
    KW = SWA_KV_HEADS * SWA_HEAD_DIM
    ridx = _rope_tile_index(dm, tm)
    return pl.pallas_call(
        _swa_prep_kernel,
        grid=(dm.R // tm,),
        in_specs=[
            pl.BlockSpec((tm, QW), lambda i: (i, col["q"] // QW)),
            pl.BlockSpec((tm, KW), lambda i: (i, col["k"] // KW)),
            pl.BlockSpec((tm, KW), lambda i: (i, col["v"] // KW)),
            pl.BlockSpec((1, SWA_HEAD_DIM), lambda i: (0, 0)),
            pl.BlockSpec((1, SWA_HEAD_DIM), lambda i: (0, 0)),
            pl.BlockSpec((tm, SWA_HEAD_DIM), ridx),
            pl.BlockSpec((tm, SWA_HEAD_DIM), ridx),
        ],
        out_specs=[
            pl.BlockSpec((tm, QW), lambda i: (i, 0)),
            pl.BlockSpec((tm, KW), lambda i: (i, 0)),
            pl.BlockSpec((tm, KW), lambda i: (i, 0)),
        ],
        out_shape=[
            jax.ShapeDtypeStruct((dm.R, QW), BF16),
            jax.ShapeDtypeStruct((dm.R, KW), BF16),
            jax.ShapeDtypeStruct((dm.R, KW), BF16),
        ],
        compiler_params=_params(1),
        name="swa_prep",
    )(z, z, z, gq.reshape(1, -1), gk.reshape(1, -1), cos_t, sin_t)


def _swa_attn_kernel(*refs, windowed, nb):
    if windowed:
        q_ref, kc_ref, kp_ref, kk_ref, kn_ref, vc_ref, vp_ref, vk_ref, vn_ref, sink_ref, o_ref = refs
    else:
        q_ref, kc_ref, vc_ref, sink_ref, o_ref = refs
    G = SWA_HEADS // SWA_KV_HEADS
    blk = q_ref.shape[0]
    Dh = SWA_HEAD_DIM
    q = jnp.concatenate([q_ref[:, g * Dh:(g + 1) * Dh] for g in range(G)], axis=0)
    sink = sink_ref[0][:, 0:1]
    scores = [_dot_nt(q, kc_ref[...])]
    values = [vc_ref[...]]
    if windowed:
        n = pl.program_id(2)
        qi = lax.broadcasted_iota(jnp.int32, (G * blk, blk), 0) % blk
        kj = lax.broadcasted_iota(jnp.int32, (G * blk, blk), 1)
        s_p = _dot_nt(q, kp_ref[...])
        s_p = jnp.where(kj >= qi, s_p, NEG_BIG)
        s_p = jnp.where(n >= 1, s_p, NEG_BIG)
        s_n = _dot_nt(q, kn_ref[...])
        s_n = jnp.where(kj <= qi, s_n, NEG_BIG)
        s_n = jnp.where(n <= nb - 2, s_n, NEG_BIG)
        scores += [s_p, _dot_nt(q, kk_ref[...]), s_n]
        values += [vp_ref[...], vk_ref[...], vn_ref[...]]
    m = sink
    for s in scores:
        m = jnp.maximum(m, jnp.max(s, axis=-1, keepdims=True))
    l = jnp.exp(sink - m)
    o = None
    for s, v in zip(scores, values):
        p = jnp.exp(s - m)
        l = l + jnp.sum(p, axis=-1, keepdims=True)
        pv = _dot(p.astype(BF16), v)
        o = pv if o is None else o + pv
    o = o / l
    for g in range(G):
        o_ref[:, g * Dh:(g + 1) * Dh] = o[g * blk:(g + 1) * blk].astype(o_ref.dtype)


def swa_attention(dm, qa, ka, va, sink, *, latent):
    G = SWA_HEADS // SWA_KV_HEADS
    Dh = SWA_HEAD_DIM
    blk = SWA_WINDOW
    L = dm.L
    sink_col = jnp.broadcast_to(sink.astype(F32).reshape(SWA_KV_HEADS, G, 1, 1),
                                (SWA_KV_HEADS, G, blk, LANE)).reshape(SWA_KV_HEADS, G * blk, LANE)
    sink_spec = pl.BlockSpec((1, G * blk, LANE), lambda b, h, n: (h, 0, 0))
    ctx_spec = pl.BlockSpec((L, Dh), lambda b, h, n: (b, h))
    if latent:
        nb = dm.S // blk
        base = dm.RC // blk

        def q_map(b, h, n):
            return (base + b * nb + n, h)

        def kv_map(off):
            return lambda b, h, n: (base + b * nb + jnp.clip(n + off, 0, nb - 1), h)

        win_specs = [pl.BlockSpec((blk, Dh), kv_map(off)) for off in (-1, 0, 1)]
        in_specs = ([pl.BlockSpec((blk, G * Dh), q_map), ctx_spec] + win_specs + [ctx_spec] + win_specs
                    + [sink_spec])
        operands = (qa, ka, ka, ka, ka, va, va, va, va, sink_col)
        n_out = dm.RL
    else:
        nb = L // blk
        in_specs = [pl.BlockSpec((blk, G * Dh), lambda b, h, n: (b * nb + n, h)), ctx_spec, ctx_spec, sink_spec]
        operands = (qa, ka, va, sink_col)
        n_out = dm.RC
    return pl.pallas_call(
        functools.partial(_swa_attn_kernel, windowed=latent, nb=nb),
        grid=(dm.B, SWA_KV_HEADS, nb),
        in_specs=in_specs,
        out_specs=pl.BlockSpec((blk, G * Dh), lambda b, h, n: (b * nb + n, h)),
        out_shape=jax.ShapeDtypeStruct((n_out, SWA_HEADS * Dh), BF16),
        compiler_params=_params(3),
        name="swa_attn_lat" if latent else "swa_attn_ctx",
    )(*operands)


def _mla_prep_kernel(cq_ref, ckv_ref, pe_ref, gqa_ref, gkva_ref, wq_ref, wkv_ref, gq_ref, gk_ref, c_ref, s1_ref,
                     s2_ref, qo_ref, ko_ref, vo_ref, vt_ref, wq_s, wkv_s):
    @pl.when(pl.program_id(0) == 0)
    def _():
        wq_s[...] = wq_ref[...].astype(BF16)
        wkv_s[...] = wkv_ref[...].astype(BF16)

    def rms(x_ref, g_ref):
        x = x_ref[...]
        ms = jnp.mean(x * x, axis=-1, keepdims=True)
        return (x * lax.rsqrt(ms + EPS) * g_ref[...]).astype(BF16)

    qf = _dot(rms(cq_ref, gqa_ref), wq_s[...])
    kvf = _dot(rms(ckv_ref, gkva_ref), wkv_s[...])
    c = c_ref[...]
    s1 = s1_ref[...]
    s2 = s2_ref[...]
    gq = gq_ref[...]
    gk = gk_ref[...]
    scale = MLA_QK ** -0.5 * math.log2(math.e)

    def rope(x):
        return x * c + pltpu.roll(x, LANE - MLA_ROPE // 2, 1) * s1 + pltpu.roll(x, MLA_ROPE // 2, 1) * s2

    pe = pe_ref[...]
    pe_ss = jnp.sum(pe * pe, axis=-1, keepdims=True)
    NW = MLA_HEADS * MLA_NOPE
    for h in range(MLA_HEADS):
        lo = h * MLA_QK_PAD
        qh = qf[:, lo:lo + MLA_QK_PAD]
        inv = lax.rsqrt(jnp.sum(qh * qh, axis=-1, keepdims=True) * (1.0 / MLA_QK) + EPS)
        qn = qh * inv * gq
        qo_ref[:, lo:lo + MLA_NOPE] = (qn[:, :MLA_NOPE] * scale).astype(BF16)
        qo_ref[:, lo + MLA_NOPE:lo + MLA_QK_PAD] = (rope(qn[:, MLA_NOPE:]) * scale).astype(BF16)
        kh = kvf[:, h * MLA_NOPE:(h + 1) * MLA_NOPE]
        inv = lax.rsqrt((jnp.sum(kh * kh, axis=-1, keepdims=True) + pe_ss) * (1.0 / MLA_QK) + EPS)
        ko_ref[:, lo:lo + MLA_NOPE] = (kh * inv * gk[:, :MLA_NOPE]).astype(BF16)
        ko_ref[:, lo + MLA_NOPE:lo + MLA_QK_PAD] = rope(pe * inv * gk[:, MLA_NOPE:]).astype(BF16)
    v = kvf[:, NW:]
    vo_ref[...] = v.astype(BF16)
    vt_ref[...] = v.T.astype(BF16)


def mla_prep(dm, z, col, gqa, gkva, w_uq, w_ukv, gq_pad, gk_pad, tabs, *, row0, n_rows, rows_per_batch):
    tm = min(dm.tm, 256)
    t0 = row0 // tm
    QW = MLA_HEADS * MLA_QK_PAD
    NW = MLA_HEADS * MLA_NOPE
    q_rank, kv_rank = w_uq.shape[0], w_ukv.shape[0]
    ridx = _rope_tile_index(dm, tm)
    rspec = pl.BlockSpec((tm, LANE), lambda i: ridx(i + t0))
    tpb = rows_per_batch // tm
    const = lambda i: (0, 0)
    return pl.pallas_call(
        _mla_prep_kernel,
        grid=(n_rows // tm,),
        in_specs=[
            pl.BlockSpec((tm, q_rank), lambda i: (i + t0, col["c_q"] // q_rank)),
            pl.BlockSpec((tm, kv_rank), lambda i: (i + t0, col["c_kv"] // kv_rank)),
            pl.BlockSpec((tm, LANE), lambda i: (i + t0, col["kpe"] // LANE)),
            pl.BlockSpec((1, q_rank), const),
            pl.BlockSpec((1, kv_rank), const),
            pl.BlockSpec((q_rank, QW), const),
            pl.BlockSpec((kv_rank, 2 * NW), const),
            pl.BlockSpec((1, MLA_QK_PAD), const),
            pl.BlockSpec((1, MLA_QK_PAD), const),
            rspec, rspec, rspec,
        ],
        out_specs=[
            pl.BlockSpec((tm, QW), lambda i: (i, 0)),
            pl.BlockSpec((tm, QW), lambda i: (i, 0)),
            pl.BlockSpec((tm, NW), lambda i: (i, 0)),
            pl.BlockSpec((NW, tm), lambda i: (i // tpb, i % tpb)),
        ],
        out_shape=[
            jax.ShapeDtypeStruct((n_rows, QW), BF16),
            jax.ShapeDtypeStruct((n_rows, QW), BF16),
            jax.ShapeDtypeStruct((n_rows, NW), BF16),
            jax.ShapeDtypeStruct((n_rows // rows_per_batch * NW, rows_per_batch), BF16),
        ],
        scratch_shapes=[pltpu.VMEM((q_rank, QW), BF16), pltpu.VMEM((kv_rank, 2 * NW), BF16)],
        compiler_params=_params(1),
        name="mla_prep",
    )(z, z, z, gqa.reshape(1, -1), gkva.reshape(1, -1), w_uq, w_ukv, gq_pad, gk_pad, *tabs)


MLA_KEY_CHUNK = 512


def _mla_attn_ctx_kernel(q_ref, kc_ref, vc_ref, o_ref):
    s = _dot_nt(q_ref[...], kc_ref[...])
    p = jnp.exp2(s - jnp.max(s, axis=-1, keepdims=True))
    o = _dot(p.astype(BF16), vc_ref[...])
    o_ref[...] = (o / jnp.sum(p, axis=-1, keepdims=True)).astype(o_ref.dtype)


def _mla_attn_lat_kernel(q_ref, kc_ref, kl_ref, vct_ref, vlt_ref, o_ref, s_scr):
    tq = q_ref.shape[0]
    q = q_ref[...]
    L, S = kc_ref.shape[0], kl_ref.shape[0]
    tk = min(MLA_KEY_CHUNK, S)
    chunks = [(kc_ref, vct_ref, 0, L, 0)] + [(kl_ref, vlt_ref, c * tk, tk, L + c * tk) for c in range(S // tk)]
    mx = jnp.full((tq, LANE), NEG_BIG, F32)
    for k_ref, _, off, w, so in chunks:
        s = _dot_nt(q, k_ref[off:off + w, :])
        s_scr[:, so:so + w] = s
        for g in range(w // LANE):
            mx = jnp.maximum(mx, s[:, g * LANE:(g + 1) * LANE])
    m = jnp.max(mx, axis=-1, keepdims=True)
    ls = jnp.zeros((tq, LANE), F32)
    acc = jnp.zeros((vct_ref.shape[0], tq), F32)
    for _, vt_ref, off, w, so in chunks:
        p = jnp.exp2(s_scr[:, so:so + w] - m)
        for g in range(w // LANE):
            ls = ls + p[:, g * LANE:(g + 1) * LANE]
        acc = acc + _dot_nt(vt_ref[:, off:off + w], p.astype(BF16))
    l = jnp.sum(ls, axis=-1, keepdims=True)
    o_ref[...] = (acc.T / l).astype(o_ref.dtype)


def mla_attention_ctx(dm, q, kc, vc):
    L = dm.L
    QP, V = MLA_QK_PAD, MLA_V
    return pl.pallas_call(
        _mla_attn_ctx_kernel,
        grid=(dm.B, MLA_HEADS),
        in_specs=[pl.BlockSpec((L, QP), lambda b, h: (b, h)),
                  pl.BlockSpec((L, QP), lambda b, h: (b, h)),
                  pl.BlockSpec((L, V), lambda b, h: (b, h))],
        out_specs=pl.BlockSpec((L, V), lambda b, h: (b, h)),
        out_shape=jax.ShapeDtypeStruct((dm.RC, MLA_HEADS * V), BF16),
        compiler_params=_params(2),
        name="mla_attn_ctx",
    )(q, kc, vc)


def mla_attention_lat(dm, q, kc, kl, vct, vlt):
    L, S = dm.L, dm.S
    tq = min(512, S)
    nq = S // tq
    QP, V = MLA_QK_PAD, MLA_V
    H = MLA_HEADS
    return pl.pallas_call(
        _mla_attn_lat_kernel,
        grid=(dm.B, H, nq),
        in_specs=[pl.BlockSpec((tq, QP), lambda b, h, n: (b * nq + n, h)),
                  pl.BlockSpec((L, QP), lambda b, h, n: (b, h)),
                  pl.BlockSpec((S, QP), lambda b, h, n: (b, h)),
                  pl.BlockSpec((V, L), lambda b, h, n: (b * H + h, 0)),
                  pl.BlockSpec((V, S), lambda b, h, n: (b * H + h, 0))],
        out_specs=pl.BlockSpec((tq, V), lambda b, h, n: (b * nq + n, h)),
        out_shape=jax.ShapeDtypeStruct((dm.RL, H * V), BF16),
        scratch_shapes=[pltpu.VMEM((tq, L + S), F32)],
        compiler_params=_params(3),
        name="mla_attn_lat",
    )(q, kc, kl, vct, vlt)


SSM_SUPER = 16
SSM_BLOCK_GROUPS = LANE // SSM_GROUP


def _ssm_in_kernel(u_ref, wi_ref, ws_ref, y_ref, s_ref):
    a = _load_row_tiles(u_ref, SSM_CHUNK).astype(BF16)
    y_ref[...] = _dot(a, wi_ref[...])
    zs = _dot(a, ws_ref[...])
    for c in range(s_ref.shape[0]):
        s_ref[c] = zs[:, c * LANE:(c + 1) * LANE]


def ssm_chunk_in(z, u_col, w_intra, w_state, *, tr):
    R = z.shape[0]
    nblk, CW, _ = w_intra.shape
    nr = tr // SSM_CHUNK
    c0 = u_col // LANE
    w_spec = pl.BlockSpec((None, CW, CW), lambda j, i: (j, 0, 0), pipeline_mode=pl.Buffered(1))
    return pl.pallas_call(
        _ssm_in_kernel,
        grid=(nblk, R // tr),
        in_specs=[pl.BlockSpec((tr, LANE), lambda j, i: (i, c0 + j)), w_spec, w_spec],
        out_specs=[pl.BlockSpec((nr, CW), lambda j, i: (i, j)),
                   pl.BlockSpec((2 * SSM_BLOCK_GROUPS, nr, LANE), lambda j, i: (0, i, j))],
        out_shape=[jax.ShapeDtypeStruct((R // SSM_CHUNK, nblk * CW), F32),
                   jax.ShapeDtypeStruct((2 * SSM_BLOCK_GROUPS, R // SSM_CHUNK, nblk * LANE), F32)],
        compiler_params=_params(2),
        name="ssm_chunk_in",
    )(z, w_intra, w_state)


def _ssm_scan_kernel(s_ref, p1_ref, p2_ref, x_ref, t_ref, e_ref, *, batch, n_ctx_sc, n_lat_sc):
    SC = SSM_SUPER
    GB = SSM_BLOCK_GROUPS
    FWD, BWD = slice(0, GB), slice(GB, 2 * GB)
    n_sc = s_ref.shape[1] // SC

    def cmul(i, rows, x):
        swapped = jnp.concatenate([x[..., SSM_STATE:], x[..., :SSM_STATE]], axis=-1)
        return p1_ref[i, rows] * x + p2_ref[i, rows] * swapped

    def chunk(i):
        return pl.ds(i, n_sc, stride=SC)

    lf = jnp.zeros((GB, n_sc, LANE), F32)
    lb = jnp.zeros((GB, n_sc, LANE), F32)
    for i in range(SC):
        x_ref[FWD, chunk(i), :] = lf
        x_ref[BWD, chunk(SC - 1 - i), :] = lb
        lf = cmul(1, FWD, lf) + s_ref[FWD, chunk(i), :]
        lb = cmul(1, BWD, lb) + s_ref[BWD, chunk(SC - 1 - i), :]
    t_ref[FWD] = lf
    t_ref[BWD] = lb
    n_ctx = batch * n_ctx_sc
    for rows, order in ((FWD, 1), (BWD, -1)):
        e = jnp.zeros((GB, batch, LANE), F32)
        for region_start, per_batch in ((0, n_ctx_sc), (n_ctx, n_lat_sc)):
            steps = range(per_batch) if order == 1 else range(per_batch - 1, -1, -1)
            for m in steps:
                idx = pl.ds(region_start + m, batch, stride=per_batch)
                e_ref[rows, idx, :] = e
                e = cmul(SC, rows, e) + t_ref[rows, idx, :]
    ef = e_ref[FWD]
    eb = e_ref[BWD]
    for i in range(SC):
        x_ref[FWD, chunk(i), :] += cmul(i, FWD, ef)
        x_ref[BWD, chunk(SC - 1 - i), :] += cmul(i, BWD, eb)


def ssm_scan(dm, s, p1, p2, layer):
    NS, NR, W = s.shape
    nblk = W // LANE
    n_sc = NR // SSM_SUPER
    n_ctx_sc = dm.L // (SSM_CHUNK * SSM_SUPER)
    n_lat_sc = dm.S // (SSM_CHUNK * SSM_SUPER)
    blk = pl.BlockSpec((NS, NR, LANE), lambda j: (0, 0, j))
    pspec = pl.BlockSpec((None, None, SSM_SUPER + 1, NS, 1, LANE), lambda j: (layer, j, 0, 0, 0, 0))
    return pl.pallas_call(
        functools.partial(_ssm_scan_kernel, batch=dm.B, n_ctx_sc=n_ctx_sc, n_lat_sc=n_lat_sc),
        grid=(nblk,),
        in_specs=[blk, pspec, pspec],
        out_specs=blk,
        out_shape=jax.ShapeDtypeStruct(s.shape, F32),
        scratch_shapes=[pltpu.VMEM((NS, n_sc, LANE), F32), pltpu.VMEM((NS, n_sc, LANE), F32)],
        compiler_params=_params(1),
        name="ssm_scan",
    )(s, p1, p2)


def _ssm_out_kernel(y_ref, x_ref, u_ref, w_ref, d_ref, o_ref):
    nr = y_ref.shape[0]
    xs = jnp.concatenate([x_ref[c] for c in range(x_ref.shape[0])], axis=1).astype(BF16)
    y = y_ref[...] + _dot(xs, w_ref[...])
    d = d_ref[...]
    for t in range(SSM_CHUNK):
        rows = pl.ds(t, nr, stride=SSM_CHUNK)
        o_ref[rows, :] = _gelu_tanh(y[:, t * LANE:(t + 1) * LANE] + d * u_ref[rows, :])


def ssm_chunk_out(y_intra, x_states, z, u_col, w_out_state, d_skip, *, tr):
    R = z.shape[0]
    nblk, CW, _ = w_out_state.shape
    nr = tr // SSM_CHUNK
    c0 = u_col // LANE
    return pl.pallas_call(
        _ssm_out_kernel,
        grid=(nblk, R // tr),
        in_specs=[pl.BlockSpec((nr, CW), lambda j, i: (i, j)),
                  pl.BlockSpec((2 * SSM_BLOCK_GROUPS, nr, LANE), lambda j, i: (0, i, j)),
                  pl.BlockSpec((tr, LANE), lambda j, i: (i, c0 + j)),
                  pl.BlockSpec((None, CW, CW), lambda j, i: (j, 0, 0), pipeline_mode=pl.Buffered(1)),
                  pl.BlockSpec((1, LANE), lambda j, i: (0, j))],
        out_specs=pl.BlockSpec((tr, LANE), lambda j, i: (i, j)),
        out_shape=jax.ShapeDtypeStruct((R, nblk * LANE), F32),
        compiler_params=_params(2),
        name="ssm_chunk_out",
    )(y_intra, x_states, z, w_out_state, d_skip.astype(F32).reshape(1, -1))


def _ssm_expand_kernel(k_ref, o_ref, *, mode):
    C, H, GB = SSM_CHUNK, SSM_GROUP, SSM_BLOCK_GROUPS
    CH = C * H
    W = o_ref.shape[1]
    ri = lax.broadcasted_iota(jnp.int32, (CH, W), 0)
    ci = lax.broadcasted_iota(jnp.int32, (CH, W), 1)
    for gl in range(GB):
        kc = k_ref[gl].astype(BF16)
        if mode == "state":
            zero = jnp.zeros((CH, LANE), BF16)
            cols = [kc[:, d * LANE:(d + 1) * LANE] if g2 == gl else zero for d in range(2) for g2 in range(GB)]
            t = jnp.concatenate(cols, axis=1)
        else:
            sel = jnp.where(ci == (ri >> 4) * LANE + gl * H + (ri & (H - 1)), 1.0, 0.0).astype(BF16)
            t = _dot(kc, sel).astype(BF16)
        if mode == "out":
            for d in range(2):
                o_ref[d * GB * LANE + gl * LANE:d * GB * LANE + (gl + 1) * LANE, :] = t[d * LANE:(d + 1) * LANE, :]
        else:
            for s in range(C):
                o_ref[s * LANE + gl * H:s * LANE + (gl + 1) * H, :] = t[s * H:(s + 1) * H, :]


def _ssm_expand(compact, layer, mode):
    _, G, CH, _ = compact.shape
    GB = SSM_BLOCK_GROUPS
    CW = CH * GB
    return pl.pallas_call(
        functools.partial(_ssm_expand_kernel, mode=mode),
        grid=(G // GB,),
        in_specs=[pl.BlockSpec((None, GB, CH, CH), lambda j: (layer, j, 0, 0))],
        out_specs=pl.BlockSpec((None, CW, CW), lambda j: (j, 0, 0)),
        out_shape=jax.ShapeDtypeStruct((G // GB, CW, CW), BF16),
        compiler_params=_params(1),
        name="ssm_expand_" + mode,
    )(compact)


def _ssm_compact_tables(lam_re, lam_im, log_step, b_re, b_im, c_re, c_im):
    C, H, P = SSM_CHUNK, SSM_GROUP, SSM_STATE
    G = lam_re.shape[1]
    delta = jnp.exp(log_step.astype(F32))[..., None]
    zr = lam_re.astype(F32) * delta
    zi = lam_im.astype(F32) * delta
    k = jnp.arange(C + 1, dtype=F32)[:, None, None, None]
    mag = jnp.exp(k * zr[None])
    pw_re = mag * jnp.cos(k * zi[None])
    pw_im = mag * jnp.sin(k * zi[None])
    lb_re, lb_im = pw_re[1], pw_im[1]
    lr, li = lam_re.astype(F32), lam_im.astype(F32)
    den = lr * lr + li * li
    f_re = ((lb_re - 1.0) * lr + lb_im * li) / den
    f_im = (lb_im * lr - (lb_re - 1.0) * li) / den
    br, bi = b_re.astype(F32), b_im.astype(F32)
    bb_re = f_re[..., None] * br - f_im[..., None] * bi
    bb_im = f_re[..., None] * bi + f_im[..., None] * br
    cr, ci = c_re.astype(F32), c_im.astype(F32)
    cl_re = cr[None] * pw_re[:, :, :, None, :] - ci[None] * pw_im[:, :, :, None, :]
    cl_im = cr[None] * pw_im[:, :, :, None, :] + ci[None] * pw_re[:, :, :, None, :]
    hp = lax.Precision.HIGHEST
    kern = (jnp.einsum("kdghp,dgpj->dgkhj", cl_re[:C], bb_re, precision=hp)
            - jnp.einsum("kdghp,dgpj->dgkhj", cl_im[:C], bb_im, precision=hp))
    k_idx = jnp.arange(C)[:, None, None]
    s_idx = jnp.arange(C)[None, :, None]
    t_idx = jnp.arange(C)[None, None, :]
    sel_f = (t_idx - s_idx == k_idx).astype(F32)
    sel_b = (s_idx - t_idx == k_idx).astype(F32)
    ksum = (jnp.einsum("kst,gkhj->gsjth", sel_f, kern[0], precision=hp)
            + jnp.einsum("kst,gkhj->gsjth", sel_b, kern[1], precision=hp))

    def state_in(d, power_of_s):
        pr = pw_re[power_of_s, d]
        pi = pw_im[power_of_s, d]
        re = pr[..., None] * bb_re[d][None] - pi[..., None] * bb_im[d][None]
        im = pr[..., None] * bb_im[d][None] + pi[..., None] * bb_re[d][None]
        return jnp.concatenate([re, im], axis=2).transpose(1, 0, 3, 2)

    def state_out(d, power_of_t):
        re = cl_re[power_of_t, d]
        im = cl_im[power_of_t, d]
        return jnp.concatenate([re, -im], axis=-1).transpose(1, 3, 0, 2)

    m_sum = jnp.stack([state_in(0, C - 1 - jnp.arange(C)), state_in(1, jnp.arange(C))])
    m_out = jnp.stack([state_out(0, 1 + jnp.arange(C)), state_out(1, C - jnp.arange(C))])

    GB = SSM_BLOCK_GROUPS
    nblk = G // GB
    ksum = ksum.reshape(G, C * H, C * H)
    m_sum = m_sum.transpose(1, 2, 3, 0, 4).reshape(G, C * H, 2 * 2 * P)
    m_out = m_out.transpose(1, 0, 2, 3, 4).reshape(G, 2 * 2 * P, C * H)
    i = (C * jnp.arange(SSM_SUPER + 1, dtype=F32))[:, None, None, None]
    mag_a = jnp.exp(i * zr[None])
    pa_re = mag_a * jnp.cos(i * zi[None])
    pa_im = mag_a * jnp.sin(i * zi[None])

    def scan_table(lo, hi):
        t = jnp.concatenate([lo, hi], axis=-1).reshape(SSM_SUPER + 1, 2, nblk, GB, 2 * P)
        return t.transpose(2, 0, 1, 3, 4).reshape(nblk, SSM_SUPER + 1, 2 * GB, 1, 2 * P)

    return ksum, m_sum, m_out, scan_table(pa_re, pa_re), scan_table(-pa_im, pa_im)


def s5_branch(dm, z, col, compact, layer, d_skip):
    ksum, m_sum, m_out, p1, p2 = compact
    w_intra = _ssm_expand(ksum, layer, "intra")
    w_state = _ssm_expand(m_sum, layer, "state")
    w_out_state = _ssm_expand(m_out, layer, "out")
    nr = _pick_tile((272, 136, 96, 64, 32, 16, 8), dm.R // SSM_CHUNK)
    tr = nr * SSM_CHUNK
    y_intra, s = ssm_chunk_in(z, col["u"], w_intra, w_state, tr=tr)
    x_states = ssm_scan(dm, s, p1, p2, layer)
    return ssm_chunk_out(y_intra, x_states, z, col["u"], w_out_state, d_skip, tr=tr)


def _merge_kernel(*refs, n_ctx_tiles):
    if n_ctx_tiles is None:
        ya_ref, yb_ref, yc_ref, ga_ref, gb_ref, gc_ref, wa_ref, wb_ref, wc_ref, o_ref, sa, sb, sc = refs
        ya, yb = ya_ref[...], yb_ref[...]
    else:
        (yac_ref, ybc_ref, ya_ref, yb_ref, yc_ref, ga_ref, gb_ref, gc_ref, wa_ref, wb_ref, wc_ref, o_ref,
         sa, sb, sc) = refs
        is_ctx = pl.program_id(1) < n_ctx_tiles
        ya = jnp.where(is_ctx, yac_ref[...], ya_ref[...])
        yb = jnp.where(is_ctx, ybc_ref[...], yb_ref[...])

    @pl.when(pl.program_id(1) == 0)
    def _():
        for w_ref, s in ((wa_ref, sa), (wb_ref, sb), (wc_ref, sc)):
            s[...] = w_ref[...].astype(BF16)

    acc = ga_ref[...].astype(F32) * _dot(ya, sa[...])
    acc = acc + gb_ref[...].astype(F32) * _dot(yb, sb[...])
    acc = acc + gc_ref[...].astype(F32) * _dot(yc_ref[...], sc[...])
    o_ref[...] = acc.astype(o_ref.dtype)


def merge_branches(dm, ys, y_row0s, gates, w_branch, layer, *, row0, n_rows, ys_ctx=None):
    BW = ys[0].shape[1]
    D = w_branch.shape[-1]
    tm = min(dm.tm, 512)
    tn = 1024
    t0 = row0 // tm
    nj = D // tn
    n_ctx_tiles = None if ys_ctx is None else dm.RC // tm

    def y_index(y0):
        off = (row0 - y0) // tm
        return lambda j, i: (jnp.maximum(i + off, 0), 0)

    y_specs = [pl.BlockSpec((tm, BW), y_index(y0)) for y0 in y_row0s]
    ctx_arrays = ()
    if ys_ctx is not None:
        ctx_spec = pl.BlockSpec((tm, BW), lambda j, i: (jnp.minimum(i, n_ctx_tiles - 1), 0))
        y_specs = [ctx_spec, ctx_spec] + y_specs
        ctx_arrays = tuple(ys_ctx)
    g_specs = [pl.BlockSpec((tm, tn), functools.partial(lambda j, i, n: (i + t0, n * nj + j), n=n))
               for n in range(N_BRANCH)]
    w_specs = [pl.BlockSpec((None, None, BW, tn), functools.partial(lambda j, i, n: (layer, n, 0, j), n=n))
               for n in range(N_BRANCH)]
    return pl.pallas_call(
        functools.partial(_merge_kernel, n_ctx_tiles=n_ctx_tiles),
        grid=(nj, n_rows // tm),
        in_specs=y_specs + g_specs + w_specs,
        out_specs=pl.BlockSpec((tm, tn), lambda j, i: (i, j)),
        out_shape=jax.ShapeDtypeStruct((n_rows, D), BF16),
        scratch_shapes=[pltpu.VMEM((BW, tn), BF16)] * 3,
        compiler_params=_params(2),
        name="merge_branches",
    )(*ctx_arrays, *ys, gates, gates, gates, w_branch, w_branch, w_branch)


def _router_kernel(h_ref, whi_ref, wlo_ref, b_ref, idx_ref, w_ref):
    half = whi_ref.shape[0] // 2
    lo, hi = _unpack_bf16_pairs(_load_row_tiles(h_ref, half // LANE))
    logits = b_ref[...]
    for w in (whi_ref, wlo_ref):
        logits = logits + _dot(lo, w[:half, :]) + _dot(hi, w[half:, :])
    lane = lax.broadcasted_iota(jnp.int32, logits.shape, 1).astype(F32)
    logits = jnp.where(lane < N_EXPERTS, logits, NEG_BIG)
    m1 = jnp.max(logits, axis=-1, keepdims=True)
    i1 = jnp.min(jnp.where(logits == m1, lane, float(LANE)), axis=-1, keepdims=True)
    rest = jnp.where(lane == i1, NEG_BIG, logits)
    m2 = jnp.max(rest, axis=-1, keepdims=True)
    i2 = jnp.min(jnp.where(rest == m2, lane, float(LANE)), axis=-1, keepdims=True)
    e = jnp.exp(m2 - m1)
    w1 = 1.0 / (1.0 + e)
    w2 = e / (1.0 + e)
    idx_ref[...] = jnp.where(lane == 0.0, i1, jnp.where(lane == 1.0, i2, 0.0)).astype(jnp.int32)
    w_ref[...] = jnp.where(lane == 0.0, w1, jnp.where(lane == 1.0, w2, 0.0))


def moe_router(hp, w_router, b_router):
    D = w_router.shape[0]
    n = D // 2 // LANE
    M = hp.shape[0] // n
    tm = _pick_tile((1024, 512, 256, 128), M)
    w_pad = jnp.zeros((D, LANE), F32).at[:, :N_EXPERTS].set(w_router.astype(F32))
    w_hi = w_pad.astype(BF16)
    w_lo = (w_pad - w_hi.astype(F32)).astype(BF16)
    b_pad = jnp.zeros((1, LANE), F32).at[0, :N_EXPERTS].set(b_router.astype(F32))
    return pl.pallas_call(
        _router_kernel,
        grid=(M // tm,),
        in_specs=[pl.BlockSpec((tm * n, LANE), lambda i: (i, 0)),
                  pl.BlockSpec((D, LANE), lambda i: (0, 0)),
                  pl.BlockSpec((D, LANE), lambda i: (0, 0)),
                  pl.BlockSpec((1, LANE), lambda i: (0, 0))],
        out_specs=[pl.BlockSpec((tm, LANE), lambda i: (i, 0)), pl.BlockSpec((tm, LANE), lambda i: (i, 0))],
        out_shape=[jax.ShapeDtypeStruct((M, LANE), jnp.int32), jax.ShapeDtypeStruct((M, LANE), F32)],
        compiler_params=_params(1),
        name="moe_router",
    )(hp, w_hi, w_lo, b_pad)


GATHER_UNROLL = 8


def _gather_rows_kernel(idx_ref, src_ref, o_ref, buf, sem, *, n):
    tg = o_ref.shape[0]
    base = pl.program_id(0) * tg

    def start(it, carry):
        for u in range(GATHER_UNROLL):
            r = it * GATHER_UNROLL + u
            src_row = pl.multiple_of(idx_ref[base + r] * n, n)
            dst_row = pl.multiple_of(r * n, n)
            pltpu.make_async_copy(src_ref.at[pl.ds(src_row, n)], buf.at[pl.ds(dst_row, n)], sem).start(
                priority=u % 2)
        return carry

    lax.fori_loop(0, tg // GATHER_UNROLL, start, 0)
    pltpu.make_async_copy(src_ref.at[pl.ds(0, tg * n)], buf, sem).wait()
    lo, hi = _unpack_bf16_pairs(_load_row_tiles(buf, n))
    half = lo.shape[1]
    o_ref[:, :half] = lo
    o_ref[:, half:] = hi


def gather_rows(src, idx, n, *, tg=1024):
    M = idx.shape[0]
    return pl.pallas_call(
        functools.partial(_gather_rows_kernel, n=n),
        grid_spec=pltpu.PrefetchScalarGridSpec(
            num_scalar_prefetch=1,
            grid=(M // tg,),
            in_specs=[pl.BlockSpec(memory_space=pl.ANY)],
            out_specs=pl.BlockSpec((tg, 2 * n * LANE), lambda i, idx_ref: (i, 0)),
            scratch_shapes=[pltpu.VMEM((tg * n, LANE), src.dtype), pltpu.SemaphoreType.DMA(())],
        ),
        out_shape=jax.ShapeDtypeStruct((M, 2 * n * LANE), BF16),
        compiler_params=_params(1),
        name="gather_rows",
    )(idx, src)


def _moe_w13_kernel(te_ref, tv_ref, a_ref, wg_ref, wu_ref, o_ref, sg, su):
    i = pl.program_id(1)
    prev = te_ref[jnp.maximum(i - 1, 0)]

    @pl.when(jnp.logical_or(i == 0, te_ref[i] != prev))
    def _():
        sg[...] = wg_ref[...].astype(BF16)
        su[...] = wu_ref[...].astype(BF16)

    @pl.when(tv_ref[i] == 1)
    def _():
        a = a_ref[...]
        o_ref[...] = (_silu(_dot(a, sg[...])) * _dot(a, su[...])).astype(o_ref.dtype)

    @pl.when(tv_ref[i] == 0)
    def _():
        o_ref[...] = jnp.zeros_like(o_ref)


def moe_w13(xs, w13, moe_idx, tile_expert, tile_valid, *, tn):
    P, D = xs.shape
    F = w13.shape[-1] // 2
    tm = MOE_TM
    nj = F // tn
    return pl.pallas_call(
        _moe_w13_kernel,
        grid_spec=pltpu.PrefetchScalarGridSpec(
            num_scalar_prefetch=2,
            grid=(nj, P // tm),
            in_specs=[pl.BlockSpec((tm, D), lambda j, i, te, tv: (i, 0)),
                      pl.BlockSpec((None, None, D, tn), lambda j, i, te, tv: (moe_idx, te[i], 0, j)),
                      pl.BlockSpec((None, None, D, tn), lambda j, i, te, tv: (moe_idx, te[i], 0, j + nj))],
            out_specs=pl.BlockSpec((tm, tn), lambda j, i, te, tv: (i, j)),
            scratch_shapes=[pltpu.VMEM((D, tn), BF16), pltpu.VMEM((D, tn), BF16)],
        ),
        out_shape=jax.ShapeDtypeStruct((P, F), BF16),
        compiler_params=_params(2),
        name="moe_w13",
    )(tile_expert, tile_valid, xs, w13, w13)


def _moe_w2_kernel(te_ref, tv_ref, ts_ref, a_ref, w_ref, o_ref):
    i = pl.program_id(1)

    @pl.when(tv_ref[i] == 1)
    def _():
        _store_row_tiles(o_ref, _dot(a_ref[...], w_ref[...]))

    @pl.when(tv_ref[i] == 0)
    def _():
        o_ref[...] = jnp.zeros_like(o_ref)


def moe_w2(act, w2, tile_expert, tile_valid, tile_src, *, tn):
    P, F = act.shape
    D = w2.shape[-1]
    tm = MOE_TM
    n = tn // LANE
    return pl.pallas_call(
        _moe_w2_kernel,
        grid_spec=pltpu.PrefetchScalarGridSpec(
            num_scalar_prefetch=3,
            grid=(D // tn, P // tm),
            in_specs=[pl.BlockSpec((tm, F), lambda j, i, te, tv, ts: (ts[i], 0)),
                      pl.BlockSpec((None, F, tn), lambda j, i, te, tv, ts: (te[i], 0, j))],
            out_specs=pl.BlockSpec((None, tm * n, LANE), lambda j, i, te, tv, ts: (j, i, 0)),
        ),
        out_shape=jax.ShapeDtypeStruct((D // tn, P * n, LANE), F32),
        compiler_params=_params(2),
        name="moe_w2",
    )(tile_expert, tile_valid, tile_src, act, w2)


def _moe_combine_kernel(p0_ref, p1_ref, y_ref, x_ref, gate_ref, w_ref, o_ref, b0, b1, sem):
    tc = o_ref.shape[0]
    n = b0.shape[1] // tc
    base = pl.program_id(0) * tc

    def start(r, carry):
        dst = pl.ds(pl.multiple_of(r * n, n), n)
        src0 = pl.ds(pl.multiple_of(p0_ref[base + r] * n, n), n)
        src1 = pl.ds(pl.multiple_of(p1_ref[base + r] * n, n), n)
        pltpu.make_async_copy(y_ref.at[:, src0], b0.at[:, dst], sem.at[0]).start(priority=0)
        pltpu.make_async_copy(y_ref.at[:, src1], b1.at[:, dst], sem.at[1]).start(priority=1)
        return carry

    lax.fori_loop(0, tc, start, 0, unroll=GATHER_UNROLL)
    pltpu.make_async_copy(y_ref.at[:, pl.ds(0, tc * n)], b0, sem.at[0]).wait()
    pltpu.make_async_copy(y_ref.at[:, pl.ds(0, tc * n)], b1, sem.at[1]).wait()
    w = w_ref[...]
    y = w[:, 0:1] * _load_row_tiles(b0, n) + w[:, 1:2] * _load_row_tiles(b1, n)
    o_ref[...] = x_ref[...] + gate_ref[0] * y


def moe_combine(dm, y_sorted, pos0, pos1, top_w, x, gate, *, mod_row0):
    M, D = x.shape
    tc = 512
    t0 = mod_row0 // tc
    J = y_sorted.shape[0]
    n = D // (J * LANE)
    return pl.pallas_call(
        _moe_combine_kernel,
        grid_spec=pltpu.PrefetchScalarGridSpec(
            num_scalar_prefetch=2,
            grid=(M // tc,),
            in_specs=[pl.BlockSpec(memory_space=pl.ANY),
                      pl.BlockSpec((tc, D), lambda i, p0, p1: (i, 0)),
                      pl.BlockSpec((1, 1, D), lambda i, p0, p1: (dm.mod_row(i + t0, tc), 0, 0)),
                      pl.BlockSpec((tc, LANE), lambda i, p0, p1: (i, 0))],
            out_specs=pl.BlockSpec((tc, D), lambda i, p0, p1: (i, 0)),
            scratch_shapes=[pltpu.VMEM((J, tc * n, LANE), F32), pltpu.VMEM((J, tc * n, LANE), F32),
                            pltpu.SemaphoreType.DMA((2,))],
        ),
        out_shape=jax.ShapeDtypeStruct((M, D), F32),
        compiler_params=_params(1),
        name="moe_combine",
    )(pos0, pos1, y_sorted, x, gate, top_w)


def moe_ffn(dm, hp, x, gate, w_router, b_router, w13, w2, moe_idx, *, mod_row0):
    M, D = x.shape
    E = N_EXPERTS
    tm = MOE_TM
    top_idx, top_w = moe_router(hp, w_router, b_router)
    e_flat = top_idx[:, :TOP_K].T.reshape(-1)
    onehot = (e_flat[:, None] == jnp.arange(E, dtype=jnp.int32)[None, :]).astype(F32)
    blk = LANE
    nb = onehot.shape[0] // blk
    oh3 = onehot.reshape(nb, blk, E)
    exact = lax.Precision.HIGHEST
    within = jnp.einsum("ij,bjk->bik", jnp.tril(jnp.ones((blk, blk), F32), -1), oh3, precision=exact)
    before = jnp.einsum("ab,bk->ak", jnp.tril(jnp.ones((nb, nb), F32), -1), jnp.sum(oh3, axis=1), precision=exact)
    rank = jnp.sum((within + before[:, None, :]) * oh3, axis=-1).reshape(-1).astype(jnp.int32)
    counts = jnp.sum(onehot, axis=0).astype(jnp.int32)
    padded = ((counts + tm - 1) // tm) * tm
    ends = jnp.cumsum(padded)
    starts = ends - padded
    pos = starts[e_flat] + rank
    P = TOP_K * M + E * tm
    n_tiles = P // tm
    tok = jnp.tile(jnp.arange(M, dtype=jnp.int32), TOP_K)
    gidx = jnp.zeros((P,), jnp.int32).at[pos].set(tok)
    tile_start = jnp.arange(n_tiles, dtype=jnp.int32) * tm
    tile_valid = (tile_start < ends[-1]).astype(jnp.int32)
    te = jnp.sum((tile_start[:, None] >= ends[None, :]).astype(jnp.int32), axis=1)
    last_e = jnp.sum((ends[-1] - 1 >= ends).astype(jnp.int32))
    tile_expert = jnp.minimum(te, last_e).astype(jnp.int32)

    xs = gather_rows(hp, gidx, D // 2 // LANE)
    F = w13.shape[-1] // 2
    act = moe_w13(xs, w13, moe_idx, tile_expert, tile_valid, tn=_pick_tile((1024, 512, 256, 128), F))
    tile_src = jnp.minimum(jnp.arange(n_tiles, dtype=jnp.int32), jnp.sum(tile_valid) - 1).astype(jnp.int32)
    y_sorted = moe_w2(act, w2[moe_idx].astype(BF16), tile_expert, tile_valid, tile_src, tn=min(1024, D))
    return moe_combine(dm, y_sorted, pos[:M], pos[M:], top_w, x, gate, mod_row0=mod_row0)


def _axial_angles(seq, rot_dim):
    rows = seq // GRID_W
    t_row = jnp.repeat(jnp.arange(rows, dtype=F32), GRID_W)
    t_col = jnp.tile(jnp.arange(GRID_W, dtype=F32), rows)
    quarter = rot_dim // 4
    inv_freq = ROPE_THETA ** (-jnp.arange(quarter, dtype=F32) / quarter)
    return jnp.concatenate([t_row[:, None] * inv_freq, t_col[:, None] * inv_freq], axis=-1)


def _rope_tables(dm):
    ident = min(dm.tm, 256)
    ang = _axial_angles(dm.S, SWA_HEAD_DIM)
    cos_a = jnp.concatenate([jnp.cos(ang), jnp.cos(ang)], axis=-1)
    sin_a = jnp.concatenate([-jnp.sin(ang), jnp.sin(ang)], axis=-1)
    cos_a = jnp.concatenate([jnp.ones((ident, LANE), F32), cos_a], axis=0)
    sin_a = jnp.concatenate([jnp.zeros((ident, LANE), F32), sin_a], axis=0)
    ang = _axial_angles(dm.S, MLA_ROPE)
    half = MLA_ROPE // 2
    zeros = jnp.zeros((dm.S, half), F32)
    pad = jnp.zeros((dm.S, LANE - MLA_ROPE), F32)
    c_b = jnp.concatenate([jnp.cos(ang), jnp.cos(ang), pad], axis=-1)
    s1_b = jnp.concatenate([-jnp.sin(ang), zeros, pad], axis=-1)
    s2_b = jnp.concatenate([zeros, jnp.sin(ang), pad], axis=-1)
    c_b = jnp.concatenate([jnp.ones((ident, LANE), F32), c_b], axis=0)
    s1_b = jnp.concatenate([jnp.zeros((ident, LANE), F32), s1_b], axis=0)
    s2_b = jnp.concatenate([jnp.zeros((ident, LANE), F32), s2_b], axis=0)
    return (cos_a, sin_a), (c_b, s1_b, s2_b)


def _pad_head_vec(g):
    return jnp.zeros((1, MLA_QK_PAD), F32).at[0, :MLA_QK].set(g.astype(F32))


def _trunk(x, c, ctx, c_ctx, mod_w, mod_b, norm_mix_g, norm_ffn_g, w_in,
           swa_q_norm_g, swa_k_norm_g, swa_sink,
           mla_q_a_norm_g, mla_w_uq, mla_kv_a_norm_g, mla_w_ukv, mla_q_norm_g, mla_k_norm_g,
           ssm_lam_re, ssm_lam_im, ssm_log_step, ssm_b_re, ssm_b_im, ssm_c_re, ssm_c_im,
           ssm_d, ssm_w_glu, ssm_b_glu, w_branch, w_out,
           ffn_w13, ffn_w2, moe_w_router, moe_b_router, moe_w13, moe_w2):
    B, S, D = x.shape
    L = ctx.shape[1]
    depth = mod_w.shape[0]
    dm = Dims(B, S, L)
    tm = dm.tm
    RC, RL, R = dm.RC, dm.RL, dm.R
    q_w = SWA_HEADS * SWA_HEAD_DIM
    kv_w = SWA_KV_HEADS * SWA_HEAD_DIM
    q_rank = mla_w_uq.shape[1]
    kv_rank = mla_w_ukv.shape[1]
    ssm_w = ssm_d.shape[1]
    n_gate = N_BRANCH * D
    src = {}
    off = 0
    for name, width in (("q", q_w), ("k", kv_w), ("v", kv_w), ("c_q", q_rank), ("c_kv", kv_rank),
                        ("kpe", MLA_ROPE), ("u", ssm_w), ("gates", n_gate)):
        src[name] = (off, width)
        off += width
    order = ("q", "u", "c_q", "k", "v", "c_kv", "kpe")
    col = {}
    off = 0
    for name in order:
        col[name] = off
        off += src[name][1]
    z_tn = 1792
    z_cols = -(-off // z_tn) * z_tn

    (cos_a, sin_a), tabs_b = _rope_tables(dm)
    ssm_compact = jax.vmap(_ssm_compact_tables)(ssm_lam_re, ssm_lam_im, ssm_log_step, ssm_b_re, ssm_b_im,
                                                ssm_c_re, ssm_c_im)
    x_ctx0 = ctx.reshape(RC, D).astype(F32)
    xall = x.reshape(RL, D).astype(F32)
    cond = jnp.zeros((8, D), F32).at[0].set(c_ctx.astype(F32)).at[1:1 + B].set(c.astype(F32))

    for layer in range(depth):
        with_ctx = layer < depth - 1
        row0 = 0 if with_ctx else RC
        n_rows = R - row0
        mods = mm1(cond, [(mod_w, (layer,), 0)], _epi_bias, n_rows=8, n_cols=6 * D, tm=8, tn=512, out_dtype=F32,
                   extras=[(mod_b.reshape(depth, 1, 6 * D), (None, 1, 512), lambda j, i: (layer, 0, j))],
                   prologue=lambda a: _silu(a).astype(BF16), name="ada_mod")
        sh_m, sc_m, g_m, sh_f, sc_f, g_f = [mods[:, i * D:(i + 1) * D].reshape(8, 1, D) for i in range(6)]

        split_input = layer == 0
        h = modulate(dm, xall, norm_mix_g[layer], sh_m, sc_m, mod_row0=0, x_ctx=x_ctx0 if split_input else None)
        w_l = w_in[layer]
        w_rest = jnp.concatenate([w_l[:, src[n][0]:src[n][0] + src[n][1]] for n in order]
                                 + [jnp.zeros((D, z_cols - off), w_l.dtype)], axis=1)
        w_gates = w_l[:, src["gates"][0]:]
        z = mm1(h, [(w_rest, (), 0)], _epi_id, n_rows=R, n_cols=z_cols, tm=min(tm, 512), tn=z_tn, out_dtype=F32,
                name="w_in")
        gates = mm1(h, [(w_gates, (), 0)], _epi_sigmoid, n_rows=R, n_cols=n_gate, tm=tm, tn=1024,
                    out_dtype=BF16, name="w_in_gates")

        qa, ka, va = swa_prep(dm, z, col, swa_q_norm_g[layer], swa_k_norm_g[layer], cos_a, sin_a)
        ya_l = swa_attention(dm, qa, ka, va, swa_sink[layer], latent=True)
        w_uq = mla_w_uq[layer].reshape(q_rank, MLA_HEADS, MLA_QK)
        w_uq = jnp.pad(w_uq, ((0, 0), (0, 0), (0, MLA_QK_PAD - MLA_QK))).reshape(q_rank, MLA_HEADS * MLA_QK_PAD)
        w_ukv = mla_w_ukv[layer].reshape(kv_rank, MLA_HEADS, MLA_NOPE + MLA_V)
        w_ukv = jnp.concatenate([w_ukv[:, :, :MLA_NOPE].reshape(kv_rank, -1),
                                 w_ukv[:, :, MLA_NOPE:].reshape(kv_rank, -1)], axis=1)
        mla_args = (dm, z, col, mla_q_a_norm_g[layer], mla_kv_a_norm_g[layer], w_uq, w_ukv,
                    _pad_head_vec(mla_q_norm_g[layer]), _pad_head_vec(mla_k_norm_g[layer]), tabs_b)
        qm_c, km_c, vm_c, vt_c = mla_prep(*mla_args, row0=0, n_rows=RC, rows_per_batch=L)
        qm_l, km_l, _, vt_l = mla_prep(*mla_args, row0=RC, n_rows=RL, rows_per_batch=S)
        yb_l = mla_attention_lat(dm, qm_l, km_c, km_l, vt_c, vt_l)
        if with_ctx:
            ys_ctx = (swa_attention(dm, qa, ka, va, swa_sink[layer], latent=False),
                      mla_attention_ctx(dm, qm_c, km_c, vm_c))
        else:
            ys_ctx = None
        yg = s5_branch(dm, z, col, ssm_compact, layer, ssm_d[layer])
        b_glu = ssm_b_glu.reshape(depth, 1, 2 * ssm_w)
        gl_tn = 1024
        yc = mm1(yg, [(ssm_w_glu, (layer,), 0), (ssm_w_glu, (layer,), ssm_w)], _epi_glu_bias,
                 n_rows=R, n_cols=ssm_w, tm=tm, tn=gl_tn, out_dtype=BF16,
                 extras=[(b_glu, (None, 1, gl_tn), lambda j, i: (layer, 0, j)),
                         (b_glu, (None, 1, gl_tn), lambda j, i: (layer, 0, j + ssm_w // gl_tn))],
                 prologue=lambda a: a.astype(BF16), name="ssm_glu")
        mixed = merge_branches(dm, (ya_l, yb_l, yc), (RC, RC, 0), gates, w_branch, layer, row0=row0, n_rows=n_rows,
                               ys_ctx=ys_ctx)
        t0 = row0 // tm
        gate_extra = (g_m, (1, 1, 1024), lambda j, i: (dm.mod_row(i + t0, tm), 0, j))
        if split_input and with_ctx:
            nct = RC // tm
            res_epi = functools.partial(_epi_residual_two_sources, n_ctx_tiles=nct)
            res_extras = [(x_ctx0, (tm, 1024), lambda j, i: (jnp.minimum(i, nct - 1), j)),
                          (xall, (tm, 1024), lambda j, i: (jnp.maximum(i - nct, 0), j)), gate_extra]
        elif split_input:
            res_epi = _epi_residual
            res_extras = [(xall, (tm, 1024), lambda j, i: (i, j)), gate_extra]
        else:
            res_epi = _epi_residual
            res_extras = [(xall, (tm, 1024), lambda j, i: (i + t0, j)), gate_extra]
        x1 = mm1(mixed, [(w_out, (layer,), 0)], res_epi, n_rows=n_rows, n_cols=D, tm=tm, tn=1024, out_dtype=F32,
                 extras=res_extras, name="w_out")
        is_moe = layer % 2 == 1
        h2 = modulate(dm, x1, norm_ffn_g[layer], sh_f, sc_f, mod_row0=row0, pack=is_moe)
        if not is_moe:
            F = ffn_w13.shape[-1] // 2
            f_tn = _pick_tile((512, 256, 128), F)
            act = mm1(h2, [(ffn_w13, (layer // 2,), 0), (ffn_w13, (layer // 2,), F)], _epi_swiglu, n_rows=n_rows,
                      n_cols=F, tm=tm, tn=f_tn, out_dtype=BF16, name="ffn_w13")
            x2 = mm2_residual(dm, act, ffn_w2[layer // 2].astype(BF16), x1, g_f, mod_row0=row0, tm=tm, tk=f_tn)
        else:
            if with_ctx:
                raise NotImplementedError("a mixture-of-experts layer that still feeds context rows")
            x2 = moe_ffn(dm, h2, x1, g_f, moe_w_router[layer // 2], moe_b_router[layer // 2], moe_w13, moe_w2,
                         layer // 2, mod_row0=row0)
        xall = x2
    return xall.reshape(B, S, D)


def kernel(x, c, ctx, c_ctx, mod_w, mod_b, norm_mix_g, norm_ffn_g, w_in, swa_q_norm_g, swa_k_norm_g, swa_sink, mla_q_a_norm_g, mla_w_uq, mla_kv_a_norm_g, mla_w_ukv, mla_q_norm_g, mla_k_norm_g, ssm_lam_re, ssm_lam_im, ssm_log_step, ssm_b_re, ssm_b_im, ssm_c_re, ssm_c_im, ssm_d, ssm_w_glu, ssm_b_glu, w_branch, w_out, ffn_w13, ffn_w2, moe_w_router, moe_b_router, moe_w13, moe_w2):
    return _trunk(x, c, ctx, c_ctx, mod_w, mod_b, norm_mix_g, norm_ffn_g, w_in, swa_q_norm_g, swa_k_norm_g, swa_sink,
                  mla_q_a_norm_g, mla_w_uq, mla_kv_a_norm_g, mla_w_ukv, mla_q_norm_g, mla_k_norm_g,
                  ssm_lam_re, ssm_lam_im, ssm_log_step, ssm_b_re, ssm_b_im, ssm_c_re, ssm_c_im,
                  ssm_d, ssm_w_glu, ssm_b_glu, w_branch, w_out, ffn_w13, ffn_w2, moe_w_router, moe_b_router,
                  moe_w13, moe_w2)
```

```python
import functools
import math

import jax
import jax.numpy as jnp
from jax import lax
from jax.experimental import pallas as pl
from jax.experimental.pallas import tpu as pltpu

F32 = jnp.float32
BF16 = jnp.bfloat16

GRID_W = 64
ROPE_THETA = 10000.0
EPS = 1e-6
SWA_HEADS = 8
SWA_KV_HEADS = 2
SWA_HEAD_DIM = 128
SWA_WINDOW = 128
MLA_HEADS = 8
MLA_NOPE = 128
MLA_ROPE = 64
MLA_V = 128
MLA_QK = MLA_NOPE + MLA_ROPE
MLA_QK_PAD = 256
SSM_GROUP = 16
SSM_STATE = 64
SSM_CHUNK = 16
N_BRANCH = 3
N_EXPERTS = 8
TOP_K = 2
LANE = 128
VMEM_LIMIT_BYTES = 56 * 1024 * 1024
MOE_TM = 512
NEG_BIG = -1e30


def _params(n_grid):
    return pltpu.CompilerParams(dimension_semantics=("arbitrary",) * n_grid, vmem_limit_bytes=VMEM_LIMIT_BYTES)


def _pick_tile(candidates, *sizes):
    for t in candidates:
        if all(s % t == 0 for s in sizes):
            return t
    raise ValueError(f"no tile in {candidates} divides {sizes}")


class Dims:
    def __init__(self, batch, seq, ctx_len):
        self.B, self.S, self.L = batch, seq, ctx_len
        self.RC = batch * ctx_len
        self.RL = batch * seq
        self.R = self.RC + self.RL
        self.tm = _pick_tile((1024, 512, 256, 128), ctx_len * batch, seq)

    def mod_row(self, tile, tm):
        nct = self.RC // tm
        return jnp.where(tile < nct, 0, 1 + (tile - nct) // (self.S // tm))


def _silu(x):
    return x * (1.0 / (1.0 + jnp.exp(-x)))


def _sigmoid(x):
    return 1.0 / (1.0 + jnp.exp(-x))


def _gelu_tanh(x):
    c = math.sqrt(2.0 / math.pi)
    return 0.5 * x * (1.0 + jnp.tanh(c * (x + 0.044715 * (x * x * x))))


def _dot(a, b):
    return jnp.dot(a, b, preferred_element_type=F32)


def _dot_nt(a, b):
    return lax.dot_general(a, b, (((1,), (1,)), ((), ())), preferred_element_type=F32)


def _store_row_tiles(ref, val):
    m, w = val.shape
    n = w // LANE
    for c in range(n):
        ref[pl.ds(c, m, stride=n), :] = val[:, c * LANE:(c + 1) * LANE]


def _load_row_tiles(ref, n):
    if len(ref.shape) == 2:
        m = ref.shape[0] // n
        return jnp.concatenate([ref[pl.ds(c, m, stride=n), :] for c in range(n)], axis=1)
    m = ref.shape[1] // n
    return jnp.concatenate([ref[j, pl.ds(c, m, stride=n), :] for j in range(ref.shape[0]) for c in range(n)], axis=1)


def _modulate_kernel(*refs, pack, n_ctx_tiles):
    if n_ctx_tiles is None:
        x_ref, g_ref, sh_ref, sc_ref, o_ref = refs
        x = x_ref[...]
    else:
        xc_ref, xl_ref, g_ref, sh_ref, sc_ref, o_ref = refs
        x = jnp.where(pl.program_id(0) < n_ctx_tiles, xc_ref[...], xl_ref[...])
    ms = jnp.mean(x * x, axis=-1, keepdims=True)
    y = x * lax.rsqrt(ms + EPS) * g_ref[...]
    y = y * (1.0 + sc_ref[0]) + sh_ref[0]
    if pack:
        _store_row_tiles(o_ref, y)
    else:
        o_ref[...] = y.astype(o_ref.dtype)


def modulate(dm, x, g, shift, scale, *, mod_row0, pack=False, x_ctx=None):
    D = x.shape[1]
    tm = min(dm.tm, 512)
    t0 = mod_row0 // tm
    if x_ctx is None:
        n_rows = x.shape[0]
        n_ctx_tiles = None
        x_specs = [pl.BlockSpec((tm, D), lambda i: (i, 0))]
        xs = (x,)
    else:
        n_rows = x_ctx.shape[0] + x.shape[0]
        n_ctx_tiles = x_ctx.shape[0] // tm
        x_specs = [pl.BlockSpec((tm, D), lambda i: (jnp.minimum(i, n_ctx_tiles - 1), 0)),
                   pl.BlockSpec((tm, D), lambda i: (jnp.maximum(i - n_ctx_tiles, 0), 0))]
        xs = (x_ctx, x)
    if pack:
        n = D // LANE
        out_spec = pl.BlockSpec((tm * n, LANE), lambda i: (i, 0))
        out_shape = jax.ShapeDtypeStruct((n_rows * n, LANE), F32)
    else:
        out_spec = pl.BlockSpec((tm, D), lambda i: (i, 0))
        out_shape = jax.ShapeDtypeStruct((n_rows, D), BF16)
    return pl.pallas_call(
        functools.partial(_modulate_kernel, pack=pack, n_ctx_tiles=n_ctx_tiles),
        grid=(n_rows // tm,),
        in_specs=x_specs + [
            pl.BlockSpec((1, D), lambda i: (0, 0)),
            pl.BlockSpec((1, 1, D), lambda i: (dm.mod_row(i + t0, tm), 0, 0)),
            pl.BlockSpec((1, 1, D), lambda i: (dm.mod_row(i + t0, tm), 0, 0)),
        ],
        out_specs=out_spec,
        out_shape=out_shape,
        compiler_params=_params(1),
        name="modulate_packed" if pack else "modulate",
    )(*xs, g.reshape(1, D), shift, scale)


def _mm1_kernel(*refs, n_w, n_extra, epilogue, prologue):
    a_ref = refs[0]
    w_refs = refs[1:1 + n_w]
    extra = refs[1 + n_w:1 + n_w + n_extra]
    o_ref = refs[1 + n_w + n_extra]
    wb = refs[2 + n_w + n_extra:]

    @pl.when(pl.program_id(1) == 0)
    def _():
        for w_ref, b in zip(w_refs, wb):
            b[...] = w_ref[...].astype(BF16)

    a = a_ref[...]
    if prologue is not None:
        a = prologue(a)
    accs = [_dot(a, b[...]) for b in wb]
    o_ref[...] = epilogue(accs, *extra).astype(o_ref.dtype)


def mm1(a, weights, epilogue, *, n_rows, n_cols, tm, tn, out_dtype, a_row0=0, extras=(), prologue=None, name):
    K = a.shape[1]
    t0 = a_row0 // tm
    in_specs = [pl.BlockSpec((tm, K), lambda j, i: (i + t0, 0))]
    operands = [a]
    for w, lead, col0 in weights:
        c0 = col0 // tn
        in_specs.append(pl.BlockSpec((None,) * len(lead) + (K, tn),
                                     functools.partial(lambda j, i, lead, c0: lead + (0, j + c0), lead=lead, c0=c0)))
        operands.append(w)
    for arr, bshape, imap in extras:
        in_specs.append(pl.BlockSpec(bshape, imap))
        operands.append(arr)
    kern = functools.partial(_mm1_kernel, n_w=len(weights), n_extra=len(extras), epilogue=epilogue,
                             prologue=prologue)
    return pl.pallas_call(
        kern,
        grid=(n_cols // tn, n_rows // tm),
        in_specs=in_specs,
        out_specs=pl.BlockSpec((tm, tn), lambda j, i: (i, j)),
        out_shape=jax.ShapeDtypeStruct((n_rows, n_cols), out_dtype),
        scratch_shapes=[pltpu.VMEM((K, tn), BF16) for _ in weights],
        compiler_params=_params(2),
        name=name,
    )(*operands)


def _epi_id(accs):
    return accs[0]


def _epi_sigmoid(accs):
    return _sigmoid(accs[0])


def _epi_swiglu(accs):
    return _silu(accs[0]) * accs[1]


def _epi_bias(accs, b_ref):
    return accs[0] + b_ref[...]


def _epi_glu_bias(accs, ba_ref, bb_ref):
    return (accs[0] + ba_ref[...]) * _sigmoid(accs[1] + bb_ref[...])


def _epi_residual(accs, x_ref, gate_ref):
    return x_ref[...] + gate_ref[0] * accs[0]


def _epi_residual_two_sources(accs, xc_ref, xl_ref, gate_ref, *, n_ctx_tiles):
    x = jnp.where(pl.program_id(1) < n_ctx_tiles, xc_ref[...], xl_ref[...])
    return x + gate_ref[0] * accs[0]


def _mm2_kernel(a_ref, w_ref, x_ref, gate_ref, o_ref, acc_ref):
    k = pl.program_id(1)

    @pl.when(k == 0)
    def _():
        acc_ref[...] = jnp.zeros_like(acc_ref)

    acc_ref[...] += _dot(a_ref[...], w_ref[...])

    @pl.when(k == pl.num_programs(1) - 1)
    def _():
        o_ref[...] = x_ref[...] + gate_ref[0] * acc_ref[...]


def mm2_residual(dm, a, w, x, gate, *, mod_row0, tm, tk):
    M, K = a.shape
    N = w.shape[1]
    t0 = mod_row0 // tm
    return pl.pallas_call(
        _mm2_kernel,
        grid=(M // tm, K // tk),
        in_specs=[
            pl.BlockSpec((tm, tk), lambda i, k: (i, k)),
            pl.BlockSpec((tk, N), lambda i, k: (k, 0)),
            pl.BlockSpec((tm, N), lambda i, k: (i, 0)),
            pl.BlockSpec((1, 1, N), lambda i, k: (dm.mod_row(i + t0, tm), 0, 0)),
        ],
        out_specs=pl.BlockSpec((tm, N), lambda i, k: (i, 0)),
        out_shape=jax.ShapeDtypeStruct((M, N), F32),
        scratch_shapes=[pltpu.VMEM((tm, N), F32)],
        compiler_params=_params(2),
        name="mm2_residual",
    )(a, w, x, gate)


def _swa_prep_kernel(q_ref, k_ref, v_ref, gq_ref, gk_ref, cos_ref, sin_ref, qo_ref, ko_ref, vo_ref):
    c = cos_ref[...]
    s = sin_ref[...]

    def norm_rope(x, g, scale):
        ms = jnp.mean(x * x, axis=-1, keepdims=True)
        y = x * lax.rsqrt(ms + EPS) * g
        return (y * c + pltpu.roll(y, SWA_HEAD_DIM // 2, 1) * s) * scale

    gq = gq_ref[...]
    gk = gk_ref[...]
    for h in range(SWA_HEADS):
        sl = slice(h * SWA_HEAD_DIM, (h + 1) * SWA_HEAD_DIM)
        qo_ref[:, sl] = norm_rope(q_ref[:, sl], gq, SWA_HEAD_DIM ** -0.5).astype(BF16)
    for h in range(SWA_KV_HEADS):
        sl = slice(h * SWA_HEAD_DIM, (h + 1) * SWA_HEAD_DIM)
        ko_ref[:, sl] = norm_rope(k_ref[:, sl], gk, 1.0).astype(BF16)
    vo_ref[...] = v_ref[...].astype(BF16)


def _rope_tile_index(dm, tm):
    nct = dm.RC // tm
    return lambda i: (jnp.where(i < nct, 0, 1 + (i - nct) % (dm.S // tm)), 0)


def swa_prep(dm, z, col, gq, gk, cos_t, sin_t):
    tm = min(dm.tm, 256)
    QW = SWA_HEADS * SWA_HEAD_DIM
    KW = SWA_KV_HEADS * SWA_HEAD_DIM
    ridx = _rope_tile_index(dm, tm)
    return pl.pallas_call(
        _swa_prep_kernel,
        grid=(dm.R // tm,),
        in_specs=[
            pl.BlockSpec((tm, QW), lambda i: (i, col["q"] // QW)),
            pl.BlockSpec((tm, KW), lambda i: (i, col["k"] // KW)),
            pl.BlockSpec((tm, KW), lambda i: (i, col["v"] // KW)),
            pl.BlockSpec((1, SWA_HEAD_DIM), lambda i: (0, 0)),
            pl.BlockSpec((1, SWA_HEAD_DIM), lambda i: (0, 0)),
            pl.BlockSpec((tm, SWA_HEAD_DIM), ridx),
            pl.BlockSpec((tm, SWA_HEAD_DIM), ridx),
        ],
        out_specs=[
            pl.BlockSpec((tm, QW), lambda i: (i, 0)),
            pl.BlockSpec((tm, KW), lambda i: (i, 0)),
            pl.BlockSpec((tm, KW), lambda i: (i, 0)),
        ],
        out_shape=[
            jax.ShapeDtypeStruct((dm.R, QW), BF16),
            jax.ShapeDtypeStruct((dm.R, KW), BF16),
            jax.ShapeDtypeStruct((dm.R, KW), BF16),
        ],
        compiler_params=_params(1),
        name="swa_prep",
    )(z, z, z, gq.reshape(1, -1), gk.reshape(1, -1), cos_t, sin_t)


def _swa_attn_kernel(*refs, windowed, nb):
    if windowed:
        q_ref, kc_ref, kp_ref, kk_ref, kn_ref, vc_ref, vp_ref, vk_ref, vn_ref, sink_ref, o_ref = refs
    else:
        q_ref, kc_ref, vc_ref, sink_ref, o_ref = refs
    G = SWA_HEADS // SWA_KV_HEADS
    blk = q_ref.shape[0]
    Dh = SWA_HEAD_DIM
    q = jnp.concatenate([q_ref[:, g * Dh:(g + 1) * Dh] for g in range(G)], axis=0)
    sink = sink_ref[0][:, 0:1]
    scores = [_dot_nt(q, kc_ref[...])]
    values = [vc_ref[...]]
    if windowed:
        n = pl.program_id(2)
        qi = lax.broadcasted_iota(jnp.int32, (G * blk, blk), 0) % blk
        kj = lax.broadcasted_iota(jnp.int32, (G * blk, blk), 1)
        s_p = _dot_nt(q, kp_ref[...])
        s_p = jnp.where(kj >= qi, s_p, NEG_BIG)
        s_p = jnp.where(n >= 1, s_p, NEG_BIG)
        s_n = _dot_nt(q, kn_ref[...])
        s_n = jnp.where(kj <= qi, s_n, NEG_BIG)
        s_n = jnp.where(n <= nb - 2, s_n, NEG_BIG)
        scores += [s_p, _dot_nt(q, kk_ref[...]), s_n]
        values += [vp_ref[...], vk_ref[...], vn_ref[...]]
    m = sink
    for s in scores:
        m = jnp.maximum(m, jnp.max(s, axis=-1, keepdims=True))
    l = jnp.exp(sink - m)
    o = None
    for s, v in zip(scores, values):
        p = jnp.exp(s - m)
        l = l + jnp.sum(p, axis=-1, keepdims=True)
        pv = _dot(p.astype(BF16), v)
        o = pv if o is None else o + pv
    o = o / l
    for g in range(G):
        o_ref[:, g * Dh:(g + 1) * Dh] = o[g * blk:(g + 1) * blk].astype(o_ref.dtype)


def swa_attention(dm, qa, ka, va, sink, *, latent):
    G = SWA_HEADS // SWA_KV_HEADS
    Dh = SWA_HEAD_DIM
    blk = SWA_WINDOW
    L = dm.L
    sink_col = jnp.broadcast_to(sink.astype(F32).reshape(SWA_KV_HEADS, G, 1, 1),
                                (SWA_KV_HEADS, G, blk, LANE)).reshape(SWA_KV_HEADS, G * blk, LANE)
    sink_spec = pl.BlockSpec((1, G * blk, LANE), lambda b, h, n: (h, 0, 0))
    ctx_spec = pl.BlockSpec((L, Dh), lambda b, h, n: (b, h))
    if latent:
        nb = dm.S // blk
        base = dm.RC // blk

        def q_map(b, h, n):
            return (base + b * nb + n, h)

        def kv_map(off):
            return lambda b, h, n: (base + b * nb + jnp.clip(n + off, 0, nb - 1), h)

        win_specs = [pl.BlockSpec((blk, Dh), kv_map(off)) for off in (-1, 0, 1)]
        in_specs = ([pl.BlockSpec((blk, G * Dh), q_map), ctx_spec] + win_specs + [ctx_spec] + win_specs
                    + [sink_spec])
        operands = (qa, ka, ka, ka, ka, va, va, va, va, sink_col)
        n_out = dm.RL
    else:
        nb = L // blk
        in_specs = [pl.BlockSpec((blk, G * Dh), lambda b, h, n: (b * nb + n, h)), ctx_spec, ctx_spec, sink_spec]
        operands = (qa, ka, va, sink_col)
        n_out = dm.RC
    return pl.pallas_call(
        functools.partial(_swa_attn_kernel, windowed=latent, nb=nb),
        grid=(dm.B, SWA_KV_HEADS, nb),
        in_specs=in_specs,
        out_specs=pl.BlockSpec((blk, G * Dh), lambda b, h, n: (b * nb + n, h)),
        out_shape=jax.ShapeDtypeStruct((n_out, SWA_HEADS * Dh), BF16),
        compiler_params=_params(3),
        name="swa_attn_lat" if latent else "swa_attn_ctx",
    )(*operands)


def _mla_prep_kernel(cq_ref, ckv_ref, pe_ref, gqa_ref, gkva_ref, wq_ref, wkv_ref, gq_ref, gk_ref, c_ref, s1_ref,
                     s2_ref, qo_ref, ko_ref, vo_ref, vt_ref, wq_s, wkv_s):
    @pl.when(pl.program_id(0) == 0)
    def _():
        wq_s[...] = wq_ref[...].astype(BF16)
        wkv_s[...] = wkv_ref[...].astype(BF16)

    def rms(x_ref, g_ref):
        x = x_ref[...]
        ms = jnp.mean(x * x, axis=-1, keepdims=True)
        return (x * lax.rsqrt(ms + EPS) * g_ref[...]).astype(BF16)

    qf = _dot(rms(cq_ref, gqa_ref), wq_s[...])
    kvf = _dot(rms(ckv_ref, gkva_ref), wkv_s[...])
    c = c_ref[...]
    s1 = s1_ref[...]
    s2 = s2_ref[...]
    gq = gq_ref[...]
    gk = gk_ref[...]
    scale = MLA_QK ** -0.5 * math.log2(math.e)

    def rope(x):
        return x * c + pltpu.roll(x, LANE - MLA_ROPE // 2, 1) * s1 + pltpu.roll(x, MLA_ROPE // 2, 1) * s2

    pe = pe_ref[...]
    pe_ss = jnp.sum(pe * pe, axis=-1, keepdims=True)
    NW = MLA_HEADS * MLA_NOPE
    for h in range(MLA_HEADS):
        lo = h * MLA_QK_PAD
        qh = qf[:, lo:lo + MLA_QK_PAD]
        inv = lax.rsqrt(jnp.sum(qh * qh, axis=-1, keepdims=True) * (1.0 / MLA_QK) + EPS)
        qn = qh * inv * gq
        qo_ref[:, lo:lo + MLA_NOPE] = (qn[:, :MLA_NOPE] * scale).astype(BF16)
        qo_ref[:, lo + MLA_NOPE:lo + MLA_QK_PAD] = (rope(qn[:, MLA_NOPE:]) * scale).astype(BF16)
        kh = kvf[:, h * MLA_NOPE:(h + 1) * MLA_NOPE]
        inv = lax.rsqrt((jnp.sum(kh * kh, axis=-1, keepdims=True) + pe_ss) * (1.0 / MLA_QK) + EPS)
        ko_ref[:, lo:lo + MLA_NOPE] = (kh * inv * gk[:, :MLA_NOPE]).astype(BF16)
        ko_ref[:, lo + MLA_NOPE:lo + MLA_QK_PAD] = rope(pe * inv * gk[:, MLA_NOPE:]).astype(BF16)
    v = kvf[:, NW:]
    vo_ref[...] = v.astype(BF16)
    vt_ref[...] = v.T.astype(BF16)


def mla_prep(dm, z, col, gqa, gkva, w_uq, w_ukv, gq_pad, gk_pad, tabs, *, row0, n_rows, rows_per_batch):
    tm = min(dm.tm, 256)
    t0 = row0 // tm
    QW = MLA_HEADS * MLA_QK_PAD
    NW = MLA_HEADS * MLA_NOPE
    q_rank, kv_rank = w_uq.shape[0], w_ukv.shape[0]
    ridx = _rope_tile_index(dm, tm)
    rspec = pl.BlockSpec((tm, LANE), lambda i: ridx(i + t0))
    tpb = rows_per_batch // tm
    const = lambda i: (0, 0)
    return pl.pallas_call(
        _mla_prep_kernel,
        grid=(n_rows // tm,),
        in_specs=[
            pl.BlockSpec((tm, q_rank), lambda i: (i + t0, col["c_q"] // q_rank)),
            pl.BlockSpec((tm, kv_rank), lambda i: (i + t0, col["c_kv"] // kv_rank)),
            pl.BlockSpec((tm, LANE), lambda i: (i + t0, col["kpe"] // LANE)),
            pl.BlockSpec((1, q_rank), const),
            pl.BlockSpec((1, kv_rank), const),
            pl.BlockSpec((q_rank, QW), const),
            pl.BlockSpec((kv_rank, 2 * NW), const),
            pl.BlockSpec((1, MLA_QK_PAD), const),
            pl.BlockSpec((1, MLA_QK_PAD), const),
            rspec, rspec, rspec,
        ],
        out_specs=[
            pl.BlockSpec((tm, QW), lambda i: (i, 0)),
            pl.BlockSpec((tm, QW), lambda i: (i, 0)),
            pl.BlockSpec((tm, NW), lambda i: (i, 0)),
            pl.BlockSpec((NW, tm), lambda i: (i // tpb, i % tpb)),
        ],
        out_shape=[
            jax.ShapeDtypeStruct((n_rows, QW), BF16),
            jax.ShapeDtypeStruct((n_rows, QW), BF16),
            jax.ShapeDtypeStruct((n_rows, NW), BF16),
            jax.ShapeDtypeStruct((n_rows // rows_per_batch * NW, rows_per_batch), BF16),
        ],
        scratch_shapes=[pltpu.VMEM((q_rank, QW), BF16), pltpu.VMEM((kv_rank, 2 * NW), BF16)],
        compiler_params=_params(1),
        name="mla_prep",
    )(z, z, z, gqa.reshape(1, -1), gkva.reshape(1, -1), w_uq, w_ukv, gq_pad, gk_pad, *tabs)


MLA_KEY_CHUNK = 512


def _mla_attn_ctx_kernel(q_ref, kc_ref, vc_ref, o_ref):
    s = _dot_nt(q_ref[...], kc_ref[...])
    p = jnp.exp2(s - jnp.max(s, axis=-1, keepdims=True))
    o = _dot(p.astype(BF16), vc_ref[...])
    o_ref[...] = (o / jnp.sum(p, axis=-1, keepdims=True)).astype(o_ref.dtype)


def _mla_attn_lat_kernel(q_ref, kc_ref, kl_ref, vct_ref, vlt_ref, o_ref, s_scr):
    tq = q_ref.shape[0]
    q = q_ref[...]
    L, S = kc_ref.shape[0], kl_ref.shape[0]
    tk = min(MLA_KEY_CHUNK, S)
    chunks = [(kc_ref, vct_ref, 0, L, 0)] + [(kl_ref, vlt_ref, c * tk, tk, L + c * tk) for c in range(S // tk)]
    mx = jnp.full((tq, LANE), NEG_BIG, F32)
    for k_ref, _, off, w, so in chunks:
        s = _dot_nt(q, k_ref[off:off + w, :])
        s_scr[:, so:so + w] = s
        for g in range(w // LANE):
            mx = jnp.maximum(mx, s[:, g * LANE:(g + 1) * LANE])
    m = jnp.max(mx, axis=-1, keepdims=True)
    ls = jnp.zeros((tq, LANE), F32)
    acc = jnp.zeros((vct_ref.shape[0], tq), F32)
    for _, vt_ref, off, w, so in chunks:
        p = jnp.exp2(s_scr[:, so:so + w] - m)
        for g in range(w // LANE):
            ls = ls + p[:, g * LANE:(g + 1) * LANE]
        acc = acc + _dot_nt(vt_ref[:, off:off + w], p.astype(BF16))
    l = jnp.sum(ls, axis=-1, keepdims=True)
    o_ref[...] = (acc.T / l).astype(o_ref.dtype)


def mla_attention_ctx(dm, q, kc, vc):
    L = dm.L
    QP, V = MLA_QK_PAD, MLA_V
    return pl.pallas_call(
        _mla_attn_ctx_kernel,
        grid=(dm.B, MLA_HEADS),
        in_specs=[pl.BlockSpec((L, QP), lambda b, h: (b, h)),
                  pl.BlockSpec((L, QP), lambda b, h: (b, h)),
                  pl.BlockSpec((L, V), lambda b, h: (b, h))],
        out_specs=pl.BlockSpec((L, V), lambda b, h: (b, h)),
        out_shape=jax.ShapeDtypeStruct((dm.RC, MLA_HEADS * V), BF16),
        compiler_params=_params(2),
        name="mla_attn_ctx",
    )(q, kc, vc)


def mla_attention_lat(dm, q, kc, kl, vct, vlt):
    L, S = dm.L, dm.S
    tq = min(512, S)
    nq = S // tq
    QP, V = MLA_QK_PAD, MLA_V
    H = MLA_HEADS
    return pl.pallas_call(
        _mla_attn_lat_kernel,
        grid=(dm.B, H, nq),
        in_specs=[pl.BlockSpec((tq, QP), lambda b, h, n: (b * nq + n, h)),
                  pl.BlockSpec((L, QP), lambda b, h, n: (b, h)),
                  pl.BlockSpec((S, QP), lambda b, h, n: (b, h)),
                  pl.BlockSpec((V, L), lambda b, h, n: (b * H + h, 0)),
                  pl.BlockSpec((V, S), lambda b, h, n: (b * H + h, 0))],
        out_specs=pl.BlockSpec((tq, V), lambda b, h, n: (b * nq + n, h)),
        out_shape=jax.ShapeDtypeStruct((dm.RL, H * V), BF16),
        scratch_shapes=[pltpu.VMEM((tq, L + S), F32)],
        compiler_params=_params(3),
        name="mla_attn_lat",
    )(q, kc, kl, vct, vlt)


SSM_SUPER = 16
SSM_BLOCK_GROUPS = LANE // SSM_GROUP


def _ssm_in_kernel(u_ref, wi_ref, ws_ref, y_ref, s_ref):
    a = _load_row_tiles(u_ref, SSM_CHUNK).astype(BF16)
    y_ref[...] = _dot(a, wi_ref[...])
    zs = _dot(a, ws_ref[...])
    for c in range(s_ref.shape[0]):
        s_ref[c] = zs[:, c * LANE:(c + 1) * LANE]


def ssm_chunk_in(z, u_col, w_intra, w_state, *, tr):
    R = z.shape[0]
    nblk, CW, _ = w_intra.shape
    nr = tr // SSM_CHUNK
    c0 = u_col // LANE
    w_spec = pl.BlockSpec((None, CW, CW), lambda j, i: (j, 0, 0), pipeline_mode=pl.Buffered(1))
    return pl.pallas_call(
        _ssm_in_kernel,
        grid=(nblk, R // tr),
        in_specs=[pl.BlockSpec((tr, LANE), lambda j, i: (i, c0 + j)), w_spec, w_spec],
        out_specs=[pl.BlockSpec((nr, CW), lambda j, i: (i, j)),
                   pl.BlockSpec((2 * SSM_BLOCK_GROUPS, nr, LANE), lambda j, i: (0, i, j))],
        out_shape=[jax.ShapeDtypeStruct((R // SSM_CHUNK, nblk * CW), F32),
                   jax.ShapeDtypeStruct((2 * SSM_BLOCK_GROUPS, R // SSM_CHUNK, nblk * LANE), F32)],
        compiler_params=_params(2),
        name="ssm_chunk_in",
    )(z, w_intra, w_state)


def _ssm_scan_kernel(s_ref, p1_ref, p2_ref, x_ref, t_ref, e_ref, *, batch, n_ctx_sc, n_lat_sc):
    SC = SSM_SUPER
    GB = SSM_BLOCK_GROUPS
    FWD, BWD = slice(0, GB), slice(GB, 2 * GB)
    n_sc = s_ref.shape[1] // SC

    def cmul(i, rows, x):
        swapped = jnp.concatenate([x[..., SSM_STATE:], x[..., :SSM_STATE]], axis=-1)
        return p1_ref[i, rows] * x + p2_ref[i, rows] * swapped

    def chunk(i):
        return pl.ds(i, n_sc, stride=SC)

    lf = jnp.zeros((GB, n_sc, LANE), F32)
    lb = jnp.zeros((GB, n_sc, LANE), F32)
    for i in range(SC):
        x_ref[FWD, chunk(i), :] = lf
        x_ref[BWD, chunk(SC - 1 - i), :] = lb
        lf = cmul(1, FWD, lf) + s_ref[FWD, chunk(i), :]
        lb = cmul(1, BWD, lb) + s_ref[BWD, chunk(SC - 1 - i), :]
    t_ref[FWD] = lf
    t_ref[BWD] = lb
    n_ctx = batch * n_ctx_sc
    for rows, order in ((FWD, 1), (BWD, -1)):
        e = jnp.zeros((GB, batch, LANE), F32)
        for region_start, per_batch in ((0, n_ctx_sc), (n_ctx, n_lat_sc)):
            steps = range(per_batch) if order == 1 else range(per_batch - 1, -1, -1)
            for m in steps:
                idx = pl.ds(region_start + m, batch, stride=per_batch)
                e_ref[rows, idx, :] = e
                e = cmul(SC, rows, e) + t_ref[rows, idx, :]
    ef = e_ref[FWD]
    eb = e_ref[BWD]
    for i in range(SC):
        x_ref[FWD, chunk(i), :] += cmul(i, FWD, ef)
        x_ref[BWD, chunk(SC - 1 - i), :] += cmul(i, BWD, eb)


def ssm_scan(dm, s, p1, p2, layer):
    NS, NR, W = s.shape
    nblk = W // LANE
    n_sc = NR // SSM_SUPER
    n_ctx_sc = dm.L // (SSM_CHUNK * SSM_SUPER)
    n_lat_sc = dm.S // (SSM_CHUNK * SSM_SUPER)
    blk = pl.BlockSpec((NS, NR, LANE), lambda j: (0, 0, j))
    pspec = pl.BlockSpec((None, None, SSM_SUPER + 1, NS, 1, LANE), lambda j: (layer, j, 0, 0, 0, 0))
    return pl.pallas_call(
        functools.partial(_ssm_scan_kernel, batch=dm.B, n_ctx_sc=n_ctx_sc, n_lat_sc=n_lat_sc),
        grid=(nblk,),
        in_specs=[blk, pspec, pspec],
        out_specs=blk,
        out_shape=jax.ShapeDtypeStruct(s.shape, F32),
        scratch_shapes=[pltpu.VMEM((NS, n_sc, LANE), F32), pltpu.VMEM((NS, n_sc, LANE), F32)],
        compiler_params=_params(1),
        name="ssm_scan",
    )(s, p1, p2)


def _ssm_out_kernel(y_ref, x_ref, u_ref, w_ref, d_ref, o_ref):
    nr = y_ref.shape[0]
    xs = jnp.concatenate([x_ref[c] for c in range(x_ref.shape[0])], axis=1).astype(BF16)
    y = y_ref[...] + _dot(xs, w_ref[...])
    d = d_ref[...]
    for t in range(SSM_CHUNK):
        rows = pl.ds(t, nr, stride=SSM_CHUNK)
        o_ref[rows, :] = _gelu_tanh(y[:, t * LANE:(t + 1) * LANE] + d * u_ref[rows, :])


def ssm_chunk_out(y_intra, x_states, z, u_col, w_out_state, d_skip, *, tr):
    R = z.shape[0]
    nblk, CW, _ = w_out_state.shape
    nr = tr // SSM_CHUNK
    c0 = u_col // LANE
    return pl.pallas_call(
        _ssm_out_kernel,
        grid=(nblk, R // tr),
        in_specs=[pl.BlockSpec((nr, CW), lambda j, i: (i, j)),
                  pl.BlockSpec((2 * SSM_BLOCK_GROUPS, nr, LANE), lambda j, i: (0, i, j)),
                  pl.BlockSpec((tr, LANE), lambda j, i: (i, c0 + j)),
                  pl.BlockSpec((None, CW, CW), lambda j, i: (j, 0, 0), pipeline_mode=pl.Buffered(1)),
                  pl.BlockSpec((1, LANE), lambda j, i: (0, j))],
        out_specs=pl.BlockSpec((tr, LANE), lambda j, i: (i, j)),
        out_shape=jax.ShapeDtypeStruct((R, nblk * LANE), F32),
        compiler_params=_params(2),
        name="ssm_chunk_out",
    )(y_intra, x_states, z, w_out_state, d_skip.astype(F32).reshape(1, -1))


def _ssm_expand_kernel(k_ref, o_ref, *, mode):
    C, H, GB = SSM_CHUNK, SSM_GROUP, SSM_BLOCK_GROUPS
    CH = C * H
    W = o_ref.shape[1]
    ri = lax.broadcasted_iota(jnp.int32, (CH, W), 0)
    ci = lax.broadcasted_iota(jnp.int32, (CH, W), 1)
    for gl in range(GB):
        kc = k_ref[gl].astype(BF16)
        if mode == "state":
            zero = jnp.zeros((CH, LANE), BF16)
            cols = [kc[:, d * LANE:(d + 1) * LANE] if g2 == gl else zero for d in range(2) for g2 in range(GB)]
            t = jnp.concatenate(cols, axis=1)
        else:
            sel = jnp.where(ci == (ri >> 4) * LANE + gl * H + (ri & (H - 1)), 1.0, 0.0).astype(BF16)
            t = _dot(kc, sel).astype(BF16)
        if mode == "out":
            for d in range(2):
                o_ref[d * GB * LANE + gl * LANE:d * GB * LANE + (gl + 1) * LANE, :] = t[d * LANE:(d + 1) * LANE, :]
        else:
            for s in range(C):
                o_ref[s * LANE + gl * H:s * LANE + (gl + 1) * H, :] = t[s * H:(s + 1) * H, :]


def _ssm_expand(compact, layer, mode):
    _, G, CH, _ = compact.shape
    GB = SSM_BLOCK_GROUPS
    CW = CH * GB
    return pl.pallas_call(
        functools.partial(_ssm_expand_kernel, mode=mode),
        grid=(G // GB,),
        in_specs=[pl.BlockSpec((None, GB, CH, CH), lambda j: (layer, j, 0, 0))],
        out_specs=pl.BlockSpec((None, CW, CW), lambda j: (j, 0, 0)),
        out_shape=jax.ShapeDtypeStruct((G // GB, CW, CW), BF16),
        compiler_params=_params(1),
        name="ssm_expand_" + mode,
    )(compact)


def _ssm_compact_tables(lam_re, lam_im, log_step, b_re, b_im, c_re, c_im):
    C, H, P = SSM_CHUNK, SSM_GROUP, SSM_STATE
    G = lam_re.shape[1]
    delta = jnp.exp(log_step.astype(F32))[..., None]
    zr = lam_re.astype(F32) * delta
    zi = lam_im.astype(F32) * delta
    k = jnp.arange(C + 1, dtype=F32)[:, None, None, None]
    mag = jnp.exp(k * zr[None])
    pw_re = mag * jnp.cos(k * zi[None])
    pw_im = mag * jnp.sin(k * zi[None])
    lb_re, lb_im = pw_re[1], pw_im[1]
    lr, li = lam_re.astype(F32), lam_im.astype(F32)
    den = lr * lr + li * li
    f_re = ((lb_re - 1.0) * lr + lb_im * li) / den
    f_im = (lb_im * lr - (lb_re - 1.0) * li) / den
    br, bi = b_re.astype(F32), b_im.astype(F32)
    bb_re = f_re[..., None] * br - f_im[..., None] * bi
    bb_im = f_re[..., None] * bi + f_im[..., None] * br
    cr, ci = c_re.astype(F32), c_im.astype(F32)
    cl_re = cr[None] * pw_re[:, :, :, None, :] - ci[None] * pw_im[:, :, :, None, :]
    cl_im = cr[None] * pw_im[:, :, :, None, :] + ci[None] * pw_re[:, :, :, None, :]
    hp = lax.Precision.HIGHEST
    kern = (jnp.einsum("kdghp,dgpj->dgkhj", cl_re[:C], bb_re, precision=hp)
            - jnp.einsum("kdghp,dgpj->dgkhj", cl_im[:C], bb_im, precision=hp))
    k_idx = jnp.arange(C)[:, None, None]
    s_idx = jnp.arange(C)[None, :, None]
    t_idx = jnp.arange(C)[None, None, :]
    sel_f = (t_idx - s_idx == k_idx).astype(F32)
    sel_b = (s_idx - t_idx == k_idx).astype(F32)
    ksum = (jnp.einsum("kst,gkhj->gsjth", sel_f, kern[0], precision=hp)
            + jnp.einsum("kst,gkhj->gsjth", sel_b, kern[1], precision=hp))

    def state_in(d, power_of_s):
        pr = pw_re[power_of_s, d]
        pi = pw_im[power_of_s, d]
        re = pr[..., None] * bb_re[d][None] - pi[..., None] * bb_im[d][None]
        im = pr[..., None] * bb_im[d][None] + pi[..., None] * bb_re[d][None]
        return jnp.concatenate([re, im], axis=2).transpose(1, 0, 3, 2)

    def state_out(d, power_of_t):
        re = cl_re[power_of_t, d]
        im = cl_im[power_of_t, d]
        return jnp.concatenate([re, -im], axis=-1).transpose(1, 3, 0, 2)

    m_sum = jnp.stack([state_in(0, C - 1 - jnp.arange(C)), state_in(1, jnp.arange(C))])
    m_out = jnp.stack([state_out(0, 1 + jnp.arange(C)), state_out(1, C - jnp.arange(C))])

    GB = SSM_BLOCK_GROUPS
    nblk = G // GB
    ksum = ksum.reshape(G, C * H, C * H)
    m_sum = m_sum.transpose(1, 2, 3, 0, 4).reshape(G, C * H, 2 * 2 * P)
    m_out = m_out.transpose(1, 0, 2, 3, 4).reshape(G, 2 * 2 * P, C * H)
    i = (C * jnp.arange(SSM_SUPER + 1, dtype=F32))[:, None, None, None]
    mag_a = jnp.exp(i * zr[None])
    pa_re = mag_a * jnp.cos(i * zi[None])
    pa_im = mag_a * jnp.sin(i * zi[None])

    def scan_table(lo, hi):
        t = jnp.concatenate([lo, hi], axis=-1).reshape(SSM_SUPER + 1, 2, nblk, GB, 2 * P)
        return t.transpose(2, 0, 1, 3, 4).reshape(nblk, SSM_SUPER + 1, 2 * GB, 1, 2 * P)

    return ksum, m_sum, m_out, scan_table(pa_re, pa_re), scan_table(-pa_im, pa_im)


def s5_branch(dm, z, col, compact, layer, d_skip):
    ksum, m_sum, m_out, p1, p2 = compact
    w_intra = _ssm_expand(ksum, layer, "intra")
    w_state = _ssm_expand(m_sum, layer, "state")
    w_out_state = _ssm_expand(m_out, layer, "out")
    nr = _pick_tile((272, 136, 96, 64, 32, 16, 8), dm.R // SSM_CHUNK)
    tr = nr * SSM_CHUNK
    y_intra, s = ssm_chunk_in(z, col["u"], w_intra, w_state, tr=tr)
    x_states = ssm_scan(dm, s, p1, p2, layer)
    return ssm_chunk_out(y_intra, x_states, z, col["u"], w_out_state, d_skip, tr=tr)


def _merge_kernel(*refs, n_ctx_tiles):
    if n_ctx_tiles is None:
        ya_ref, yb_ref, yc_ref, ga_ref, gb_ref, gc_ref, wa_ref, wb_ref, wc_ref, o_ref, sa, sb, sc = refs
        ya, yb = ya_ref[...], yb_ref[...]
    else:
        (yac_ref, ybc_ref, ya_ref, yb_ref, yc_ref, ga_ref, gb_ref, gc_ref, wa_ref, wb_ref, wc_ref, o_ref,
         sa, sb, sc) = refs
        is_ctx = pl.program_id(1) < n_ctx_tiles
        ya = jnp.where(is_ctx, yac_ref[...], ya_ref[...])
        yb = jnp.where(is_ctx, ybc_ref[...], yb_ref[...])

    @pl.when(pl.program_id(1) == 0)
    def _():
        for w_ref, s in ((wa_ref, sa), (wb_ref, sb), (wc_ref, sc)):
            s[...] = w_ref[...].astype(BF16)

    acc = ga_ref[...].astype(F32) * _dot(ya, sa[...])
    acc = acc + gb_ref[...].astype(F32) * _dot(yb, sb[...])
    acc = acc + gc_ref[...].astype(F32) * _dot(yc_ref[...], sc[...])
    o_ref[...] = acc.astype(o_ref.dtype)


def merge_branches(dm, ys, y_row0s, gates, w_branch, layer, *, row0, n_rows, ys_ctx=None):
    BW = ys[0].shape[1]
    D = w_branch.shape[-1]
    tm = min(dm.tm, 512)
    tn = 1024
    t0 = row0 // tm
    nj = D // tn
    n_ctx_tiles = None if ys_ctx is None else dm.RC // tm

    def y_index(y0):
        off = (row0 - y0) // tm
        return lambda j, i: (jnp.maximum(i + off, 0), 0)

    y_specs = [pl.BlockSpec((tm, BW), y_index(y0)) for y0 in y_row0s]
    ctx_arrays = ()
    if ys_ctx is not None:
        ctx_spec = pl.BlockSpec((tm, BW), lambda j, i: (jnp.minimum(i, n_ctx_tiles - 1), 0))
        y_specs = [ctx_spec, ctx_spec] + y_specs
        ctx_arrays = tuple(ys_ctx)
    g_specs = [pl.BlockSpec((tm, tn), functools.partial(lambda j, i, n: (i + t0, n * nj + j), n=n))
               for n in range(N_BRANCH)]
    w_specs = [pl.BlockSpec((None, None, BW, tn), functools.partial(lambda j, i, n: (layer, n, 0, j), n=n))
               for n in range(N_BRANCH)]
    return pl.pallas_call(
        functools.partial(_merge_kernel, n_ctx_tiles=n_ctx_tiles),
        grid=(nj, n_rows // tm),
        in_specs=y_specs + g_specs + w_specs,
        out_specs=pl.BlockSpec((tm, tn), lambda j, i: (i, j)),
        out_shape=jax.ShapeDtypeStruct((n_rows, D), BF16),
        scratch_shapes=[pltpu.VMEM((BW, tn), BF16)] * 3,
        compiler_params=_params(2),
        name="merge_branches",
    )(*ctx_arrays, *ys, gates, gates, gates, w_branch, w_branch, w_branch)


def _router_kernel(h_ref, whi_ref, wlo_ref, b_ref, idx_ref, w_ref):
    h = _load_row_tiles(h_ref, whi_ref.shape[0] // LANE).astype(BF16)
    logits = b_ref[...] + _dot(h, whi_ref[...]) + _dot(h, wlo_ref[...])
    lane = lax.broadcasted_iota(jnp.int32, logits.shape, 1).astype(F32)
    logits = jnp.where(lane < N_EXPERTS, logits, NEG_BIG)
    m1 = jnp.max(logits, axis=-1, keepdims=True)
    i1 = jnp.min(jnp.where(logits == m1, lane, float(LANE)), axis=-1, keepdims=True)
    rest = jnp.where(lane == i1, NEG_BIG, logits)
    m2 = jnp.max(rest, axis=-1, keepdims=True)
    i2 = jnp.min(jnp.where(rest == m2, lane, float(LANE)), axis=-1, keepdims=True)
    e = jnp.exp(m2 - m1)
    w1 = 1.0 / (1.0 + e)
    w2 = e / (1.0 + e)
    idx_ref[...] = jnp.where(lane == 0.0, i1, jnp.where(lane == 1.0, i2, 0.0)).astype(jnp.int32)
    w_ref[...] = jnp.where(lane == 0.0, w1, jnp.where(lane == 1.0, w2, 0.0))


def moe_router(hp, w_router, b_router):
    D = w_router.shape[0]
    n = D // LANE
    M = hp.shape[0] // n
    tm = _pick_tile((1024, 512, 256, 128), M)
    w_pad = jnp.zeros((D, LANE), F32).at[:, :N_EXPERTS].set(w_router.astype(F32))
    w_hi = w_pad.astype(BF16)
    w_lo = (w_pad - w_hi.astype(F32)).astype(BF16)
    b_pad = jnp.zeros((1, LANE), F32).at[0, :N_EXPERTS].set(b_router.astype(F32))
    return pl.pallas_call(
        _router_kernel,
        grid=(M // tm,),
        in_specs=[pl.BlockSpec((tm * n, LANE), lambda i: (i, 0)),
                  pl.BlockSpec((D, LANE), lambda i: (0, 0)),
                  pl.BlockSpec((D, LANE), lambda i: (0, 0)),
                  pl.BlockSpec((1, LANE), lambda i: (0, 0))],
        out_specs=[pl.BlockSpec((tm, LANE), lambda i: (i, 0)), pl.BlockSpec((tm, LANE), lambda i: (i, 0))],
        out_shape=[jax.ShapeDtypeStruct((M, LANE), jnp.int32), jax.ShapeDtypeStruct((M, LANE), F32)],
        compiler_params=_params(1),
        name="moe_router",
    )(hp, w_hi, w_lo, b_pad)


GATHER_UNROLL = 8


def _gather_rows_kernel(idx_ref, src_ref, o_ref, buf, sem, *, n):
    tg = o_ref.shape[0]
    base = pl.program_id(0) * tg

    def start(it, carry):
        for u in range(GATHER_UNROLL):
            r = it * GATHER_UNROLL + u
            src_row = pl.multiple_of(idx_ref[base + r] * n, n)
            dst_row = pl.multiple_of(r * n, n)
            pltpu.make_async_copy(src_ref.at[pl.ds(src_row, n)], buf.at[pl.ds(dst_row, n)], sem).start(
                priority=u % 2)
        return carry

    lax.fori_loop(0, tg // GATHER_UNROLL, start, 0)
    pltpu.make_async_copy(src_ref.at[pl.ds(0, tg * n)], buf, sem).wait()
    o_ref[...] = _load_row_tiles(buf, n).astype(o_ref.dtype)


def gather_rows(src, idx, n, *, tg=1024):
    M = idx.shape[0]
    return pl.pallas_call(
        functools.partial(_gather_rows_kernel, n=n),
        grid_spec=pltpu.PrefetchScalarGridSpec(
            num_scalar_prefetch=1,
            grid=(M // tg,),
            in_specs=[pl.BlockSpec(memory_space=pl.ANY)],
            out_specs=pl.BlockSpec((tg, n * LANE), lambda i, idx_ref: (i, 0)),
            scratch_shapes=[pltpu.VMEM((tg * n, LANE), src.dtype), pltpu.SemaphoreType.DMA(())],
        ),
        out_shape=jax.ShapeDtypeStruct((M, n * LANE), BF16),
        compiler_params=_params(1),
        name="gather_rows",
    )(idx, src)


def _moe_w13_kernel(te_ref, tv_ref, a_ref, wg_ref, wu_ref, o_ref, sg, su):
    i = pl.program_id(1)
    prev = te_ref[jnp.maximum(i - 1, 0)]

    @pl.when(jnp.logical_or(i == 0, te_ref[i] != prev))
    def _():
        sg[...] = wg_ref[...].astype(BF16)
        su[...] = wu_ref[...].astype(BF16)

    @pl.when(tv_ref[i] == 1)
    def _():
        a = a_ref[...]
        o_ref[...] = (_silu(_dot(a, sg[...])) * _dot(a, su[...])).astype(o_ref.dtype)

    @pl.when(tv_ref[i] == 0)
    def _():
        o_ref[...] = jnp.zeros_like(o_ref)


def moe_w13(xs, w13, moe_idx, tile_expert, tile_valid, *, tn):
    P, D = xs.shape
    F = w13.shape[-1] // 2
    tm = MOE_TM
    nj = F // tn
    return pl.pallas_call(
        _moe_w13_kernel,
        grid_spec=pltpu.PrefetchScalarGridSpec(
            num_scalar_prefetch=2,
            grid=(nj, P // tm),
            in_specs=[pl.BlockSpec((tm, D), lambda j, i, te, tv: (i, 0)),
                      pl.BlockSpec((None, None, D, tn), lambda j, i, te, tv: (moe_idx, te[i], 0, j)),
                      pl.BlockSpec((None, None, D, tn), lambda j, i, te, tv: (moe_idx, te[i], 0, j + nj))],
            out_specs=pl.BlockSpec((tm, tn), lambda j, i, te, tv: (i, j)),
            scratch_shapes=[pltpu.VMEM((D, tn), BF16), pltpu.VMEM((D, tn), BF16)],
        ),
        out_shape=jax.ShapeDtypeStruct((P, F), BF16),
        compiler_params=_params(2),
        name="moe_w13",
    )(tile_expert, tile_valid, xs, w13, w13)


def _moe_w2_kernel(te_ref, tv_ref, ts_ref, a_ref, w_ref, o_ref):
    i = pl.program_id(1)

    @pl.when(tv_ref[i] == 1)
    def _():
        _store_row_tiles(o_ref, _dot(a_ref[...], w_ref[...]))

    @pl.when(tv_ref[i] == 0)
    def _():
        o_ref[...] = jnp.zeros_like(o_ref)


def moe_w2(act, w2, tile_expert, tile_valid, tile_src, *, tn):
    P, F = act.shape
    D = w2.shape[-1]
    tm = MOE_TM
    n = tn // LANE
    return pl.pallas_call(
        _moe_w2_kernel,
        grid_spec=pltpu.PrefetchScalarGridSpec(
            num_scalar_prefetch=3,
            grid=(D // tn, P // tm),
            in_specs=[pl.BlockSpec((tm, F), lambda j, i, te, tv, ts: (ts[i], 0)),
                      pl.BlockSpec((None, F, tn), lambda j, i, te, tv, ts: (te[i], 0, j))],
            out_specs=pl.BlockSpec((None, tm * n, LANE), lambda j, i, te, tv, ts: (j, i, 0)),
        ),
        out_shape=jax.ShapeDtypeStruct((D // tn, P * n, LANE), F32),
        compiler_params=_params(2),
        name="moe_w2",
    )(tile_expert, tile_valid, tile_src, act, w2)


def _moe_combine_kernel(p0_ref, p1_ref, y_ref, x_ref, gate_ref, w_ref, o_ref, b0, b1, sem):
    tc = o_ref.shape[0]
    n = b0.shape[1] // tc
    base = pl.program_id(0) * tc

    def start(r, carry):
        dst = pl.ds(pl.multiple_of(r * n, n), n)
        src0 = pl.ds(pl.multiple_of(p0_ref[base + r] * n, n), n)
        src1 = pl.ds(pl.multiple_of(p1_ref[base + r] * n, n), n)
        pltpu.make_async_copy(y_ref.at[:, src0], b0.at[:, dst], sem.at[0]).start(priority=0)
        pltpu.make_async_copy(y_ref.at[:, src1], b1.at[:, dst], sem.at[1]).start(priority=1)
        return carry

    lax.fori_loop(0, tc, start, 0, unroll=GATHER_UNROLL)
    pltpu.make_async_copy(y_ref.at[:, pl.ds(0, tc * n)], b0, sem.at[0]).wait()
    pltpu.make_async_copy(y_ref.at[:, pl.ds(0, tc * n)], b1, sem.at[1]).wait()
    w = w_ref[...]
    y = w[:, 0:1] * _load_row_tiles(b0, n) + w[:, 1:2] * _load_row_tiles(b1, n)
    o_ref[...] = x_ref[...] + gate_ref[0] * y


def moe_combine(dm, y_sorted, pos0, pos1, top_w, x, gate, *, mod_row0):
    M, D = x.shape
    tc = 512
    t0 = mod_row0 // tc
    J = y_sorted.shape[0]
    n = D // (J * LANE)
    return pl.pallas_call(
        _moe_combine_kernel,
        grid_spec=pltpu.PrefetchScalarGridSpec(
            num_scalar_prefetch=2,
            grid=(M // tc,),
            in_specs=[pl.BlockSpec(memory_space=pl.ANY),
                      pl.BlockSpec((tc, D), lambda i, p0, p1: (i, 0)),
                      pl.BlockSpec((1, 1, D), lambda i, p0, p1: (dm.mod_row(i + t0, tc), 0, 0)),
                      pl.BlockSpec((tc, LANE), lambda i, p0, p1: (i, 0))],
            out_specs=pl.BlockSpec((tc, D), lambda i, p0, p1: (i, 0)),
            scratch_shapes=[pltpu.VMEM((J, tc * n, LANE), F32), pltpu.VMEM((J, tc * n, LANE), F32),
                            pltpu.SemaphoreType.DMA((2,))],
        ),
        out_shape=jax.ShapeDtypeStruct((M, D), F32),
        compiler_params=_params(1),
        name="moe_combine",
    )(pos0, pos1, y_sorted, x, gate, top_w)


def moe_ffn(dm, hp, x, gate, w_router, b_router, w13, w2, moe_idx, *, mod_row0):
    M, D = x.shape
    E = N_EXPERTS
    tm = MOE_TM
    top_idx, top_w = moe_router(hp, w_router, b_router)
    e_flat = top_idx[:, :TOP_K].T.reshape(-1)
    onehot = (e_flat[:, None] == jnp.arange(E, dtype=jnp.int32)[None, :]).astype(F32)
    blk = LANE
    nb = onehot.shape[0] // blk
    oh3 = onehot.reshape(nb, blk, E)
    exact = lax.Precision.HIGHEST
    within = jnp.einsum("ij,bjk->bik", jnp.tril(jnp.ones((blk, blk), F32), -1), oh3, precision=exact)
    before = jnp.einsum("ab,bk->ak", jnp.tril(jnp.ones((nb, nb), F32), -1), jnp.sum(oh3, axis=1), precision=exact)
    rank = jnp.sum((within + before[:, None, :]) * oh3, axis=-1).reshape(-1).astype(jnp.int32)
    counts = jnp.sum(onehot, axis=0).astype(jnp.int32)
    padded = ((counts + tm - 1) // tm) * tm
    ends = jnp.cumsum(padded)
    starts = ends - padded
    pos = starts[e_flat] + rank
    P = TOP_K * M + E * tm
    n_tiles = P // tm
    tok = jnp.tile(jnp.arange(M, dtype=jnp.int32), TOP_K)
    gidx = jnp.zeros((P,), jnp.int32).at[pos].set(tok)
    tile_start = jnp.arange(n_tiles, dtype=jnp.int32) * tm
    tile_valid = (tile_start < ends[-1]).astype(jnp.int32)
    te = jnp.sum((tile_start[:, None] >= ends[None, :]).astype(jnp.int32), axis=1)
    last_e = jnp.sum((ends[-1] - 1 >= ends).astype(jnp.int32))
    tile_expert = jnp.minimum(te, last_e).astype(jnp.int32)

    xs = gather_rows(hp, gidx, D // LANE)
    F = w13.shape[-1] // 2
    act = moe_w13(xs, w13, moe_idx, tile_expert, tile_valid, tn=_pick_tile((1024, 512, 256, 128), F))
    tile_src = jnp.minimum(jnp.arange(n_tiles, dtype=jnp.int32), jnp.sum(tile_valid) - 1).astype(jnp.int32)
    y_sorted = moe_w2(act, w2[moe_idx].astype(BF16), tile_expert, tile_valid, tile_src, tn=min(1024, D))
    return moe_combine(dm, y_sorted, pos[:M], pos[M:], top_w, x, gate, mod_row0=mod_row0)


def _axial_angles(seq, rot_dim):
    rows = seq // GRID_W
    t_row = jnp.repeat(jnp.arange(rows, dtype=F32), GRID_W)
    t_col = jnp.tile(jnp.arange(GRID_W, dtype=F32), rows)
    quarter = rot_dim // 4
    inv_freq = ROPE_THETA ** (-jnp.arange(quarter, dtype=F32) / quarter)
    return jnp.concatenate([t_row[:, None] * inv_freq, t_col[:, None] * inv_freq], axis=-1)


def _rope_tables(dm):
    ident = min(dm.tm, 256)
    ang = _axial_angles(dm.S, SWA_HEAD_DIM)
    cos_a = jnp.concatenate([jnp.cos(ang), jnp.cos(ang)], axis=-1)
    sin_a = jnp.concatenate([-jnp.sin(ang), jnp.sin(ang)], axis=-1)
    cos_a = jnp.concatenate([jnp.ones((ident, LANE), F32), cos_a], axis=0)
    sin_a = jnp.concatenate([jnp.zeros((ident, LANE), F32), sin_a], axis=0)
    ang = _axial_angles(dm.S, MLA_ROPE)
    half = MLA_ROPE // 2
    zeros = jnp.zeros((dm.S, half), F32)
    pad = jnp.zeros((dm.S, LANE - MLA_ROPE), F32)
    c_b = jnp.concatenate([jnp.cos(ang), jnp.cos(ang), pad], axis=-1)
    s1_b = jnp.concatenate([-jnp.sin(ang), zeros, pad], axis=-1)
    s2_b = jnp.concatenate([zeros, jnp.sin(ang), pad], axis=-1)
    c_b = jnp.concatenate([jnp.ones((ident, LANE), F32), c_b], axis=0)
    s1_b = jnp.concatenate([jnp.zeros((ident, LANE), F32), s1_b], axis=0)
    s2_b = jnp.concatenate([jnp.zeros((ident, LANE), F32), s2_b], axis=0)
    return (cos_a, sin_a), (c_b, s1_b, s2_b)


def _pad_head_vec(g):
    return jnp.zeros((1, MLA_QK_PAD), F32).at[0, :MLA_QK].set(g.astype(F32))


def _trunk(x, c, ctx, c_ctx, mod_w, mod_b, norm_mix_g, norm_ffn_g, w_in,
           swa_q_norm_g, swa_k_norm_g, swa_sink,
           mla_q_a_norm_g, mla_w_uq, mla_kv_a_norm_g, mla_w_ukv, mla_q_norm_g, mla_k_norm_g,
           ssm_lam_re, ssm_lam_im, ssm_log_step, ssm_b_re, ssm_b_im, ssm_c_re, ssm_c_im,
           ssm_d, ssm_w_glu, ssm_b_glu, w_branch, w_out,
           ffn_w13, ffn_w2, moe_w_router, moe_b_router, moe_w13, moe_w2):
    B, S, D = x.shape
    L = ctx.shape[1]
    depth = mod_w.shape[0]
    dm = Dims(B, S, L)
    tm = dm.tm
    RC, RL, R = dm.RC, dm.RL, dm.R
    q_w = SWA_HEADS * SWA_HEAD_DIM
    kv_w = SWA_KV_HEADS * SWA_HEAD_DIM
    q_rank = mla_w_uq.shape[1]
    kv_rank = mla_w_ukv.shape[1]
    ssm_w = ssm_d.shape[1]
    n_gate = N_BRANCH * D
    src = {}
    off = 0
    for name, width in (("q", q_w), ("k", kv_w), ("v", kv_w), ("c_q", q_rank), ("c_kv", kv_rank),
                        ("kpe", MLA_ROPE), ("u", ssm_w), ("gates", n_gate)):
        src[name] = (off, width)
        off += width
    order = ("q", "u", "c_q", "k", "v", "c_kv", "kpe")
    col = {}
    off = 0
    for name in order:
        col[name] = off
        off += src[name][1]
    z_tn = 1792
    z_cols = -(-off // z_tn) * z_tn

    (cos_a, sin_a), tabs_b = _rope_tables(dm)
    ssm_compact = jax.vmap(_ssm_compact_tables)(ssm_lam_re, ssm_lam_im, ssm_log_step, ssm_b_re, ssm_b_im,
                                                ssm_c_re, ssm_c_im)
    x_ctx0 = ctx.reshape(RC, D).astype(F32)
    xall = x.reshape(RL, D).astype(F32)
    cond = jnp.zeros((8, D), F32).at[0].set(c_ctx.astype(F32)).at[1:1 + B].set(c.astype(F32))

    for layer in range(depth):
        with_ctx = layer < depth - 1
        row0 = 0 if with_ctx else RC
        n_rows = R - row0
        mods = mm1(cond, [(mod_w, (layer,), 0)], _epi_bias, n_rows=8, n_cols=6 * D, tm=8, tn=512, out_dtype=F32,
                   extras=[(mod_b.reshape(depth, 1, 6 * D), (None, 1, 512), lambda j, i: (layer, 0, j))],
                   prologue=lambda a: _silu(a).astype(BF16), name="ada_mod")
        sh_m, sc_m, g_m, sh_f, sc_f, g_f = [mods[:, i * D:(i + 1) * D].reshape(8, 1, D) for i in range(6)]

        split_input = layer == 0
        h = modulate(dm, xall, norm_mix_g[layer], sh_m, sc_m, mod_row0=0, x_ctx=x_ctx0 if split_input else None)
        w_l = w_in[layer]
        w_rest = jnp.concatenate([w_l[:, src[n][0]:src[n][0] + src[n][1]] for n in order]
                                 + [jnp.zeros((D, z_cols - off), w_l.dtype)], axis=1)
        w_gates = w_l[:, src["gates"][0]:]
        z = mm1(h, [(w_rest, (), 0)], _epi_id, n_rows=R, n_cols=z_cols, tm=min(tm, 512), tn=z_tn, out_dtype=F32,
                name="w_in")
        gates = mm1(h, [(w_gates, (), 0)], _epi_sigmoid, n_rows=R, n_cols=n_gate, tm=min(tm, 512), tn=2048,
                    out_dtype=BF16, name="w_in_gates")

        qa, ka, va = swa_prep(dm, z, col, swa_q_norm_g[layer], swa_k_norm_g[layer], cos_a, sin_a)
        ya_l = swa_attention(dm, qa, ka, va, swa_sink[layer], latent=True)
        w_uq = mla_w_uq[layer].reshape(q_rank, MLA_HEADS, MLA_QK)
        w_uq = jnp.pad(w_uq, ((0, 0), (0, 0), (0, MLA_QK_PAD - MLA_QK))).reshape(q_rank, MLA_HEADS * MLA_QK_PAD)
        w_ukv = mla_w_ukv[layer].reshape(kv_rank, MLA_HEADS, MLA_NOPE + MLA_V)
        w_ukv = jnp.concatenate([w_ukv[:, :, :MLA_NOPE].reshape(kv_rank, -1),
                                 w_ukv[:, :, MLA_NOPE:].reshape(kv_rank, -1)], axis=1)
        mla_args = (dm, z, col, mla_q_a_norm_g[layer], mla_kv_a_norm_g[layer], w_uq, w_ukv,
                    _pad_head_vec(mla_q_norm_g[layer]), _pad_head_vec(mla_k_norm_g[layer]), tabs_b)
        qm_c, km_c, vm_c, vt_c = mla_prep(*mla_args, row0=0, n_rows=RC, rows_per_batch=L)
        qm_l, km_l, _, vt_l = mla_prep(*mla_args, row0=RC, n_rows=RL, rows_per_batch=S)
        yb_l = mla_attention_lat(dm, qm_l, km_c, km_l, vt_c, vt_l)
        if with_ctx:
            ys_ctx = (swa_attention(dm, qa, ka, va, swa_sink[layer], latent=False),
                      mla_attention_ctx(dm, qm_c, km_c, vm_c))
        else:
            ys_ctx = None
        yg = s5_branch(dm, z, col, ssm_compact, layer, ssm_d[layer])
        b_glu = ssm_b_glu.reshape(depth, 1, 2 * ssm_w)
        gl_tn = 1024
        yc = mm1(yg, [(ssm_w_glu, (layer,), 0), (ssm_w_glu, (layer,), ssm_w)], _epi_glu_bias,
                 n_rows=R, n_cols=ssm_w, tm=tm, tn=gl_tn, out_dtype=BF16,
                 extras=[(b_glu, (None, 1, gl_tn), lambda j, i: (layer, 0, j)),
                         (b_glu, (None, 1, gl_tn), lambda j, i: (layer, 0, j + ssm_w // gl_tn))],
                 prologue=lambda a: a.astype(BF16), name="ssm_glu")
        mixed = merge_branches(dm, (ya_l, yb_l, yc), (RC, RC, 0), gates, w_branch, layer, row0=row0, n_rows=n_rows,
                               ys_ctx=ys_ctx)
        t0 = row0 // tm
        gate_extra = (g_m, (1, 1, 1024), lambda j, i: (dm.mod_row(i + t0, tm), 0, j))
        if split_input and with_ctx:
            nct = RC // tm
            res_epi = functools.partial(_epi_residual_two_sources, n_ctx_tiles=nct)
            res_extras = [(x_ctx0, (tm, 1024), lambda j, i: (jnp.minimum(i, nct - 1), j)),
                          (xall, (tm, 1024), lambda j, i: (jnp.maximum(i - nct, 0), j)), gate_extra]
        elif split_input:
            res_epi = _epi_residual
            res_extras = [(xall, (tm, 1024), lambda j, i: (i, j)), gate_extra]
        else:
            res_epi = _epi_residual
            res_extras = [(xall, (tm, 1024), lambda j, i: (i + t0, j)), gate_extra]
        x1 = mm1(mixed, [(w_out, (layer,), 0)], res_epi, n_rows=n_rows, n_cols=D, tm=tm, tn=1024, out_dtype=F32,
                 extras=res_extras, name="w_out")
        is_moe = layer % 2 == 1
        h2 = modulate(dm, x1, norm_ffn_g[layer], sh_f, sc_f, mod_row0=row0, pack=is_moe)
        if not is_moe:
            F = ffn_w13.shape[-1] // 2
            f_tn = _pick_tile((512, 256, 128), F)
            act = mm1(h2, [(ffn_w13, (layer // 2,), 0), (ffn_w13, (layer // 2,), F)], _epi_swiglu, n_rows=n_rows,
                      n_cols=F, tm=tm, tn=f_tn, out_dtype=BF16, name="ffn_w13")
            x2 = mm2_residual(dm, act, ffn_w2[layer // 2].astype(BF16), x1, g_f, mod_row0=row0, tm=tm, tk=f_tn)
        else:
            if with_ctx:
                raise NotImplementedError("a mixture-of-experts layer that still feeds context rows")
            x2 = moe_ffn(dm, h2, x1, g_f, moe_w_router[layer // 2], moe_b_router[layer // 2], moe_w13, moe_w2,
                         layer // 2, mod_row0=row0)
        xall = x2
    return xall.reshape(B, S, D)


def kernel(x, c, ctx, c_ctx, mod_w, mod_b, norm_mix_g, norm_ffn_g, w_in, swa_q_norm_g, swa_k_norm_g, swa_sink, mla_q_a_norm_g, mla_w_uq, mla_kv_a_norm_g, mla_w_ukv, mla_q_norm_g, mla_k_norm_g, ssm_lam_re, ssm_lam_im, ssm_log_step, ssm_b_re, ssm_b_im, ssm_c_re, ssm_c_im, ssm_d, ssm_w_glu, ssm_b_glu, w_branch, w_out, ffn_w13, ffn_w2, moe_w_router, moe_b_router, moe_w13, moe_w2):
    return _trunk(x, c, ctx, c_ctx, mod_w, mod_b, norm_mix_g, norm_ffn_g, w_in, swa_q_norm_g, swa_k_norm_g, swa_sink,
                  mla_q_a_norm_g, mla_w_uq, mla_kv_a_norm_g, mla_w_ukv, mla_q_norm_g, mla_k_norm_g,
                  ssm_lam_re, ssm_lam_im, ssm_log_step, ssm_b_re, ssm_b_im, ssm_c_re, ssm_c_im,
                  ssm_d, ssm_w_glu, ssm_b_glu, w_branch, w_out, ffn_w13, ffn_w2, moe_w_router, moe_b_router,
                  moe_w13, moe_w2)
```
